```python
import math
import jax, jax.numpy as jnp
from jax import lax
import numpy as np

D_MODEL = 1024
BATCH = 4
SEQ = 4096
DEPTH = 4

HEAD_DIM = 64
ROPE_DIM = 16
ROPE_THETA = 500000.0
GRID_W = 64
NA_HEADS = 8
NA_WIN_R = 8
NA_WIN_C = 16
NA_QCOL_BLOCK = 16
NA_KCOL_SPAN = 32
DIFF_HEADS = 4
DIFF_V_DIM = 128
DIFF_Q_BLOCK = 128
DIL_PATTERNS = ((128, 1), (512, 4), (2048, 16))
DIL_HEADS_PER_GROUP = 4
DIL_HEADS = 12
DIL_Q_BLOCK = 64
A_W = 512
B_QK_W = 512
B_V_W = 512
C_W = 768
C_OUT_W = 256
N_BRANCH = 3
IN_W = 3 * A_W + 2 * B_QK_W + B_V_W + 3 * C_W + N_BRANCH * D_MODEL
N_GROUPS = 4
EXPERTS_PER_GROUP = 4
N_EXPERTS = 16
TOP_K = 2
D_FF_EXPERT = D_MODEL // 2
NORM_EPS = 1e-6
SUBLN_EPS = 1e-5
NEG_INF = -1e30

kernel_name = 'hybrid_na_diff_dilated_hmoe_encoder'


def rmsnorm(x, g, eps=NORM_EPS):
    xf = x.astype(jnp.float32)
    y = xf * lax.rsqrt(jnp.mean(xf * xf, axis=-1, keepdims=True) + eps)
    return (y * g.astype(jnp.float32)).astype(x.dtype)


def rope_tables(T):
    inv = 1.0 / (ROPE_THETA ** (jnp.arange(0, ROPE_DIM, 2, dtype=jnp.float32) / ROPE_DIM))
    ang = jnp.arange(T, dtype=jnp.float32)[:, None] * inv[None, :]
    return jnp.cos(ang), jnp.sin(ang)


def partial_rope(x, cos, sin):
    half = ROPE_DIM // 2
    xr = x[..., :ROPE_DIM].astype(jnp.float32)
    x1, x2 = xr[..., :half], xr[..., half:]
    c = cos[None, :, None, :]
    s = sin[None, :, None, :]
    rot = jnp.concatenate([x1 * c - x2 * s, x2 * c + x1 * s], axis=-1).astype(x.dtype)
    return jnp.concatenate([rot, x[..., ROPE_DIM:]], axis=-1)


def neighbourhood_attention(q, k, v, rpb):
    B, T, H, dh = q.shape
    rows = T // GRID_W
    wr = min(NA_WIN_R, rows)
    ncb = GRID_W // NA_QCOL_BLOCK

    def grid(t):
        return t.reshape(B, rows, GRID_W, H, dh).transpose(0, 3, 1, 2, 4)

    qg, kg, vg = grid(q), grid(k), grid(v)
    qc = np.arange(GRID_W).reshape(ncb, NA_QCOL_BLOCK)
    cs = np.clip(np.arange(ncb) * NA_QCOL_BLOCK - NA_WIN_C // 2, 0, GRID_W - NA_KCOL_SPAN)
    kc = cs[:, None] + np.arange(NA_KCOL_SPAN)[None, :]
    c0 = np.clip(qc - NA_WIN_C // 2, 0, GRID_W - NA_WIN_C)
    col_ok = (kc[:, None, :] >= c0[:, :, None]) & (kc[:, None, :] < c0[:, :, None] + NA_WIN_C)
    dc_idx = np.clip(kc[:, None, :] - qc[:, :, None] + NA_WIN_C - 1, 0, 2 * NA_WIN_C - 2)
    rpb_c = rpb[:, :, dc_idx]
    col_mask = jnp.asarray(col_ok)[None, None, :, :, None, :]
    kc_j = jnp.asarray(kc)
    scale = dh ** -0.5

    def one_row(r):
        r0 = jnp.clip(r - wr // 2, 0, rows - wr)
        q_r = lax.dynamic_index_in_dim(qg, r, axis=2, keepdims=False)
        k_b = lax.dynamic_slice_in_dim(kg, r0, wr, axis=2)[:, :, :, kc_j]
        v_b = lax.dynamic_slice_in_dim(vg, r0, wr, axis=2)[:, :, :, kc_j]
        q_r = q_r.reshape(B, H, ncb, NA_QCOL_BLOCK, dh)
        s = jnp.einsum('bhcqd,bhrckd->bhcqrk', q_r, k_b,
                       preferred_element_type=jnp.float32) * scale
        dr = r0 + jnp.arange(wr) - r + NA_WIN_R - 1
        bias = jnp.take(rpb_c, dr, axis=1).transpose(0, 2, 3, 1, 4)
        s = jnp.where(col_mask, s + bias.astype(jnp.float32)[None], NEG_INF)
        p = jax.nn.softmax(s.reshape(B, H, ncb, NA_QCOL_BLOCK, wr * NA_KCOL_SPAN), axis=-1)
        p = p.reshape(s.shape).astype(v.dtype)
        o = jnp.einsum('bhcqrk,bhrckd->bhcqd', p, v_b)
        return o.reshape(B, H, GRID_W, dh)

    out = lax.map(one_row, jnp.arange(rows))
    return out.transpose(1, 0, 3, 2, 4).reshape(B, T, H * dh)


def differential_attention(q, k, v, lam, lam_init, subln_g):
    B, T, H, _, dh = q.shape
    nb = T // DIFF_Q_BLOCK
    scale = dh ** -0.5
    qb = q.reshape(B, nb, DIFF_Q_BLOCK, H, 2, dh).transpose(1, 0, 3, 4, 2, 5)
    kt = k.transpose(0, 2, 3, 1, 4)
    vt = v.transpose(0, 2, 1, 3)

    def one_block(q_blk):
        s = jnp.einsum('bhcqd,bhckd->bhcqk', q_blk, kt,
                       preferred_element_type=jnp.float32) * scale
        p = jax.nn.softmax(s, axis=-1)
        a = p[:, :, 0] - lam * p[:, :, 1]
        return jnp.einsum('bhqk,bhkd->bhqd', a.astype(v.dtype), vt)

    o = lax.map(one_block, qb)
    o = o.transpose(1, 0, 3, 2, 4).reshape(B, T, H, 2 * dh)
    o = rmsnorm(o, subln_g, SUBLN_EPS) * (1.0 - lam_init)
    return o.reshape(B, T, H * 2 * dh)


def dilated_group(q, k, v, dil, radius):
    B, T, H, dh = q.shape
    L = T // dil
    nb = -(-L // DIL_Q_BLOCK)
    Lp = nb * DIL_Q_BLOCK
    kb = DIL_Q_BLOCK + 2 * radius

    def phase(t):
        return t.reshape(B, L, dil, H, dh).transpose(0, 2, 3, 1, 4)

    qp = jnp.pad(phase(q), ((0, 0), (0, 0), (0, 0), (0, Lp - L), (0, 0)))
    pad_k = ((0, 0), (0, 0), (0, 0), (radius, Lp - L + radius), (0, 0))
    kp = jnp.pad(phase(k), pad_k)
    vp = jnp.pad(phase(v), pad_k)
    idx = np.arange(nb)[:, None] * DIL_Q_BLOCK + np.arange(kb)[None, :]
    k_blk = kp[:, :, :, idx]
    v_blk = vp[:, :, :, idx]
    q_blk = qp.reshape(B, dil, H, nb, DIL_Q_BLOCK, dh)
    s = jnp.einsum('bphnqd,bphnkd->bphnqk', q_blk, k_blk,
                   preferred_element_type=jnp.float32) * (dh ** -0.5)
    key_pos = idx - radius
    q_pos = np.arange(Lp).reshape(nb, DIL_Q_BLOCK)
    rel = key_pos[:, None, :] - q_pos[:, :, None]
    ok = (np.abs(rel) <= radius) & (key_pos[:, None, :] >= 0) & (key_pos[:, None, :] < L)
    s = jnp.where(jnp.asarray(ok), s, NEG_INF)
    lse = jax.nn.logsumexp(s, axis=-1)
    p = jnp.exp(s - lse[..., None]).astype(v.dtype)
    o = jnp.einsum('bphnqk,bphnkd->bphnqd', p, v_blk)
    o = o.reshape(B, dil, H, Lp, dh)[:, :, :, :L].transpose(0, 3, 1, 2, 4).reshape(B, T, H, dh)
    lse = lse.reshape(B, dil, H, Lp)[:, :, :, :L].transpose(0, 3, 1, 2).reshape(B, T, H)
    return o, lse


def dilated_mixture(q, k, v):
    B, T = q.shape[0], q.shape[1]
    outs, lses = [], []
    for g, (window, dil) in enumerate(DIL_PATTERNS):
        sl = slice(g * DIL_HEADS_PER_GROUP, (g + 1) * DIL_HEADS_PER_GROUP)
        o, l = dilated_group(q[:, :, sl], k[:, :, sl], v[:, :, sl], dil, window // (2 * dil))
        outs.append(o)
        lses.append(l)
    o = jnp.stack(outs, axis=0)
    wts = jax.nn.softmax(jnp.stack(lses, axis=0), axis=0)
    y = jnp.sum(o * wts[..., None].astype(o.dtype), axis=0)
    return y.reshape(B, T, C_OUT_W)


def hybrid_mixer(xn, w_in, rpb, lq1, lk1, lq2, lk2, subln_g, w_pa, w_pb, w_pc, w_o, lam_init, cos, sin):
    B, T, _ = xn.shape
    proj = xn @ w_in
    widths = [A_W] * 3 + [B_QK_W] * 2 + [B_V_W] + [C_W] * 3 + [D_MODEL] * 2
    cuts = [int(c) for c in np.cumsum(widths)]
    qa, ka, va, qb, kb, vb, qc, kc, vc, ga, gb, gc = jnp.split(proj, cuts, axis=-1)

    def heads(t, h):
        return t.reshape(B, T, h, HEAD_DIM)

    ya = neighbourhood_attention(heads(qa, NA_HEADS), heads(ka, NA_HEADS), heads(va, NA_HEADS), rpb)
    qb = partial_rope(heads(qb, 2 * DIFF_HEADS), cos, sin).reshape(B, T, DIFF_HEADS, 2, HEAD_DIM)
    kb = partial_rope(heads(kb, 2 * DIFF_HEADS), cos, sin).reshape(B, T, DIFF_HEADS, 2, HEAD_DIM)
    vb = vb.reshape(B, T, DIFF_HEADS, DIFF_V_DIM)
    f32 = jnp.float32
    lam = (jnp.exp(jnp.sum(lq1.astype(f32) * lk1.astype(f32)))
           - jnp.exp(jnp.sum(lq2.astype(f32) * lk2.astype(f32))) + lam_init)
    yb = differential_attention(qb, kb, vb, lam, lam_init, subln_g)
    yc = dilated_mixture(partial_rope(heads(qc, DIL_HEADS), cos, sin),
                         partial_rope(heads(kc, DIL_HEADS), cos, sin),
                         heads(vc, DIL_HEADS))
    merged = (jax.nn.sigmoid(ga) * (ya @ w_pa) + jax.nn.sigmoid(gb) * (yb @ w_pb)
              + jax.nn.sigmoid(gc) * (yc @ w_pc))
    return merged @ w_o


def hierarchical_moe(xn, wg, bg, we, be, w1, w3, w2):
    B, T, D = xn.shape
    xf = xn.reshape(-1, D)
    N = xf.shape[0]
    g_logits = (xf @ wg).astype(jnp.float32) + bg.astype(jnp.float32)
    g_prob = jax.nn.softmax(g_logits, axis=-1)
    g_sel = jnp.argmax(g_logits, axis=-1)
    g_gate = jnp.take_along_axis(g_prob, g_sel[:, None], axis=-1)
    e_logits = ((xf @ we).astype(jnp.float32) + be.astype(jnp.float32)).reshape(N, N_GROUPS, EXPERTS_PER_GROUP)
    e_logits = jnp.take_along_axis(e_logits, g_sel[:, None, None], axis=1)[:, 0]
    top_v, top_i = lax.top_k(e_logits, TOP_K)
    top_w = jax.nn.softmax(top_v, axis=-1) * g_gate
    eid = g_sel[:, None] * EXPERTS_PER_GROUP + top_i
    combine = jnp.sum(jax.nn.one_hot(eid, N_EXPERTS, dtype=jnp.float32) * top_w[..., None], axis=1)
    combine = combine.astype(xf.dtype)
    out = jnp.zeros_like(xf)
    for e in range(N_EXPERTS):
        h = jax.nn.silu(xf @ w1[e]) * (xf @ w3[e])
        out = out + combine[:, e:e + 1] * (h @ w2[e])
    return out.reshape(B, T, D)


def setup_inputs(seed: int = 0) -> dict:
    key = jax.random.key(seed)
    ks = jax.random.split(key, 22)
    L = DEPTH

    def nrm(k, shape, scale):
        return jax.random.normal(k, shape, jnp.float32) * scale

    return {
        'x': nrm(ks[0], (BATCH, SEQ, D_MODEL), 1.0),
        'w_in': nrm(ks[1], (L, D_MODEL, IN_W), D_MODEL ** -0.5),
        'na_rpb': nrm(ks[2], (L, NA_HEADS, 2 * NA_WIN_R - 1, 2 * NA_WIN_C - 1), 0.02),
        'lam_q1': nrm(ks[3], (L, HEAD_DIM), 0.1),
        'lam_k1': nrm(ks[4], (L, HEAD_DIM), 0.1),
        'lam_q2': nrm(ks[5], (L, HEAD_DIM), 0.1),
        'lam_k2': nrm(ks[6], (L, HEAD_DIM), 0.1),
        'diff_subln': 1.0 + nrm(ks[7], (L, DIFF_V_DIM), 0.01),
        'w_pa': nrm(ks[8], (L, A_W, D_MODEL), A_W ** -0.5),
        'w_pb': nrm(ks[9], (L, B_V_W, D_MODEL), B_V_W ** -0.5),
        'w_pc': nrm(ks[10], (L, C_OUT_W, D_MODEL), C_OUT_W ** -0.5),
        'w_o': nrm(ks[11], (L, D_MODEL, D_MODEL), D_MODEL ** -0.5),
        'norm_mix': 1.0 + nrm(ks[12], (L, D_MODEL), 0.01),
        'norm_ffn': 1.0 + nrm(ks[13], (L, D_MODEL), 0.01),
        'router_group_w': nrm(ks[14], (L, D_MODEL, N_GROUPS), D_MODEL ** -0.5),
        'router_group_b': nrm(ks[15], (L, N_GROUPS), 0.01),
        'router_expert_w': nrm(ks[16], (L, D_MODEL, N_EXPERTS), D_MODEL ** -0.5),
        'router_expert_b': nrm(ks[17], (L, N_EXPERTS), 0.01),
        'w1': nrm(ks[18], (L, N_EXPERTS, D_MODEL, D_FF_EXPERT), D_MODEL ** -0.5),
        'w3': nrm(ks[19], (L, N_EXPERTS, D_MODEL, D_FF_EXPERT), D_MODEL ** -0.5),
        'w2': nrm(ks[20], (L, N_EXPERTS, D_FF_EXPERT, D_MODEL), D_FF_EXPERT ** -0.5),
        'norm_final': 1.0 + nrm(ks[21], (D_MODEL,), 0.01),
    }


def reference(x, w_in, na_rpb, lam_q1, lam_k1, lam_q2, lam_k2, diff_subln, w_pa, w_pb, w_pc, w_o,
              norm_mix, norm_ffn, router_group_w, router_group_b, router_expert_w, router_expert_b,
              w1, w3, w2, norm_final):
    T = x.shape[1]
    cos, sin = rope_tables(T)
    h = x
    for l in range(DEPTH):
        lam_init = 0.8 - 0.6 * math.exp(-0.3 * l)
        h = h + hybrid_mixer(rmsnorm(h, norm_mix[l]), w_in[l], na_rpb[l], lam_q1[l], lam_k1[l],
                             lam_q2[l], lam_k2[l], diff_subln[l], w_pa[l], w_pb[l], w_pc[l], w_o[l],
                             lam_init, cos, sin)
        h = h + hierarchical_moe(rmsnorm(h, norm_ffn[l]), router_group_w[l], router_group_b[l],
                                 router_expert_w[l], router_expert_b[l], w1[l], w3[l], w2[l])
    return rmsnorm(h, norm_final)
```

```python
import functools
import math

import jax
import jax.numpy as jnp
import numpy as np
from jax import lax
from jax.experimental import pallas as pl
from jax.experimental.pallas import tpu as pltpu

F32 = jnp.float32
BF16 = jnp.bfloat16

D_MODEL = 1024
HEAD_DIM = 64
ROPE_DIM = 16
ROPE_THETA = 500000.0
GRID_W = 64
NA_HEADS = 8
NA_WIN_R = 8
NA_WIN_C = 16
DIFF_HEADS = 4
DIL_PATTERNS = ((128, 1), (512, 4), (2048, 16))
N_GROUPS = 4
EXPERTS_PER_GROUP = 4
N_EXPERTS = 16
D_FF_EXPERT = D_MODEL // 2
NORM_EPS = 1e-6
SUBLN_EPS = 1e-5
NEG_INF = -1e30

LANES = 128
MXU_N = 256
VMEM_LIMIT = 56 * 1024 * 1024

A_W, B_W, C_W = 512, 512, 768
OFF_GATE = 0
OFF_QA = 3 * D_MODEL
OFF_KA = OFF_QA + A_W
OFF_VA = OFF_KA + A_W
OFF_QB = OFF_VA + A_W
OFF_KB = OFF_QB + B_W
OFF_VB = OFF_KB + B_W
OFF_QC = OFF_VB + B_W
OFF_KC = OFF_QC + C_W
OFF_VC = OFF_KC + C_W
IN_W = OFF_VC + C_W
_REF_GATE_OFF = 3 * A_W + 3 * B_W + 3 * C_W
ROPE_RANGES = ((OFF_QB // MXU_N, OFF_VB // MXU_N), (OFF_QC // MXU_N, OFF_VC // MXU_N))


def _cparams(*sem):
    return pltpu.CompilerParams(dimension_semantics=sem, vmem_limit_bytes=VMEM_LIMIT)


def _inproj_kernel(x_ref, g_ref, w_ref, rope_ref, o_ref, xn_ref, *, tn):
    j = pl.program_id(1)

    @pl.when(j == 0)
    def _():
        x = x_ref[...]
        ms = jnp.mean(x * x, axis=-1, keepdims=True)
        xn_ref[...] = (x * lax.rsqrt(ms + NORM_EPS) * g_ref[...]).astype(BF16)

    xn = xn_ref[...]
    nsub = tn // MXU_N
    for c in range(nsub):
        sl = slice(c * MXU_N, (c + 1) * MXU_N)
        acc = jnp.dot(xn, w_ref[:, sl], preferred_element_type=F32)
        blk = j * nsub + c
        roped = ((blk >= ROPE_RANGES[0][0]) & (blk < ROPE_RANGES[0][1])) | (
            (blk >= ROPE_RANGES[1][0]) & (blk < ROPE_RANGES[1][1]))

        @pl.when(roped)
        def _():
            reps = MXU_N // LANES
            cos = jnp.concatenate([rope_ref[0]] * reps, axis=1)
            s_up = jnp.concatenate([rope_ref[1]] * reps, axis=1)
            s_dn = jnp.concatenate([rope_ref[2]] * reps, axis=1)
            half = ROPE_DIM // 2
            y = acc * cos + pltpu.roll(acc, MXU_N - half, 1) * s_up + pltpu.roll(acc, half, 1) * s_dn
            o_ref[:, sl] = y.astype(o_ref.dtype)

        @pl.when(jnp.logical_not(roped))
        def _():
            o_ref[:, sl] = acc.astype(o_ref.dtype)


def _inproj(h2d, g, w_bf, rope_tab, T, tm=1024, tn=768):
    n = h2d.shape[0]
    tpb = T // tm
    return pl.pallas_call(
        functools.partial(_inproj_kernel, tn=tn),
        grid=(n // tm, IN_W // tn),
        in_specs=[
            pl.BlockSpec((tm, D_MODEL), lambda i, j: (i, 0)),
            pl.BlockSpec((1, D_MODEL), lambda i, j: (0, 0)),
            pl.BlockSpec((D_MODEL, tn), lambda i, j: (0, j)),
            pl.BlockSpec((3, tm, LANES), lambda i, j: (0, i % tpb, 0)),
        ],
        out_specs=pl.BlockSpec((tm, tn), lambda i, j: (i, j)),
        out_shape=jax.ShapeDtypeStruct((n, IN_W), BF16),
        scratch_shapes=[pltpu.VMEM((tm, D_MODEL), BF16)],
        compiler_params=_cparams("parallel", "arbitrary"),
        name="inproj",
    )(h2d, g, w_bf, rope_tab)


def _rope_table(T):
    half = ROPE_DIM // 2
    inv = 1.0 / (ROPE_THETA ** (jnp.arange(0, ROPE_DIM, 2, dtype=F32) / ROPE_DIM))
    ang = jnp.arange(T, dtype=F32)[:, None] * inv[None, :]
    cos, sin = jnp.cos(ang), jnp.sin(ang)
    zeros = jnp.zeros((T, HEAD_DIM - ROPE_DIM), F32)
    z8 = jnp.zeros((T, half), F32)
    c64 = jnp.concatenate([cos, cos, zeros + 1.0], axis=1)
    up64 = jnp.concatenate([-sin, z8, zeros], axis=1)
    dn64 = jnp.concatenate([z8, sin, zeros], axis=1)
    reps = LANES // HEAD_DIM
    return jnp.stack([jnp.tile(c64, (1, reps)), jnp.tile(up64, (1, reps)), jnp.tile(dn64, (1, reps))])


def _na_kernel(q_ref, k_ref, v_ref, b_ref, o_ref, *, rows, wr):
    lane = lax.broadcasted_iota(jnp.int32, (GRID_W, LANES), 1)
    lo = lane < HEAD_DIM
    scale = HEAD_DIM ** -0.5

    def row(r, carry):
        r0 = jnp.clip(r - wr // 2, 0, rows - wr)
        d0 = r0 - r + NA_WIN_R - 1 - (NA_WIN_R - wr)
        q = q_ref[pl.ds(pl.multiple_of(r * GRID_W, GRID_W), GRID_W), :] * scale
        ks = pl.multiple_of(r0 * GRID_W, GRID_W)
        kw = k_ref[pl.ds(ks, wr * GRID_W), :]
        vw = v_ref[pl.ds(ks, wr * GRID_W), :]
        outs = []
        for hh in range(2):
            qh = jnp.where(lo if hh == 0 else jnp.logical_not(lo), q, jnp.zeros_like(q))
            s = lax.dot_general(qh, kw, (((1,), (1,)), ((), ())), preferred_element_type=F32)
            s = s + b_ref[hh, d0]
            m = jnp.max(s, axis=-1, keepdims=True)
            p = jnp.exp(s - m)
            l = jnp.sum(p, axis=-1, keepdims=True)
            outs.append(jnp.dot(p.astype(BF16), vw, preferred_element_type=F32) / l)
        o = jnp.where(lo, outs[0], outs[1])
        o_ref[pl.ds(pl.multiple_of(r * GRID_W, GRID_W), GRID_W), :] = o.astype(o_ref.dtype)
        return carry

    lax.fori_loop(0, rows, row, 0)


def _na_bias_table(rpb, wr):
    qc = np.arange(GRID_W)[:, None]
    kc = np.arange(GRID_W)[None, :]
    c0 = np.clip(qc - NA_WIN_C // 2, 0, GRID_W - NA_WIN_C)
    ok = (kc >= c0) & (kc < c0 + NA_WIN_C)
    dc = np.clip(kc - qc + NA_WIN_C - 1, 0, 2 * NA_WIN_C - 2)
    b = rpb.astype(F32)[:, :, dc]
    b = jnp.where(jnp.asarray(ok)[None, None], b, NEG_INF)
    off = NA_WIN_R - wr
    tabs = []
    for d0 in range(wr):
        sl = b[:, off + d0: off + d0 + wr]
        tabs.append(sl.transpose(0, 2, 1, 3).reshape(b.shape[0], GRID_W, wr * GRID_W))
    return jnp.stack(tabs, axis=1)


def _na_attention(proj, bias_tab, B, T):
    rows = T // GRID_W
    wr = min(NA_WIN_R, rows)
    cq, ck, cv = OFF_QA // LANES, OFF_KA // LANES, OFF_VA // LANES
    return pl.pallas_call(
        functools.partial(_na_kernel, rows=rows, wr=wr),
        grid=(B, NA_HEADS // 2),
        in_specs=[
            pl.BlockSpec((T, LANES), lambda b, h: (b, cq + h)),
            pl.BlockSpec((T, LANES), lambda b, h: (b, ck + h)),
            pl.BlockSpec((T, LANES), lambda b, h: (b, cv + h)),
            pl.BlockSpec((2, wr, GRID_W, wr * GRID_W), lambda b, h: (h, 0, 0, 0)),
        ],
        out_specs=pl.BlockSpec((T, LANES), lambda b, h: (b, h)),
        out_shape=jax.ShapeDtypeStruct((B * T, A_W), BF16),
        compiler_params=_cparams("parallel", "arbitrary"),
        name="na_attn",
    )(proj, proj, proj, bias_tab)


def _diff_kernel(lq1_ref, lk1_ref, lq2_ref, lk2_ref, q_ref, k_ref, v_ref, g_ref, o_ref, *, lam_init):
    lam = (jnp.exp(jnp.sum(lq1_ref[...] * lk1_ref[...], keepdims=True))
           - jnp.exp(jnp.sum(lq2_ref[...] * lk2_ref[...], keepdims=True)) + lam_init)
    q = q_ref[...] * (HEAD_DIM ** -0.5)
    lane = lax.broadcasted_iota(jnp.int32, q.shape, 1)
    lo = lane < HEAD_DIM
    k = k_ref[...]
    zero = jnp.zeros_like(q)
    nt = (((1,), (1,)), ((), ()))
    s0 = lax.dot_general(jnp.where(lo, q, zero), k, nt, preferred_element_type=F32)
    p0 = jnp.exp(s0 - jnp.max(s0, axis=-1, keepdims=True))
    r0 = 1.0 / jnp.sum(p0, axis=-1, keepdims=True)
    s1 = lax.dot_general(jnp.where(lo, zero, q), k, nt, preferred_element_type=F32)
    p1 = jnp.exp(s1 - jnp.max(s1, axis=-1, keepdims=True))
    r1 = lam / jnp.sum(p1, axis=-1, keepdims=True)
    a = p0 * r0 - p1 * r1
    o = jnp.dot(a.astype(BF16), v_ref[...], preferred_element_type=F32)
    ms = jnp.mean(o * o, axis=-1, keepdims=True)
    o = o * lax.rsqrt(ms + SUBLN_EPS) * g_ref[...] * (1.0 - lam_init)
    o_ref[...] = o.astype(o_ref.dtype)


def _diff_attention(proj, lq1, lk1, lq2, lk2, subln_g, lam_init, B, T, tq=256):
    cq, ck, cv = OFF_QB // LANES, OFF_KB // LANES, OFF_VB // LANES
    nq = T // tq
    vec = pl.BlockSpec((1, HEAD_DIM), lambda b, h, i: (0, 0))
    return pl.pallas_call(
        functools.partial(_diff_kernel, lam_init=lam_init),
        grid=(B, DIFF_HEADS, nq),
        in_specs=[
            vec, vec, vec, vec,
            pl.BlockSpec((tq, LANES), lambda b, h, i: (b * nq + i, cq + h)),
            pl.BlockSpec((T, LANES), lambda b, h, i: (b, ck + h)),
            pl.BlockSpec((T, LANES), lambda b, h, i: (b, cv + h)),
            pl.BlockSpec((1, LANES), lambda b, h, i: (0, 0)),
        ],
        out_specs=pl.BlockSpec((tq, LANES), lambda b, h, i: (b * nq + i, h)),
        out_shape=jax.ShapeDtypeStruct((B * T, B_W), BF16),
        compiler_params=_cparams("parallel", "parallel", "arbitrary"),
        name="diff_attn",
    )(lq1, lk1, lq2, lk2, proj, proj, proj, subln_g)


DIL_Q = 128


def _dil_kernel(q_ref, k_ref, v_ref, o_ref, lse_ref, *, L, radius):
    kw_len = DIL_Q + 2 * radius
    lane = lax.broadcasted_iota(jnp.int32, (DIL_Q, LANES), 1)
    lo = lane < HEAD_DIM
    rel = (lax.broadcasted_iota(jnp.int32, (DIL_Q, kw_len), 1)
           - lax.broadcasted_iota(jnp.int32, (DIL_Q, kw_len), 0))
    scale = HEAD_DIM ** -0.5

    def blk(i, carry):
        l0 = pl.multiple_of(i * DIL_Q, DIL_Q)
        ks = pl.multiple_of(jnp.clip(l0 - radius, 0, L - kw_len), radius)
        q = q_ref[pl.ds(l0, DIL_Q), :] * scale
        kw = k_ref[pl.ds(ks, kw_len), :]
        vw = v_ref[pl.ds(ks, kw_len), :]
        ok = jnp.abs(rel + (ks - l0)) <= radius
        outs, lses = [], []
        for hh in range(2):
            qh = jnp.where(lo if hh == 0 else jnp.logical_not(lo), q, jnp.zeros_like(q))
            s = lax.dot_general(qh, kw, (((1,), (1,)), ((), ())), preferred_element_type=F32)
            s = jnp.where(ok, s, NEG_INF)
            m = jnp.max(s, axis=-1, keepdims=True)
            p = jnp.exp(s - m)
            l = jnp.sum(p, axis=-1, keepdims=True)
            outs.append(jnp.dot(p.astype(BF16), vw, preferred_element_type=F32) / l)
            lses.append(m + jnp.log(l))
        o_ref[pl.ds(l0, DIL_Q), :] = jnp.where(lo, outs[0], outs[1]).astype(o_ref.dtype)
        lse_ref[pl.ds(l0, DIL_Q), :] = jnp.where(lo, lses[0], lses[1])
        return carry

    lax.fori_loop(0, L // DIL_Q, blk, 0)


def _dilated_group(qg, kg, vg, radius):
    B, dil, L, W = qg.shape
    assert L >= DIL_Q + 2 * radius and L % DIL_Q == 0
    spec = pl.BlockSpec((None, None, L, LANES), lambda b, p, h: (b, p, 0, h))
    return pl.pallas_call(
        functools.partial(_dil_kernel, L=L, radius=radius),
        grid=(B, dil, W // LANES),
        in_specs=[spec, spec, spec],
        out_specs=[spec, spec],
        out_shape=[jax.ShapeDtypeStruct(qg.shape, BF16), jax.ShapeDtypeStruct(qg.shape, F32)],
        compiler_params=_cparams("parallel", "parallel", "arbitrary"),
        name="dil_attn",
    )(qg, kg, vg)


def _dilated_branch(proj, B, T):
    outs, lses = [], []
    gw = C_W // len(DIL_PATTERNS)
    for g, (window, dil) in enumerate(DIL_PATTERNS):
        L = T // dil

        def phase(off):
            t = proj[:, off + g * gw: off + (g + 1) * gw]
            return t.reshape(B, L, dil, gw).transpose(0, 2, 1, 3)

        o, lse = _dilated_group(phase(OFF_QC), phase(OFF_KC), phase(OFF_VC), window // (2 * dil))
        outs.append(o.transpose(0, 2, 1, 3).reshape(B * T, gw))
        lses.append(lse.transpose(0, 2, 1, 3).reshape(B * T, gw))
    return outs, lses


def _merge_kernel(h_ref, ga_ref, gb_ref, gc_ref, ya_ref, yb_ref, o0_ref, o1_ref, o2_ref,
                  l0_ref, l1_ref, l2_ref, wpa_ref, wpb_ref, wpc_ref, wo_ref, out_ref):
    l0, l1, l2 = l0_ref[...], l1_ref[...], l2_ref[...]
    m = jnp.maximum(jnp.maximum(l0, l1), l2)
    e0, e1, e2 = jnp.exp(l0 - m), jnp.exp(l1 - m), jnp.exp(l2 - m)
    yc = (o0_ref[...].astype(F32) * e0 + o1_ref[...].astype(F32) * e1 + o2_ref[...].astype(F32) * e2) / (e0 + e1 + e2)
    merged = jax.nn.sigmoid(ga_ref[...].astype(F32)) * jnp.dot(ya_ref[...], wpa_ref[...], preferred_element_type=F32)
    merged += jax.nn.sigmoid(gb_ref[...].astype(F32)) * jnp.dot(yb_ref[...], wpb_ref[...], preferred_element_type=F32)
    merged += jax.nn.sigmoid(gc_ref[...].astype(F32)) * jnp.dot(yc.astype(BF16), wpc_ref[...],
                                                                preferred_element_type=F32)
    out_ref[...] = h_ref[...] + jnp.dot(merged.astype(BF16), wo_ref[...], preferred_element_type=F32)


def _merge(h2d, proj, ya, yb, outs, lses, wpa, wpb, wpc, wo, tm=512):
    n = h2d.shape[0]
    gw = C_W // len(DIL_PATTERNS)
    row = lambda w: pl.BlockSpec((tm, w), lambda i: (i, 0))
    full = lambda a: pl.BlockSpec(a.shape, lambda i: (0, 0))
    gate = lambda c: pl.BlockSpec((tm, D_MODEL), lambda i: (i, c))
    return pl.pallas_call(
        _merge_kernel,
        grid=(n // tm,),
        in_specs=[row(D_MODEL), gate(0), gate(1), gate(2), row(A_W), row(B_W),
                  row(gw), row(gw), row(gw), row(gw), row(gw), row(gw),
                  full(wpa), full(wpb), full(wpc), full(wo)],
        out_specs=row(D_MODEL),
        out_shape=jax.ShapeDtypeStruct((n, D_MODEL), F32),
        compiler_params=_cparams("parallel"),
        name="merge_outproj",
    )(h2d, proj, proj, proj, ya, yb, *outs, *lses, wpa, wpb, wpc, wo)


def _moe_kernel(h_ref, g_ref, wr_ref, br_ref, w1_ref, w3_ref, w2_ref, out_ref, xn_ref, comb_ref, acc_ref):
    e = pl.program_id(1)
    tm = h_ref.shape[0]
    lane = lax.broadcasted_iota(jnp.int32, (tm, LANES), 1)

    @pl.when(e == 0)
    def _():
        x = h_ref[...]
        ms = jnp.mean(x * x, axis=-1, keepdims=True)
        xn = x * lax.rsqrt(ms + NORM_EPS) * g_ref[...]
        xn_ref[...] = xn.astype(BF16)
        logits = jnp.dot(xn, wr_ref[...], preferred_element_type=F32, precision=lax.Precision.HIGHEST) + br_ref[...]
        lanef = lane.astype(F32)
        big = float(LANES)
        is_g = lane < N_GROUPS
        gl = jnp.where(is_g, logits, -jnp.inf)
        gmax = jnp.max(gl, axis=-1, keepdims=True)
        g_sel = jnp.min(jnp.where(gl == gmax, lanef, big), axis=-1, keepdims=True).astype(jnp.int32)
        g_gate = 1.0 / jnp.sum(jnp.exp(gl - gmax), axis=-1, keepdims=True)
        eidx = lane - N_GROUPS
        in_grp = (eidx >= g_sel * EXPERTS_PER_GROUP) & (eidx < (g_sel + 1) * EXPERTS_PER_GROUP)
        el = jnp.where(in_grp, logits, -jnp.inf)
        t1 = jnp.max(el, axis=-1, keepdims=True)
        i1 = jnp.min(jnp.where(el == t1, lanef, big), axis=-1, keepdims=True).astype(jnp.int32)
        el2 = jnp.where(lane == i1, -jnp.inf, el)
        t2 = jnp.max(el2, axis=-1, keepdims=True)
        i2 = jnp.min(jnp.where(el2 == t2, lanef, big), axis=-1, keepdims=True).astype(jnp.int32)
        x2 = jnp.exp(t2 - t1)
        den = 1.0 + x2
        comb_ref[...] = (jnp.where(lane == i1, g_gate / den, 0.0) + jnp.where(lane == i2, g_gate * x2 / den, 0.0))
        acc_ref[...] = jnp.zeros_like(acc_ref)

    xn = xn_ref[...]
    a = jnp.dot(xn, w1_ref[...], preferred_element_type=F32)
    b = jnp.dot(xn, w3_ref[...], preferred_element_type=F32)
    hmid = (a * jax.nn.sigmoid(a) * b).astype(BF16)
    y = jnp.dot(hmid, w2_ref[...], preferred_element_type=F32)
    ce = jnp.sum(jnp.where(lane == e + N_GROUPS, comb_ref[...], 0.0), axis=-1, keepdims=True)
    acc_ref[...] += ce * y

    @pl.when(e == N_EXPERTS - 1)
    def _():
        out_ref[...] = h_ref[...] + acc_ref[...]


def _moe(h2d, g, w_router, b_router, w1, w3, w2, tm=1024):
    n = h2d.shape[0]
    return pl.pallas_call(
        _moe_kernel,
        grid=(n // tm, N_EXPERTS),
        in_specs=[
            pl.BlockSpec((tm, D_MODEL), lambda i, e: (i, 0)),
            pl.BlockSpec((1, D_MODEL), lambda i, e: (0, 0)),
            pl.BlockSpec((D_MODEL, LANES), lambda i, e: (0, 0)),
            pl.BlockSpec((1, LANES), lambda i, e: (0, 0)),
            pl.BlockSpec((None, D_MODEL, D_FF_EXPERT), lambda i, e: (e, 0, 0)),
            pl.BlockSpec((None, D_MODEL, D_FF_EXPERT), lambda i, e: (e, 0, 0)),
            pl.BlockSpec((None, D_FF_EXPERT, D_MODEL), lambda i, e: (e, 0, 0)),
        ],
        out_specs=pl.BlockSpec((tm, D_MODEL), lambda i, e: (i, 0)),
        out_shape=jax.ShapeDtypeStruct((n, D_MODEL), F32),
        scratch_shapes=[pltpu.VMEM((tm, D_MODEL), BF16), pltpu.VMEM((tm, LANES), F32),
                        pltpu.VMEM((tm, D_MODEL), F32)],
        compiler_params=_cparams("parallel", "arbitrary"),
        name="moe",
    )(h2d, g, w_router, b_router, w1, w3, w2)


def _norm_kernel(x_ref, g_ref, o_ref):
    x = x_ref[...]
    ms = jnp.mean(x * x, axis=-1, keepdims=True)
    o_ref[...] = x * lax.rsqrt(ms + NORM_EPS) * g_ref[...]


def _final_norm(h2d, g, tm=1024):
    n = h2d.shape[0]
    return pl.pallas_call(
        _norm_kernel,
        grid=(n // tm,),
        in_specs=[pl.BlockSpec((tm, D_MODEL), lambda i: (i, 0)), pl.BlockSpec((1, D_MODEL), lambda i: (0, 0))],
        out_specs=pl.BlockSpec((tm, D_MODEL), lambda i: (i, 0)),
        out_shape=jax.ShapeDtypeStruct((n, D_MODEL), F32),
        compiler_params=_cparams("parallel"),
        name="final_norm",
    )(h2d, g)


def _router_params(wg, bg, we, be):
    pad = LANES - N_GROUPS - N_EXPERTS
    w = jnp.concatenate([wg, we, jnp.zeros((D_MODEL, pad), F32)], axis=1).astype(F32)
    b = jnp.concatenate([bg, be, jnp.zeros((pad,), F32)]).astype(F32)[None, :]
    return w, b


def kernel(x, w_in, na_rpb, lam_q1, lam_k1, lam_q2, lam_k2, diff_subln, w_pa, w_pb, w_pc, w_o, norm_mix, norm_ffn,
           router_group_w, router_group_b, router_expert_w, router_expert_b, w1, w3, w2, norm_final):
    B, T, D = x.shape
    depth = w_in.shape[0]
    rows = T // GRID_W
    wr = min(NA_WIN_R, rows)
    rope_tab = _rope_table(T)
    h = x.reshape(B * T, D)
    for l in range(depth):
        lam_init = 0.8 - 0.6 * math.exp(-0.3 * l)
        w_in_bf = jnp.concatenate([w_in[l][:, _REF_GATE_OFF:], w_in[l][:, :_REF_GATE_OFF]], axis=1).astype(BF16)
        proj = _inproj(h, norm_mix[l][None, :], w_in_bf, rope_tab, T)
        ya = _na_attention(proj, _na_bias_table(na_rpb[l], wr), B, T)
        yb = _diff_attention(proj, lam_q1[l][None, :], lam_k1[l][None, :], lam_q2[l][None, :], lam_k2[l][None, :],
                             diff_subln[l][None, :], lam_init, B, T)
        outs, lses = _dilated_branch(proj, B, T)
        h = _merge(h, proj, ya, yb, outs, lses, w_pa[l].astype(BF16), w_pb[l].astype(BF16), w_pc[l].astype(BF16),
                   w_o[l].astype(BF16))
        w_router, b_router = _router_params(router_group_w[l], router_group_b[l], router_expert_w[l],
                                            router_expert_b[l])
        h = _moe(h, norm_ffn[l][None, :], w_router, b_router, w1[l].astype(BF16), w3[l].astype(BF16),
                 w2[l].astype(BF16))
    return _final_norm(h, norm_final[None, :]).reshape(B, T, D)
```

```python
import functools
import math

import jax
import jax.numpy as jnp
import numpy as np
from jax import lax
from jax.experimental import pallas as pl
from jax.experimental.pallas import tpu as pltpu

F32 = jnp.float32
BF16 = jnp.bfloat16

D_MODEL = 1024
HEAD_DIM = 64
ROPE_DIM = 16
ROPE_THETA = 500000.0
GRID_W = 64
NA_HEADS = 8
NA_WIN_R = 8
NA_WIN_C = 16
DIFF_HEADS = 4
DIL_PATTERNS = ((128, 1), (512, 4), (2048, 16))
N_GROUPS = 4
EXPERTS_PER_GROUP = 4
N_EXPERTS = 16
D_FF_EXPERT = D_MODEL // 2
NORM_EPS = 1e-6
SUBLN_EPS = 1e-5
NEG_INF = -1e30

LANES = 128
MXU_N = 256
VMEM_LIMIT = 56 * 1024 * 1024

A_W, B_W, C_W = 512, 512, 768
N_STEPS = len(DIL_PATTERNS)
GROUP_W = C_W // N_STEPS
_MAIN_ORDER = ("qb", "kb", "va", "qa", "ga", "gb", "gc", "ka", "vb")
_REF_ORDER = ("qa", "ka", "va", "qb", "kb", "vb", "qc", "kc", "vc", "ga", "gb", "gc")
_WIDTH = dict(qa=A_W, ka=A_W, va=A_W, qb=B_W, kb=B_W, vb=B_W, qc=C_W, kc=C_W, vc=C_W,
              ga=D_MODEL, gb=D_MODEL, gc=D_MODEL)


def _offsets(order):
    off, out = 0, {}
    for name in order:
        out[name] = off
        off += _WIDTH[name]
    return out, off


_OFF, MAIN_W = _offsets(_MAIN_ORDER)
_REF_OFF, IN_W = _offsets(_REF_ORDER)
OFF_QA, OFF_KA, OFF_VA = _OFF["qa"], _OFF["ka"], _OFF["va"]
OFF_QB, OFF_KB, OFF_VB = _OFF["qb"], _OFF["kb"], _OFF["vb"]
OFF_GATE = _OFF["ga"]
MAIN_STEP_W = MAIN_W // N_STEPS
STEP_W = MAIN_STEP_W + 3 * GROUP_W
MAIN_ROPE_W = 2 * B_W
assert OFF_GATE % D_MODEL == 0 and OFF_QB == 0 and OFF_KB == B_W and MAIN_ROPE_W <= MAIN_STEP_W
assert MAIN_W % N_STEPS == 0 and MAIN_STEP_W % MXU_N == 0 and N_STEPS * STEP_W == IN_W


def _permute_cols(w):
    main = jnp.concatenate([w[:, _REF_OFF[n]:_REF_OFF[n] + _WIDTH[n]] for n in _MAIN_ORDER], axis=1)
    parts = []
    for s in range(N_STEPS):
        parts.append(main[:, s * MAIN_STEP_W:(s + 1) * MAIN_STEP_W])
        for n in ("qc", "kc", "vc"):
            parts.append(w[:, _REF_OFF[n] + s * GROUP_W:_REF_OFF[n] + (s + 1) * GROUP_W])
    return jnp.concatenate(parts, axis=1)


def _cparams(*sem):
    return pltpu.CompilerParams(dimension_semantics=sem, vmem_limit_bytes=VMEM_LIMIT)


def _inproj_kernel(x_ref, g_ref, w_ref, rope_ref, main_ref, c0_ref, c1_ref, c2_ref, xn_ref, y_ref, *, tm):
    j = pl.program_id(1)
    c_refs = (c0_ref, c1_ref, c2_ref)
    reps = MXU_N // LANES
    half = ROPE_DIM // 2
    assert GROUP_W == MXU_N

    def rope(y):
        cos = jnp.concatenate([rope_ref[0]] * reps, axis=1)
        s_up = jnp.concatenate([rope_ref[1]] * reps, axis=1)
        s_dn = jnp.concatenate([rope_ref[2]] * reps, axis=1)
        return y * cos + pltpu.roll(y, MXU_N - half, 1) * s_up + pltpu.roll(y, half, 1) * s_dn

    def step(s):
        xn = xn_ref[...]
        for c in range(MAIN_STEP_W // MXU_N):
            sl = slice(c * MXU_N, (c + 1) * MXU_N)
            y = jnp.dot(xn, w_ref[:, sl], preferred_element_type=F32)
            if s == 0 and c < MAIN_ROPE_W // MXU_N:
                y = rope(y)
            main_ref[:, sl] = y.astype(main_ref.dtype)
        dil = DIL_PATTERNS[s][1]
        for c in range(3):
            wsl = slice(MAIN_STEP_W + c * GROUP_W, MAIN_STEP_W + (c + 1) * GROUP_W)
            osl = slice(c * GROUP_W, (c + 1) * GROUP_W)
            y = jnp.dot(xn, w_ref[:, wsl], preferred_element_type=F32)
            if c < 2:
                y = rope(y)
            if dil == 1:
                c_refs[s][0, :, osl] = y.astype(BF16)
            else:
                for hb in range(reps):
                    y_ref[c, hb] = y[:, hb * LANES:(hb + 1) * LANES]
                for p in range(dil):
                    for hb in range(reps):
                        c_refs[s][p, :, c * GROUP_W + hb * LANES:c * GROUP_W + (hb + 1) * LANES] = (
                            y_ref[c, hb, pl.ds(p, tm // dil, stride=dil), :].astype(BF16))

    @pl.when(j == 0)
    def _():
        x = x_ref[...]
        ms = jnp.mean(x * x, axis=-1, keepdims=True)
        xn_ref[...] = (x * lax.rsqrt(ms + NORM_EPS) * g_ref[...]).astype(BF16)
        step(0)

    for s in range(1, N_STEPS):
        pl.when(j == s)(functools.partial(step, s))


def _inproj(h2d, g, w_bf, rope_tab, B, T, tm=1024):
    n = h2d.shape[0]
    tpb = T // tm
    c_specs, c_shapes = [], []
    for _, dil in DIL_PATTERNS:
        assert tm % dil == 0
        c_specs.append(pl.BlockSpec((None, dil, tm // dil, 3 * GROUP_W), lambda i, j: (i // tpb, 0, i % tpb, 0)))
        c_shapes.append(jax.ShapeDtypeStruct((B, dil, T // dil, 3 * GROUP_W), BF16))
    return pl.pallas_call(
        functools.partial(_inproj_kernel, tm=tm),
        grid=(n // tm, N_STEPS),
        in_specs=[
            pl.BlockSpec((tm, D_MODEL), lambda i, j: (i, 0)),
            pl.BlockSpec((1, D_MODEL), lambda i, j: (0, 0)),
            pl.BlockSpec((D_MODEL, STEP_W), lambda i, j: (0, j)),
            pl.BlockSpec((3, tm, LANES), lambda i, j: (0, i % tpb, 0)),
        ],
        out_specs=[pl.BlockSpec((tm, MAIN_STEP_W), lambda i, j: (i, j))] + c_specs,
        out_shape=[jax.ShapeDtypeStruct((n, MAIN_W), BF16)] + c_shapes,
        scratch_shapes=[pltpu.VMEM((tm, D_MODEL), BF16), pltpu.VMEM((3, MXU_N // LANES, tm, LANES), F32)],
        compiler_params=_cparams("parallel", "arbitrary"),
        name="inproj",
    )(h2d, g, w_bf, rope_tab)


def _rope_table(T):
    half = ROPE_DIM // 2
    inv = 1.0 / (ROPE_THETA ** (jnp.arange(0, ROPE_DIM, 2, dtype=F32) / ROPE_DIM))
    ang = jnp.arange(T, dtype=F32)[:, None] * inv[None, :]
    cos, sin = jnp.cos(ang), jnp.sin(ang)
    zeros = jnp.zeros((T, HEAD_DIM - ROPE_DIM), F32)
    z8 = jnp.zeros((T, half), F32)
    c64 = jnp.concatenate([cos, cos, zeros + 1.0], axis=1)
    up64 = jnp.concatenate([-sin, z8, zeros], axis=1)
    dn64 = jnp.concatenate([z8, sin, zeros], axis=1)
    reps = LANES // HEAD_DIM
    return jnp.stack([jnp.tile(c64, (1, reps)), jnp.tile(up64, (1, reps)), jnp.tile(dn64, (1, reps))])


NA_ROWS_PER_ITER = 4


def _na_kernel(q_ref, k_ref, v_ref, b_ref, o_ref, *, rows, wr):
    lane = lax.broadcasted_iota(jnp.int32, (GRID_W, LANES), 1)
    lo = lane < HEAD_DIM
    scale = HEAD_DIM ** -0.5

    def row_group(gi, carry):
        scores, windows = [], []
        for u in range(NA_ROWS_PER_ITER):
            r = gi * NA_ROWS_PER_ITER + u
            r0 = jnp.clip(r - wr // 2, 0, rows - wr)
            d0 = r0 - r + NA_WIN_R - 1 - (NA_WIN_R - wr)
            q = q_ref[pl.ds(pl.multiple_of(r * GRID_W, GRID_W), GRID_W), :] * scale
            ks = pl.multiple_of(r0 * GRID_W, GRID_W)
            kw = k_ref[pl.ds(ks, wr * GRID_W), :]
            windows.append(ks)
            for hh in range(2):
                qh = jnp.where(lo if hh == 0 else jnp.logical_not(lo), q, jnp.zeros_like(q))
                s = lax.dot_general(qh, kw, (((1,), (1,)), ((), ())), preferred_element_type=F32)
                scores.append(s + b_ref[hh, d0])
        probs, sums = [], []
        for s in scores:
            p = jnp.exp(s - jnp.max(s, axis=-1, keepdims=True))
            sums.append(jnp.sum(p, axis=-1, keepdims=True))
            probs.append(p.astype(BF16))
        for u in range(NA_ROWS_PER_ITER):
            r = gi * NA_ROWS_PER_ITER + u
            vw = v_ref[pl.ds(windows[u], wr * GRID_W), :]
            outs = [jnp.dot(probs[2 * u + hh], vw, preferred_element_type=F32) / sums[2 * u + hh] for hh in range(2)]
            o = jnp.where(lo, outs[0], outs[1])
            o_ref[pl.ds(pl.multiple_of(r * GRID_W, GRID_W), GRID_W), :] = o.astype(o_ref.dtype)
        return carry

    lax.fori_loop(0, rows // NA_ROWS_PER_ITER, row_group, 0)


def _na_bias_table(rpb, wr):
    qc = np.arange(GRID_W)[:, None]
    kc = np.arange(GRID_W)[None, :]
    c0 = np.clip(qc - NA_WIN_C // 2, 0, GRID_W - NA_WIN_C)
    ok = (kc >= c0) & (kc < c0 + NA_WIN_C)
    dc = np.clip(kc - qc + NA_WIN_C - 1, 0, 2 * NA_WIN_C - 2)
    b = rpb.astype(F32)[:, :, dc]
    b = jnp.where(jnp.asarray(ok)[None, None], b, NEG_INF)
    off = NA_WIN_R - wr
    tabs = []
    for d0 in range(wr):
        sl = b[:, off + d0: off + d0 + wr]
        tabs.append(sl.transpose(0, 2, 1, 3).reshape(b.shape[0], GRID_W, wr * GRID_W))
    return jnp.stack(tabs, axis=1)


def _na_attention(proj, bias_tab, B, T):
    rows = T // GRID_W
    wr = min(NA_WIN_R, rows)
    cq, ck, cv = OFF_QA // LANES, OFF_KA // LANES, OFF_VA // LANES
    return pl.pallas_call(
        functools.partial(_na_kernel, rows=rows, wr=wr),
        grid=(B, NA_HEADS // 2),
        in_specs=[
            pl.BlockSpec((T, LANES), lambda b, h: (b, cq + h)),
            pl.BlockSpec((T, LANES), lambda b, h: (b, ck + h)),
            pl.BlockSpec((T, LANES), lambda b, h: (b, cv + h)),
            pl.BlockSpec((2, wr, GRID_W, wr * GRID_W), lambda b, h: (h, 0, 0, 0)),
        ],
        out_specs=pl.BlockSpec((T, LANES), lambda b, h: (b, h)),
        out_shape=jax.ShapeDtypeStruct((B * T, A_W), BF16),
        compiler_params=_cparams("parallel", "arbitrary"),
        name="na_attn",
    )(proj, proj, proj, bias_tab)


def _diff_kernel(lq1_ref, lk1_ref, lq2_ref, lk2_ref, q_ref, k_ref, v_ref, g_ref, o_ref, *, lam_init):
    lam = (jnp.exp(jnp.sum(lq1_ref[...] * lk1_ref[...], keepdims=True))
           - jnp.exp(jnp.sum(lq2_ref[...] * lk2_ref[...], keepdims=True)) + lam_init)
    q = q_ref[...] * (HEAD_DIM ** -0.5)
    lane = lax.broadcasted_iota(jnp.int32, q.shape, 1)
    lo = lane < HEAD_DIM
    k = k_ref[...]
    zero = jnp.zeros_like(q)
    nt = (((1,), (1,)), ((), ()))
    s0 = lax.dot_general(jnp.where(lo, q, zero), k, nt, preferred_element_type=F32)
    p0 = jnp.exp(s0 - jnp.max(s0, axis=-1, keepdims=True))
    r0 = 1.0 / jnp.sum(p0, axis=-1, keepdims=True)
    s1 = lax.dot_general(jnp.where(lo, zero, q), k, nt, preferred_element_type=F32)
    p1 = jnp.exp(s1 - jnp.max(s1, axis=-1, keepdims=True))
    r1 = lam / jnp.sum(p1, axis=-1, keepdims=True)
    a = p0 * r0 - p1 * r1
    o = jnp.dot(a.astype(BF16), v_ref[...], preferred_element_type=F32)
    ms = jnp.mean(o * o, axis=-1, keepdims=True)
    o = o * lax.rsqrt(ms + SUBLN_EPS) * g_ref[...] * (1.0 - lam_init)
    o_ref[...] = o.astype(o_ref.dtype)


def _diff_attention(proj, lq1, lk1, lq2, lk2, subln_g, lam_init, B, T, tq=256):
    cq, ck, cv = OFF_QB // LANES, OFF_KB // LANES, OFF_VB // LANES
    nq = T // tq
    vec = pl.BlockSpec((1, HEAD_DIM), lambda b, h, i: (0, 0))
    return pl.pallas_call(
        functools.partial(_diff_kernel, lam_init=lam_init),
        grid=(B, DIFF_HEADS, nq),
        in_specs=[
            vec, vec, vec, vec,
            pl.BlockSpec((tq, LANES), lambda b, h, i: (b * nq + i, cq + h)),
            pl.BlockSpec((T, LANES), lambda b, h, i: (b, ck + h)),
            pl.BlockSpec((T, LANES), lambda b, h, i: (b, cv + h)),
            pl.BlockSpec((1, LANES), lambda b, h, i: (0, 0)),
        ],
        out_specs=pl.BlockSpec((tq, LANES), lambda b, h, i: (b * nq + i, h)),
        out_shape=jax.ShapeDtypeStruct((B * T, B_W), BF16),
        compiler_params=_cparams("parallel", "parallel", "arbitrary"),
        name="diff_attn",
    )(lq1, lk1, lq2, lk2, proj, proj, proj, subln_g)


DIL_Q = 128


DIL_BLOCKS_PER_ITER = 2


def _dil_kernel(q_ref, k_ref, v_ref, o_ref, lse_ref, *, L, dil, radius):
    kw_len = DIL_Q + 2 * radius
    lane = lax.broadcasted_iota(jnp.int32, (DIL_Q, LANES), 1)
    lo = lane < HEAD_DIM
    rel = (lax.broadcasted_iota(jnp.int32, (DIL_Q, kw_len), 1)
           - lax.broadcasted_iota(jnp.int32, (DIL_Q, kw_len), 0))
    scale = HEAD_DIM ** -0.5
    nt = (((1,), (1,)), ((), ()))

    def blocks(p, gi):
        scores, starts, maxes = [], [], []
        for u in range(DIL_BLOCKS_PER_ITER):
            l0 = pl.multiple_of((gi * DIL_BLOCKS_PER_ITER + u) * DIL_Q, DIL_Q)
            ks = pl.multiple_of(jnp.clip(l0 - radius, 0, L - kw_len), radius)
            q = q_ref[p, pl.ds(l0, DIL_Q), :] * scale
            kw = k_ref[p, pl.ds(ks, kw_len), :]
            ok = jnp.abs(rel + (ks - l0)) <= radius
            starts.append((l0, ks))
            for hh in range(2):
                qh = jnp.where(lo if hh == 0 else jnp.logical_not(lo), q, jnp.zeros_like(q))
                s = lax.dot_general(qh, kw, nt, preferred_element_type=F32)
                scores.append(jnp.where(ok, s, NEG_INF))
        probs, sums = [], []
        for s in scores:
            m = jnp.max(s, axis=-1, keepdims=True)
            e = jnp.exp(s - m)
            maxes.append(m)
            sums.append(jnp.sum(e, axis=-1, keepdims=True))
            probs.append(e.astype(BF16))
        for u in range(DIL_BLOCKS_PER_ITER):
            l0, ks = starts[u]
            vw = v_ref[p, pl.ds(ks, kw_len), :]
            outs = [jnp.dot(probs[2 * u + hh], vw, preferred_element_type=F32) / sums[2 * u + hh] for hh in range(2)]
            lses = [maxes[2 * u + hh] + jnp.log(sums[2 * u + hh]) for hh in range(2)]
            o = jnp.where(lo, outs[0], outs[1])
            lse = jnp.where(lo, lses[0], lses[1])
            if dil == 1:
                o_ref[pl.ds(l0, DIL_Q), :] = o
                lse_ref[pl.ds(l0, DIL_Q), :] = lse
            else:
                o_ref[pl.ds(l0 * dil + p, DIL_Q, stride=dil), :] = o
                lse_ref[pl.ds(l0 * dil + p, DIL_Q, stride=dil), :] = lse

    n_iter = L // (DIL_Q * DIL_BLOCKS_PER_ITER)

    def phase(p, carry):
        lax.fori_loop(0, n_iter, lambda gi, c: (blocks(p, gi), c)[1], 0)
        return carry

    lax.fori_loop(0, dil, phase, 0)


def _dilated_group(cg, radius):
    B, dil, L, _ = cg.shape
    assert L >= DIL_Q + 2 * radius and L % (DIL_Q * DIL_BLOCKS_PER_ITER) == 0
    hp = GROUP_W // LANES
    in_spec = lambda c: pl.BlockSpec((None, dil, L, LANES), lambda b, h: (b, 0, 0, c * hp + h))
    out_spec = pl.BlockSpec((L * dil, LANES), lambda b, h: (b, h))
    out_shape = jax.ShapeDtypeStruct((B * L * dil, GROUP_W), F32)
    return pl.pallas_call(
        functools.partial(_dil_kernel, L=L, dil=dil, radius=radius),
        grid=(B, hp),
        in_specs=[in_spec(0), in_spec(1), in_spec(2)],
        out_specs=[out_spec, out_spec],
        out_shape=[out_shape, out_shape],
        compiler_params=_cparams("parallel", "arbitrary"),
        name="dil_attn",
    )(cg, cg, cg)


def _dilated_branch(cgs):
    outs, lses = [], []
    for cg, (window, dil) in zip(cgs, DIL_PATTERNS):
        o, lse = _dilated_group(cg, window // (2 * dil))
        outs.append(o)
        lses.append(lse)
    return outs, lses


def _merge_kernel(h_ref, ga_ref, gb_ref, gc_ref, ya_ref, yb_ref, o0_ref, o1_ref, o2_ref,
                  l0_ref, l1_ref, l2_ref, wpa_ref, wpb_ref, wpc_ref, wo_ref, out_ref):
    l0, l1, l2 = l0_ref[...], l1_ref[...], l2_ref[...]
    m = jnp.maximum(jnp.maximum(l0, l1), l2)
    e0, e1, e2 = jnp.exp(l0 - m), jnp.exp(l1 - m), jnp.exp(l2 - m)
    yc = (o0_ref[...].astype(F32) * e0 + o1_ref[...].astype(F32) * e1 + o2_ref[...].astype(F32) * e2) / (e0 + e1 + e2)
    merged = jax.nn.sigmoid(ga_ref[...].astype(F32)) * jnp.dot(ya_ref[...], wpa_ref[...], preferred_element_type=F32)
    merged += jax.nn.sigmoid(gb_ref[...].astype(F32)) * jnp.dot(yb_ref[...], wpb_ref[...], preferred_element_type=F32)
    merged += jax.nn.sigmoid(gc_ref[...].astype(F32)) * jnp.dot(yc.astype(BF16), wpc_ref[...],
                                                                preferred_element_type=F32)
    out_ref[...] = h_ref[...] + jnp.dot(merged.astype(BF16), wo_ref[...], preferred_element_type=F32)


def _merge(h2d, proj, ya, yb, outs, lses, wpa, wpb, wpc, wo, tm=512):
    n = h2d.shape[0]
    gw = GROUP_W
    row = lambda w: pl.BlockSpec((tm, w), lambda i: (i, 0))
    full = lambda a: pl.BlockSpec(a.shape, lambda i: (0, 0))
    gate = lambda c: pl.BlockSpec((tm, D_MODEL), lambda i: (i, OFF_GATE // D_MODEL + c))
    return pl.pallas_call(
        _merge_kernel,
        grid=(n // tm,),
        in_specs=[row(D_MODEL), gate(0), gate(1), gate(2), row(A_W), row(B_W),
                  row(gw), row(gw), row(gw), row(gw), row(gw), row(gw),
                  full(wpa), full(wpb), full(wpc), full(wo)],
        out_specs=row(D_MODEL),
        out_shape=jax.ShapeDtypeStruct((n, D_MODEL), F32),
        compiler_params=_cparams("parallel"),
        name="merge_outproj",
    )(h2d, proj, proj, proj, ya, yb, *outs, *lses, wpa, wpb, wpc, wo)


def _moe_kernel(h_ref, g_ref, wr_ref, br_ref, w1_ref, w3_ref, w2_ref, out_ref, xn_ref, comb_ref, acc_ref):
    e = pl.program_id(1)
    tm = h_ref.shape[0]
    lane = lax.broadcasted_iota(jnp.int32, (tm, LANES), 1)

    @pl.when(e == 0)
    def _():
        x = h_ref[...]
        ms = jnp.mean(x * x, axis=-1, keepdims=True)
        xn = x * lax.rsqrt(ms + NORM_EPS) * g_ref[...]
        xn_ref[...] = xn.astype(BF16)
        logits = jnp.dot(xn, wr_ref[...], preferred_element_type=F32, precision=lax.Precision.HIGHEST) + br_ref[...]
        lanef = lane.astype(F32)
        big = float(LANES)
        is_g = lane < N_GROUPS
        gl = jnp.where(is_g, logits, -jnp.inf)
        gmax = jnp.max(gl, axis=-1, keepdims=True)
        g_sel = jnp.min(jnp.where(gl == gmax, lanef, big), axis=-1, keepdims=True).astype(jnp.int32)
        g_gate = 1.0 / jnp.sum(jnp.exp(gl - gmax), axis=-1, keepdims=True)
        eidx = lane - N_GROUPS
        in_grp = (eidx >= g_sel * EXPERTS_PER_GROUP) & (eidx < (g_sel + 1) * EXPERTS_PER_GROUP)
        el = jnp.where(in_grp, logits, -jnp.inf)
        t1 = jnp.max(el, axis=-1, keepdims=True)
        i1 = jnp.min(jnp.where(el == t1, lanef, big), axis=-1, keepdims=True).astype(jnp.int32)
        el2 = jnp.where(lane == i1, -jnp.inf, el)
        t2 = jnp.max(el2, axis=-1, keepdims=True)
        i2 = jnp.min(jnp.where(el2 == t2, lanef, big), axis=-1, keepdims=True).astype(jnp.int32)
        x2 = jnp.exp(t2 - t1)
        den = 1.0 + x2
        comb_ref[...] = (jnp.where(lane == i1, g_gate / den, 0.0) + jnp.where(lane == i2, g_gate * x2 / den, 0.0))
        acc_ref[...] = jnp.zeros_like(acc_ref)

    xn = xn_ref[...]
    a = jnp.dot(xn, w1_ref[...], preferred_element_type=F32)
    b = jnp.dot(xn, w3_ref[...], preferred_element_type=F32)
    hmid = (a * jax.nn.sigmoid(a) * b).astype(BF16)
    y = jnp.dot(hmid, w2_ref[...], preferred_element_type=F32)
    ce = jnp.sum(jnp.where(lane == e + N_GROUPS, comb_ref[...], 0.0), axis=-1, keepdims=True)
    acc_ref[...] += ce * y

    @pl.when(e == N_EXPERTS - 1)
    def _():
        out_ref[...] = h_ref[...] + acc_ref[...]


def _moe(h2d, g, w_router, b_router, w1, w3, w2, tm=1024):
    n = h2d.shape[0]
    return pl.pallas_call(
        _moe_kernel,
        grid=(n // tm, N_EXPERTS),
        in_specs=[
            pl.BlockSpec((tm, D_MODEL), lambda i, e: (i, 0)),
            pl.BlockSpec((1, D_MODEL), lambda i, e: (0, 0)),
            pl.BlockSpec((D_MODEL, LANES), lambda i, e: (0, 0)),
            pl.BlockSpec((1, LANES), lambda i, e: (0, 0)),
            pl.BlockSpec((None, D_MODEL, D_FF_EXPERT), lambda i, e: (e, 0, 0)),
            pl.BlockSpec((None, D_MODEL, D_FF_EXPERT), lambda i, e: (e, 0, 0)),
            pl.BlockSpec((None, D_FF_EXPERT, D_MODEL), lambda i, e: (e, 0, 0)),
        ],
        out_specs=pl.BlockSpec((tm, D_MODEL), lambda i, e: (i, 0)),
        out_shape=jax.ShapeDtypeStruct((n, D_MODEL), F32),
        scratch_shapes=[pltpu.VMEM((tm, D_MODEL), BF16), pltpu.VMEM((tm, LANES), F32),
                        pltpu.VMEM((tm, D_MODEL), F32)],
        compiler_params=_cparams("parallel", "arbitrary"),
        name="moe",
    )(h2d, g, w_router, b_router, w1, w3, w2)


def _norm_kernel(x_ref, g_ref, o_ref):
    x = x_ref[...]
    ms = jnp.mean(x * x, axis=-1, keepdims=True)
    o_ref[...] = x * lax.rsqrt(ms + NORM_EPS) * g_ref[...]


def _final_norm(h2d, g, tm=1024):
    n = h2d.shape[0]
    return pl.pallas_call(
        _norm_kernel,
        grid=(n // tm,),
        in_specs=[pl.BlockSpec((tm, D_MODEL), lambda i: (i, 0)), pl.BlockSpec((1, D_MODEL), lambda i: (0, 0))],
        out_specs=pl.BlockSpec((tm, D_MODEL), lambda i: (i, 0)),
        out_shape=jax.ShapeDtypeStruct((n, D_MODEL), F32),
        compiler_params=_cparams("parallel"),
        name="final_norm",
    )(h2d, g)


def _router_params(wg, bg, we, be):
    pad = LANES - N_GROUPS - N_EXPERTS
    w = jnp.concatenate([wg, we, jnp.zeros((D_MODEL, pad), F32)], axis=1).astype(F32)
    b = jnp.concatenate([bg, be, jnp.zeros((pad,), F32)]).astype(F32)[None, :]
    return w, b


def kernel(x, w_in, na_rpb, lam_q1, lam_k1, lam_q2, lam_k2, diff_subln, w_pa, w_pb, w_pc, w_o, norm_mix, norm_ffn,
           router_group_w, router_group_b, router_expert_w, router_expert_b, w1, w3, w2, norm_final):
    B, T, D = x.shape
    depth = w_in.shape[0]
    rows = T // GRID_W
    wr = min(NA_WIN_R, rows)
    rope_tab = _rope_table(T)
    h = x.reshape(B * T, D)
    for l in range(depth):
        lam_init = 0.8 - 0.6 * math.exp(-0.3 * l)
        w_in_bf = _permute_cols(w_in[l]).astype(BF16)
        proj, *cgs = _inproj(h, norm_mix[l][None, :], w_in_bf, rope_tab, B, T)
        ya = _na_attention(proj, _na_bias_table(na_rpb[l], wr), B, T)
        yb = _diff_attention(proj, lam_q1[l][None, :], lam_k1[l][None, :], lam_q2[l][None, :], lam_k2[l][None, :],
                             diff_subln[l][None, :], lam_init, B, T)
        outs, lses = _dilated_branch(cgs)
        h = _merge(h, proj, ya, yb, outs, lses, w_pa[l].astype(BF16), w_pb[l].astype(BF16), w_pc[l].astype(BF16),
                   w_o[l].astype(BF16))
        w_router, b_router = _router_params(router_group_w[l], router_group_b[l], router_expert_w[l],
                                            router_expert_b[l])
        h = _moe(h, norm_ffn[l][None, :], w_router, b_router, w1[l].astype(BF16), w3[l].astype(BF16),
                 w2[l].astype(BF16))
    return _final_norm(h, norm_final[None, :]).reshape(B, T, D)
```

```python
import functools
import math

import jax
import jax.numpy as jnp
import numpy as np
from jax import lax
from jax.experimental import pallas as pl
from jax.experimental.pallas import tpu as pltpu

F32 = jnp.float32
BF16 = jnp.bfloat16

D_MODEL = 1024
HEAD_DIM = 64
ROPE_DIM = 16
ROPE_THETA = 500000.0
GRID_W = 64
NA_HEADS = 8
NA_WIN_R = 8
NA_WIN_C = 16
DIFF_HEADS = 4
DIL_PATTERNS = ((128, 1), (512, 4), (2048, 16))
N_GROUPS = 4
EXPERTS_PER_GROUP = 4
N_EXPERTS = 16
D_FF_EXPERT = D_MODEL // 2
NORM_EPS = 1e-6
SUBLN_EPS = 1e-5
NEG_INF = -1e30

LANES = 128
MXU_N = 256
VMEM_LIMIT = 56 * 1024 * 1024

A_W, B_W, C_W = 512, 512, 768
N_STEPS = len(DIL_PATTERNS)
GROUP_W = C_W // N_STEPS
_MAIN_ORDER = ("qb", "kb", "va", "qa", "ga", "gb", "gc", "ka", "vb")
_REF_ORDER = ("qa", "ka", "va", "qb", "kb", "vb", "qc", "kc", "vc", "ga", "gb", "gc")
_WIDTH = dict(qa=A_W, ka=A_W, va=A_W, qb=B_W, kb=B_W, vb=B_W, qc=C_W, kc=C_W, vc=C_W,
              ga=D_MODEL, gb=D_MODEL, gc=D_MODEL)


def _offsets(order):
    off, out = 0, {}
    for name in order:
        out[name] = off
        off += _WIDTH[name]
    return out, off


_OFF, MAIN_W = _offsets(_MAIN_ORDER)
_REF_OFF, IN_W = _offsets(_REF_ORDER)
OFF_QA, OFF_KA, OFF_VA = _OFF["qa"], _OFF["ka"], _OFF["va"]
OFF_QB, OFF_KB, OFF_VB = _OFF["qb"], _OFF["kb"], _OFF["vb"]
OFF_GATE = _OFF["ga"]
MAIN_STEP_W = MAIN_W // N_STEPS
STEP_W = MAIN_STEP_W + 3 * GROUP_W
MAIN_ROPE_W = 2 * B_W
QB_SCALE = HEAD_DIM ** -0.5 * math.log2(math.e)
assert OFF_GATE % D_MODEL == 0 and OFF_QB == 0 and OFF_KB == B_W and MAIN_ROPE_W <= MAIN_STEP_W
assert MAIN_W % N_STEPS == 0 and MAIN_STEP_W % MXU_N == 0 and N_STEPS * STEP_W == IN_W


def _permute_cols(w):
    cols = {n: w[:, _REF_OFF[n]:_REF_OFF[n] + _WIDTH[n]] for n in _MAIN_ORDER}
    cols["qb"] = cols["qb"] * QB_SCALE
    main = jnp.concatenate([cols[n] for n in _MAIN_ORDER], axis=1)
    parts = []
    for s in range(N_STEPS):
        parts.append(main[:, s * MAIN_STEP_W:(s + 1) * MAIN_STEP_W])
        for n in ("qc", "kc", "vc"):
            parts.append(w[:, _REF_OFF[n] + s * GROUP_W:_REF_OFF[n] + (s + 1) * GROUP_W])
    return jnp.concatenate(parts, axis=1)


def _cparams(*sem):
    return pltpu.CompilerParams(dimension_semantics=sem, vmem_limit_bytes=VMEM_LIMIT)


def _inproj_kernel(x_ref, g_ref, w_ref, rope_ref, main_ref, c0_ref, c1_ref, c2_ref, xn_ref, y_ref, *, tm):
    j = pl.program_id(1)
    c_refs = (c0_ref, c1_ref, c2_ref)
    reps = MXU_N // LANES
    half = ROPE_DIM // 2
    assert GROUP_W == MXU_N

    def rope(y):
        cos = jnp.concatenate([rope_ref[0]] * reps, axis=1)
        s_up = jnp.concatenate([rope_ref[1]] * reps, axis=1)
        s_dn = jnp.concatenate([rope_ref[2]] * reps, axis=1)
        return y * cos + pltpu.roll(y, MXU_N - half, 1) * s_up + pltpu.roll(y, half, 1) * s_dn

    def step(s):
        xn = xn_ref[...]
        for c in range(MAIN_STEP_W // MXU_N):
            sl = slice(c * MXU_N, (c + 1) * MXU_N)
            y = jnp.dot(xn, w_ref[:, sl], preferred_element_type=F32)
            if s == 0 and c < MAIN_ROPE_W // MXU_N:
                y = rope(y)
            main_ref[:, sl] = y.astype(main_ref.dtype)
        dil = DIL_PATTERNS[s][1]
        for c in range(3):
            wsl = slice(MAIN_STEP_W + c * GROUP_W, MAIN_STEP_W + (c + 1) * GROUP_W)
            osl = slice(c * GROUP_W, (c + 1) * GROUP_W)
            y = jnp.dot(xn, w_ref[:, wsl], preferred_element_type=F32)
            if c < 2:
                y = rope(y)
            if dil == 1:
                c_refs[s][0, :, osl] = y.astype(BF16)
            else:
                for hb in range(reps):
                    y_ref[c, hb] = y[:, hb * LANES:(hb + 1) * LANES]
                for p in range(dil):
                    for hb in range(reps):
                        c_refs[s][p, :, c * GROUP_W + hb * LANES:c * GROUP_W + (hb + 1) * LANES] = (
                            y_ref[c, hb, pl.ds(p, tm // dil, stride=dil), :].astype(BF16))

    @pl.when(j == 0)
    def _():
        x = x_ref[...]
        ms = jnp.mean(x * x, axis=-1, keepdims=True)
        xn_ref[...] = (x * lax.rsqrt(ms + NORM_EPS) * g_ref[...]).astype(BF16)
        step(0)

    for s in range(1, N_STEPS):
        pl.when(j == s)(functools.partial(step, s))


def _inproj(h2d, g, w_bf, rope_tab, B, T, tm=1024):
    n = h2d.shape[0]
    tpb = T // tm
    c_specs, c_shapes = [], []
    for _, dil in DIL_PATTERNS:
        assert tm % dil == 0
        c_specs.append(pl.BlockSpec((None, dil, tm // dil, 3 * GROUP_W), lambda i, j: (i // tpb, 0, i % tpb, 0)))
        c_shapes.append(jax.ShapeDtypeStruct((B, dil, T // dil, 3 * GROUP_W), BF16))
    return pl.pallas_call(
        functools.partial(_inproj_kernel, tm=tm),
        grid=(n // tm, N_STEPS),
        in_specs=[
            pl.BlockSpec((tm, D_MODEL), lambda i, j: (i, 0)),
            pl.BlockSpec((1, D_MODEL), lambda i, j: (0, 0)),
            pl.BlockSpec((D_MODEL, STEP_W), lambda i, j: (0, j)),
            pl.BlockSpec((3, tm, LANES), lambda i, j: (0, i % tpb, 0)),
        ],
        out_specs=[pl.BlockSpec((tm, MAIN_STEP_W), lambda i, j: (i, j))] + c_specs,
        out_shape=[jax.ShapeDtypeStruct((n, MAIN_W), BF16)] + c_shapes,
        scratch_shapes=[pltpu.VMEM((tm, D_MODEL), BF16), pltpu.VMEM((3, MXU_N // LANES, tm, LANES), F32)],
        compiler_params=_cparams("parallel", "arbitrary"),
        name="inproj",
    )(h2d, g, w_bf, rope_tab)


def _rope_table(T):
    half = ROPE_DIM // 2
    inv = 1.0 / (ROPE_THETA ** (jnp.arange(0, ROPE_DIM, 2, dtype=F32) / ROPE_DIM))
    ang = jnp.arange(T, dtype=F32)[:, None] * inv[None, :]
    cos, sin = jnp.cos(ang), jnp.sin(ang)
    zeros = jnp.zeros((T, HEAD_DIM - ROPE_DIM), F32)
    z8 = jnp.zeros((T, half), F32)
    c64 = jnp.concatenate([cos, cos, zeros + 1.0], axis=1)
    up64 = jnp.concatenate([-sin, z8, zeros], axis=1)
    dn64 = jnp.concatenate([z8, sin, zeros], axis=1)
    reps = LANES // HEAD_DIM
    return jnp.stack([jnp.tile(c64, (1, reps)), jnp.tile(up64, (1, reps)), jnp.tile(dn64, (1, reps))])


NA_ROWS_PER_ITER = 4


def _na_kernel(q_ref, k_ref, v_ref, b_ref, o_ref, *, rows, wr):
    lane = lax.broadcasted_iota(jnp.int32, (GRID_W, LANES), 1)
    lo = lane < HEAD_DIM
    scale = HEAD_DIM ** -0.5

    def row_group(gi, carry):
        scores, windows = [], []
        for u in range(NA_ROWS_PER_ITER):
            r = gi * NA_ROWS_PER_ITER + u
            r0 = jnp.clip(r - wr // 2, 0, rows - wr)
            d0 = r0 - r + NA_WIN_R - 1 - (NA_WIN_R - wr)
            q = q_ref[pl.ds(pl.multiple_of(r * GRID_W, GRID_W), GRID_W), :] * scale
            ks = pl.multiple_of(r0 * GRID_W, GRID_W)
            kw = k_ref[pl.ds(ks, wr * GRID_W), :]
            windows.append(ks)
            for hh in range(2):
                qh = jnp.where(lo if hh == 0 else jnp.logical_not(lo), q, jnp.zeros_like(q))
                s = lax.dot_general(qh, kw, (((1,), (1,)), ((), ())), preferred_element_type=F32)
                scores.append(s + b_ref[hh, d0])
        probs, sums = [], []
        for s in scores:
            p = jnp.exp(s - jnp.max(s, axis=-1, keepdims=True))
            sums.append(jnp.sum(p, axis=-1, keepdims=True))
            probs.append(p.astype(BF16))
        for u in range(NA_ROWS_PER_ITER):
            r = gi * NA_ROWS_PER_ITER + u
            vw = v_ref[pl.ds(windows[u], wr * GRID_W), :]
            outs = [jnp.dot(probs[2 * u + hh], vw, preferred_element_type=F32) / sums[2 * u + hh] for hh in range(2)]
            o = jnp.where(lo, outs[0], outs[1])
            o_ref[pl.ds(pl.multiple_of(r * GRID_W, GRID_W), GRID_W), :] = o.astype(o_ref.dtype)
        return carry

    lax.fori_loop(0, rows // NA_ROWS_PER_ITER, row_group, 0)


def _na_bias_table(rpb, wr):
    qc = np.arange(GRID_W)[:, None]
    kc = np.arange(GRID_W)[None, :]
    c0 = np.clip(qc - NA_WIN_C // 2, 0, GRID_W - NA_WIN_C)
    ok = (kc >= c0) & (kc < c0 + NA_WIN_C)
    dc = np.clip(kc - qc + NA_WIN_C - 1, 0, 2 * NA_WIN_C - 2)
    b = rpb.astype(F32)[:, :, dc]
    b = jnp.where(jnp.asarray(ok)[None, None], b, NEG_INF)
    off = NA_WIN_R - wr
    tabs = []
    for d0 in range(wr):
        sl = b[:, off + d0: off + d0 + wr]
        tabs.append(sl.transpose(0, 2, 1, 3).reshape(b.shape[0], GRID_W, wr * GRID_W))
    return jnp.stack(tabs, axis=1)


def _na_attention(proj, bias_tab, B, T):
    rows = T // GRID_W
    wr = min(NA_WIN_R, rows)
    cq, ck, cv = OFF_QA // LANES, OFF_KA // LANES, OFF_VA // LANES
    return pl.pallas_call(
        functools.partial(_na_kernel, rows=rows, wr=wr),
        grid=(B, NA_HEADS // 2),
        in_specs=[
            pl.BlockSpec((T, LANES), lambda b, h: (b, cq + h)),
            pl.BlockSpec((T, LANES), lambda b, h: (b, ck + h)),
            pl.BlockSpec((T, LANES), lambda b, h: (b, cv + h)),
            pl.BlockSpec((2, wr, GRID_W, wr * GRID_W), lambda b, h: (h, 0, 0, 0)),
        ],
        out_specs=pl.BlockSpec((T, LANES), lambda b, h: (b, h)),
        out_shape=jax.ShapeDtypeStruct((B * T, A_W), BF16),
        compiler_params=_cparams("parallel", "arbitrary"),
        name="na_attn",
    )(proj, proj, proj, bias_tab)


def _diff_kernel(lq1_ref, lk1_ref, lq2_ref, lk2_ref, q_ref, k_ref, v_ref, g_ref, o_ref,
                 s00_ref, s01_ref, s10_ref, s11_ref, a0_ref, a1_ref, r0_ref, r1_ref, *, lam_init, tq):
    s_ref = ((s00_ref, s01_ref), (s10_ref, s11_ref))
    a_ref = (a0_ref, a1_ref)
    r_ref = (r0_ref, r1_ref)
    lam = (jnp.exp(jnp.sum(lq1_ref[...] * lk1_ref[...], keepdims=True))
           - jnp.exp(jnp.sum(lq2_ref[...] * lk2_ref[...], keepdims=True)) + lam_init)
    lane = lax.broadcasted_iota(jnp.int32, (tq, LANES), 1)
    lo = lane < HEAD_DIM
    nt = (((1,), (1,)), ((), ()))
    nq = q_ref.shape[0] // tq
    assert nq >= 4 and nq % 2 == 0

    def rows(i):
        return pl.ds(pl.multiple_of(i * tq, tq), tq)

    def scores(i, slot):
        q = q_ref[rows(i), :]
        zero = jnp.zeros_like(q)
        s_ref[slot][0][...] = lax.dot_general(jnp.where(lo, q, zero), k_ref[...], nt, preferred_element_type=F32)
        s_ref[slot][1][...] = lax.dot_general(jnp.where(lo, zero, q), k_ref[...], nt, preferred_element_type=F32)

    def softmax(slot):
        s0 = s_ref[slot][0][...]
        p0 = jnp.exp2(s0 - jnp.max(s0, axis=-1, keepdims=True))
        l0 = jnp.sum(p0, axis=-1, keepdims=True)
        s1 = s_ref[slot][1][...]
        p1 = jnp.exp2(s1 - jnp.max(s1, axis=-1, keepdims=True))
        l1 = jnp.sum(p1, axis=-1, keepdims=True)
        a_ref[slot][...] = (p0 - (lam * l0 / l1) * p1).astype(BF16)
        r_ref[slot][...] = jnp.broadcast_to(1.0 / l0, (tq, LANES))

    def values(i, slot):
        o = jnp.dot(a_ref[slot][...], v_ref[...], preferred_element_type=F32) * r_ref[slot][...]
        ms = jnp.mean(o * o, axis=-1, keepdims=True)
        o = o * lax.rsqrt(ms + SUBLN_EPS) * g_ref[...] * (1.0 - lam_init)
        o_ref[rows(i), :] = o.astype(o_ref.dtype)

    scores(0, 0)
    scores(1, 1)
    softmax(0)

    def pair(j, carry):
        i = 1 + 2 * j
        values(i - 1, 0)
        scores(i + 1, 0)
        softmax(1)
        values(i, 1)
        scores(i + 2, 1)
        softmax(0)
        return carry

    lax.fori_loop(0, (nq - 2) // 2, pair, 0)
    values(nq - 2, 0)
    softmax(1)
    values(nq - 1, 1)


def _diff_attention(proj, lq1, lk1, lq2, lk2, subln_g, lam_init, B, T, tq=256):
    cq, ck, cv = OFF_QB // LANES, OFF_KB // LANES, OFF_VB // LANES
    vec = pl.BlockSpec((1, HEAD_DIM), lambda b, h: (0, 0))
    return pl.pallas_call(
        functools.partial(_diff_kernel, lam_init=lam_init, tq=tq),
        grid=(B, DIFF_HEADS),
        in_specs=[
            vec, vec, vec, vec,
            pl.BlockSpec((T, LANES), lambda b, h: (b, cq + h)),
            pl.BlockSpec((T, LANES), lambda b, h: (b, ck + h)),
            pl.BlockSpec((T, LANES), lambda b, h: (b, cv + h)),
            pl.BlockSpec((1, LANES), lambda b, h: (0, 0)),
        ],
        out_specs=pl.BlockSpec((T, LANES), lambda b, h: (b, h)),
        out_shape=jax.ShapeDtypeStruct((B * T, B_W), BF16),
        scratch_shapes=[pltpu.VMEM((tq, T), F32)] * 4 + [pltpu.VMEM((tq, T), BF16)] * 2
        + [pltpu.VMEM((tq, LANES), F32)] * 2,
        compiler_params=_cparams("parallel", "arbitrary"),
        name="diff_attn",
    )(lq1, lk1, lq2, lk2, proj, proj, proj, subln_g)


DIL_Q = 128


DIL_BLOCKS_PER_ITER = 2


def _dil_kernel(q_ref, k_ref, v_ref, o_ref, lse_ref, *, L, dil, radius):
    kw_len = DIL_Q + 2 * radius
    lane = lax.broadcasted_iota(jnp.int32, (DIL_Q, LANES), 1)
    lo = lane < HEAD_DIM
    rel = (lax.broadcasted_iota(jnp.int32, (DIL_Q, kw_len), 1)
           - lax.broadcasted_iota(jnp.int32, (DIL_Q, kw_len), 0))
    scale = HEAD_DIM ** -0.5
    nt = (((1,), (1,)), ((), ()))

    def blocks(p, gi):
        scores, starts, maxes = [], [], []
        for u in range(DIL_BLOCKS_PER_ITER):
            l0 = pl.multiple_of((gi * DIL_BLOCKS_PER_ITER + u) * DIL_Q, DIL_Q)
            ks = pl.multiple_of(jnp.clip(l0 - radius, 0, L - kw_len), radius)
            q = q_ref[p, pl.ds(l0, DIL_Q), :] * scale
            kw = k_ref[p, pl.ds(ks, kw_len), :]
            ok = jnp.abs(rel + (ks - l0)) <= radius
            starts.append((l0, ks))
            for hh in range(2):
                qh = jnp.where(lo if hh == 0 else jnp.logical_not(lo), q, jnp.zeros_like(q))
                s = lax.dot_general(qh, kw, nt, preferred_element_type=F32)
                scores.append(jnp.where(ok, s, NEG_INF))
        probs, sums = [], []
        for s in scores:
            m = jnp.max(s, axis=-1, keepdims=True)
            e = jnp.exp(s - m)
            maxes.append(m)
            sums.append(jnp.sum(e, axis=-1, keepdims=True))
            probs.append(e.astype(BF16))
        for u in range(DIL_BLOCKS_PER_ITER):
            l0, ks = starts[u]
            vw = v_ref[p, pl.ds(ks, kw_len), :]
            outs = [jnp.dot(probs[2 * u + hh], vw, preferred_element_type=F32) / sums[2 * u + hh] for hh in range(2)]
            lses = [maxes[2 * u + hh] + jnp.log(sums[2 * u + hh]) for hh in range(2)]
            o = jnp.where(lo, outs[0], outs[1])
            lse = jnp.where(lo, lses[0], lses[1])
            if dil == 1:
                o_ref[pl.ds(l0, DIL_Q), :] = o
                lse_ref[pl.ds(l0, DIL_Q), :] = lse
            else:
                o_ref[pl.ds(l0 * dil + p, DIL_Q, stride=dil), :] = o
                lse_ref[pl.ds(l0 * dil + p, DIL_Q, stride=dil), :] = lse

    n_iter = L // (DIL_Q * DIL_BLOCKS_PER_ITER)

    def phase(p, carry):
        lax.fori_loop(0, n_iter, lambda gi, c: (blocks(p, gi), c)[1], 0)
        return carry

    lax.fori_loop(0, dil, phase, 0)


def _dilated_group(cg, radius):
    B, dil, L, _ = cg.shape
    assert L >= DIL_Q + 2 * radius and L % (DIL_Q * DIL_BLOCKS_PER_ITER) == 0
    hp = GROUP_W // LANES
    in_spec = lambda c: pl.BlockSpec((None, dil, L, LANES), lambda b, h: (b, 0, 0, c * hp + h))
    out_spec = pl.BlockSpec((L * dil, LANES), lambda b, h: (b, h))
    out_shape = jax.ShapeDtypeStruct((B * L * dil, GROUP_W), F32)
    return pl.pallas_call(
        functools.partial(_dil_kernel, L=L, dil=dil, radius=radius),
        grid=(B, hp),
        in_specs=[in_spec(0), in_spec(1), in_spec(2)],
        out_specs=[out_spec, out_spec],
        out_shape=[out_shape, out_shape],
        compiler_params=_cparams("parallel", "arbitrary"),
        name="dil_attn",
    )(cg, cg, cg)


def _dilated_branch(cgs):
    outs, lses = [], []
    for cg, (window, dil) in zip(cgs, DIL_PATTERNS):
        o, lse = _dilated_group(cg, window // (2 * dil))
        outs.append(o)
        lses.append(lse)
    return outs, lses


def _merge_kernel(h_ref, ga_ref, gb_ref, gc_ref, ya_ref, yb_ref, o0_ref, o1_ref, o2_ref,
                  l0_ref, l1_ref, l2_ref, wpa_ref, wpb_ref, wpc_ref, wo_ref, out_ref):
    l0, l1, l2 = l0_ref[...], l1_ref[...], l2_ref[...]
    m = jnp.maximum(jnp.maximum(l0, l1), l2)
    e0, e1, e2 = jnp.exp(l0 - m), jnp.exp(l1 - m), jnp.exp(l2 - m)
    yc = (o0_ref[...].astype(F32) * e0 + o1_ref[...].astype(F32) * e1 + o2_ref[...].astype(F32) * e2) / (e0 + e1 + e2)
    merged = jax.nn.sigmoid(ga_ref[...].astype(F32)) * jnp.dot(ya_ref[...], wpa_ref[...], preferred_element_type=F32)
    merged += jax.nn.sigmoid(gb_ref[...].astype(F32)) * jnp.dot(yb_ref[...], wpb_ref[...], preferred_element_type=F32)
    merged += jax.nn.sigmoid(gc_ref[...].astype(F32)) * jnp.dot(yc.astype(BF16), wpc_ref[...],
                                                                preferred_element_type=F32)
    out_ref[...] = h_ref[...] + jnp.dot(merged.astype(BF16), wo_ref[...], preferred_element_type=F32)


def _merge(h2d, proj, ya, yb, outs, lses, wpa, wpb, wpc, wo, tm=512):
    n = h2d.shape[0]
    gw = GROUP_W
    row = lambda w: pl.BlockSpec((tm, w), lambda i: (i, 0))
    full = lambda a: pl.BlockSpec(a.shape, lambda i: (0, 0))
    gate = lambda c: pl.BlockSpec((tm, D_MODEL), lambda i: (i, OFF_GATE // D_MODEL + c))
    return pl.pallas_call(
        _merge_kernel,
        grid=(n // tm,),
        in_specs=[row(D_MODEL), gate(0), gate(1), gate(2), row(A_W), row(B_W),
                  row(gw), row(gw), row(gw), row(gw), row(gw), row(gw),
                  full(wpa), full(wpb), full(wpc), full(wo)],
        out_specs=row(D_MODEL),
        out_shape=jax.ShapeDtypeStruct((n, D_MODEL), F32),
        compiler_params=_cparams("parallel"),
        name="merge_outproj",
    )(h2d, proj, proj, proj, ya, yb, *outs, *lses, wpa, wpb, wpc, wo)


def _moe_kernel(h_ref, g_ref, wr_ref, br_ref, w1_ref, w3_ref, w2_ref, out_ref, xn_ref, comb_ref, acc_ref):
    e = pl.program_id(1)
    tm = h_ref.shape[0]
    lane = lax.broadcasted_iota(jnp.int32, (tm, LANES), 1)

    @pl.when(e == 0)
    def _():
        x = h_ref[...]
        ms = jnp.mean(x * x, axis=-1, keepdims=True)
        xn = x * lax.rsqrt(ms + NORM_EPS) * g_ref[...]
        xn_ref[...] = xn.astype(BF16)
        logits = jnp.dot(xn, wr_ref[...], preferred_element_type=F32, precision=lax.Precision.HIGHEST) + br_ref[...]
        lanef = lane.astype(F32)
        big = float(LANES)
        is_g = lane < N_GROUPS
        gl = jnp.where(is_g, logits, -jnp.inf)
        gmax = jnp.max(gl, axis=-1, keepdims=True)
        g_sel = jnp.min(jnp.where(gl == gmax, lanef, big), axis=-1, keepdims=True).astype(jnp.int32)
        g_gate = 1.0 / jnp.sum(jnp.exp(gl - gmax), axis=-1, keepdims=True)
        eidx = lane - N_GROUPS
        in_grp = (eidx >= g_sel * EXPERTS_PER_GROUP) & (eidx < (g_sel + 1) * EXPERTS_PER_GROUP)
        el = jnp.where(in_grp, logits, -jnp.inf)
        t1 = jnp.max(el, axis=-1, keepdims=True)
        i1 = jnp.min(jnp.where(el == t1, lanef, big), axis=-1, keepdims=True).astype(jnp.int32)
        el2 = jnp.where(lane == i1, -jnp.inf, el)
        t2 = jnp.max(el2, axis=-1, keepdims=True)
        i2 = jnp.min(jnp.where(el2 == t2, lanef, big), axis=-1, keepdims=True).astype(jnp.int32)
        x2 = jnp.exp(t2 - t1)
        den = 1.0 + x2
        comb_ref[...] = (jnp.where(lane == i1, g_gate / den, 0.0) + jnp.where(lane == i2, g_gate * x2 / den, 0.0))
        acc_ref[...] = jnp.zeros_like(acc_ref)

    xn = xn_ref[...]
    a = jnp.dot(xn, w1_ref[...].astype(BF16), preferred_element_type=F32)
    b = jnp.dot(xn, w3_ref[...].astype(BF16), preferred_element_type=F32)
    hmid = (a * jax.nn.sigmoid(a) * b).astype(BF16)
    y = jnp.dot(hmid, w2_ref[...].astype(BF16), preferred_element_type=F32)
    ce = jnp.sum(jnp.where(lane == e + N_GROUPS, comb_ref[...], 0.0), axis=-1, keepdims=True)
    acc_ref[...] += ce * y

    @pl.when(e == N_EXPERTS - 1)
    def _():
        out_ref[...] = h_ref[...] + acc_ref[...]


def _moe(h2d, g, w_router, b_router, w1, w3, w2, layer, tm=1024):
    n = h2d.shape[0]
    return pl.pallas_call(
        _moe_kernel,
        grid=(n // tm, N_EXPERTS),
        in_specs=[
            pl.BlockSpec((tm, D_MODEL), lambda i, e: (i, 0)),
            pl.BlockSpec((1, D_MODEL), lambda i, e: (0, 0)),
            pl.BlockSpec((D_MODEL, LANES), lambda i, e: (0, 0)),
            pl.BlockSpec((1, LANES), lambda i, e: (0, 0)),
            pl.BlockSpec((None, None, D_MODEL, D_FF_EXPERT), lambda i, e: (layer, e, 0, 0)),
            pl.BlockSpec((None, None, D_MODEL, D_FF_EXPERT), lambda i, e: (layer, e, 0, 0)),
            pl.BlockSpec((None, None, D_FF_EXPERT, D_MODEL), lambda i, e: (layer, e, 0, 0)),
        ],
        out_specs=pl.BlockSpec((tm, D_MODEL), lambda i, e: (i, 0)),
        out_shape=jax.ShapeDtypeStruct((n, D_MODEL), F32),
        scratch_shapes=[pltpu.VMEM((tm, D_MODEL), BF16), pltpu.VMEM((tm, LANES), F32),
                        pltpu.VMEM((tm, D_MODEL), F32)],
        compiler_params=_cparams("parallel", "arbitrary"),
        name="moe",
    )(h2d, g, w_router, b_router, w1, w3, w2)


def _norm_kernel(x_ref, g_ref, o_ref):
    x = x_ref[...]
    ms = jnp.mean(x * x, axis=-1, keepdims=True)
    o_ref[...] = x * lax.rsqrt(ms + NORM_EPS) * g_ref[...]


def _final_norm(h2d, g, tm=1024):
    n = h2d.shape[0]
    return pl.pallas_call(
        _norm_kernel,
        grid=(n // tm,),
        in_specs=[pl.BlockSpec((tm, D_MODEL), lambda i: (i, 0)), pl.BlockSpec((1, D_MODEL), lambda i: (0, 0))],
        out_specs=pl.BlockSpec((tm, D_MODEL), lambda i: (i, 0)),
        out_shape=jax.ShapeDtypeStruct((n, D_MODEL), F32),
        compiler_params=_cparams("parallel"),
        name="final_norm",
    )(h2d, g)


def _router_params(wg, bg, we, be):
    pad = LANES - N_GROUPS - N_EXPERTS
    w = jnp.concatenate([wg, we, jnp.zeros((D_MODEL, pad), F32)], axis=1).astype(F32)
    b = jnp.concatenate([bg, be, jnp.zeros((pad,), F32)]).astype(F32)[None, :]
    return w, b


def kernel(x, w_in, na_rpb, lam_q1, lam_k1, lam_q2, lam_k2, diff_subln, w_pa, w_pb, w_pc, w_o, norm_mix, norm_ffn,
           router_group_w, router_group_b, router_expert_w, router_expert_b, w1, w3, w2, norm_final):
    B, T, D = x.shape
    depth = w_in.shape[0]
    rows = T // GRID_W
    wr = min(NA_WIN_R, rows)
    rope_tab = _rope_table(T)
    h = x.reshape(B * T, D)
    for l in range(depth):
        lam_init = 0.8 - 0.6 * math.exp(-0.3 * l)
        w_in_bf = _permute_cols(w_in[l]).astype(BF16)
        proj, *cgs = _inproj(h, norm_mix[l][None, :], w_in_bf, rope_tab, B, T)
        ya = _na_attention(proj, _na_bias_table(na_rpb[l], wr), B, T)
        yb = _diff_attention(proj, lam_q1[l][None, :], lam_k1[l][None, :], lam_q2[l][None, :], lam_k2[l][None, :],
                             diff_subln[l][None, :], lam_init, B, T)
        outs, lses = _dilated_branch(cgs)
        h = _merge(h, proj, ya, yb, outs, lses, w_pa[l].astype(BF16), w_pb[l].astype(BF16), w_pc[l].astype(BF16),
                   w_o[l].astype(BF16))
        w_router, b_router = _router_params(router_group_w[l], router_group_b[l], router_expert_w[l],
                                            router_expert_b[l])
        h = _moe(h, norm_ffn[l][None, :], w_router, b_router, w1, w3, w2, l)
    return _final_norm(h, norm_final[None, :]).reshape(B, T, D)
```

```python
import functools
import math

import jax
import jax.numpy as jnp
import numpy as np
from jax import lax
from jax.experimental import pallas as pl
from jax.experimental.pallas import tpu as pltpu

F32 = jnp.float32
BF16 = jnp.bfloat16

D_MODEL = 1024
HEAD_DIM = 64
ROPE_DIM = 16
ROPE_THETA = 500000.0
GRID_W = 64
NA_HEADS = 8
NA_WIN_R = 8
NA_WIN_C = 16
DIFF_HEADS = 4
DIL_PATTERNS = ((128, 1), (512, 4), (2048, 16))
N_GROUPS = 4
EXPERTS_PER_GROUP = 4
N_EXPERTS = 16
D_FF_EXPERT = D_MODEL // 2
NORM_EPS = 1e-6
SUBLN_EPS = 1e-5
NEG_INF = -1e30

LANES = 128
MXU_N = 256
VMEM_LIMIT = 56 * 1024 * 1024

A_W, B_W, C_W = 512, 512, 768
N_STEPS = len(DIL_PATTERNS)
GROUP_W = C_W // N_STEPS
_MAIN_ORDER = ("qb", "kb", "va", "qa", "ga", "gb", "gc", "ka", "vb")
_REF_ORDER = ("qa", "ka", "va", "qb", "kb", "vb", "qc", "kc", "vc", "ga", "gb", "gc")
_WIDTH = dict(qa=A_W, ka=A_W, va=A_W, qb=B_W, kb=B_W, vb=B_W, qc=C_W, kc=C_W, vc=C_W,
              ga=D_MODEL, gb=D_MODEL, gc=D_MODEL)


def _offsets(order):
    off, out = 0, {}
    for name in order:
        out[name] = off
        off += _WIDTH[name]
    return out, off


_OFF, MAIN_W = _offsets(_MAIN_ORDER)
_REF_OFF, IN_W = _offsets(_REF_ORDER)
OFF_QA, OFF_KA, OFF_VA = _OFF["qa"], _OFF["ka"], _OFF["va"]
OFF_QB, OFF_KB, OFF_VB = _OFF["qb"], _OFF["kb"], _OFF["vb"]
OFF_GATE = _OFF["ga"]
MAIN_STEP_W = MAIN_W // N_STEPS
STEP_W = MAIN_STEP_W + 3 * GROUP_W
MAIN_ROPE_W = 2 * B_W
QB_SCALE = HEAD_DIM ** -0.5 * math.log2(math.e)
assert OFF_GATE % D_MODEL == 0 and OFF_QB == 0 and OFF_KB == B_W and MAIN_ROPE_W <= MAIN_STEP_W
assert MAIN_W % N_STEPS == 0 and MAIN_STEP_W % MXU_N == 0 and N_STEPS * STEP_W == IN_W


def _permute_cols(w):
    cols = {n: w[..., _REF_OFF[n]:_REF_OFF[n] + _WIDTH[n]] for n in _MAIN_ORDER}
    cols["qb"] = cols["qb"] * QB_SCALE
    main = jnp.concatenate([cols[n] for n in _MAIN_ORDER], axis=-1)
    parts = []
    for s in range(N_STEPS):
        parts.append(main[..., s * MAIN_STEP_W:(s + 1) * MAIN_STEP_W])
        for n in ("qc", "kc", "vc"):
            parts.append(w[..., _REF_OFF[n] + s * GROUP_W:_REF_OFF[n] + (s + 1) * GROUP_W])
    return jnp.concatenate(parts, axis=-1)


def _cparams(*sem):
    return pltpu.CompilerParams(dimension_semantics=sem, vmem_limit_bytes=VMEM_LIMIT)


def _inproj_kernel(x_ref, g_ref, w_ref, rope_ref, main_ref, c0_ref, c1_ref, c2_ref, xn_ref, y_ref, *, tm):
    j = pl.program_id(1)
    c_refs = (c0_ref, c1_ref, c2_ref)
    reps = MXU_N // LANES
    half = ROPE_DIM // 2
    assert GROUP_W == MXU_N

    def rope(y):
        cos = jnp.concatenate([rope_ref[0]] * reps, axis=1)
        s_up = jnp.concatenate([rope_ref[1]] * reps, axis=1)
        s_dn = jnp.concatenate([rope_ref[2]] * reps, axis=1)
        return y * cos + pltpu.roll(y, MXU_N - half, 1) * s_up + pltpu.roll(y, half, 1) * s_dn

    def step(s):
        xn = xn_ref[...]
        for c in range(MAIN_STEP_W // MXU_N):
            sl = slice(c * MXU_N, (c + 1) * MXU_N)
            y = jnp.dot(xn, w_ref[:, sl], preferred_element_type=F32)
            if s == 0 and c < MAIN_ROPE_W // MXU_N:
                y = rope(y)
            main_ref[:, sl] = y.astype(main_ref.dtype)
        dil = DIL_PATTERNS[s][1]
        for c in range(3):
            wsl = slice(MAIN_STEP_W + c * GROUP_W, MAIN_STEP_W + (c + 1) * GROUP_W)
            osl = slice(c * GROUP_W, (c + 1) * GROUP_W)
            y = jnp.dot(xn, w_ref[:, wsl], preferred_element_type=F32)
            if c < 2:
                y = rope(y)
            if dil == 1:
                c_refs[s][0, :, osl] = y.astype(BF16)
            else:
                for hb in range(reps):
                    y_ref[c, hb] = y[:, hb * LANES:(hb + 1) * LANES]
                for p in range(dil):
                    for hb in range(reps):
                        c_refs[s][p, :, c * GROUP_W + hb * LANES:c * GROUP_W + (hb + 1) * LANES] = (
                            y_ref[c, hb, pl.ds(p, tm // dil, stride=dil), :].astype(BF16))

    @pl.when(j == 0)
    def _():
        x = x_ref[...]
        ms = jnp.mean(x * x, axis=-1, keepdims=True)
        xn_ref[...] = (x * lax.rsqrt(ms + NORM_EPS) * g_ref[...]).astype(BF16)
        step(0)

    for s in range(1, N_STEPS):
        pl.when(j == s)(functools.partial(step, s))


def _inproj(h2d, g, w_bf, rope_tab, layer, B, T, tm=1024):
    n = h2d.shape[0]
    tpb = T // tm
    c_specs, c_shapes = [], []
    for _, dil in DIL_PATTERNS:
        assert tm % dil == 0
        c_specs.append(pl.BlockSpec((None, dil, tm // dil, 3 * GROUP_W), lambda i, j: (i // tpb, 0, i % tpb, 0)))
        c_shapes.append(jax.ShapeDtypeStruct((B, dil, T // dil, 3 * GROUP_W), BF16))
    return pl.pallas_call(
        functools.partial(_inproj_kernel, tm=tm),
        grid=(n // tm, N_STEPS),
        in_specs=[
            pl.BlockSpec((tm, D_MODEL), lambda i, j: (i, 0)),
            pl.BlockSpec((None, 1, D_MODEL), lambda i, j: (layer, 0, 0)),
            pl.BlockSpec((None, D_MODEL, STEP_W), lambda i, j: (layer, 0, j)),
            pl.BlockSpec((3, tm, LANES), lambda i, j: (0, i % tpb, 0)),
        ],
        out_specs=[pl.BlockSpec((tm, MAIN_STEP_W), lambda i, j: (i, j))] + c_specs,
        out_shape=[jax.ShapeDtypeStruct((n, MAIN_W), BF16)] + c_shapes,
        scratch_shapes=[pltpu.VMEM((tm, D_MODEL), BF16), pltpu.VMEM((3, MXU_N // LANES, tm, LANES), F32)],
        compiler_params=_cparams("parallel", "arbitrary"),
        name="inproj",
    )(h2d, g, w_bf, rope_tab)


def _rope_table(T):
    half = ROPE_DIM // 2
    inv = 1.0 / (ROPE_THETA ** (jnp.arange(0, ROPE_DIM, 2, dtype=F32) / ROPE_DIM))
    ang = jnp.arange(T, dtype=F32)[:, None] * inv[None, :]
    cos, sin = jnp.cos(ang), jnp.sin(ang)
    zeros = jnp.zeros((T, HEAD_DIM - ROPE_DIM), F32)
    z8 = jnp.zeros((T, half), F32)
    c64 = jnp.concatenate([cos, cos, zeros + 1.0], axis=1)
    up64 = jnp.concatenate([-sin, z8, zeros], axis=1)
    dn64 = jnp.concatenate([z8, sin, zeros], axis=1)
    reps = LANES // HEAD_DIM
    return jnp.stack([jnp.tile(c64, (1, reps)), jnp.tile(up64, (1, reps)), jnp.tile(dn64, (1, reps))])


NA_ROWS_PER_ITER = 4


def _na_kernel(q_ref, k_ref, v_ref, b_ref, o_ref, *, rows, wr):
    lane = lax.broadcasted_iota(jnp.int32, (GRID_W, LANES), 1)
    lo = lane < HEAD_DIM
    scale = HEAD_DIM ** -0.5

    def row_group(gi, carry):
        scores, windows = [], []
        for u in range(NA_ROWS_PER_ITER):
            r = gi * NA_ROWS_PER_ITER + u
            r0 = jnp.clip(r - wr // 2, 0, rows - wr)
            d0 = r0 - r + NA_WIN_R - 1 - (NA_WIN_R - wr)
            q = q_ref[pl.ds(pl.multiple_of(r * GRID_W, GRID_W), GRID_W), :] * scale
            ks = pl.multiple_of(r0 * GRID_W, GRID_W)
            kw = k_ref[pl.ds(ks, wr * GRID_W), :]
            windows.append(ks)
            for hh in range(2):
                qh = jnp.where(lo if hh == 0 else jnp.logical_not(lo), q, jnp.zeros_like(q))
                s = lax.dot_general(qh, kw, (((1,), (1,)), ((), ())), preferred_element_type=F32)
                scores.append(s + b_ref[hh, d0])
        probs, sums = [], []
        for s in scores:
            p = jnp.exp(s - jnp.max(s, axis=-1, keepdims=True))
            sums.append(jnp.sum(p, axis=-1, keepdims=True))
            probs.append(p.astype(BF16))
        for u in range(NA_ROWS_PER_ITER):
            r = gi * NA_ROWS_PER_ITER + u
            vw = v_ref[pl.ds(windows[u], wr * GRID_W), :]
            outs = [jnp.dot(probs[2 * u + hh], vw, preferred_element_type=F32) / sums[2 * u + hh] for hh in range(2)]
            o = jnp.where(lo, outs[0], outs[1])
            o_ref[pl.ds(pl.multiple_of(r * GRID_W, GRID_W), GRID_W), :] = o.astype(o_ref.dtype)
        return carry

    lax.fori_loop(0, rows // NA_ROWS_PER_ITER, row_group, 0)


def _na_bias_table(rpb, wr):
    qc = np.arange(GRID_W)[:, None]
    kc = np.arange(GRID_W)[None, :]
    c0 = np.clip(qc - NA_WIN_C // 2, 0, GRID_W - NA_WIN_C)
    ok = (kc >= c0) & (kc < c0 + NA_WIN_C)
    dc = np.clip(kc - qc + NA_WIN_C - 1, 0, 2 * NA_WIN_C - 2)
    onehot = (np.arange(2 * NA_WIN_C - 1)[:, None, None] == dc[None]).astype(np.float32)
    b = jnp.einsum("lhrd,dqk->lhrqk", rpb.astype(F32), onehot, precision=lax.Precision.HIGHEST)
    b = jnp.where(jnp.asarray(ok), b, NEG_INF)
    row_idx = (NA_WIN_R - wr) + np.arange(wr)[:, None] + np.arange(wr)[None, :]
    t = b[:, :, row_idx]
    n_l, n_h = rpb.shape[0], rpb.shape[1]
    return t.transpose(0, 1, 2, 4, 3, 5).reshape(n_l, n_h, wr, GRID_W, wr * GRID_W)


def _na_attention(proj, bias_tab, layer, B, T):
    rows = T // GRID_W
    wr = min(NA_WIN_R, rows)
    cq, ck, cv = OFF_QA // LANES, OFF_KA // LANES, OFF_VA // LANES
    return pl.pallas_call(
        functools.partial(_na_kernel, rows=rows, wr=wr),
        grid=(B, NA_HEADS // 2),
        in_specs=[
            pl.BlockSpec((T, LANES), lambda b, h: (b, cq + h)),
            pl.BlockSpec((T, LANES), lambda b, h: (b, ck + h)),
            pl.BlockSpec((T, LANES), lambda b, h: (b, cv + h)),
            pl.BlockSpec((None, 2, wr, GRID_W, wr * GRID_W), lambda b, h: (layer, h, 0, 0, 0)),
        ],
        out_specs=pl.BlockSpec((T, LANES), lambda b, h: (b, h)),
        out_shape=jax.ShapeDtypeStruct((B * T, A_W), BF16),
        compiler_params=_cparams("parallel", "arbitrary"),
        name="na_attn",
    )(proj, proj, proj, bias_tab)


DIFF_KEY_CHUNK = 512


def _diff_kernel(lq1_ref, lk1_ref, lq2_ref, lk2_ref, q_ref, k_ref, v_ref, g_ref, o_ref, vt_ref, *, lam_init, tq):
    lam = (jnp.exp(jnp.sum(lq1_ref[...] * lk1_ref[...], keepdims=True))
           - jnp.exp(jnp.sum(lq2_ref[...] * lk2_ref[...], keepdims=True)) + lam_init)
    T = k_ref.shape[0]
    ck = DIFF_KEY_CHUNK
    n_chunks = T // ck
    lane = lax.broadcasted_iota(jnp.int32, (tq, LANES), 1)
    lo = lane < HEAD_DIM
    nt = (((1,), (1,)), ((), ()))
    vt_ref[...] = v_ref[...].T

    def q_block(i, carry):
        rows = pl.ds(pl.multiple_of(i * tq, tq), tq)
        q = q_ref[rows, :]
        zero = jnp.zeros_like(q)
        qs = (jnp.where(lo, q, zero), jnp.where(lo, zero, q))

        def scores(c):
            kc = k_ref[c * ck:(c + 1) * ck, :]
            return [lax.dot_general(kc, qm, nt, preferred_element_type=F32) for qm in qs]

        m = [jnp.full((1, tq), NEG_INF, F32)] * 2
        l = [jnp.zeros((1, tq), F32)] * 2
        acc = [jnp.zeros((LANES, tq), F32)] * 2
        s_next = scores(0)
        for c in range(n_chunks):
            s_cur = s_next
            if c + 1 < n_chunks:
                s_next = scores(c + 1)
            vt = vt_ref[:, c * ck:(c + 1) * ck]
            for j in range(2):
                m_new = jnp.maximum(m[j], jnp.max(s_cur[j], axis=0, keepdims=True))
                alpha = jnp.exp2(m[j] - m_new)
                p = jnp.exp2(s_cur[j] - m_new)
                l[j] = alpha * l[j] + jnp.sum(p, axis=0, keepdims=True)
                acc[j] = alpha * acc[j] + jnp.dot(vt, p.astype(BF16), preferred_element_type=F32)
                m[j] = m_new
        ot = acc[0] / l[0] - lam * (acc[1] / l[1])
        o = ot.T
        ms = jnp.mean(o * o, axis=-1, keepdims=True)
        o = o * lax.rsqrt(ms + SUBLN_EPS) * g_ref[...] * (1.0 - lam_init)
        o_ref[rows, :] = o.astype(o_ref.dtype)
        return carry

    lax.fori_loop(0, q_ref.shape[0] // tq, q_block, 0, unroll=2)


def _diff_attention(proj, lq1, lk1, lq2, lk2, subln_g, layer, lam_init, B, T, tq=512):
    cq, ck, cv = OFF_QB // LANES, OFF_KB // LANES, OFF_VB // LANES
    vec = pl.BlockSpec((None, 1, HEAD_DIM), lambda b, h: (layer, 0, 0))
    return pl.pallas_call(
        functools.partial(_diff_kernel, lam_init=lam_init, tq=tq),
        grid=(B, DIFF_HEADS),
        in_specs=[
            vec, vec, vec, vec,
            pl.BlockSpec((T, LANES), lambda b, h: (b, cq + h)),
            pl.BlockSpec((T, LANES), lambda b, h: (b, ck + h)),
            pl.BlockSpec((T, LANES), lambda b, h: (b, cv + h)),
            pl.BlockSpec((None, 1, LANES), lambda b, h: (layer, 0, 0)),
        ],
        out_specs=pl.BlockSpec((T, LANES), lambda b, h: (b, h)),
        out_shape=jax.ShapeDtypeStruct((B * T, B_W), BF16),
        scratch_shapes=[pltpu.VMEM((LANES, T), BF16)],
        compiler_params=_cparams("parallel", "arbitrary"),
        name="diff_attn",
    )(lq1, lk1, lq2, lk2, proj, proj, proj, subln_g)


DIL_Q = 128


DIL_BLOCKS_PER_ITER = 2


def _dil_kernel(q_ref, k_ref, v_ref, o_ref, lse_ref, *, L, dil, radius):
    kw_len = DIL_Q + 2 * radius
    lane = lax.broadcasted_iota(jnp.int32, (DIL_Q, LANES), 1)
    lo = lane < HEAD_DIM
    rel = (lax.broadcasted_iota(jnp.int32, (DIL_Q, kw_len), 1)
           - lax.broadcasted_iota(jnp.int32, (DIL_Q, kw_len), 0))
    scale = HEAD_DIM ** -0.5
    nt = (((1,), (1,)), ((), ()))

    def blocks(p, gi):
        scores, starts, maxes = [], [], []
        for u in range(DIL_BLOCKS_PER_ITER):
            l0 = pl.multiple_of((gi * DIL_BLOCKS_PER_ITER + u) * DIL_Q, DIL_Q)
            ks = pl.multiple_of(jnp.clip(l0 - radius, 0, L - kw_len), radius)
            q = q_ref[p, pl.ds(l0, DIL_Q), :] * scale
            kw = k_ref[p, pl.ds(ks, kw_len), :]
            ok = jnp.abs(rel + (ks - l0)) <= radius
            starts.append((l0, ks))
            for hh in range(2):
                qh = jnp.where(lo if hh == 0 else jnp.logical_not(lo), q, jnp.zeros_like(q))
                s = lax.dot_general(qh, kw, nt, preferred_element_type=F32)
                scores.append(jnp.where(ok, s, NEG_INF))
        probs, sums = [], []
        for s in scores:
            m = jnp.max(s, axis=-1, keepdims=True)
            e = jnp.exp(s - m)
            maxes.append(m)
            sums.append(jnp.sum(e, axis=-1, keepdims=True))
            probs.append(e.astype(BF16))
        for u in range(DIL_BLOCKS_PER_ITER):
            l0, ks = starts[u]
            vw = v_ref[p, pl.ds(ks, kw_len), :]
            outs = [jnp.dot(probs[2 * u + hh], vw, preferred_element_type=F32) / sums[2 * u + hh] for hh in range(2)]
            lses = [maxes[2 * u + hh] + jnp.log(sums[2 * u + hh]) for hh in range(2)]
            o = jnp.where(lo, outs[0], outs[1])
            lse = jnp.where(lo, lses[0], lses[1])
            if dil == 1:
                o_ref[pl.ds(l0, DIL_Q), :] = o
                lse_ref[pl.ds(l0, DIL_Q), :] = lse
            else:
                o_ref[pl.ds(l0 * dil + p, DIL_Q, stride=dil), :] = o
                lse_ref[pl.ds(l0 * dil + p, DIL_Q, stride=dil), :] = lse

    n_iter = L // (DIL_Q * DIL_BLOCKS_PER_ITER)

    def phase(p, carry):
        lax.fori_loop(0, n_iter, lambda gi, c: (blocks(p, gi), c)[1], 0)
        return carry

    lax.fori_loop(0, dil, phase, 0)


def _dilated_group(cg, radius):
    B, dil, L, _ = cg.shape
    assert L >= DIL_Q + 2 * radius and L % (DIL_Q * DIL_BLOCKS_PER_ITER) == 0
    hp = GROUP_W // LANES
    in_spec = lambda c: pl.BlockSpec((None, dil, L, LANES), lambda b, h: (b, 0, 0, c * hp + h))
    out_spec = pl.BlockSpec((L * dil, LANES), lambda b, h: (b, h))
    out_shape = jax.ShapeDtypeStruct((B * L * dil, GROUP_W), F32)
    return pl.pallas_call(
        functools.partial(_dil_kernel, L=L, dil=dil, radius=radius),
        grid=(B, hp),
        in_specs=[in_spec(0), in_spec(1), in_spec(2)],
        out_specs=[out_spec, out_spec],
        out_shape=[out_shape, out_shape],
        compiler_params=_cparams("parallel", "arbitrary"),
        name="dil_attn",
    )(cg, cg, cg)


def _dilated_branch(cgs):
    outs, lses = [], []
    for cg, (window, dil) in zip(cgs, DIL_PATTERNS):
        o, lse = _dilated_group(cg, window // (2 * dil))
        outs.append(o)
        lses.append(lse)
    return outs, lses


def _merge_kernel(h_ref, ga_ref, gb_ref, gc_ref, ya_ref, yb_ref, o0_ref, o1_ref, o2_ref,
                  l0_ref, l1_ref, l2_ref, wpa_ref, wpb_ref, wpc_ref, wo_ref, out_ref):
    l0, l1, l2 = l0_ref[...], l1_ref[...], l2_ref[...]
    m = jnp.maximum(jnp.maximum(l0, l1), l2)
    e0, e1, e2 = jnp.exp(l0 - m), jnp.exp(l1 - m), jnp.exp(l2 - m)
    yc = (o0_ref[...].astype(F32) * e0 + o1_ref[...].astype(F32) * e1 + o2_ref[...].astype(F32) * e2) / (e0 + e1 + e2)
    merged = jax.nn.sigmoid(ga_ref[...].astype(F32)) * jnp.dot(ya_ref[...], wpa_ref[...], preferred_element_type=F32)
    merged += jax.nn.sigmoid(gb_ref[...].astype(F32)) * jnp.dot(yb_ref[...], wpb_ref[...], preferred_element_type=F32)
    merged += jax.nn.sigmoid(gc_ref[...].astype(F32)) * jnp.dot(yc.astype(BF16), wpc_ref[...],
                                                                preferred_element_type=F32)
    out_ref[...] = h_ref[...] + jnp.dot(merged.astype(BF16), wo_ref[...], preferred_element_type=F32)


def _merge(h2d, proj, ya, yb, outs, lses, wpa, wpb, wpc, wo, layer, tm=512):
    n = h2d.shape[0]
    gw = GROUP_W
    row = lambda w: pl.BlockSpec((tm, w), lambda i: (i, 0))
    full = lambda a: pl.BlockSpec((None,) + a.shape[1:], lambda i: (layer, 0, 0))
    gate = lambda c: pl.BlockSpec((tm, D_MODEL), lambda i: (i, OFF_GATE // D_MODEL + c))
    return pl.pallas_call(
        _merge_kernel,
        grid=(n // tm,),
        in_specs=[row(D_MODEL), gate(0), gate(1), gate(2), row(A_W), row(B_W),
                  row(gw), row(gw), row(gw), row(gw), row(gw), row(gw),
                  full(wpa), full(wpb), full(wpc), full(wo)],
        out_specs=row(D_MODEL),
        out_shape=jax.ShapeDtypeStruct((n, D_MODEL), F32),
        compiler_params=_cparams("parallel"),
        name="merge_outproj",
    )(h2d, proj, proj, proj, ya, yb, *outs, *lses, wpa, wpb, wpc, wo)


def _moe_kernel(h_ref, g_ref, wr_ref, br_ref, w1_ref, w3_ref, w2_ref, out_ref, xn_ref, comb_ref, acc_ref):
    e = pl.program_id(1)
    tm = h_ref.shape[0]
    lane = lax.broadcasted_iota(jnp.int32, (tm, LANES), 1)

    @pl.when(e == 0)
    def _():
        x = h_ref[...]
        ms = jnp.mean(x * x, axis=-1, keepdims=True)
        xn = x * lax.rsqrt(ms + NORM_EPS) * g_ref[...]
        xn_ref[...] = xn.astype(BF16)
        logits = jnp.dot(xn, wr_ref[...], preferred_element_type=F32, precision=lax.Precision.HIGHEST) + br_ref[...]
        lanef = lane.astype(F32)
        big = float(LANES)
        is_g = lane < N_GROUPS
        gl = jnp.where(is_g, logits, -jnp.inf)
        gmax = jnp.max(gl, axis=-1, keepdims=True)
        g_sel = jnp.min(jnp.where(gl == gmax, lanef, big), axis=-1, keepdims=True).astype(jnp.int32)
        g_gate = 1.0 / jnp.sum(jnp.exp(gl - gmax), axis=-1, keepdims=True)
        eidx = lane - N_GROUPS
        in_grp = (eidx >= g_sel * EXPERTS_PER_GROUP) & (eidx < (g_sel + 1) * EXPERTS_PER_GROUP)
        el = jnp.where(in_grp, logits, -jnp.inf)
        t1 = jnp.max(el, axis=-1, keepdims=True)
        i1 = jnp.min(jnp.where(el == t1, lanef, big), axis=-1, keepdims=True).astype(jnp.int32)
        el2 = jnp.where(lane == i1, -jnp.inf, el)
        t2 = jnp.max(el2, axis=-1, keepdims=True)
        i2 = jnp.min(jnp.where(el2 == t2, lanef, big), axis=-1, keepdims=True).astype(jnp.int32)
        x2 = jnp.exp(t2 - t1)
        den = 1.0 + x2
        comb_ref[...] = (jnp.where(lane == i1, g_gate / den, 0.0) + jnp.where(lane == i2, g_gate * x2 / den, 0.0))
        acc_ref[...] = jnp.zeros_like(acc_ref)

    xn = xn_ref[...]
    a = jnp.dot(xn, w1_ref[...].astype(BF16), preferred_element_type=F32)
    b = jnp.dot(xn, w3_ref[...].astype(BF16), preferred_element_type=F32)
    hmid = (a * jax.nn.sigmoid(a) * b).astype(BF16)
    y = jnp.dot(hmid, w2_ref[...].astype(BF16), preferred_element_type=F32)
    ce = jnp.sum(jnp.where(lane == e + N_GROUPS, comb_ref[...], 0.0), axis=-1, keepdims=True)
    acc_ref[...] += ce * y

    @pl.when(e == N_EXPERTS - 1)
    def _():
        out_ref[...] = h_ref[...] + acc_ref[...]


def _moe(h2d, g, w_router, b_router, w1, w3, w2, layer, tm=1024):
    n = h2d.shape[0]
    return pl.pallas_call(
        _moe_kernel,
        grid=(n // tm, N_EXPERTS),
        in_specs=[
            pl.BlockSpec((tm, D_MODEL), lambda i, e: (i, 0)),
            pl.BlockSpec((None, 1, D_MODEL), lambda i, e: (layer, 0, 0)),
            pl.BlockSpec((None, D_MODEL, LANES), lambda i, e: (layer, 0, 0)),
            pl.BlockSpec((None, 1, LANES), lambda i, e: (layer, 0, 0)),
            pl.BlockSpec((None, None, D_MODEL, D_FF_EXPERT), lambda i, e: (layer, e, 0, 0)),
            pl.BlockSpec((None, None, D_MODEL, D_FF_EXPERT), lambda i, e: (layer, e, 0, 0)),
            pl.BlockSpec((None, None, D_FF_EXPERT, D_MODEL), lambda i, e: (layer, e, 0, 0)),
        ],
        out_specs=pl.BlockSpec((tm, D_MODEL), lambda i, e: (i, 0)),
        out_shape=jax.ShapeDtypeStruct((n, D_MODEL), F32),
        scratch_shapes=[pltpu.VMEM((tm, D_MODEL), BF16), pltpu.VMEM((tm, LANES), F32),
                        pltpu.VMEM((tm, D_MODEL), F32)],
        compiler_params=_cparams("parallel", "arbitrary"),
        name="moe",
    )(h2d, g, w_router, b_router, w1, w3, w2)


def _norm_kernel(x_ref, g_ref, o_ref):
    x = x_ref[...]
    ms = jnp.mean(x * x, axis=-1, keepdims=True)
    o_ref[...] = x * lax.rsqrt(ms + NORM_EPS) * g_ref[...]


def _final_norm(h2d, g, tm=1024):
    n = h2d.shape[0]
    return pl.pallas_call(
        _norm_kernel,
        grid=(n // tm,),
        in_specs=[pl.BlockSpec((tm, D_MODEL), lambda i: (i, 0)), pl.BlockSpec((1, D_MODEL), lambda i: (0, 0))],
        out_specs=pl.BlockSpec((tm, D_MODEL), lambda i: (i, 0)),
        out_shape=jax.ShapeDtypeStruct((n, D_MODEL), F32),
        compiler_params=_cparams("parallel"),
        name="final_norm",
    )(h2d, g)


def _router_params(wg, bg, we, be):
    n_l = wg.shape[0]
    pad = LANES - N_GROUPS - N_EXPERTS
    w = jnp.concatenate([wg, we, jnp.zeros((n_l, D_MODEL, pad), F32)], axis=-1).astype(F32)
    b = jnp.concatenate([bg, be, jnp.zeros((n_l, pad), F32)], axis=-1).astype(F32)[:, None, :]
    return w, b


def kernel(x, w_in, na_rpb, lam_q1, lam_k1, lam_q2, lam_k2, diff_subln, w_pa, w_pb, w_pc, w_o, norm_mix, norm_ffn,
           router_group_w, router_group_b, router_expert_w, router_expert_b, w1, w3, w2, norm_final):
    B, T, D = x.shape
    depth = w_in.shape[0]
    rows = T // GRID_W
    wr = min(NA_WIN_R, rows)
    rope_tab = _rope_table(T)
    row3 = lambda a: a[:, None, :]
    w_in_bf = _permute_cols(w_in).astype(BF16)
    bias_tab = _na_bias_table(na_rpb, wr)
    g_mix, g_ffn = row3(norm_mix), row3(norm_ffn)
    lams = [row3(a) for a in (lam_q1, lam_k1, lam_q2, lam_k2)]
    subln = row3(diff_subln)
    wpa, wpb, wpc, wo = (a.astype(BF16) for a in (w_pa, w_pb, w_pc, w_o))
    w_router, b_router = _router_params(router_group_w, router_group_b, router_expert_w, router_expert_b)
    h = x.reshape(B * T, D)
    for l in range(depth):
        lam_init = 0.8 - 0.6 * math.exp(-0.3 * l)
        proj, *cgs = _inproj(h, g_mix, w_in_bf, rope_tab, l, B, T)
        ya = _na_attention(proj, bias_tab, l, B, T)
        yb = _diff_attention(proj, *lams, subln, l, lam_init, B, T)
        outs, lses = _dilated_branch(cgs)
        h = _merge(h, proj, ya, yb, outs, lses, wpa, wpb, wpc, wo, l)
        h = _moe(h, g_ffn, w_router, b_router, w1, w3, w2, l)
    return _final_norm(h, norm_final[None, :]).reshape(B, T, D)
```

```python
import functools
import math

import jax
import jax.numpy as jnp
import numpy as np
from jax import lax
from jax.experimental import pallas as pl
from jax.experimental.pallas import tpu as pltpu

F32 = jnp.float32
BF16 = jnp.bfloat16

D_MODEL = 1024
HEAD_DIM = 64
ROPE_DIM = 16
ROPE_THETA = 500000.0
GRID_W = 64
NA_HEADS = 8
NA_WIN_R = 8
NA_WIN_C = 16
DIFF_HEADS = 4
DIL_PATTERNS = ((128, 1), (512, 4), (2048, 16))
N_GROUPS = 4
EXPERTS_PER_GROUP = 4
N_EXPERTS = 16
D_FF_EXPERT = D_MODEL // 2
NORM_EPS = 1e-6
SUBLN_EPS = 1e-5
NEG_INF = -1e30

LANES = 128
MXU_N = 256
VMEM_LIMIT = 56 * 1024 * 1024

A_W, B_W, C_W = 512, 512, 768
N_STEPS = len(DIL_PATTERNS)
GROUP_W = C_W // N_STEPS
_MAIN_ORDER = ("qb", "kb", "va", "qa", "ga", "gb", "gc", "ka", "vb")
_REF_ORDER = ("qa", "ka", "va", "qb", "kb", "vb", "qc", "kc", "vc", "ga", "gb", "gc")
_WIDTH = dict(qa=A_W, ka=A_W, va=A_W, qb=B_W, kb=B_W, vb=B_W, qc=C_W, kc=C_W, vc=C_W,
              ga=D_MODEL, gb=D_MODEL, gc=D_MODEL)


def _offsets(order):
    off, out = 0, {}
    for name in order:
        out[name] = off
        off += _WIDTH[name]
    return out, off


_OFF, MAIN_W = _offsets(_MAIN_ORDER)
_REF_OFF, IN_W = _offsets(_REF_ORDER)
OFF_QA, OFF_KA, OFF_VA = _OFF["qa"], _OFF["ka"], _OFF["va"]
OFF_QB, OFF_KB, OFF_VB = _OFF["qb"], _OFF["kb"], _OFF["vb"]
OFF_GATE = _OFF["ga"]
MAIN_STEP_W = MAIN_W // N_STEPS
STEP_W = MAIN_STEP_W + 3 * GROUP_W
MAIN_ROPE_W = 2 * B_W
QB_SCALE = HEAD_DIM ** -0.5 * math.log2(math.e)
assert OFF_GATE % D_MODEL == 0 and OFF_QB == 0 and OFF_KB == B_W and MAIN_ROPE_W <= MAIN_STEP_W
assert MAIN_W % N_STEPS == 0 and MAIN_STEP_W % MXU_N == 0 and N_STEPS * STEP_W == IN_W


def _permute_cols(w):
    cols = {n: w[..., _REF_OFF[n]:_REF_OFF[n] + _WIDTH[n]] for n in _MAIN_ORDER}
    cols["qb"] = cols["qb"] * QB_SCALE
    main = jnp.concatenate([cols[n] for n in _MAIN_ORDER], axis=-1)
    parts = []
    for s in range(N_STEPS):
        parts.append(main[..., s * MAIN_STEP_W:(s + 1) * MAIN_STEP_W])
        for n in ("qc", "kc", "vc"):
            parts.append(w[..., _REF_OFF[n] + s * GROUP_W:_REF_OFF[n] + (s + 1) * GROUP_W])
    return jnp.concatenate(parts, axis=-1)


def _cparams(*sem):
    return pltpu.CompilerParams(dimension_semantics=sem, vmem_limit_bytes=VMEM_LIMIT)


def _inproj_kernel(x_ref, g_ref, w_ref, rope_ref, main_ref, c0_ref, c1_ref, c2_ref, xn_ref, y_ref, *, tm):
    j = pl.program_id(1)
    c_refs = (c0_ref, c1_ref, c2_ref)
    reps = MXU_N // LANES
    half = ROPE_DIM // 2
    assert GROUP_W == MXU_N

    def rope(y):
        cos = jnp.concatenate([rope_ref[0]] * reps, axis=1)
        s_up = jnp.concatenate([rope_ref[1]] * reps, axis=1)
        s_dn = jnp.concatenate([rope_ref[2]] * reps, axis=1)
        return y * cos + pltpu.roll(y, MXU_N - half, 1) * s_up + pltpu.roll(y, half, 1) * s_dn

    def step(s):
        xn = xn_ref[...]
        for c in range(MAIN_STEP_W // MXU_N):
            sl = slice(c * MXU_N, (c + 1) * MXU_N)
            y = jnp.dot(xn, w_ref[:, sl], preferred_element_type=F32)
            if s == 0 and c < MAIN_ROPE_W // MXU_N:
                y = rope(y)
            main_ref[:, sl] = y.astype(main_ref.dtype)
        dil = DIL_PATTERNS[s][1]
        for c in range(3):
            wsl = slice(MAIN_STEP_W + c * GROUP_W, MAIN_STEP_W + (c + 1) * GROUP_W)
            osl = slice(c * GROUP_W, (c + 1) * GROUP_W)
            y = jnp.dot(xn, w_ref[:, wsl], preferred_element_type=F32)
            if c < 2:
                y = rope(y)
            if dil == 1:
                c_refs[s][0, :, osl] = y.astype(BF16)
            else:
                for hb in range(reps):
                    y_ref[c, hb] = y[:, hb * LANES:(hb + 1) * LANES]
                for p in range(dil):
                    for hb in range(reps):
                        c_refs[s][p, :, c * GROUP_W + hb * LANES:c * GROUP_W + (hb + 1) * LANES] = (
                            y_ref[c, hb, pl.ds(p, tm // dil, stride=dil), :].astype(BF16))

    @pl.when(j == 0)
    def _():
        x = x_ref[...]
        ms = jnp.mean(x * x, axis=-1, keepdims=True)
        xn_ref[...] = (x * lax.rsqrt(ms + NORM_EPS) * g_ref[...]).astype(BF16)
        step(0)

    for s in range(1, N_STEPS):
        pl.when(j == s)(functools.partial(step, s))


def _inproj(h2d, g, w_bf, rope_tab, layer, B, T, tm=1024):
    n = h2d.shape[0]
    tpb = T // tm
    c_specs, c_shapes = [], []
    for _, dil in DIL_PATTERNS:
        assert tm % dil == 0
        c_specs.append(pl.BlockSpec((None, dil, tm // dil, 3 * GROUP_W), lambda i, j: (i // tpb, 0, i % tpb, 0)))
        c_shapes.append(jax.ShapeDtypeStruct((B, dil, T // dil, 3 * GROUP_W), BF16))
    return pl.pallas_call(
        functools.partial(_inproj_kernel, tm=tm),
        grid=(n // tm, N_STEPS),
        in_specs=[
            pl.BlockSpec((tm, D_MODEL), lambda i, j: (i, 0)),
            pl.BlockSpec((None, 1, D_MODEL), lambda i, j: (layer, 0, 0)),
            pl.BlockSpec((None, D_MODEL, STEP_W), lambda i, j: (layer, 0, j)),
            pl.BlockSpec((3, tm, LANES), lambda i, j: (0, i % tpb, 0)),
        ],
        out_specs=[pl.BlockSpec((tm, MAIN_STEP_W), lambda i, j: (i, j))] + c_specs,
        out_shape=[jax.ShapeDtypeStruct((n, MAIN_W), BF16)] + c_shapes,
        scratch_shapes=[pltpu.VMEM((tm, D_MODEL), BF16), pltpu.VMEM((3, MXU_N // LANES, tm, LANES), F32)],
        compiler_params=_cparams("parallel", "arbitrary"),
        name="inproj",
    )(h2d, g, w_bf, rope_tab)


def _rope_table(T):
    half = ROPE_DIM // 2
    inv = 1.0 / (ROPE_THETA ** (jnp.arange(0, ROPE_DIM, 2, dtype=F32) / ROPE_DIM))
    ang = jnp.arange(T, dtype=F32)[:, None] * inv[None, :]
    cos, sin = jnp.cos(ang), jnp.sin(ang)
    zeros = jnp.zeros((T, HEAD_DIM - ROPE_DIM), F32)
    z8 = jnp.zeros((T, half), F32)
    c64 = jnp.concatenate([cos, cos, zeros + 1.0], axis=1)
    up64 = jnp.concatenate([-sin, z8, zeros], axis=1)
    dn64 = jnp.concatenate([z8, sin, zeros], axis=1)
    reps = LANES // HEAD_DIM
    return jnp.stack([jnp.tile(c64, (1, reps)), jnp.tile(up64, (1, reps)), jnp.tile(dn64, (1, reps))])


NA_ROWS_PER_ITER = 4


def _na_kernel(q_ref, k_ref, v_ref, b_ref, o_ref, *, rows, wr):
    lane = lax.broadcasted_iota(jnp.int32, (GRID_W, LANES), 1)
    lo = lane < HEAD_DIM
    scale = HEAD_DIM ** -0.5

    def row_group(gi, carry):
        scores, windows = [], []
        for u in range(NA_ROWS_PER_ITER):
            r = gi * NA_ROWS_PER_ITER + u
            r0 = jnp.clip(r - wr // 2, 0, rows - wr)
            d0 = r0 - r + NA_WIN_R - 1 - (NA_WIN_R - wr)
            q = q_ref[pl.ds(pl.multiple_of(r * GRID_W, GRID_W), GRID_W), :] * scale
            ks = pl.multiple_of(r0 * GRID_W, GRID_W)
            kw = k_ref[pl.ds(ks, wr * GRID_W), :]
            windows.append(ks)
            for hh in range(2):
                qh = jnp.where(lo if hh == 0 else jnp.logical_not(lo), q, jnp.zeros_like(q))
                s = lax.dot_general(qh, kw, (((1,), (1,)), ((), ())), preferred_element_type=F32)
                scores.append(s + b_ref[hh, d0])
        probs, sums = [], []
        for s in scores:
            p = jnp.exp(s - jnp.max(s, axis=-1, keepdims=True))
            sums.append(jnp.sum(p, axis=-1, keepdims=True))
            probs.append(p.astype(BF16))
        for u in range(NA_ROWS_PER_ITER):
            r = gi * NA_ROWS_PER_ITER + u
            vw = v_ref[pl.ds(windows[u], wr * GRID_W), :]
            outs = [jnp.dot(probs[2 * u + hh], vw, preferred_element_type=F32) / sums[2 * u + hh] for hh in range(2)]
            o = jnp.where(lo, outs[0], outs[1])
            o_ref[pl.ds(pl.multiple_of(r * GRID_W, GRID_W), GRID_W), :] = o.astype(o_ref.dtype)
        return carry

    lax.fori_loop(0, rows // NA_ROWS_PER_ITER, row_group, 0)


def _na_bias_table(rpb, wr):
    qc = np.arange(GRID_W)[:, None]
    kc = np.arange(GRID_W)[None, :]
    c0 = np.clip(qc - NA_WIN_C // 2, 0, GRID_W - NA_WIN_C)
    ok = (kc >= c0) & (kc < c0 + NA_WIN_C)
    dc = np.clip(kc - qc + NA_WIN_C - 1, 0, 2 * NA_WIN_C - 2)
    onehot = (np.arange(2 * NA_WIN_C - 1)[:, None, None] == dc[None]).astype(np.float32)
    b = jnp.einsum("lhrd,dqk->lhrqk", rpb.astype(F32), onehot, precision=lax.Precision.HIGHEST)
    b = jnp.where(jnp.asarray(ok), b, NEG_INF)
    row_idx = (NA_WIN_R - wr) + np.arange(wr)[:, None] + np.arange(wr)[None, :]
    t = b[:, :, row_idx]
    n_l, n_h = rpb.shape[0], rpb.shape[1]
    return t.transpose(0, 1, 2, 4, 3, 5).reshape(n_l, n_h, wr, GRID_W, wr * GRID_W)


def _na_attention(proj, bias_tab, layer, B, T):
    rows = T // GRID_W
    wr = min(NA_WIN_R, rows)
    cq, ck, cv = OFF_QA // LANES, OFF_KA // LANES, OFF_VA // LANES
    return pl.pallas_call(
        functools.partial(_na_kernel, rows=rows, wr=wr),
        grid=(B, NA_HEADS // 2),
        in_specs=[
            pl.BlockSpec((T, LANES), lambda b, h: (b, cq + h)),
            pl.BlockSpec((T, LANES), lambda b, h: (b, ck + h)),
            pl.BlockSpec((T, LANES), lambda b, h: (b, cv + h)),
            pl.BlockSpec((None, 2, wr, GRID_W, wr * GRID_W), lambda b, h: (layer, h, 0, 0, 0)),
        ],
        out_specs=pl.BlockSpec((T, LANES), lambda b, h: (b, h)),
        out_shape=jax.ShapeDtypeStruct((B * T, A_W), BF16),
        compiler_params=_cparams("parallel", "arbitrary"),
        name="na_attn",
    )(proj, proj, proj, bias_tab)


DIFF_KEY_CHUNK = 512


def _diff_kernel(lq1_ref, lk1_ref, lq2_ref, lk2_ref, q_ref, k_ref, v_ref, g_ref, o_ref, vt_ref, *, lam_init, tq):
    lam = (jnp.exp(jnp.sum(lq1_ref[...] * lk1_ref[...], keepdims=True))
           - jnp.exp(jnp.sum(lq2_ref[...] * lk2_ref[...], keepdims=True)) + lam_init)
    T = k_ref.shape[0]
    ck = DIFF_KEY_CHUNK
    n_chunks = T // ck
    lane = lax.broadcasted_iota(jnp.int32, (tq, LANES), 1)
    lo = lane < HEAD_DIM
    nt = (((1,), (1,)), ((), ()))
    vt_ref[...] = v_ref[...].T

    def q_block(i, carry):
        rows = pl.ds(pl.multiple_of(i * tq, tq), tq)
        q = q_ref[rows, :]
        zero = jnp.zeros_like(q)
        qs = (jnp.where(lo, q, zero), jnp.where(lo, zero, q))

        def scores(c):
            kc = k_ref[c * ck:(c + 1) * ck, :]
            return [lax.dot_general(kc, qm, nt, preferred_element_type=F32) for qm in qs]

        m = [jnp.full((1, tq), NEG_INF, F32)] * 2
        l = [jnp.zeros((1, tq), F32)] * 2
        acc = [jnp.zeros((LANES, tq), F32)] * 2
        s_next = scores(0)
        for c in range(n_chunks):
            s_cur = s_next
            if c + 1 < n_chunks:
                s_next = scores(c + 1)
            vt = vt_ref[:, c * ck:(c + 1) * ck]
            for j in range(2):
                m_new = jnp.maximum(m[j], jnp.max(s_cur[j], axis=0, keepdims=True))
                alpha = jnp.exp2(m[j] - m_new)
                p = jnp.exp2(s_cur[j] - m_new)
                l[j] = alpha * l[j] + jnp.sum(p, axis=0, keepdims=True)
                acc[j] = alpha * acc[j] + jnp.dot(vt, p.astype(BF16), preferred_element_type=F32)
                m[j] = m_new
        ot = acc[0] / l[0] - lam * (acc[1] / l[1])
        o = ot.T
        ms = jnp.mean(o * o, axis=-1, keepdims=True)
        o = o * lax.rsqrt(ms + SUBLN_EPS) * g_ref[...] * (1.0 - lam_init)
        o_ref[rows, :] = o.astype(o_ref.dtype)
        return carry

    lax.fori_loop(0, q_ref.shape[0] // tq, q_block, 0, unroll=2)


def _diff_attention(proj, lq1, lk1, lq2, lk2, subln_g, layer, lam_init, B, T, tq=512):
    cq, ck, cv = OFF_QB // LANES, OFF_KB // LANES, OFF_VB // LANES
    vec = pl.BlockSpec((None, 1, HEAD_DIM), lambda b, h: (layer, 0, 0))
    return pl.pallas_call(
        functools.partial(_diff_kernel, lam_init=lam_init, tq=tq),
        grid=(B, DIFF_HEADS),
        in_specs=[
            vec, vec, vec, vec,
            pl.BlockSpec((T, LANES), lambda b, h: (b, cq + h)),
            pl.BlockSpec((T, LANES), lambda b, h: (b, ck + h)),
            pl.BlockSpec((T, LANES), lambda b, h: (b, cv + h)),
            pl.BlockSpec((None, 1, LANES), lambda b, h: (layer, 0, 0)),
        ],
        out_specs=pl.BlockSpec((T, LANES), lambda b, h: (b, h)),
        out_shape=jax.ShapeDtypeStruct((B * T, B_W), BF16),
        scratch_shapes=[pltpu.VMEM((LANES, T), BF16)],
        compiler_params=_cparams("parallel", "arbitrary"),
        name="diff_attn",
    )(lq1, lk1, lq2, lk2, proj, proj, proj, subln_g)


DIL_Q = 128


DIL_BLOCKS_PER_ITER = 2


def _dil_kernel(q_ref, k_ref, v_ref, o_ref, lse_ref, *, L, dil, radius):
    kw_len = DIL_Q + 2 * radius
    lane = lax.broadcasted_iota(jnp.int32, (DIL_Q, LANES), 1)
    lo = lane < HEAD_DIM
    rel = (lax.broadcasted_iota(jnp.int32, (DIL_Q, kw_len), 1)
           - lax.broadcasted_iota(jnp.int32, (DIL_Q, kw_len), 0))
    scale = HEAD_DIM ** -0.5
    nt = (((1,), (1,)), ((), ()))

    def blocks(p, gi):
        scores, starts, maxes = [], [], []
        for u in range(DIL_BLOCKS_PER_ITER):
            l0 = pl.multiple_of((gi * DIL_BLOCKS_PER_ITER + u) * DIL_Q, DIL_Q)
            ks = pl.multiple_of(jnp.clip(l0 - radius, 0, L - kw_len), radius)
            q = q_ref[p, pl.ds(l0, DIL_Q), :] * scale
            kw = k_ref[p, pl.ds(ks, kw_len), :]
            ok = jnp.abs(rel + (ks - l0)) <= radius
            starts.append((l0, ks))
            for hh in range(2):
                qh = jnp.where(lo if hh == 0 else jnp.logical_not(lo), q, jnp.zeros_like(q))
                s = lax.dot_general(qh, kw, nt, preferred_element_type=F32)
                scores.append(jnp.where(ok, s, NEG_INF))
        probs, sums = [], []
        for s in scores:
            m = jnp.max(s, axis=-1, keepdims=True)
            e = jnp.exp(s - m)
            maxes.append(m)
            sums.append(jnp.sum(e, axis=-1, keepdims=True))
            probs.append(e.astype(BF16))
        for u in range(DIL_BLOCKS_PER_ITER):
            l0, ks = starts[u]
            vw = v_ref[p, pl.ds(ks, kw_len), :]
            outs = [jnp.dot(probs[2 * u + hh], vw, preferred_element_type=F32) / sums[2 * u + hh] for hh in range(2)]
            lses = [maxes[2 * u + hh] + jnp.log(sums[2 * u + hh]) for hh in range(2)]
            o = jnp.where(lo, outs[0], outs[1])
            lse = jnp.where(lo, lses[0], lses[1])
            if dil == 1:
                o_ref[pl.ds(l0, DIL_Q), :] = o
                lse_ref[pl.ds(l0, DIL_Q), :] = lse
            else:
                o_ref[pl.ds(l0 * dil + p, DIL_Q, stride=dil), :] = o
                lse_ref[pl.ds(l0 * dil + p, DIL_Q, stride=dil), :] = lse

    n_iter = L // (DIL_Q * DIL_BLOCKS_PER_ITER)

    def phase(p, carry):
        lax.fori_loop(0, n_iter, lambda gi, c: (blocks(p, gi), c)[1], 0)
        return carry

    lax.fori_loop(0, dil, phase, 0)


def _dilated_group(cg, radius):
    B, dil, L, _ = cg.shape
    assert L >= DIL_Q + 2 * radius and L % (DIL_Q * DIL_BLOCKS_PER_ITER) == 0
    hp = GROUP_W // LANES
    in_spec = lambda c: pl.BlockSpec((None, dil, L, LANES), lambda b, h: (b, 0, 0, c * hp + h))
    out_spec = pl.BlockSpec((L * dil, LANES), lambda b, h: (b, h))
    out_shape = jax.ShapeDtypeStruct((B * L * dil, GROUP_W), F32)
    return pl.pallas_call(
        functools.partial(_dil_kernel, L=L, dil=dil, radius=radius),
        grid=(B, hp),
        in_specs=[in_spec(0), in_spec(1), in_spec(2)],
        out_specs=[out_spec, out_spec],
        out_shape=[out_shape, out_shape],
        compiler_params=_cparams("parallel", "arbitrary"),
        name="dil_attn",
    )(cg, cg, cg)


def _dilated_branch(cgs):
    outs, lses = [], []
    for cg, (window, dil) in zip(cgs, DIL_PATTERNS):
        o, lse = _dilated_group(cg, window // (2 * dil))
        outs.append(o)
        lses.append(lse)
    return outs, lses


def _merge_kernel(h_ref, ga_ref, gb_ref, gc_ref, ya_ref, yb_ref, o0_ref, o1_ref, o2_ref,
                  l0_ref, l1_ref, l2_ref, wpa_ref, wpb_ref, wpc_ref, wo_ref, out_ref):
    l0, l1, l2 = l0_ref[...], l1_ref[...], l2_ref[...]
    m = jnp.maximum(jnp.maximum(l0, l1), l2)
    e0, e1, e2 = jnp.exp(l0 - m), jnp.exp(l1 - m), jnp.exp(l2 - m)
    yc = (o0_ref[...].astype(F32) * e0 + o1_ref[...].astype(F32) * e1 + o2_ref[...].astype(F32) * e2) / (e0 + e1 + e2)
    merged = jax.nn.sigmoid(ga_ref[...].astype(F32)) * jnp.dot(ya_ref[...], wpa_ref[...], preferred_element_type=F32)
    merged += jax.nn.sigmoid(gb_ref[...].astype(F32)) * jnp.dot(yb_ref[...], wpb_ref[...], preferred_element_type=F32)
    merged += jax.nn.sigmoid(gc_ref[...].astype(F32)) * jnp.dot(yc.astype(BF16), wpc_ref[...],
                                                                preferred_element_type=F32)
    out_ref[...] = h_ref[...] + jnp.dot(merged.astype(BF16), wo_ref[...], preferred_element_type=F32)


def _merge(h2d, proj, ya, yb, outs, lses, wpa, wpb, wpc, wo, layer, tm=512):
    n = h2d.shape[0]
    gw = GROUP_W
    row = lambda w: pl.BlockSpec((tm, w), lambda i: (i, 0))
    full = lambda a: pl.BlockSpec((None,) + a.shape[1:], lambda i: (layer, 0, 0))
    gate = lambda c: pl.BlockSpec((tm, D_MODEL), lambda i: (i, OFF_GATE // D_MODEL + c))
    return pl.pallas_call(
        _merge_kernel,
        grid=(n // tm,),
        in_specs=[row(D_MODEL), gate(0), gate(1), gate(2), row(A_W), row(B_W),
                  row(gw), row(gw), row(gw), row(gw), row(gw), row(gw),
                  full(wpa), full(wpb), full(wpc), full(wo)],
        out_specs=row(D_MODEL),
        out_shape=jax.ShapeDtypeStruct((n, D_MODEL), F32),
        compiler_params=_cparams("parallel"),
        name="merge_outproj",
    )(h2d, proj, proj, proj, ya, yb, *outs, *lses, wpa, wpb, wpc, wo)


MOE_TILE = 256
MOE_TOKENS_PER_STEP = 512
ROUTE_ROWS = 8


def _router_kernel(h_ref, g_ref, wr_ref, br_ref, route_ref, wts_ref, cnt_ref, base_ref):
    i = pl.program_id(0)
    tm = h_ref.shape[0]
    lane = lax.broadcasted_iota(jnp.int32, (tm, LANES), 1)

    @pl.when(i == 0)
    def _():
        base_ref[...] = jnp.zeros_like(base_ref)

    x = h_ref[...]
    ms = jnp.mean(x * x, axis=-1, keepdims=True)
    xn = x * lax.rsqrt(ms + NORM_EPS) * g_ref[...]
    logits = jnp.dot(xn, wr_ref[...], preferred_element_type=F32, precision=lax.Precision.HIGHEST) + br_ref[...]
    lanef = lane.astype(F32)
    big = float(LANES)
    gl = jnp.where(lane < N_GROUPS, logits, -jnp.inf)
    gmax = jnp.max(gl, axis=-1, keepdims=True)
    g_sel = jnp.min(jnp.where(gl == gmax, lanef, big), axis=-1, keepdims=True).astype(jnp.int32)
    g_gate = 1.0 / jnp.sum(jnp.exp(gl - gmax), axis=-1, keepdims=True)
    eidx = lane - N_GROUPS
    in_grp = (eidx >= g_sel * EXPERTS_PER_GROUP) & (eidx < (g_sel + 1) * EXPERTS_PER_GROUP)
    el = jnp.where(in_grp, logits, -jnp.inf)
    t1 = jnp.max(el, axis=-1, keepdims=True)
    i1 = jnp.min(jnp.where(el == t1, lanef, big), axis=-1, keepdims=True).astype(jnp.int32)
    el2 = jnp.where(lane == i1, -jnp.inf, el)
    t2 = jnp.max(el2, axis=-1, keepdims=True)
    i2 = jnp.min(jnp.where(el2 == t2, lanef, big), axis=-1, keepdims=True).astype(jnp.int32)
    x2 = jnp.exp(t2 - t1)
    den = 1.0 + x2
    wts_ref[...] = jnp.concatenate([jnp.broadcast_to(g_gate / den, (tm, LANES)),
                                    jnp.broadcast_to(g_gate * x2 / den, (tm, LANES))], axis=1)
    member = jnp.where(lane == i1, 1.0, 0.0) + jnp.where(lane == i2, 1.0, 0.0)
    tri = (lax.broadcasted_iota(jnp.int32, (tm, tm), 0) > lax.broadcasted_iota(jnp.int32, (tm, tm), 1))
    earlier = jnp.dot(jnp.where(tri, 1.0, 0.0).astype(BF16), member.astype(BF16),
                      preferred_element_type=F32) + base_ref[0:1, :]
    rank1 = jnp.sum(jnp.where(lane == i1, earlier, 0.0), axis=-1, keepdims=True)
    rank2 = jnp.sum(jnp.where(lane == i2, earlier, 0.0), axis=-1, keepdims=True)
    base_ref[...] = base_ref[...] + jnp.sum(member, axis=0, keepdims=True)
    cnt_ref[...] = base_ref[...]
    table = jnp.where(lane == 0, (i1 - N_GROUPS).astype(F32),
                      jnp.where(lane == 1, (i2 - N_GROUPS).astype(F32),
                                jnp.where(lane == 2, rank1, jnp.where(lane == 3, rank2, 0.0))))
    route_ref[...] = table.T[:ROUTE_ROWS, :]


def _router(h2d, g, w_router, b_router, layer, tm=1024):
    n = h2d.shape[0]
    return pl.pallas_call(
        _router_kernel,
        grid=(n // tm,),
        in_specs=[
            pl.BlockSpec((tm, D_MODEL), lambda i: (i, 0)),
            pl.BlockSpec((None, 1, D_MODEL), lambda i: (layer, 0, 0)),
            pl.BlockSpec((None, D_MODEL, LANES), lambda i: (layer, 0, 0)),
            pl.BlockSpec((None, 1, LANES), lambda i: (layer, 0, 0)),
        ],
        out_specs=[pl.BlockSpec((ROUTE_ROWS, tm), lambda i: (0, i)),
                   pl.BlockSpec((tm, 2 * LANES), lambda i: (i, 0)),
                   pl.BlockSpec((8, LANES), lambda i: (0, 0))],
        out_shape=[jax.ShapeDtypeStruct((ROUTE_ROWS, n), F32), jax.ShapeDtypeStruct((n, 2 * LANES), F32),
                   jax.ShapeDtypeStruct((8, LANES), F32)],
        scratch_shapes=[pltpu.VMEM((8, LANES), F32)],
        compiler_params=_cparams("arbitrary"),
        name="moe_router",
    )(h2d, g, w_router, b_router)


def _dispatch_tables(route, counts, n):
    cnt = counts[0, N_GROUPS:N_GROUPS + N_EXPERTS].astype(jnp.int32)
    padded = (cnt + MOE_TILE - 1) // MOE_TILE * MOE_TILE
    ends = jnp.cumsum(padded)
    starts = ends - padded
    eids = jnp.arange(N_EXPERTS, dtype=jnp.int32)
    pos = []
    for k in range(2):
        e = route[k].astype(jnp.int32)
        start_e = jnp.sum(jnp.where(e[:, None] == eids[None, :], starts[None, :], 0), axis=1)
        pos.append(start_e + route[2 + k].astype(jnp.int32))
    tb = MOE_TOKENS_PER_STEP
    idx = jnp.concatenate([pos[0].reshape(n // tb, 1, tb), pos[1].reshape(n // tb, 1, tb)], axis=-1)
    n_tiles = 2 * n // MOE_TILE + N_EXPERTS
    tile_ids = jnp.arange(n_tiles, dtype=jnp.int32)
    n_used = ends[-1] // MOE_TILE
    tile_e = jnp.sum((tile_ids[:, None] * MOE_TILE >= ends[None, :]).astype(jnp.int32), axis=1)
    last_e = jnp.max(jnp.where(tile_ids < n_used, tile_e, 0))
    tile_e = jnp.where(tile_ids < n_used, tile_e, last_e).astype(jnp.int32)
    zero_rows = jnp.where(padded > 0, ends - MOE_TILE, n_tiles * MOE_TILE).astype(jnp.int32)
    return idx, tile_e, n_used.reshape(1).astype(jnp.int32), zero_rows, n_tiles


def _dispatch_kernel(zero_rows_ref, n_used_ref, idx_ref, h_hbm, xs_hbm, zeros_ref, sem):
    i = pl.program_id(0)
    tb = MOE_TOKENS_PER_STEP
    n_tiles = xs_hbm.shape[0] // MOE_TILE - 1

    def zero_tile(row0):
        return pltpu.make_async_copy(zeros_ref, xs_hbm.at[pl.ds(pl.multiple_of(row0, MOE_TILE), MOE_TILE)], sem)

    @pl.when(i == 0)
    def _():
        zeros_ref[...] = jnp.zeros_like(zeros_ref)
        fills = [zero_tile(zero_rows_ref[e]) for e in range(N_EXPERTS)]
        for c in fills:
            c.start()
        for c in fills:
            c.wait()

        def tail(j, carry):
            tile = n_used_ref[0] + j

            @pl.when(tile <= n_tiles)
            def _():
                c = zero_tile(tile * MOE_TILE)
                c.start()
                c.wait()
            return carry

        lax.fori_loop(0, N_EXPERTS + 1, tail, 0)

    def row(t, carry):
        src = h_hbm.at[pl.ds(i * tb + t, 1)]
        pltpu.make_async_copy(src, xs_hbm.at[pl.ds(idx_ref[0, 0, t], 1)], sem).start()
        pltpu.make_async_copy(src, xs_hbm.at[pl.ds(idx_ref[0, 0, tb + t], 1)], sem).start()
        return carry

    lax.fori_loop(0, tb, row, 0, unroll=8)
    for _ in range(2):
        pltpu.make_async_copy(h_hbm.at[pl.ds(0, tb)], xs_hbm.at[pl.ds(0, tb)], sem).wait()


def _dispatch(h2d, idx, zero_rows, n_used, n_tiles):
    n = h2d.shape[0]
    tb = MOE_TOKENS_PER_STEP
    assert 2 * n // MOE_TILE + N_EXPERTS == n_tiles
    return pl.pallas_call(
        _dispatch_kernel,
        grid_spec=pltpu.PrefetchScalarGridSpec(
            num_scalar_prefetch=2,
            grid=(n // tb,),
            in_specs=[pl.BlockSpec((1, 1, 2 * tb), lambda i, z, u: (i, 0, 0), memory_space=pltpu.SMEM),
                      pl.BlockSpec(memory_space=pl.ANY)],
            out_specs=pl.BlockSpec(memory_space=pl.ANY),
            scratch_shapes=[pltpu.VMEM((MOE_TILE, D_MODEL), F32), pltpu.SemaphoreType.DMA(())],
        ),
        out_shape=jax.ShapeDtypeStruct(((n_tiles + 1) * MOE_TILE, D_MODEL), F32),
        compiler_params=pltpu.CompilerParams(dimension_semantics=("arbitrary",), vmem_limit_bytes=VMEM_LIMIT,
                                             disable_bounds_checks=True),
        name="moe_dispatch",
    )(zero_rows, n_used, idx, h2d)


def _expert_kernel(tile_e_ref, n_used_ref, x_ref, g_ref, w1_ref, w3_ref, w2_ref, y_ref):
    used = pl.program_id(0) < n_used_ref[0]

    @pl.when(jnp.logical_not(used))
    def _():
        y_ref[...] = jnp.zeros_like(y_ref)

    @pl.when(used)
    def _():
        x = x_ref[...]
        ms = jnp.mean(x * x, axis=-1, keepdims=True)
        xn = (x * lax.rsqrt(ms + NORM_EPS) * g_ref[...]).astype(BF16)
        a = jnp.dot(xn, w1_ref[...].astype(BF16), preferred_element_type=F32)
        b = jnp.dot(xn, w3_ref[...].astype(BF16), preferred_element_type=F32)
        hmid = (a * jax.nn.sigmoid(a) * b).astype(BF16)
        y_ref[...] = jnp.dot(hmid, w2_ref[...].astype(BF16), preferred_element_type=F32)


def _experts(xs, g, w1, w3, w2, tile_e, n_used, layer, n_tiles):
    row_map = lambda i, te, nu: (i, 0)
    w_map = lambda i, te, nu: (layer, te[i], 0, 0)
    return pl.pallas_call(
        _expert_kernel,
        grid_spec=pltpu.PrefetchScalarGridSpec(
            num_scalar_prefetch=2,
            grid=(n_tiles,),
            in_specs=[pl.BlockSpec((MOE_TILE, D_MODEL), row_map),
                      pl.BlockSpec((None, 1, D_MODEL), lambda i, te, nu: (layer, 0, 0)),
                      pl.BlockSpec((None, None, D_MODEL, D_FF_EXPERT), w_map),
                      pl.BlockSpec((None, None, D_MODEL, D_FF_EXPERT), w_map),
                      pl.BlockSpec((None, None, D_FF_EXPERT, D_MODEL), w_map)],
            out_specs=pl.BlockSpec((MOE_TILE, D_MODEL), row_map),
        ),
        out_shape=jax.ShapeDtypeStruct((n_tiles * MOE_TILE, D_MODEL), F32),
        compiler_params=_cparams("arbitrary"),
        name="moe_experts",
    )(tile_e, n_used, xs, g, w1, w3, w2)


def _combine_kernel(idx_ref, ys_hbm, h_ref, wts_ref, o_ref, buf_ref, sem):
    tb = h_ref.shape[0]

    def row(t, carry):
        pltpu.make_async_copy(ys_hbm.at[pl.ds(idx_ref[0, 0, t], 1)], buf_ref.at[0, pl.ds(t, 1)], sem).start()
        pltpu.make_async_copy(ys_hbm.at[pl.ds(idx_ref[0, 0, tb + t], 1)], buf_ref.at[1, pl.ds(t, 1)], sem).start()
        return carry

    lax.fori_loop(0, tb, row, 0, unroll=8)
    for k in range(2):
        pltpu.make_async_copy(ys_hbm.at[pl.ds(0, tb)], buf_ref.at[k], sem).wait()
    reps = D_MODEL // LANES
    w = wts_ref[...]
    w_top1 = jnp.concatenate([w[:, :LANES]] * reps, axis=1)
    w_top2 = jnp.concatenate([w[:, LANES:]] * reps, axis=1)
    o_ref[...] = h_ref[...] + w_top1 * buf_ref[0] + w_top2 * buf_ref[1]


def _combine(h2d, ys, idx, wts):
    n = h2d.shape[0]
    tb = MOE_TOKENS_PER_STEP
    return pl.pallas_call(
        _combine_kernel,
        grid=(n // tb,),
        in_specs=[pl.BlockSpec((1, 1, 2 * tb), lambda i: (i, 0, 0), memory_space=pltpu.SMEM),
                  pl.BlockSpec(memory_space=pl.ANY),
                  pl.BlockSpec((tb, D_MODEL), lambda i: (i, 0)),
                  pl.BlockSpec((tb, 2 * LANES), lambda i: (i, 0))],
        out_specs=pl.BlockSpec((tb, D_MODEL), lambda i: (i, 0)),
        out_shape=jax.ShapeDtypeStruct((n, D_MODEL), F32),
        scratch_shapes=[pltpu.VMEM((2, tb, D_MODEL), F32), pltpu.SemaphoreType.DMA(())],
        compiler_params=pltpu.CompilerParams(dimension_semantics=("arbitrary",), vmem_limit_bytes=VMEM_LIMIT,
                                             disable_bounds_checks=True),
        name="moe_combine",
    )(idx, ys, h2d, wts)


def _moe(h2d, g, w_router, b_router, w1, w3, w2, layer):
    n = h2d.shape[0]
    route, wts, counts = _router(h2d, g, w_router, b_router, layer)
    idx, tile_e, n_used, zero_rows, n_tiles = _dispatch_tables(route, counts, n)
    xs = _dispatch(h2d, idx, zero_rows, n_used, n_tiles)
    ys = _experts(xs, g, w1, w3, w2, tile_e, n_used, layer, n_tiles)
    return _combine(h2d, ys, idx, wts)


def _norm_kernel(x_ref, g_ref, o_ref):
    x = x_ref[...]
    ms = jnp.mean(x * x, axis=-1, keepdims=True)
    o_ref[...] = x * lax.rsqrt(ms + NORM_EPS) * g_ref[...]


def _final_norm(h2d, g, tm=1024):
    n = h2d.shape[0]
    return pl.pallas_call(
        _norm_kernel,
        grid=(n // tm,),
        in_specs=[pl.BlockSpec((tm, D_MODEL), lambda i: (i, 0)), pl.BlockSpec((1, D_MODEL), lambda i: (0, 0))],
        out_specs=pl.BlockSpec((tm, D_MODEL), lambda i: (i, 0)),
        out_shape=jax.ShapeDtypeStruct((n, D_MODEL), F32),
        compiler_params=_cparams("parallel"),
        name="final_norm",
    )(h2d, g)


def _router_params(wg, bg, we, be):
    n_l = wg.shape[0]
    pad = LANES - N_GROUPS - N_EXPERTS
    w = jnp.concatenate([wg, we, jnp.zeros((n_l, D_MODEL, pad), F32)], axis=-1).astype(F32)
    b = jnp.concatenate([bg, be, jnp.zeros((n_l, pad), F32)], axis=-1).astype(F32)[:, None, :]
    return w, b


def kernel(x, w_in, na_rpb, lam_q1, lam_k1, lam_q2, lam_k2, diff_subln, w_pa, w_pb, w_pc, w_o, norm_mix, norm_ffn,
           router_group_w, router_group_b, router_expert_w, router_expert_b, w1, w3, w2, norm_final):
    B, T, D = x.shape
    depth = w_in.shape[0]
    rows = T // GRID_W
    wr = min(NA_WIN_R, rows)
    rope_tab = _rope_table(T)
    row3 = lambda a: a[:, None, :]
    w_in_bf = _permute_cols(w_in).astype(BF16)
    bias_tab = _na_bias_table(na_rpb, wr)
    g_mix, g_ffn = row3(norm_mix), row3(norm_ffn)
    lams = [row3(a) for a in (lam_q1, lam_k1, lam_q2, lam_k2)]
    subln = row3(diff_subln)
    wpa, wpb, wpc, wo = (a.astype(BF16) for a in (w_pa, w_pb, w_pc, w_o))
    w_router, b_router = _router_params(router_group_w, router_group_b, router_expert_w, router_expert_b)
    h = x.reshape(B * T, D)
    for l in range(depth):
        lam_init = 0.8 - 0.6 * math.exp(-0.3 * l)
        proj, *cgs = _inproj(h, g_mix, w_in_bf, rope_tab, l, B, T)
        ya = _na_attention(proj, bias_tab, l, B, T)
        yb = _diff_attention(proj, *lams, subln, l, lam_init, B, T)
        outs, lses = _dilated_branch(cgs)
        h = _merge(h, proj, ya, yb, outs, lses, wpa, wpb, wpc, wo, l)
        h = _moe(h, g_ffn, w_router, b_router, w1, w3, w2, l)
    return _final_norm(h, norm_final[None, :]).reshape(B, T, D)
```

```python
import functools
import math

import jax
import jax.numpy as jnp
import numpy as np
from jax import lax
from jax.experimental import pallas as pl
from jax.experimental.pallas import tpu as pltpu

F32 = jnp.float32
BF16 = jnp.bfloat16

D_MODEL = 1024
HEAD_DIM = 64
ROPE_DIM = 16
ROPE_THETA = 500000.0
GRID_W = 64
NA_HEADS = 8
NA_WIN_R = 8
NA_WIN_C = 16
DIFF_HEADS = 4
DIL_PATTERNS = ((128, 1), (512, 4), (2048, 16))
N_GROUPS = 4
EXPERTS_PER_GROUP = 4
N_EXPERTS = 16
D_FF_EXPERT = D_MODEL // 2
NORM_EPS = 1e-6
SUBLN_EPS = 1e-5
NEG_INF = -1e30

LANES = 128
MXU_N = 256
VMEM_LIMIT = 56 * 1024 * 1024

A_W, B_W, C_W = 512, 512, 768
N_STEPS = len(DIL_PATTERNS)
GROUP_W = C_W // N_STEPS
_MAIN_ORDER = ("qb", "kb", "va", "qa", "ga", "gb", "gc", "ka", "vb")
_REF_ORDER = ("qa", "ka", "va", "qb", "kb", "vb", "qc", "kc", "vc", "ga", "gb", "gc")
_WIDTH = dict(qa=A_W, ka=A_W, va=A_W, qb=B_W, kb=B_W, vb=B_W, qc=C_W, kc=C_W, vc=C_W,
              ga=D_MODEL, gb=D_MODEL, gc=D_MODEL)


def _offsets(order):
    off, out = 0, {}
    for name in order:
        out[name] = off
        off += _WIDTH[name]
    return out, off


_OFF, MAIN_W = _offsets(_MAIN_ORDER)
_REF_OFF, IN_W = _offsets(_REF_ORDER)
OFF_QA, OFF_KA, OFF_VA = _OFF["qa"], _OFF["ka"], _OFF["va"]
OFF_QB, OFF_KB, OFF_VB = _OFF["qb"], _OFF["kb"], _OFF["vb"]
OFF_GATE = _OFF["ga"]
MAIN_STEP_W = MAIN_W // N_STEPS
STEP_W = MAIN_STEP_W + 3 * GROUP_W
MAIN_ROPE_W = 2 * B_W
QB_SCALE = HEAD_DIM ** -0.5 * math.log2(math.e)
assert OFF_GATE % D_MODEL == 0 and OFF_QB == 0 and OFF_KB == B_W and MAIN_ROPE_W <= MAIN_STEP_W
assert MAIN_W % N_STEPS == 0 and MAIN_STEP_W % MXU_N == 0 and N_STEPS * STEP_W == IN_W


def _permute_cols(w):
    cols = {n: w[..., _REF_OFF[n]:_REF_OFF[n] + _WIDTH[n]] for n in _MAIN_ORDER}
    cols["qb"] = cols["qb"] * QB_SCALE
    main = jnp.concatenate([cols[n] for n in _MAIN_ORDER], axis=-1)
    parts = []
    for s in range(N_STEPS):
        parts.append(main[..., s * MAIN_STEP_W:(s + 1) * MAIN_STEP_W])
        for n in ("qc", "kc", "vc"):
            parts.append(w[..., _REF_OFF[n] + s * GROUP_W:_REF_OFF[n] + (s + 1) * GROUP_W])
    return jnp.concatenate(parts, axis=-1)


def _cparams(*sem):
    return pltpu.CompilerParams(dimension_semantics=sem, vmem_limit_bytes=VMEM_LIMIT)


def _inproj_kernel(x_ref, g_ref, w_ref, rope_ref, main_ref, c0_ref, c1_ref, c2_ref, xn_ref, y_ref, *, tm):
    j = pl.program_id(1)
    c_refs = (c0_ref, c1_ref, c2_ref)
    reps = MXU_N // LANES
    half = ROPE_DIM // 2
    assert GROUP_W == MXU_N

    def rope(y):
        cos = jnp.concatenate([rope_ref[0]] * reps, axis=1)
        s_up = jnp.concatenate([rope_ref[1]] * reps, axis=1)
        s_dn = jnp.concatenate([rope_ref[2]] * reps, axis=1)
        return y * cos + pltpu.roll(y, MXU_N - half, 1) * s_up + pltpu.roll(y, half, 1) * s_dn

    def step(s):
        xn = xn_ref[...]
        for c in range(MAIN_STEP_W // MXU_N):
            sl = slice(c * MXU_N, (c + 1) * MXU_N)
            y = jnp.dot(xn, w_ref[:, sl], preferred_element_type=F32)
            if s == 0 and c < MAIN_ROPE_W // MXU_N:
                y = rope(y)
            main_ref[:, sl] = y.astype(main_ref.dtype)
        dil = DIL_PATTERNS[s][1]
        for c in range(3):
            wsl = slice(MAIN_STEP_W + c * GROUP_W, MAIN_STEP_W + (c + 1) * GROUP_W)
            osl = slice(c * GROUP_W, (c + 1) * GROUP_W)
            y = jnp.dot(xn, w_ref[:, wsl], preferred_element_type=F32)
            if c < 2:
                y = rope(y)
            if dil == 1:
                c_refs[s][0, :, osl] = y.astype(BF16)
            else:
                for hb in range(reps):
                    y_ref[c, hb] = y[:, hb * LANES:(hb + 1) * LANES]
                for p in range(dil):
                    for hb in range(reps):
                        c_refs[s][p, :, c * GROUP_W + hb * LANES:c * GROUP_W + (hb + 1) * LANES] = (
                            y_ref[c, hb, pl.ds(p, tm // dil, stride=dil), :].astype(BF16))

    @pl.when(j == 0)
    def _():
        x = x_ref[...]
        ms = jnp.mean(x * x, axis=-1, keepdims=True)
        xn_ref[...] = (x * lax.rsqrt(ms + NORM_EPS) * g_ref[...]).astype(BF16)
        step(0)

    for s in range(1, N_STEPS):
        pl.when(j == s)(functools.partial(step, s))


def _inproj(h2d, g, w_bf, rope_tab, layer, B, T, tm=1024):
    n = h2d.shape[0]
    tpb = T // tm
    c_specs, c_shapes = [], []
    for _, dil in DIL_PATTERNS:
        assert tm % dil == 0
        c_specs.append(pl.BlockSpec((None, dil, tm // dil, 3 * GROUP_W), lambda i, j: (i // tpb, 0, i % tpb, 0)))
        c_shapes.append(jax.ShapeDtypeStruct((B, dil, T // dil, 3 * GROUP_W), BF16))
    return pl.pallas_call(
        functools.partial(_inproj_kernel, tm=tm),
        grid=(n // tm, N_STEPS),
        in_specs=[
            pl.BlockSpec((tm, D_MODEL), lambda i, j: (i, 0)),
            pl.BlockSpec((None, 1, D_MODEL), lambda i, j: (layer, 0, 0)),
            pl.BlockSpec((None, D_MODEL, STEP_W), lambda i, j: (layer, 0, j)),
            pl.BlockSpec((3, tm, LANES), lambda i, j: (0, i % tpb, 0)),
        ],
        out_specs=[pl.BlockSpec((tm, MAIN_STEP_W), lambda i, j: (i, j))] + c_specs,
        out_shape=[jax.ShapeDtypeStruct((n, MAIN_W), BF16)] + c_shapes,
        scratch_shapes=[pltpu.VMEM((tm, D_MODEL), BF16), pltpu.VMEM((3, MXU_N // LANES, tm, LANES), F32)],
        compiler_params=_cparams("parallel", "arbitrary"),
        name="inproj",
    )(h2d, g, w_bf, rope_tab)


def _rope_table(T):
    half = ROPE_DIM // 2
    inv = 1.0 / (ROPE_THETA ** (jnp.arange(0, ROPE_DIM, 2, dtype=F32) / ROPE_DIM))
    ang = jnp.arange(T, dtype=F32)[:, None] * inv[None, :]
    cos, sin = jnp.cos(ang), jnp.sin(ang)
    zeros = jnp.zeros((T, HEAD_DIM - ROPE_DIM), F32)
    z8 = jnp.zeros((T, half), F32)
    c64 = jnp.concatenate([cos, cos, zeros + 1.0], axis=1)
    up64 = jnp.concatenate([-sin, z8, zeros], axis=1)
    dn64 = jnp.concatenate([z8, sin, zeros], axis=1)
    reps = LANES // HEAD_DIM
    return jnp.stack([jnp.tile(c64, (1, reps)), jnp.tile(up64, (1, reps)), jnp.tile(dn64, (1, reps))])


NA_ROWS_PER_ITER = 4


def _na_kernel(q_ref, k_ref, v_ref, b_ref, o_ref, *, rows, wr):
    lane = lax.broadcasted_iota(jnp.int32, (GRID_W, LANES), 1)
    lo = lane < HEAD_DIM
    scale = HEAD_DIM ** -0.5

    def row_group(gi, carry):
        scores, windows = [], []
        for u in range(NA_ROWS_PER_ITER):
            r = gi * NA_ROWS_PER_ITER + u
            r0 = jnp.clip(r - wr // 2, 0, rows - wr)
            d0 = r0 - r + NA_WIN_R - 1 - (NA_WIN_R - wr)
            q = q_ref[pl.ds(pl.multiple_of(r * GRID_W, GRID_W), GRID_W), :] * scale
            ks = pl.multiple_of(r0 * GRID_W, GRID_W)
            kw = k_ref[pl.ds(ks, wr * GRID_W), :]
            windows.append(ks)
            for hh in range(2):
                qh = jnp.where(lo if hh == 0 else jnp.logical_not(lo), q, jnp.zeros_like(q))
                s = lax.dot_general(qh, kw, (((1,), (1,)), ((), ())), preferred_element_type=F32)
                scores.append(s + b_ref[hh, d0])
        probs, sums = [], []
        for s in scores:
            p = jnp.exp(s - jnp.max(s, axis=-1, keepdims=True))
            sums.append(jnp.sum(p, axis=-1, keepdims=True))
            probs.append(p.astype(BF16))
        for u in range(NA_ROWS_PER_ITER):
            r = gi * NA_ROWS_PER_ITER + u
            vw = v_ref[pl.ds(windows[u], wr * GRID_W), :]
            outs = [jnp.dot(probs[2 * u + hh], vw, preferred_element_type=F32) / sums[2 * u + hh] for hh in range(2)]
            o = jnp.where(lo, outs[0], outs[1])
            o_ref[pl.ds(pl.multiple_of(r * GRID_W, GRID_W), GRID_W), :] = o.astype(o_ref.dtype)
        return carry

    lax.fori_loop(0, rows // NA_ROWS_PER_ITER, row_group, 0)


def _na_bias_table(rpb, wr):
    qc = np.arange(GRID_W)[:, None]
    kc = np.arange(GRID_W)[None, :]
    c0 = np.clip(qc - NA_WIN_C // 2, 0, GRID_W - NA_WIN_C)
    ok = (kc >= c0) & (kc < c0 + NA_WIN_C)
    dc = np.clip(kc - qc + NA_WIN_C - 1, 0, 2 * NA_WIN_C - 2)
    onehot = (np.arange(2 * NA_WIN_C - 1)[:, None, None] == dc[None]).astype(np.float32)
    b = jnp.einsum("lhrd,dqk->lhrqk", rpb.astype(F32), onehot, precision=lax.Precision.HIGHEST)
    b = jnp.where(jnp.asarray(ok), b, NEG_INF)
    row_idx = (NA_WIN_R - wr) + np.arange(wr)[:, None] + np.arange(wr)[None, :]
    t = b[:, :, row_idx]
    n_l, n_h = rpb.shape[0], rpb.shape[1]
    return t.transpose(0, 1, 2, 4, 3, 5).reshape(n_l, n_h, wr, GRID_W, wr * GRID_W)


def _na_attention(proj, bias_tab, layer, B, T):
    rows = T // GRID_W
    wr = min(NA_WIN_R, rows)
    cq, ck, cv = OFF_QA // LANES, OFF_KA // LANES, OFF_VA // LANES
    return pl.pallas_call(
        functools.partial(_na_kernel, rows=rows, wr=wr),
        grid=(B, NA_HEADS // 2),
        in_specs=[
            pl.BlockSpec((T, LANES), lambda b, h: (b, cq + h)),
            pl.BlockSpec((T, LANES), lambda b, h: (b, ck + h)),
            pl.BlockSpec((T, LANES), lambda b, h: (b, cv + h)),
            pl.BlockSpec((None, 2, wr, GRID_W, wr * GRID_W), lambda b, h: (layer, h, 0, 0, 0)),
        ],
        out_specs=pl.BlockSpec((T, LANES), lambda b, h: (b, h)),
        out_shape=jax.ShapeDtypeStruct((B * T, A_W), BF16),
        compiler_params=_cparams("parallel", "arbitrary"),
        name="na_attn",
    )(proj, proj, proj, bias_tab)


DIFF_KEY_CHUNK = 512


def _diff_kernel(lq1_ref, lk1_ref, lq2_ref, lk2_ref, q_ref, k_ref, v_ref, g_ref, o_ref, vt_ref, *, lam_init, tq):
    lam = (jnp.exp(jnp.sum(lq1_ref[...] * lk1_ref[...], keepdims=True))
           - jnp.exp(jnp.sum(lq2_ref[...] * lk2_ref[...], keepdims=True)) + lam_init)
    T = k_ref.shape[0]
    ck = DIFF_KEY_CHUNK
    n_chunks = T // ck
    lane = lax.broadcasted_iota(jnp.int32, (tq, LANES), 1)
    lo = lane < HEAD_DIM
    nt = (((1,), (1,)), ((), ()))
    vt_ref[...] = v_ref[...].T

    def q_block(i, carry):
        rows = pl.ds(pl.multiple_of(i * tq, tq), tq)
        q = q_ref[rows, :]
        zero = jnp.zeros_like(q)
        qs = (jnp.where(lo, q, zero), jnp.where(lo, zero, q))

        def scores(c):
            kc = k_ref[c * ck:(c + 1) * ck, :]
            return [lax.dot_general(kc, qm, nt, preferred_element_type=F32) for qm in qs]

        m = [jnp.full((1, tq), NEG_INF, F32)] * 2
        l = [jnp.zeros((1, tq), F32)] * 2
        acc = [jnp.zeros((LANES, tq), F32)] * 2
        s_next = scores(0)
        for c in range(n_chunks):
            s_cur = s_next
            if c + 1 < n_chunks:
                s_next = scores(c + 1)
            vt = vt_ref[:, c * ck:(c + 1) * ck]
            for j in range(2):
                m_new = jnp.maximum(m[j], jnp.max(s_cur[j], axis=0, keepdims=True))
                alpha = jnp.exp2(m[j] - m_new)
                p = jnp.exp2(s_cur[j] - m_new)
                l[j] = alpha * l[j] + jnp.sum(p, axis=0, keepdims=True)
                acc[j] = alpha * acc[j] + jnp.dot(vt, p.astype(BF16), preferred_element_type=F32)
                m[j] = m_new
        ot = acc[0] / l[0] - lam * (acc[1] / l[1])
        o = ot.T
        ms = jnp.mean(o * o, axis=-1, keepdims=True)
        o = o * lax.rsqrt(ms + SUBLN_EPS) * g_ref[...] * (1.0 - lam_init)
        o_ref[rows, :] = o.astype(o_ref.dtype)
        return carry

    lax.fori_loop(0, q_ref.shape[0] // tq, q_block, 0, unroll=2)


def _diff_attention(proj, lq1, lk1, lq2, lk2, subln_g, layer, lam_init, B, T, tq=512):
    cq, ck, cv = OFF_QB // LANES, OFF_KB // LANES, OFF_VB // LANES
    vec = pl.BlockSpec((None, 1, HEAD_DIM), lambda b, h: (layer, 0, 0))
    return pl.pallas_call(
        functools.partial(_diff_kernel, lam_init=lam_init, tq=tq),
        grid=(B, DIFF_HEADS),
        in_specs=[
            vec, vec, vec, vec,
            pl.BlockSpec((T, LANES), lambda b, h: (b, cq + h)),
            pl.BlockSpec((T, LANES), lambda b, h: (b, ck + h)),
            pl.BlockSpec((T, LANES), lambda b, h: (b, cv + h)),
            pl.BlockSpec((None, 1, LANES), lambda b, h: (layer, 0, 0)),
        ],
        out_specs=pl.BlockSpec((T, LANES), lambda b, h: (b, h)),
        out_shape=jax.ShapeDtypeStruct((B * T, B_W), BF16),
        scratch_shapes=[pltpu.VMEM((LANES, T), BF16)],
        compiler_params=_cparams("parallel", "arbitrary"),
        name="diff_attn",
    )(lq1, lk1, lq2, lk2, proj, proj, proj, subln_g)


DIL_Q = 128


DIL_BLOCKS_PER_ITER = 2


def _dil_kernel(q_ref, k_ref, v_ref, o_ref, lse_ref, *, L, dil, radius):
    kw_len = DIL_Q + 2 * radius
    lane = lax.broadcasted_iota(jnp.int32, (DIL_Q, LANES), 1)
    lo = lane < HEAD_DIM
    rel = (lax.broadcasted_iota(jnp.int32, (DIL_Q, kw_len), 1)
           - lax.broadcasted_iota(jnp.int32, (DIL_Q, kw_len), 0))
    scale = HEAD_DIM ** -0.5
    nt = (((1,), (1,)), ((), ()))

    def blocks(p, gi):
        scores, starts, maxes = [], [], []
        for u in range(DIL_BLOCKS_PER_ITER):
            l0 = pl.multiple_of((gi * DIL_BLOCKS_PER_ITER + u) * DIL_Q, DIL_Q)
            ks = pl.multiple_of(jnp.clip(l0 - radius, 0, L - kw_len), radius)
            q = q_ref[p, pl.ds(l0, DIL_Q), :] * scale
            kw = k_ref[p, pl.ds(ks, kw_len), :]
            ok = jnp.abs(rel + (ks - l0)) <= radius
            starts.append((l0, ks))
            for hh in range(2):
                qh = jnp.where(lo if hh == 0 else jnp.logical_not(lo), q, jnp.zeros_like(q))
                s = lax.dot_general(qh, kw, nt, preferred_element_type=F32)
                scores.append(jnp.where(ok, s, NEG_INF))
        probs, sums = [], []
        for s in scores:
            m = jnp.max(s, axis=-1, keepdims=True)
            e = jnp.exp(s - m)
            maxes.append(m)
            sums.append(jnp.sum(e, axis=-1, keepdims=True))
            probs.append(e.astype(BF16))
        for u in range(DIL_BLOCKS_PER_ITER):
            l0, ks = starts[u]
            vw = v_ref[p, pl.ds(ks, kw_len), :]
            outs = [jnp.dot(probs[2 * u + hh], vw, preferred_element_type=F32) / sums[2 * u + hh] for hh in range(2)]
            lses = [maxes[2 * u + hh] + jnp.log(sums[2 * u + hh]) for hh in range(2)]
            o = jnp.where(lo, outs[0], outs[1])
            lse = jnp.where(lo, lses[0], lses[1])
            if dil == 1:
                o_ref[pl.ds(l0, DIL_Q), :] = o
                lse_ref[pl.ds(l0, DIL_Q), :] = lse
            else:
                o_ref[pl.ds(l0 * dil + p, DIL_Q, stride=dil), :] = o
                lse_ref[pl.ds(l0 * dil + p, DIL_Q, stride=dil), :] = lse

    n_iter = L // (DIL_Q * DIL_BLOCKS_PER_ITER)

    def phase(p, carry):
        lax.fori_loop(0, n_iter, lambda gi, c: (blocks(p, gi), c)[1], 0)
        return carry

    lax.fori_loop(0, dil, phase, 0)


def _dilated_group(cg, radius):
    B, dil, L, _ = cg.shape
    assert L >= DIL_Q + 2 * radius and L % (DIL_Q * DIL_BLOCKS_PER_ITER) == 0
    hp = GROUP_W // LANES
    in_spec = lambda c: pl.BlockSpec((None, dil, L, LANES), lambda b, h: (b, 0, 0, c * hp + h))
    out_spec = pl.BlockSpec((L * dil, LANES), lambda b, h: (b, h))
    out_shape = jax.ShapeDtypeStruct((B * L * dil, GROUP_W), F32)
    return pl.pallas_call(
        functools.partial(_dil_kernel, L=L, dil=dil, radius=radius),
        grid=(B, hp),
        in_specs=[in_spec(0), in_spec(1), in_spec(2)],
        out_specs=[out_spec, out_spec],
        out_shape=[out_shape, out_shape],
        compiler_params=_cparams("parallel", "arbitrary"),
        name="dil_attn",
    )(cg, cg, cg)


def _dilated_branch(cgs):
    outs, lses = [], []
    for cg, (window, dil) in zip(cgs, DIL_PATTERNS):
        o, lse = _dilated_group(cg, window // (2 * dil))
        outs.append(o)
        lses.append(lse)
    return outs, lses


def _merge_kernel(h_ref, ga_ref, gb_ref, gc_ref, ya_ref, yb_ref, o0_ref, o1_ref, o2_ref,
                  l0_ref, l1_ref, l2_ref, wpa_ref, wpb_ref, wpc_ref, wo_ref, out_ref):
    l0, l1, l2 = l0_ref[...], l1_ref[...], l2_ref[...]
    m = jnp.maximum(jnp.maximum(l0, l1), l2)
    e0, e1, e2 = jnp.exp(l0 - m), jnp.exp(l1 - m), jnp.exp(l2 - m)
    yc = (o0_ref[...].astype(F32) * e0 + o1_ref[...].astype(F32) * e1 + o2_ref[...].astype(F32) * e2) / (e0 + e1 + e2)
    merged = jax.nn.sigmoid(ga_ref[...].astype(F32)) * jnp.dot(ya_ref[...], wpa_ref[...], preferred_element_type=F32)
    merged += jax.nn.sigmoid(gb_ref[...].astype(F32)) * jnp.dot(yb_ref[...], wpb_ref[...], preferred_element_type=F32)
    merged += jax.nn.sigmoid(gc_ref[...].astype(F32)) * jnp.dot(yc.astype(BF16), wpc_ref[...],
                                                                preferred_element_type=F32)
    out_ref[...] = h_ref[...] + jnp.dot(merged.astype(BF16), wo_ref[...], preferred_element_type=F32)


def _merge(h2d, proj, ya, yb, outs, lses, wpa, wpb, wpc, wo, layer, tm=512):
    n = h2d.shape[0]
    gw = GROUP_W
    row = lambda w: pl.BlockSpec((tm, w), lambda i: (i, 0))
    full = lambda a: pl.BlockSpec((None,) + a.shape[1:], lambda i: (layer, 0, 0))
    gate = lambda c: pl.BlockSpec((tm, D_MODEL), lambda i: (i, OFF_GATE // D_MODEL + c))
    return pl.pallas_call(
        _merge_kernel,
        grid=(n // tm,),
        in_specs=[row(D_MODEL), gate(0), gate(1), gate(2), row(A_W), row(B_W),
                  row(gw), row(gw), row(gw), row(gw), row(gw), row(gw),
                  full(wpa), full(wpb), full(wpc), full(wo)],
        out_specs=row(D_MODEL),
        out_shape=jax.ShapeDtypeStruct((n, D_MODEL), F32),
        compiler_params=_cparams("parallel"),
        name="merge_outproj",
    )(h2d, proj, proj, proj, ya, yb, *outs, *lses, wpa, wpb, wpc, wo)


MOE_TILE = 256
MOE_TOKENS_PER_STEP = 512
ROUTE_ROWS = 8


ROUTER_ROWS = 32


def _router_kernel(h_ref, g_ref, wr_ref, br_ref, upper_ref, route_ref, wts_ref, cnt_ref, base_ref):
    i = pl.program_id(0)
    tm = h_ref.shape[0]
    lane_reps = tm // LANES
    row = lax.broadcasted_iota(jnp.int32, (ROUTER_ROWS, tm), 0)
    rowf = row.astype(F32)
    big = float(ROUTER_ROWS)
    nt = (((1,), (1,)), ((), ()))

    @pl.when(i == 0)
    def _():
        base_ref[...] = jnp.zeros_like(base_ref)

    x = h_ref[...]
    ms = jnp.mean(x * x, axis=-1, keepdims=True)
    xn = x * lax.rsqrt(ms + NORM_EPS) * g_ref[...]
    logits = lax.dot_general(wr_ref[...], xn, nt, preferred_element_type=F32, precision=lax.Precision.HIGHEST)
    logits = logits + jnp.concatenate([br_ref[...]] * lane_reps, axis=1)
    gl = jnp.where(row < N_GROUPS, logits, -jnp.inf)
    gmax = jnp.max(gl, axis=0, keepdims=True)
    g_sel = jnp.min(jnp.where(gl == gmax, rowf, big), axis=0, keepdims=True).astype(jnp.int32)
    g_gate = 1.0 / jnp.sum(jnp.exp(gl - gmax), axis=0, keepdims=True)
    eidx = row - N_GROUPS
    in_grp = (eidx >= g_sel * EXPERTS_PER_GROUP) & (eidx < (g_sel + 1) * EXPERTS_PER_GROUP)
    el = jnp.where(in_grp, logits, -jnp.inf)
    t1 = jnp.max(el, axis=0, keepdims=True)
    i1 = jnp.min(jnp.where(el == t1, rowf, big), axis=0, keepdims=True).astype(jnp.int32)
    el2 = jnp.where(row == i1, -jnp.inf, el)
    t2 = jnp.max(el2, axis=0, keepdims=True)
    i2 = jnp.min(jnp.where(el2 == t2, rowf, big), axis=0, keepdims=True).astype(jnp.int32)
    x2 = jnp.exp(t2 - t1)
    den = 1.0 + x2
    wts_ref[:, :LANES] = jnp.broadcast_to(g_gate / den, (LANES, tm)).T
    wts_ref[:, LANES:] = jnp.broadcast_to(g_gate * x2 / den, (LANES, tm)).T
    member = jnp.where(row == i1, 1.0, 0.0) + jnp.where(row == i2, 1.0, 0.0)
    earlier = jnp.dot(member.astype(BF16), upper_ref[...], preferred_element_type=F32)
    earlier = earlier + jnp.concatenate([base_ref[...]] * lane_reps, axis=1)
    rank1 = jnp.sum(jnp.where(row == i1, earlier, 0.0), axis=0, keepdims=True)
    rank2 = jnp.sum(jnp.where(row == i2, earlier, 0.0), axis=0, keepdims=True)
    base_ref[...] = base_ref[...] + jnp.sum(member, axis=1, keepdims=True)
    cnt_ref[...] = base_ref[...]
    out_row = lax.broadcasted_iota(jnp.int32, (ROUTE_ROWS, tm), 0)
    route_ref[...] = jnp.where(out_row == 0, (i1 - N_GROUPS).astype(F32),
                               jnp.where(out_row == 1, (i2 - N_GROUPS).astype(F32),
                                         jnp.where(out_row == 2, rank1, jnp.where(out_row == 3, rank2, 0.0))))


def _router(h2d, g, w_router, b_router, layer, tm=1024):
    n = h2d.shape[0]
    return pl.pallas_call(
        _router_kernel,
        grid=(n // tm,),
        in_specs=[
            pl.BlockSpec((tm, D_MODEL), lambda i: (i, 0)),
            pl.BlockSpec((None, 1, D_MODEL), lambda i: (layer, 0, 0)),
            pl.BlockSpec((None, ROUTER_ROWS, D_MODEL), lambda i: (layer, 0, 0)),
            pl.BlockSpec((None, ROUTER_ROWS, LANES), lambda i: (layer, 0, 0)),
            pl.BlockSpec((tm, tm), lambda i: (0, 0)),
        ],
        out_specs=[pl.BlockSpec((ROUTE_ROWS, tm), lambda i: (0, i)),
                   pl.BlockSpec((tm, 2 * LANES), lambda i: (i, 0)),
                   pl.BlockSpec((ROUTER_ROWS, LANES), lambda i: (0, 0))],
        out_shape=[jax.ShapeDtypeStruct((ROUTE_ROWS, n), F32), jax.ShapeDtypeStruct((n, 2 * LANES), F32),
                   jax.ShapeDtypeStruct((ROUTER_ROWS, LANES), F32)],
        scratch_shapes=[pltpu.VMEM((ROUTER_ROWS, LANES), F32)],
        compiler_params=_cparams("arbitrary"),
        name="moe_router",
    )(h2d, g, w_router, b_router, jnp.triu(jnp.ones((tm, tm), BF16), k=1))


def _dispatch_tables(route, counts, n):
    cnt = counts[N_GROUPS:N_GROUPS + N_EXPERTS, 0].astype(jnp.int32)
    padded = (cnt + MOE_TILE - 1) // MOE_TILE * MOE_TILE
    ends = jnp.cumsum(padded)
    starts = ends - padded
    eids = jnp.arange(N_EXPERTS, dtype=jnp.int32)
    pos = []
    for k in range(2):
        e = route[k].astype(jnp.int32)
        start_e = jnp.sum(jnp.where(e[:, None] == eids[None, :], starts[None, :], 0), axis=1)
        pos.append(start_e + route[2 + k].astype(jnp.int32))
    tb = MOE_TOKENS_PER_STEP
    idx = jnp.concatenate([pos[0].reshape(n // tb, 1, tb), pos[1].reshape(n // tb, 1, tb)], axis=-1)
    n_tiles = 2 * n // MOE_TILE + N_EXPERTS
    tile_ids = jnp.arange(n_tiles, dtype=jnp.int32)
    n_used = ends[-1] // MOE_TILE
    tile_e = jnp.sum((tile_ids[:, None] * MOE_TILE >= ends[None, :]).astype(jnp.int32), axis=1)
    last_e = jnp.max(jnp.where(tile_ids < n_used, tile_e, 0))
    tile_e = jnp.where(tile_ids < n_used, tile_e, last_e).astype(jnp.int32)
    zero_rows = jnp.where(padded > 0, ends - MOE_TILE, n_tiles * MOE_TILE).astype(jnp.int32)
    return idx, tile_e, n_used.reshape(1).astype(jnp.int32), zero_rows, n_tiles


def _dispatch_kernel(zero_rows_ref, n_used_ref, idx_ref, h_ref, xs_hbm, zeros_ref, sem):
    i = pl.program_id(0)
    tb = MOE_TOKENS_PER_STEP
    n_tiles = xs_hbm.shape[0] // MOE_TILE - 1

    def zero_tile(row0):
        return pltpu.make_async_copy(zeros_ref, xs_hbm.at[pl.ds(pl.multiple_of(row0, MOE_TILE), MOE_TILE)], sem)

    @pl.when(i == 0)
    def _():
        zeros_ref[...] = jnp.zeros_like(zeros_ref)
        fills = [zero_tile(zero_rows_ref[e]) for e in range(N_EXPERTS)]
        for c in fills:
            c.start()
        for c in fills:
            c.wait()

        def tail(j, carry):
            tile = n_used_ref[0] + j

            @pl.when(tile <= n_tiles)
            def _():
                c = zero_tile(tile * MOE_TILE)
                c.start()
                c.wait()
            return carry

        lax.fori_loop(0, N_EXPERTS + 1, tail, 0)

    def row(t, carry):
        src = h_ref.at[pl.ds(t, 1)]
        pltpu.make_async_copy(src, xs_hbm.at[pl.ds(idx_ref[0, 0, t], 1)], sem).start()
        pltpu.make_async_copy(src, xs_hbm.at[pl.ds(idx_ref[0, 0, tb + t], 1)], sem).start()
        return carry

    lax.fori_loop(0, tb, row, 0, unroll=8)
    for _ in range(2):
        pltpu.make_async_copy(h_ref, xs_hbm.at[pl.ds(0, tb)], sem).wait()


def _dispatch(h2d, idx, zero_rows, n_used, n_tiles):
    n = h2d.shape[0]
    tb = MOE_TOKENS_PER_STEP
    assert 2 * n // MOE_TILE + N_EXPERTS == n_tiles
    return pl.pallas_call(
        _dispatch_kernel,
        grid_spec=pltpu.PrefetchScalarGridSpec(
            num_scalar_prefetch=2,
            grid=(n // tb,),
            in_specs=[pl.BlockSpec((1, 1, 2 * tb), lambda i, z, u: (i, 0, 0), memory_space=pltpu.SMEM),
                      pl.BlockSpec((tb, D_MODEL), lambda i, z, u: (i, 0))],
            out_specs=pl.BlockSpec(memory_space=pl.ANY),
            scratch_shapes=[pltpu.VMEM((MOE_TILE, D_MODEL), F32), pltpu.SemaphoreType.DMA(())],
        ),
        out_shape=jax.ShapeDtypeStruct(((n_tiles + 1) * MOE_TILE, D_MODEL), F32),
        compiler_params=pltpu.CompilerParams(dimension_semantics=("arbitrary",), vmem_limit_bytes=VMEM_LIMIT,
                                             disable_bounds_checks=True),
        name="moe_dispatch",
    )(zero_rows, n_used, idx, h2d)


def _expert_kernel(tile_e_ref, n_used_ref, x_ref, g_ref, w1_ref, w3_ref, w2_ref, y_ref):
    used = pl.program_id(0) < n_used_ref[0]

    @pl.when(jnp.logical_not(used))
    def _():
        y_ref[...] = jnp.zeros_like(y_ref)

    @pl.when(used)
    def _():
        x = x_ref[...]
        ms = jnp.mean(x * x, axis=-1, keepdims=True)
        xn = (x * lax.rsqrt(ms + NORM_EPS) * g_ref[...]).astype(BF16)
        a = jnp.dot(xn, w1_ref[...].astype(BF16), preferred_element_type=F32)
        b = jnp.dot(xn, w3_ref[...].astype(BF16), preferred_element_type=F32)
        hmid = (a * jax.nn.sigmoid(a) * b).astype(BF16)
        y_ref[...] = jnp.dot(hmid, w2_ref[...].astype(BF16), preferred_element_type=F32)


def _experts(xs, g, w1, w3, w2, tile_e, n_used, layer, n_tiles):
    row_map = lambda i, te, nu: (i, 0)
    w_map = lambda i, te, nu: (layer, te[i], 0, 0)
    return pl.pallas_call(
        _expert_kernel,
        grid_spec=pltpu.PrefetchScalarGridSpec(
            num_scalar_prefetch=2,
            grid=(n_tiles,),
            in_specs=[pl.BlockSpec((MOE_TILE, D_MODEL), row_map),
                      pl.BlockSpec((None, 1, D_MODEL), lambda i, te, nu: (layer, 0, 0)),
                      pl.BlockSpec((None, None, D_MODEL, D_FF_EXPERT), w_map),
                      pl.BlockSpec((None, None, D_MODEL, D_FF_EXPERT), w_map),
                      pl.BlockSpec((None, None, D_FF_EXPERT, D_MODEL), w_map)],
            out_specs=pl.BlockSpec((MOE_TILE, D_MODEL), row_map),
        ),
        out_shape=jax.ShapeDtypeStruct((n_tiles * MOE_TILE, D_MODEL), F32),
        compiler_params=_cparams("arbitrary"),
        name="moe_experts",
    )(tile_e, n_used, xs, g, w1, w3, w2)


def _combine_kernel(idx_ref, ys_hbm, h_ref, wts_ref, o_ref, buf_ref, sem):
    tb = h_ref.shape[0]

    def row(t, carry):
        pltpu.make_async_copy(ys_hbm.at[pl.ds(idx_ref[0, 0, t], 1)], buf_ref.at[0, pl.ds(t, 1)], sem).start()
        pltpu.make_async_copy(ys_hbm.at[pl.ds(idx_ref[0, 0, tb + t], 1)], buf_ref.at[1, pl.ds(t, 1)], sem).start()
        return carry

    lax.fori_loop(0, tb, row, 0, unroll=8)
    for k in range(2):
        pltpu.make_async_copy(ys_hbm.at[pl.ds(0, tb)], buf_ref.at[k], sem).wait()
    reps = D_MODEL // LANES
    w = wts_ref[...]
    w_top1 = jnp.concatenate([w[:, :LANES]] * reps, axis=1)
    w_top2 = jnp.concatenate([w[:, LANES:]] * reps, axis=1)
    o_ref[...] = h_ref[...] + w_top1 * buf_ref[0] + w_top2 * buf_ref[1]


def _combine(h2d, ys, idx, wts):
    n = h2d.shape[0]
    tb = MOE_TOKENS_PER_STEP
    return pl.pallas_call(
        _combine_kernel,
        grid=(n // tb,),
        in_specs=[pl.BlockSpec((1, 1, 2 * tb), lambda i: (i, 0, 0), memory_space=pltpu.SMEM),
                  pl.BlockSpec(memory_space=pl.ANY),
                  pl.BlockSpec((tb, D_MODEL), lambda i: (i, 0)),
                  pl.BlockSpec((tb, 2 * LANES), lambda i: (i, 0))],
        out_specs=pl.BlockSpec((tb, D_MODEL), lambda i: (i, 0)),
        out_shape=jax.ShapeDtypeStruct((n, D_MODEL), F32),
        scratch_shapes=[pltpu.VMEM((2, tb, D_MODEL), F32), pltpu.SemaphoreType.DMA(())],
        compiler_params=pltpu.CompilerParams(dimension_semantics=("arbitrary",), vmem_limit_bytes=VMEM_LIMIT,
                                             disable_bounds_checks=True),
        name="moe_combine",
    )(idx, ys, h2d, wts)


def _moe(h2d, g, w_router, b_router, w1, w3, w2, layer):
    n = h2d.shape[0]
    route, wts, counts = _router(h2d, g, w_router, b_router, layer)
    idx, tile_e, n_used, zero_rows, n_tiles = _dispatch_tables(route, counts, n)
    xs = _dispatch(h2d, idx, zero_rows, n_used, n_tiles)
    ys = _experts(xs, g, w1, w3, w2, tile_e, n_used, layer, n_tiles)
    return _combine(h2d, ys, idx, wts)


def _norm_kernel(x_ref, g_ref, o_ref):
    x = x_ref[...]
    ms = jnp.mean(x * x, axis=-1, keepdims=True)
    o_ref[...] = x * lax.rsqrt(ms + NORM_EPS) * g_ref[...]


def _final_norm(h2d, g, tm=1024):
    n = h2d.shape[0]
    return pl.pallas_call(
        _norm_kernel,
        grid=(n // tm,),
        in_specs=[pl.BlockSpec((tm, D_MODEL), lambda i: (i, 0)), pl.BlockSpec((1, D_MODEL), lambda i: (0, 0))],
        out_specs=pl.BlockSpec((tm, D_MODEL), lambda i: (i, 0)),
        out_shape=jax.ShapeDtypeStruct((n, D_MODEL), F32),
        compiler_params=_cparams("parallel"),
        name="final_norm",
    )(h2d, g)


def _router_params(wg, bg, we, be):
    n_l = wg.shape[0]
    pad = ROUTER_ROWS - N_GROUPS - N_EXPERTS
    w = jnp.concatenate([wg, we, jnp.zeros((n_l, D_MODEL, pad), F32)], axis=-1).astype(F32).transpose(0, 2, 1)
    b = jnp.concatenate([bg, be, jnp.zeros((n_l, pad), F32)], axis=-1).astype(F32)
    return w, jnp.broadcast_to(b[:, :, None], (n_l, ROUTER_ROWS, LANES))


def kernel(x, w_in, na_rpb, lam_q1, lam_k1, lam_q2, lam_k2, diff_subln, w_pa, w_pb, w_pc, w_o, norm_mix, norm_ffn,
           router_group_w, router_group_b, router_expert_w, router_expert_b, w1, w3, w2, norm_final):
    B, T, D = x.shape
    depth = w_in.shape[0]
    rows = T // GRID_W
    wr = min(NA_WIN_R, rows)
    rope_tab = _rope_table(T)
    row3 = lambda a: a[:, None, :]
    w_in_bf = _permute_cols(w_in).astype(BF16)
    bias_tab = _na_bias_table(na_rpb, wr)
    g_mix, g_ffn = row3(norm_mix), row3(norm_ffn)
    lams = [row3(a) for a in (lam_q1, lam_k1, lam_q2, lam_k2)]
    subln = row3(diff_subln)
    wpa, wpb, wpc, wo = (a.astype(BF16) for a in (w_pa, w_pb, w_pc, w_o))
    w_router, b_router = _router_params(router_group_w, router_group_b, router_expert_w, router_expert_b)
    h = x.reshape(B * T, D)
    for l in range(depth):
        lam_init = 0.8 - 0.6 * math.exp(-0.3 * l)
        proj, *cgs = _inproj(h, g_mix, w_in_bf, rope_tab, l, B, T)
        ya = _na_attention(proj, bias_tab, l, B, T)
        yb = _diff_attention(proj, *lams, subln, l, lam_init, B, T)
        outs, lses = _dilated_branch(cgs)
        h = _merge(h, proj, ya, yb, outs, lses, wpa, wpb, wpc, wo, l)
        h = _moe(h, g_ffn, w_router, b_router, w1, w3, w2, l)
    return _final_norm(h, norm_final[None, :]).reshape(B, T, D)
```

```python
import functools
import math

import jax
import jax.numpy as jnp
import numpy as np
from jax import lax
from jax.experimental import pallas as pl
from jax.experimental.pallas import tpu as pltpu

F32 = jnp.float32
BF16 = jnp.bfloat16

D_MODEL = 1024
HEAD_DIM = 64
ROPE_DIM = 16
ROPE_THETA = 500000.0
GRID_W = 64
NA_HEADS = 8
NA_WIN_R = 8
NA_WIN_C = 16
DIFF_HEADS = 4
DIL_PATTERNS = ((128, 1), (512, 4), (2048, 16))
N_GROUPS = 4
EXPERTS_PER_GROUP = 4
N_EXPERTS = 16
D_FF_EXPERT = D_MODEL // 2
NORM_EPS = 1e-6
SUBLN_EPS = 1e-5
NEG_INF = -1e30

LANES = 128
MXU_N = 256
VMEM_LIMIT = 56 * 1024 * 1024

A_W, B_W, C_W = 512, 512, 768
N_STEPS = len(DIL_PATTERNS)
GROUP_W = C_W // N_STEPS
_MAIN_ORDER = ("qb", "kb", "va", "qa", "ga", "gb", "gc", "ka", "vb")
_REF_ORDER = ("qa", "ka", "va", "qb", "kb", "vb", "qc", "kc", "vc", "ga", "gb", "gc")
_WIDTH = dict(qa=A_W, ka=A_W, va=A_W, qb=B_W, kb=B_W, vb=B_W, qc=C_W, kc=C_W, vc=C_W,
              ga=D_MODEL, gb=D_MODEL, gc=D_MODEL)


def _offsets(order):
    off, out = 0, {}
    for name in order:
        out[name] = off
        off += _WIDTH[name]
    return out, off


_OFF, MAIN_W = _offsets(_MAIN_ORDER)
_REF_OFF, IN_W = _offsets(_REF_ORDER)
OFF_QA, OFF_KA, OFF_VA = _OFF["qa"], _OFF["ka"], _OFF["va"]
OFF_QB, OFF_KB, OFF_VB = _OFF["qb"], _OFF["kb"], _OFF["vb"]
OFF_GATE = _OFF["ga"]
MAIN_STEP_W = MAIN_W // N_STEPS
STEP_W = MAIN_STEP_W + 3 * GROUP_W
MAIN_ROPE_W = 2 * B_W
QB_SCALE = HEAD_DIM ** -0.5 * math.log2(math.e)
assert OFF_GATE % D_MODEL == 0 and OFF_QB == 0 and OFF_KB == B_W and MAIN_ROPE_W <= MAIN_STEP_W
assert MAIN_W % N_STEPS == 0 and MAIN_STEP_W % MXU_N == 0 and N_STEPS * STEP_W == IN_W


def _permute_cols(w):
    cols = {n: w[..., _REF_OFF[n]:_REF_OFF[n] + _WIDTH[n]] for n in _MAIN_ORDER}
    cols["qb"] = cols["qb"] * QB_SCALE
    main = jnp.concatenate([cols[n] for n in _MAIN_ORDER], axis=-1)
    parts = []
    for s in range(N_STEPS):
        parts.append(main[..., s * MAIN_STEP_W:(s + 1) * MAIN_STEP_W])
        for n in ("qc", "kc", "vc"):
            parts.append(w[..., _REF_OFF[n] + s * GROUP_W:_REF_OFF[n] + (s + 1) * GROUP_W])
    return jnp.concatenate(parts, axis=-1)


def _cparams(*sem):
    return pltpu.CompilerParams(dimension_semantics=sem, vmem_limit_bytes=VMEM_LIMIT)


def _inproj_kernel(x_ref, g_ref, w_ref, rope_ref, main_ref, c0_ref, c1_ref, c2_ref, xn_ref, y_ref, *, tm):
    j = pl.program_id(1)
    c_refs = (c0_ref, c1_ref, c2_ref)
    reps = MXU_N // LANES
    half = ROPE_DIM // 2
    assert GROUP_W == MXU_N

    def rope(y):
        cos = jnp.concatenate([rope_ref[0]] * reps, axis=1)
        s_up = jnp.concatenate([rope_ref[1]] * reps, axis=1)
        s_dn = jnp.concatenate([rope_ref[2]] * reps, axis=1)
        return y * cos + pltpu.roll(y, MXU_N - half, 1) * s_up + pltpu.roll(y, half, 1) * s_dn

    def step(s):
        xn = xn_ref[...]
        for c in range(MAIN_STEP_W // MXU_N):
            sl = slice(c * MXU_N, (c + 1) * MXU_N)
            y = jnp.dot(xn, w_ref[:, sl], preferred_element_type=F32)
            if s == 0 and c < MAIN_ROPE_W // MXU_N:
                y = rope(y)
            main_ref[:, sl] = y.astype(main_ref.dtype)
        dil = DIL_PATTERNS[s][1]
        for c in range(3):
            wsl = slice(MAIN_STEP_W + c * GROUP_W, MAIN_STEP_W + (c + 1) * GROUP_W)
            osl = slice(c * GROUP_W, (c + 1) * GROUP_W)
            y = jnp.dot(xn, w_ref[:, wsl], preferred_element_type=F32)
            if c < 2:
                y = rope(y)
            if dil == 1:
                c_refs[s][0, :, osl] = y.astype(BF16)
            else:
                for hb in range(reps):
                    y_ref[c, hb] = y[:, hb * LANES:(hb + 1) * LANES]
                for p in range(dil):
                    for hb in range(reps):
                        c_refs[s][p, :, c * GROUP_W + hb * LANES:c * GROUP_W + (hb + 1) * LANES] = (
                            y_ref[c, hb, pl.ds(p, tm // dil, stride=dil), :].astype(BF16))

    @pl.when(j == 0)
    def _():
        x = x_ref[...]
        ms = jnp.mean(x * x, axis=-1, keepdims=True)
        xn_ref[...] = (x * lax.rsqrt(ms + NORM_EPS) * g_ref[...]).astype(BF16)
        step(0)

    for s in range(1, N_STEPS):
        pl.when(j == s)(functools.partial(step, s))


def _inproj(h2d, g, w_bf, rope_tab, layer, B, T, tm=1024):
    n = h2d.shape[0]
    tpb = T // tm
    c_specs, c_shapes = [], []
    for _, dil in DIL_PATTERNS:
        assert tm % dil == 0
        c_specs.append(pl.BlockSpec((None, dil, tm // dil, 3 * GROUP_W), lambda i, j: (i // tpb, 0, i % tpb, 0)))
        c_shapes.append(jax.ShapeDtypeStruct((B, dil, T // dil, 3 * GROUP_W), BF16))
    return pl.pallas_call(
        functools.partial(_inproj_kernel, tm=tm),
        grid=(n // tm, N_STEPS),
        in_specs=[
            pl.BlockSpec((tm, D_MODEL), lambda i, j: (i, 0)),
            pl.BlockSpec((None, 1, D_MODEL), lambda i, j: (layer, 0, 0)),
            pl.BlockSpec((None, D_MODEL, STEP_W), lambda i, j: (layer, 0, j)),
            pl.BlockSpec((3, tm, LANES), lambda i, j: (0, i % tpb, 0)),
        ],
        out_specs=[pl.BlockSpec((tm, MAIN_STEP_W), lambda i, j: (i, j))] + c_specs,
        out_shape=[jax.ShapeDtypeStruct((n, MAIN_W), BF16)] + c_shapes,
        scratch_shapes=[pltpu.VMEM((tm, D_MODEL), BF16), pltpu.VMEM((3, MXU_N // LANES, tm, LANES), F32)],
        compiler_params=_cparams("parallel", "arbitrary"),
        name="inproj",
    )(h2d, g, w_bf, rope_tab)


def _rope_table(T):
    half = ROPE_DIM // 2
    inv = 1.0 / (ROPE_THETA ** (jnp.arange(0, ROPE_DIM, 2, dtype=F32) / ROPE_DIM))
    ang = jnp.arange(T, dtype=F32)[:, None] * inv[None, :]
    cos, sin = jnp.cos(ang), jnp.sin(ang)
    zeros = jnp.zeros((T, HEAD_DIM - ROPE_DIM), F32)
    z8 = jnp.zeros((T, half), F32)
    c64 = jnp.concatenate([cos, cos, zeros + 1.0], axis=1)
    up64 = jnp.concatenate([-sin, z8, zeros], axis=1)
    dn64 = jnp.concatenate([z8, sin, zeros], axis=1)
    reps = LANES // HEAD_DIM
    return jnp.stack([jnp.tile(c64, (1, reps)), jnp.tile(up64, (1, reps)), jnp.tile(dn64, (1, reps))])


NA_ROWS_PER_ITER = 8


def _na_kernel(q_ref, k_ref, v_ref, b_ref, o_ref, *, rows, wr):
    lane = lax.broadcasted_iota(jnp.int32, (GRID_W, LANES), 1)
    lo = lane < HEAD_DIM
    scale = HEAD_DIM ** -0.5

    def row_group(gi, carry):
        scores, windows = [], []
        for u in range(NA_ROWS_PER_ITER):
            r = gi * NA_ROWS_PER_ITER + u
            r0 = jnp.clip(r - wr // 2, 0, rows - wr)
            d0 = r0 - r + NA_WIN_R - 1 - (NA_WIN_R - wr)
            q = q_ref[pl.ds(pl.multiple_of(r * GRID_W, GRID_W), GRID_W), :] * scale
            ks = pl.multiple_of(r0 * GRID_W, GRID_W)
            kw = k_ref[pl.ds(ks, wr * GRID_W), :]
            windows.append(ks)
            for hh in range(2):
                qh = jnp.where(lo if hh == 0 else jnp.logical_not(lo), q, jnp.zeros_like(q))
                s = lax.dot_general(qh, kw, (((1,), (1,)), ((), ())), preferred_element_type=F32)
                scores.append(s + b_ref[hh, d0])
        probs, sums = [], []
        for s in scores:
            p = jnp.exp(s - jnp.max(s, axis=-1, keepdims=True))
            sums.append(jnp.sum(p, axis=-1, keepdims=True))
            probs.append(p.astype(BF16))
        for u in range(NA_ROWS_PER_ITER):
            r = gi * NA_ROWS_PER_ITER + u
            vw = v_ref[pl.ds(windows[u], wr * GRID_W), :]
            outs = [jnp.dot(probs[2 * u + hh], vw, preferred_element_type=F32) / sums[2 * u + hh] for hh in range(2)]
            o = jnp.where(lo, outs[0], outs[1])
            o_ref[pl.ds(pl.multiple_of(r * GRID_W, GRID_W), GRID_W), :] = o.astype(o_ref.dtype)
        return carry

    lax.fori_loop(0, rows // NA_ROWS_PER_ITER, row_group, 0)


def _na_bias_table(rpb, wr):
    qc = np.arange(GRID_W)[:, None]
    kc = np.arange(GRID_W)[None, :]
    c0 = np.clip(qc - NA_WIN_C // 2, 0, GRID_W - NA_WIN_C)
    ok = (kc >= c0) & (kc < c0 + NA_WIN_C)
    dc = np.clip(kc - qc + NA_WIN_C - 1, 0, 2 * NA_WIN_C - 2)
    onehot = (np.arange(2 * NA_WIN_C - 1)[:, None, None] == dc[None]).astype(np.float32)
    b = jnp.einsum("lhrd,dqk->lhrqk", rpb.astype(F32), onehot, precision=lax.Precision.HIGHEST)
    b = jnp.where(jnp.asarray(ok), b, NEG_INF)
    row_idx = (NA_WIN_R - wr) + np.arange(wr)[:, None] + np.arange(wr)[None, :]
    t = b[:, :, row_idx]
    n_l, n_h = rpb.shape[0], rpb.shape[1]
    return t.transpose(0, 1, 2, 4, 3, 5).reshape(n_l, n_h, wr, GRID_W, wr * GRID_W)


def _na_attention(proj, bias_tab, layer, B, T):
    rows = T // GRID_W
    wr = min(NA_WIN_R, rows)
    cq, ck, cv = OFF_QA // LANES, OFF_KA // LANES, OFF_VA // LANES
    return pl.pallas_call(
        functools.partial(_na_kernel, rows=rows, wr=wr),
        grid=(B, NA_HEADS // 2),
        in_specs=[
            pl.BlockSpec((T, LANES), lambda b, h: (b, cq + h)),
            pl.BlockSpec((T, LANES), lambda b, h: (b, ck + h)),
            pl.BlockSpec((T, LANES), lambda b, h: (b, cv + h)),
            pl.BlockSpec((None, 2, wr, GRID_W, wr * GRID_W), lambda b, h: (layer, h, 0, 0, 0)),
        ],
        out_specs=pl.BlockSpec((T, LANES), lambda b, h: (b, h)),
        out_shape=jax.ShapeDtypeStruct((B * T, A_W), BF16),
        compiler_params=_cparams("parallel", "arbitrary"),
        name="na_attn",
    )(proj, proj, proj, bias_tab)


DIFF_KEY_CHUNK = 512


def _diff_kernel(lq1_ref, lk1_ref, lq2_ref, lk2_ref, q_ref, k_ref, v_ref, g_ref, o_ref, vt_ref, *, lam_init, tq):
    lam = (jnp.exp(jnp.sum(lq1_ref[...] * lk1_ref[...], keepdims=True))
           - jnp.exp(jnp.sum(lq2_ref[...] * lk2_ref[...], keepdims=True)) + lam_init)
    T = k_ref.shape[0]
    ck = DIFF_KEY_CHUNK
    n_chunks = T // ck
    lane = lax.broadcasted_iota(jnp.int32, (tq, LANES), 1)
    lo = lane < HEAD_DIM
    nt = (((1,), (1,)), ((), ()))
    vt_ref[...] = v_ref[...].T

    def q_block(i, carry):
        rows = pl.ds(pl.multiple_of(i * tq, tq), tq)
        q = q_ref[rows, :]
        zero = jnp.zeros_like(q)
        qs = (jnp.where(lo, q, zero), jnp.where(lo, zero, q))

        def scores(c):
            kc = k_ref[c * ck:(c + 1) * ck, :]
            return [lax.dot_general(kc, qm, nt, preferred_element_type=F32) for qm in qs]

        m = [jnp.full((1, tq), NEG_INF, F32)] * 2
        l = [jnp.zeros((1, tq), F32)] * 2
        acc = [jnp.zeros((LANES, tq), F32)] * 2
        s_next = scores(0)
        for c in range(n_chunks):
            s_cur = s_next
            if c + 1 < n_chunks:
                s_next = scores(c + 1)
            vt = vt_ref[:, c * ck:(c + 1) * ck]
            for j in range(2):
                m_new = jnp.maximum(m[j], jnp.max(s_cur[j], axis=0, keepdims=True))
                alpha = jnp.exp2(m[j] - m_new)
                p = jnp.exp2(s_cur[j] - m_new)
                l[j] = alpha * l[j] + jnp.sum(p, axis=0, keepdims=True)
                acc[j] = alpha * acc[j] + jnp.dot(vt, p.astype(BF16), preferred_element_type=F32)
                m[j] = m_new
        ot = acc[0] / l[0] - lam * (acc[1] / l[1])
        o = ot.T
        ms = jnp.mean(o * o, axis=-1, keepdims=True)
        o = o * lax.rsqrt(ms + SUBLN_EPS) * g_ref[...] * (1.0 - lam_init)
        o_ref[rows, :] = o.astype(o_ref.dtype)
        return carry

    lax.fori_loop(0, q_ref.shape[0] // tq, q_block, 0, unroll=2)


def _diff_attention(proj, lq1, lk1, lq2, lk2, subln_g, layer, lam_init, B, T, tq=512):
    cq, ck, cv = OFF_QB // LANES, OFF_KB // LANES, OFF_VB // LANES
    vec = pl.BlockSpec((None, 1, HEAD_DIM), lambda b, h: (layer, 0, 0))
    return pl.pallas_call(
        functools.partial(_diff_kernel, lam_init=lam_init, tq=tq),
        grid=(B, DIFF_HEADS),
        in_specs=[
            vec, vec, vec, vec,
            pl.BlockSpec((T, LANES), lambda b, h: (b, cq + h)),
            pl.BlockSpec((T, LANES), lambda b, h: (b, ck + h)),
            pl.BlockSpec((T, LANES), lambda b, h: (b, cv + h)),
            pl.BlockSpec((None, 1, LANES), lambda b, h: (layer, 0, 0)),
        ],
        out_specs=pl.BlockSpec((T, LANES), lambda b, h: (b, h)),
        out_shape=jax.ShapeDtypeStruct((B * T, B_W), BF16),
        scratch_shapes=[pltpu.VMEM((LANES, T), BF16)],
        compiler_params=_cparams("parallel", "arbitrary"),
        name="diff_attn",
    )(lq1, lk1, lq2, lk2, proj, proj, proj, subln_g)


DIL_Q = 128


DIL_BLOCKS_PER_ITER = 4


def _dil_kernel(q_ref, k_ref, v_ref, o_ref, lse_ref, *, L, dil, radius):
    kw_len = DIL_Q + 2 * radius
    lane = lax.broadcasted_iota(jnp.int32, (DIL_Q, LANES), 1)
    lo = lane < HEAD_DIM
    rel = (lax.broadcasted_iota(jnp.int32, (DIL_Q, kw_len), 1)
           - lax.broadcasted_iota(jnp.int32, (DIL_Q, kw_len), 0))
    scale = HEAD_DIM ** -0.5
    nt = (((1,), (1,)), ((), ()))

    blocks_per_phase = L // DIL_Q

    def blocks(gi, carry):
        scores, starts, maxes = [], [], []
        for u in range(DIL_BLOCKS_PER_ITER):
            item = gi * DIL_BLOCKS_PER_ITER + u
            p = item // blocks_per_phase
            l0 = pl.multiple_of((item % blocks_per_phase) * DIL_Q, DIL_Q)
            ks = pl.multiple_of(jnp.clip(l0 - radius, 0, L - kw_len), radius)
            q = q_ref[p, pl.ds(l0, DIL_Q), :] * scale
            kw = k_ref[p, pl.ds(ks, kw_len), :]
            ok = jnp.abs(rel + (ks - l0)) <= radius
            starts.append((p, l0, ks))
            for hh in range(2):
                qh = jnp.where(lo if hh == 0 else jnp.logical_not(lo), q, jnp.zeros_like(q))
                s = lax.dot_general(qh, kw, nt, preferred_element_type=F32)
                scores.append(jnp.where(ok, s, NEG_INF))
        probs, sums = [], []
        for s in scores:
            m = jnp.max(s, axis=-1, keepdims=True)
            e = jnp.exp(s - m)
            maxes.append(m)
            sums.append(jnp.sum(e, axis=-1, keepdims=True))
            probs.append(e.astype(BF16))
        for u in range(DIL_BLOCKS_PER_ITER):
            p, l0, ks = starts[u]
            vw = v_ref[p, pl.ds(ks, kw_len), :]
            outs = [jnp.dot(probs[2 * u + hh], vw, preferred_element_type=F32) / sums[2 * u + hh] for hh in range(2)]
            lses = [maxes[2 * u + hh] + jnp.log(sums[2 * u + hh]) for hh in range(2)]
            o = jnp.where(lo, outs[0], outs[1])
            lse = jnp.where(lo, lses[0], lses[1])
            if dil == 1:
                o_ref[pl.ds(l0, DIL_Q), :] = o
                lse_ref[pl.ds(l0, DIL_Q), :] = lse
            else:
                o_ref[pl.ds(l0 * dil + p, DIL_Q, stride=dil), :] = o
                lse_ref[pl.ds(l0 * dil + p, DIL_Q, stride=dil), :] = lse
        return carry

    lax.fori_loop(0, dil * blocks_per_phase // DIL_BLOCKS_PER_ITER, blocks, 0)


def _dilated_group(cg, radius):
    B, dil, L, _ = cg.shape
    assert L >= DIL_Q + 2 * radius and L % DIL_Q == 0 and (dil * L // DIL_Q) % DIL_BLOCKS_PER_ITER == 0
    hp = GROUP_W // LANES
    in_spec = lambda c: pl.BlockSpec((None, dil, L, LANES), lambda b, h: (b, 0, 0, c * hp + h))
    out_spec = pl.BlockSpec((L * dil, LANES), lambda b, h: (b, h))
    out_shape = jax.ShapeDtypeStruct((B * L * dil, GROUP_W), F32)
    return pl.pallas_call(
        functools.partial(_dil_kernel, L=L, dil=dil, radius=radius),
        grid=(B, hp),
        in_specs=[in_spec(0), in_spec(1), in_spec(2)],
        out_specs=[out_spec, out_spec],
        out_shape=[out_shape, out_shape],
        compiler_params=_cparams("parallel", "arbitrary"),
        name="dil_attn",
    )(cg, cg, cg)


def _dilated_branch(cgs):
    outs, lses = [], []
    for cg, (window, dil) in zip(cgs, DIL_PATTERNS):
        o, lse = _dilated_group(cg, window // (2 * dil))
        outs.append(o)
        lses.append(lse)
    return outs, lses


def _merge_kernel(h_ref, ga_ref, gb_ref, gc_ref, ya_ref, yb_ref, o0_ref, o1_ref, o2_ref,
                  l0_ref, l1_ref, l2_ref, wpa_ref, wpb_ref, wpc_ref, wo_ref, out_ref):
    l0, l1, l2 = l0_ref[...], l1_ref[...], l2_ref[...]
    m = jnp.maximum(jnp.maximum(l0, l1), l2)
    e0, e1, e2 = jnp.exp(l0 - m), jnp.exp(l1 - m), jnp.exp(l2 - m)
    yc = (o0_ref[...].astype(F32) * e0 + o1_ref[...].astype(F32) * e1 + o2_ref[...].astype(F32) * e2) / (e0 + e1 + e2)
    merged = jax.nn.sigmoid(ga_ref[...].astype(F32)) * jnp.dot(ya_ref[...], wpa_ref[...], preferred_element_type=F32)
    merged += jax.nn.sigmoid(gb_ref[...].astype(F32)) * jnp.dot(yb_ref[...], wpb_ref[...], preferred_element_type=F32)
    merged += jax.nn.sigmoid(gc_ref[...].astype(F32)) * jnp.dot(yc.astype(BF16), wpc_ref[...],
                                                                preferred_element_type=F32)
    out_ref[...] = h_ref[...] + jnp.dot(merged.astype(BF16), wo_ref[...], preferred_element_type=F32)


def _merge(h2d, proj, ya, yb, outs, lses, wpa, wpb, wpc, wo, layer, tm=512):
    n = h2d.shape[0]
    gw = GROUP_W
    row = lambda w: pl.BlockSpec((tm, w), lambda i: (i, 0))
    full = lambda a: pl.BlockSpec((None,) + a.shape[1:], lambda i: (layer, 0, 0))
    gate = lambda c: pl.BlockSpec((tm, D_MODEL), lambda i: (i, OFF_GATE // D_MODEL + c))
    return pl.pallas_call(
        _merge_kernel,
        grid=(n // tm,),
        in_specs=[row(D_MODEL), gate(0), gate(1), gate(2), row(A_W), row(B_W),
                  row(gw), row(gw), row(gw), row(gw), row(gw), row(gw),
                  full(wpa), full(wpb), full(wpc), full(wo)],
        out_specs=row(D_MODEL),
        out_shape=jax.ShapeDtypeStruct((n, D_MODEL), F32),
        compiler_params=_cparams("parallel"),
        name="merge_outproj",
    )(h2d, proj, proj, proj, ya, yb, *outs, *lses, wpa, wpb, wpc, wo)


MOE_TILE = 512
MOE_TOKENS_PER_STEP = 512
ROUTE_ROWS = 8


ROUTER_ROWS = 32


def _router_kernel(h_ref, g_ref, wr_ref, br_ref, upper_ref, route_ref, wts_ref, cnt_ref, base_ref):
    i = pl.program_id(0)
    tm = h_ref.shape[0]
    lane_reps = tm // LANES
    row = lax.broadcasted_iota(jnp.int32, (ROUTER_ROWS, tm), 0)
    rowf = row.astype(F32)
    big = float(ROUTER_ROWS)
    nt = (((1,), (1,)), ((), ()))

    @pl.when(i == 0)
    def _():
        base_ref[...] = jnp.zeros_like(base_ref)

    x = h_ref[...]
    ms = jnp.mean(x * x, axis=-1, keepdims=True)
    xn = x * lax.rsqrt(ms + NORM_EPS) * g_ref[...]
    logits = lax.dot_general(wr_ref[...], xn, nt, preferred_element_type=F32, precision=lax.Precision.HIGHEST)
    logits = logits + jnp.concatenate([br_ref[...]] * lane_reps, axis=1)
    gl = jnp.where(row < N_GROUPS, logits, -jnp.inf)
    gmax = jnp.max(gl, axis=0, keepdims=True)
    g_sel = jnp.min(jnp.where(gl == gmax, rowf, big), axis=0, keepdims=True).astype(jnp.int32)
    g_gate = 1.0 / jnp.sum(jnp.exp(gl - gmax), axis=0, keepdims=True)
    eidx = row - N_GROUPS
    in_grp = (eidx >= g_sel * EXPERTS_PER_GROUP) & (eidx < (g_sel + 1) * EXPERTS_PER_GROUP)
    el = jnp.where(in_grp, logits, -jnp.inf)
    t1 = jnp.max(el, axis=0, keepdims=True)
    i1 = jnp.min(jnp.where(el == t1, rowf, big), axis=0, keepdims=True).astype(jnp.int32)
    el2 = jnp.where(row == i1, -jnp.inf, el)
    t2 = jnp.max(el2, axis=0, keepdims=True)
    i2 = jnp.min(jnp.where(el2 == t2, rowf, big), axis=0, keepdims=True).astype(jnp.int32)
    x2 = jnp.exp(t2 - t1)
    den = 1.0 + x2
    wts_ref[:, :LANES] = jnp.broadcast_to(g_gate / den, (LANES, tm)).T
    wts_ref[:, LANES:] = jnp.broadcast_to(g_gate * x2 / den, (LANES, tm)).T
    member = jnp.where(row == i1, 1.0, 0.0) + jnp.where(row == i2, 1.0, 0.0)
    earlier = jnp.dot(member.astype(BF16), upper_ref[...], preferred_element_type=F32)
    earlier = earlier + jnp.concatenate([base_ref[...]] * lane_reps, axis=1)
    rank1 = jnp.sum(jnp.where(row == i1, earlier, 0.0), axis=0, keepdims=True)
    rank2 = jnp.sum(jnp.where(row == i2, earlier, 0.0), axis=0, keepdims=True)
    base_ref[...] = base_ref[...] + jnp.sum(member, axis=1, keepdims=True)
    cnt_ref[...] = base_ref[...]
    out_row = lax.broadcasted_iota(jnp.int32, (ROUTE_ROWS, tm), 0)
    route_ref[...] = jnp.where(out_row == 0, (i1 - N_GROUPS).astype(F32),
                               jnp.where(out_row == 1, (i2 - N_GROUPS).astype(F32),
                                         jnp.where(out_row == 2, rank1, jnp.where(out_row == 3, rank2, 0.0))))


def _router(h2d, g, w_router, b_router, layer, tm=1024):
    n = h2d.shape[0]
    return pl.pallas_call(
        _router_kernel,
        grid=(n // tm,),
        in_specs=[
            pl.BlockSpec((tm, D_MODEL), lambda i: (i, 0)),
            pl.BlockSpec((None, 1, D_MODEL), lambda i: (layer, 0, 0)),
            pl.BlockSpec((None, ROUTER_ROWS, D_MODEL), lambda i: (layer, 0, 0)),
            pl.BlockSpec((None, ROUTER_ROWS, LANES), lambda i: (layer, 0, 0)),
            pl.BlockSpec((tm, tm), lambda i: (0, 0)),
        ],
        out_specs=[pl.BlockSpec((ROUTE_ROWS, tm), lambda i: (0, i)),
                   pl.BlockSpec((tm, 2 * LANES), lambda i: (i, 0)),
                   pl.BlockSpec((ROUTER_ROWS, LANES), lambda i: (0, 0))],
        out_shape=[jax.ShapeDtypeStruct((ROUTE_ROWS, n), F32), jax.ShapeDtypeStruct((n, 2 * LANES), F32),
                   jax.ShapeDtypeStruct((ROUTER_ROWS, LANES), F32)],
        scratch_shapes=[pltpu.VMEM((ROUTER_ROWS, LANES), F32)],
        compiler_params=_cparams("arbitrary"),
        name="moe_router",
    )(h2d, g, w_router, b_router, jnp.triu(jnp.ones((tm, tm), BF16), k=1))


def _dispatch_tables(route, counts, n):
    cnt = counts[N_GROUPS:N_GROUPS + N_EXPERTS, 0].astype(jnp.int32)
    padded = (cnt + MOE_TILE - 1) // MOE_TILE * MOE_TILE
    ends = jnp.cumsum(padded)
    starts = ends - padded
    eids = jnp.arange(N_EXPERTS, dtype=jnp.int32)
    pos = []
    for k in range(2):
        e = route[k].astype(jnp.int32)
        start_e = jnp.sum(jnp.where(e[:, None] == eids[None, :], starts[None, :], 0), axis=1)
        pos.append(start_e + route[2 + k].astype(jnp.int32))
    tb = MOE_TOKENS_PER_STEP
    idx = jnp.concatenate([pos[0].reshape(n // tb, 1, tb), pos[1].reshape(n // tb, 1, tb)], axis=-1)
    n_tiles = 2 * n // MOE_TILE + N_EXPERTS
    tile_ids = jnp.arange(n_tiles, dtype=jnp.int32)
    n_used = ends[-1] // MOE_TILE
    tile_e = jnp.sum((tile_ids[:, None] * MOE_TILE >= ends[None, :]).astype(jnp.int32), axis=1)
    last_e = jnp.max(jnp.where(tile_ids < n_used, tile_e, 0))
    tile_e = jnp.where(tile_ids < n_used, tile_e, last_e).astype(jnp.int32)
    zero_rows = jnp.where(padded > 0, ends - MOE_TILE, n_tiles * MOE_TILE).astype(jnp.int32)
    return idx, tile_e, n_used.reshape(1).astype(jnp.int32), zero_rows, n_tiles


def _dispatch_kernel(zero_rows_ref, n_used_ref, idx_ref, h_ref, xs_hbm, zeros_ref, sem):
    i = pl.program_id(0)
    tb = MOE_TOKENS_PER_STEP
    n_tiles = xs_hbm.shape[0] // MOE_TILE - 1

    def zero_tile(row0):
        return pltpu.make_async_copy(zeros_ref, xs_hbm.at[pl.ds(pl.multiple_of(row0, MOE_TILE), MOE_TILE)], sem)

    @pl.when(i == 0)
    def _():
        zeros_ref[...] = jnp.zeros_like(zeros_ref)
        fills = [zero_tile(zero_rows_ref[e]) for e in range(N_EXPERTS)]
        for c in fills:
            c.start()
        for c in fills:
            c.wait()

        def tail(j, carry):
            tile = n_used_ref[0] + j

            @pl.when(tile <= n_tiles)
            def _():
                c = zero_tile(tile * MOE_TILE)
                c.start()
                c.wait()
            return carry

        lax.fori_loop(0, N_EXPERTS + 1, tail, 0)

    def row(t, carry):
        src = h_ref.at[pl.ds(t, 1)]
        pltpu.make_async_copy(src, xs_hbm.at[pl.ds(idx_ref[0, 0, t], 1)], sem).start()
        pltpu.make_async_copy(src, xs_hbm.at[pl.ds(idx_ref[0, 0, tb + t], 1)], sem).start()
        return carry

    lax.fori_loop(0, tb, row, 0, unroll=8)
    for _ in range(2):
        pltpu.make_async_copy(h_ref, xs_hbm.at[pl.ds(0, tb)], sem).wait()


def _dispatch(h2d, idx, zero_rows, n_used, n_tiles):
    n = h2d.shape[0]
    tb = MOE_TOKENS_PER_STEP
    assert 2 * n // MOE_TILE + N_EXPERTS == n_tiles
    return pl.pallas_call(
        _dispatch_kernel,
        grid_spec=pltpu.PrefetchScalarGridSpec(
            num_scalar_prefetch=2,
            grid=(n // tb,),
            in_specs=[pl.BlockSpec((1, 1, 2 * tb), lambda i, z, u: (i, 0, 0), memory_space=pltpu.SMEM),
                      pl.BlockSpec((tb, D_MODEL), lambda i, z, u: (i, 0))],
            out_specs=pl.BlockSpec(memory_space=pl.ANY),
            scratch_shapes=[pltpu.VMEM((MOE_TILE, D_MODEL), F32), pltpu.SemaphoreType.DMA(())],
        ),
        out_shape=jax.ShapeDtypeStruct(((n_tiles + 1) * MOE_TILE, D_MODEL), F32),
        compiler_params=pltpu.CompilerParams(dimension_semantics=("arbitrary",), vmem_limit_bytes=VMEM_LIMIT,
                                             disable_bounds_checks=True),
        name="moe_dispatch",
    )(zero_rows, n_used, idx, h2d)


def _expert_kernel(tile_e_ref, n_used_ref, x_ref, g_ref, w1_ref, w3_ref, w2_ref, y_ref, w1b_ref, w3b_ref, w2b_ref):
    i = pl.program_id(0)
    used = i < n_used_ref[0]

    @pl.when(jnp.logical_not(used))
    def _():
        y_ref[...] = jnp.zeros_like(y_ref)

    @pl.when(used & ((i == 0) | (tile_e_ref[i] != tile_e_ref[jnp.maximum(i - 1, 0)])))
    def _():
        w1b_ref[...] = w1_ref[...].astype(BF16)
        w3b_ref[...] = w3_ref[...].astype(BF16)
        w2b_ref[...] = w2_ref[...].astype(BF16)

    @pl.when(used)
    def _():
        x = x_ref[...]
        ms = jnp.mean(x * x, axis=-1, keepdims=True)
        xn = (x * lax.rsqrt(ms + NORM_EPS) * g_ref[...]).astype(BF16)
        a = jnp.dot(xn, w1b_ref[...], preferred_element_type=F32)
        b = jnp.dot(xn, w3b_ref[...], preferred_element_type=F32)
        hmid = (a * jax.nn.sigmoid(a) * b).astype(BF16)
        y_ref[...] = jnp.dot(hmid, w2b_ref[...], preferred_element_type=F32)


def _experts(xs, g, w1, w3, w2, tile_e, n_used, layer, n_tiles):
    row_map = lambda i, te, nu: (i, 0)
    w_map = lambda i, te, nu: (layer, te[i], 0, 0)
    return pl.pallas_call(
        _expert_kernel,
        grid_spec=pltpu.PrefetchScalarGridSpec(
            num_scalar_prefetch=2,
            grid=(n_tiles,),
            in_specs=[pl.BlockSpec((MOE_TILE, D_MODEL), row_map),
                      pl.BlockSpec((None, 1, D_MODEL), lambda i, te, nu: (layer, 0, 0)),
                      pl.BlockSpec((None, None, D_MODEL, D_FF_EXPERT), w_map),
                      pl.BlockSpec((None, None, D_MODEL, D_FF_EXPERT), w_map),
                      pl.BlockSpec((None, None, D_FF_EXPERT, D_MODEL), w_map)],
            out_specs=pl.BlockSpec((MOE_TILE, D_MODEL), row_map),
            scratch_shapes=[pltpu.VMEM((D_MODEL, D_FF_EXPERT), BF16), pltpu.VMEM((D_MODEL, D_FF_EXPERT), BF16),
                            pltpu.VMEM((D_FF_EXPERT, D_MODEL), BF16)],
        ),
        out_shape=jax.ShapeDtypeStruct((n_tiles * MOE_TILE, D_MODEL), F32),
        compiler_params=_cparams("arbitrary"),
        name="moe_experts",
    )(tile_e, n_used, xs, g, w1, w3, w2)


def _combine_kernel(idx_ref, ys_hbm, h_ref, wts_ref, o_ref, buf_ref, sem):
    tb = h_ref.shape[0]

    def row(t, carry):
        pltpu.make_async_copy(ys_hbm.at[pl.ds(idx_ref[0, 0, t], 1)], buf_ref.at[0, pl.ds(t, 1)], sem).start()
        pltpu.make_async_copy(ys_hbm.at[pl.ds(idx_ref[0, 0, tb + t], 1)], buf_ref.at[1, pl.ds(t, 1)], sem).start()
        return carry

    lax.fori_loop(0, tb, row, 0, unroll=8)
    for k in range(2):
        pltpu.make_async_copy(ys_hbm.at[pl.ds(0, tb)], buf_ref.at[k], sem).wait()
    reps = D_MODEL // LANES
    w = wts_ref[...]
    w_top1 = jnp.concatenate([w[:, :LANES]] * reps, axis=1)
    w_top2 = jnp.concatenate([w[:, LANES:]] * reps, axis=1)
    o_ref[...] = h_ref[...] + w_top1 * buf_ref[0] + w_top2 * buf_ref[1]


def _combine(h2d, ys, idx, wts):
    n = h2d.shape[0]
    tb = MOE_TOKENS_PER_STEP
    return pl.pallas_call(
        _combine_kernel,
        grid=(n // tb,),
        in_specs=[pl.BlockSpec((1, 1, 2 * tb), lambda i: (i, 0, 0), memory_space=pltpu.SMEM),
                  pl.BlockSpec(memory_space=pl.ANY),
                  pl.BlockSpec((tb, D_MODEL), lambda i: (i, 0)),
                  pl.BlockSpec((tb, 2 * LANES), lambda i: (i, 0))],
        out_specs=pl.BlockSpec((tb, D_MODEL), lambda i: (i, 0)),
        out_shape=jax.ShapeDtypeStruct((n, D_MODEL), F32),
        scratch_shapes=[pltpu.VMEM((2, tb, D_MODEL), F32), pltpu.SemaphoreType.DMA(())],
        compiler_params=pltpu.CompilerParams(dimension_semantics=("arbitrary",), vmem_limit_bytes=VMEM_LIMIT,
                                             disable_bounds_checks=True),
        name="moe_combine",
    )(idx, ys, h2d, wts)


def _moe(h2d, g, w_router, b_router, w1, w3, w2, layer):
    n = h2d.shape[0]
    route, wts, counts = _router(h2d, g, w_router, b_router, layer)
    idx, tile_e, n_used, zero_rows, n_tiles = _dispatch_tables(route, counts, n)
    xs = _dispatch(h2d, idx, zero_rows, n_used, n_tiles)
    ys = _experts(xs, g, w1, w3, w2, tile_e, n_used, layer, n_tiles)
    return _combine(h2d, ys, idx, wts)


def _norm_kernel(x_ref, g_ref, o_ref):
    x = x_ref[...]
    ms = jnp.mean(x * x, axis=-1, keepdims=True)
    o_ref[...] = x * lax.rsqrt(ms + NORM_EPS) * g_ref[...]


def _final_norm(h2d, g, tm=1024):
    n = h2d.shape[0]
    return pl.pallas_call(
        _norm_kernel,
        grid=(n // tm,),
        in_specs=[pl.BlockSpec((tm, D_MODEL), lambda i: (i, 0)), pl.BlockSpec((1, D_MODEL), lambda i: (0, 0))],
        out_specs=pl.BlockSpec((tm, D_MODEL), lambda i: (i, 0)),
        out_shape=jax.ShapeDtypeStruct((n, D_MODEL), F32),
        compiler_params=_cparams("parallel"),
        name="final_norm",
    )(h2d, g)


def _router_params(wg, bg, we, be):
    n_l = wg.shape[0]
    pad = ROUTER_ROWS - N_GROUPS - N_EXPERTS
    w = jnp.concatenate([wg, we, jnp.zeros((n_l, D_MODEL, pad), F32)], axis=-1).astype(F32).transpose(0, 2, 1)
    b = jnp.concatenate([bg, be, jnp.zeros((n_l, pad), F32)], axis=-1).astype(F32)
    return w, jnp.broadcast_to(b[:, :, None], (n_l, ROUTER_ROWS, LANES))


def kernel(x, w_in, na_rpb, lam_q1, lam_k1, lam_q2, lam_k2, diff_subln, w_pa, w_pb, w_pc, w_o, norm_mix, norm_ffn,
           router_group_w, router_group_b, router_expert_w, router_expert_b, w1, w3, w2, norm_final):
    B, T, D = x.shape
    depth = w_in.shape[0]
    rows = T // GRID_W
    wr = min(NA_WIN_R, rows)
    rope_tab = _rope_table(T)
    row3 = lambda a: a[:, None, :]
    w_in_bf = _permute_cols(w_in).astype(BF16)
    bias_tab = _na_bias_table(na_rpb, wr)
    g_mix, g_ffn = row3(norm_mix), row3(norm_ffn)
    lams = [row3(a) for a in (lam_q1, lam_k1, lam_q2, lam_k2)]
    subln = row3(diff_subln)
    wpa, wpb, wpc, wo = (a.astype(BF16) for a in (w_pa, w_pb, w_pc, w_o))
    w_router, b_router = _router_params(router_group_w, router_group_b, router_expert_w, router_expert_b)
    h = x.reshape(B * T, D)
    for l in range(depth):
        lam_init = 0.8 - 0.6 * math.exp(-0.3 * l)
        proj, *cgs = _inproj(h, g_mix, w_in_bf, rope_tab, l, B, T)
        ya = _na_attention(proj, bias_tab, l, B, T)
        yb = _diff_attention(proj, *lams, subln, l, lam_init, B, T)
        outs, lses = _dilated_branch(cgs)
        h = _merge(h, proj, ya, yb, outs, lses, wpa, wpb, wpc, wo, l)
        h = _moe(h, g_ffn, w_router, b_router, w1, w3, w2, l)
    return _final_norm(h, norm_final[None, :]).reshape(B, T, D)
```

```python
import functools
import math

import jax
import jax.numpy as jnp
import numpy as np
from jax import lax
from jax.experimental import pallas as pl
from jax.experimental.pallas import tpu as pltpu

F32 = jnp.float32
BF16 = jnp.bfloat16

D_MODEL = 1024
HEAD_DIM = 64
ROPE_DIM = 16
ROPE_THETA = 500000.0
GRID_W = 64
NA_HEADS = 8
NA_WIN_R = 8
NA_WIN_C = 16
DIFF_HEADS = 4
DIL_PATTERNS = ((128, 1), (512, 4), (2048, 16))
N_GROUPS = 4
EXPERTS_PER_GROUP = 4
N_EXPERTS = 16
D_FF_EXPERT = D_MODEL // 2
NORM_EPS = 1e-6
SUBLN_EPS = 1e-5
NEG_INF = -1e30

LANES = 128
MXU_N = 256
VMEM_LIMIT = 56 * 1024 * 1024

A_W, B_W, C_W = 512, 512, 768
N_STEPS = len(DIL_PATTERNS)
GROUP_W = C_W // N_STEPS
_MAIN_ORDER = ("qb", "kb", "va", "qa", "ga", "gb", "gc", "ka", "vb")
_REF_ORDER = ("qa", "ka", "va", "qb", "kb", "vb", "qc", "kc", "vc", "ga", "gb", "gc")
_WIDTH = dict(qa=A_W, ka=A_W, va=A_W, qb=B_W, kb=B_W, vb=B_W, qc=C_W, kc=C_W, vc=C_W,
              ga=D_MODEL, gb=D_MODEL, gc=D_MODEL)


def _offsets(order):
    off, out = 0, {}
    for name in order:
        out[name] = off
        off += _WIDTH[name]
    return out, off


_OFF, MAIN_W = _offsets(_MAIN_ORDER)
_REF_OFF, IN_W = _offsets(_REF_ORDER)
OFF_QA, OFF_KA, OFF_VA = _OFF["qa"], _OFF["ka"], _OFF["va"]
OFF_QB, OFF_KB, OFF_VB = _OFF["qb"], _OFF["kb"], _OFF["vb"]
OFF_GATE = _OFF["ga"]
MAIN_STEP_W = MAIN_W // N_STEPS
STEP_W = MAIN_STEP_W + 3 * GROUP_W
MAIN_ROPE_W = 2 * B_W
QB_SCALE = HEAD_DIM ** -0.5 * math.log2(math.e)
assert OFF_GATE % D_MODEL == 0 and OFF_QB == 0 and OFF_KB == B_W and MAIN_ROPE_W <= MAIN_STEP_W
assert MAIN_W % N_STEPS == 0 and MAIN_STEP_W % MXU_N == 0 and N_STEPS * STEP_W == IN_W


def _permute_cols(w):
    cols = {n: w[..., _REF_OFF[n]:_REF_OFF[n] + _WIDTH[n]] for n in _MAIN_ORDER}
    cols["qb"] = cols["qb"] * QB_SCALE
    main = jnp.concatenate([cols[n] for n in _MAIN_ORDER], axis=-1)
    parts = []
    for s in range(N_STEPS):
        parts.append(main[..., s * MAIN_STEP_W:(s + 1) * MAIN_STEP_W])
        for n in ("qc", "kc", "vc"):
            parts.append(w[..., _REF_OFF[n] + s * GROUP_W:_REF_OFF[n] + (s + 1) * GROUP_W])
    return jnp.concatenate(parts, axis=-1)


def _cparams(*sem):
    return pltpu.CompilerParams(dimension_semantics=sem, vmem_limit_bytes=VMEM_LIMIT)


def _inproj_kernel(x_ref, g_ref, w_ref, rope_ref, main_ref, c0_ref, c1_ref, c2_ref, xn_ref, y_ref, *, tm):
    j = pl.program_id(1)
    c_refs = (c0_ref, c1_ref, c2_ref)
    reps = MXU_N // LANES
    half = ROPE_DIM // 2
    assert GROUP_W == MXU_N

    def rope(y):
        cos = jnp.concatenate([rope_ref[0]] * reps, axis=1)
        s_up = jnp.concatenate([rope_ref[1]] * reps, axis=1)
        s_dn = jnp.concatenate([rope_ref[2]] * reps, axis=1)
        return y * cos + pltpu.roll(y, MXU_N - half, 1) * s_up + pltpu.roll(y, half, 1) * s_dn

    def step(s):
        xn = xn_ref[...]
        for c in range(MAIN_STEP_W // MXU_N):
            sl = slice(c * MXU_N, (c + 1) * MXU_N)
            y = jnp.dot(xn, w_ref[:, sl], preferred_element_type=F32)
            if s == 0 and c < MAIN_ROPE_W // MXU_N:
                y = rope(y)
            main_ref[:, sl] = y.astype(main_ref.dtype)
        dil = DIL_PATTERNS[s][1]
        for c in range(3):
            wsl = slice(MAIN_STEP_W + c * GROUP_W, MAIN_STEP_W + (c + 1) * GROUP_W)
            osl = slice(c * GROUP_W, (c + 1) * GROUP_W)
            y = jnp.dot(xn, w_ref[:, wsl], preferred_element_type=F32)
            if c < 2:
                y = rope(y)
            if dil == 1:
                c_refs[s][0, :, osl] = y.astype(BF16)
            else:
                for hb in range(reps):
                    y_ref[c, hb] = y[:, hb * LANES:(hb + 1) * LANES]
                for p in range(dil):
                    for hb in range(reps):
                        c_refs[s][p, :, c * GROUP_W + hb * LANES:c * GROUP_W + (hb + 1) * LANES] = (
                            y_ref[c, hb, pl.ds(p, tm // dil, stride=dil), :].astype(BF16))

    @pl.when(j == 0)
    def _():
        x = x_ref[...]
        ms = jnp.mean(x * x, axis=-1, keepdims=True)
        xn_ref[...] = (x * lax.rsqrt(ms + NORM_EPS) * g_ref[...]).astype(BF16)
        step(0)

    for s in range(1, N_STEPS):
        pl.when(j == s)(functools.partial(step, s))


def _inproj(h2d, g, w_bf, rope_tab, layer, B, T, tm=1024):
    n = h2d.shape[0]
    tpb = T // tm
    c_specs, c_shapes = [], []
    for _, dil in DIL_PATTERNS:
        assert tm % dil == 0
        c_specs.append(pl.BlockSpec((None, dil, tm // dil, 3 * GROUP_W), lambda i, j: (i // tpb, 0, i % tpb, 0)))
        c_shapes.append(jax.ShapeDtypeStruct((B, dil, T // dil, 3 * GROUP_W), BF16))
    return pl.pallas_call(
        functools.partial(_inproj_kernel, tm=tm),
        grid=(n // tm, N_STEPS),
        in_specs=[
            pl.BlockSpec((tm, D_MODEL), lambda i, j: (i, 0)),
            pl.BlockSpec((None, 1, D_MODEL), lambda i, j: (layer, 0, 0)),
            pl.BlockSpec((None, D_MODEL, STEP_W), lambda i, j: (layer, 0, j)),
            pl.BlockSpec((3, tm, LANES), lambda i, j: (0, i % tpb, 0)),
        ],
        out_specs=[pl.BlockSpec((tm, MAIN_STEP_W), lambda i, j: (i, j))] + c_specs,
        out_shape=[jax.ShapeDtypeStruct((n, MAIN_W), BF16)] + c_shapes,
        scratch_shapes=[pltpu.VMEM((tm, D_MODEL), BF16), pltpu.VMEM((3, MXU_N // LANES, tm, LANES), F32)],
        compiler_params=_cparams("parallel", "arbitrary"),
        name="inproj",
    )(h2d, g, w_bf, rope_tab)


def _rope_table(T):
    half = ROPE_DIM // 2
    inv = 1.0 / (ROPE_THETA ** (jnp.arange(0, ROPE_DIM, 2, dtype=F32) / ROPE_DIM))
    ang = jnp.arange(T, dtype=F32)[:, None] * inv[None, :]
    cos, sin = jnp.cos(ang), jnp.sin(ang)
    zeros = jnp.zeros((T, HEAD_DIM - ROPE_DIM), F32)
    z8 = jnp.zeros((T, half), F32)
    c64 = jnp.concatenate([cos, cos, zeros + 1.0], axis=1)
    up64 = jnp.concatenate([-sin, z8, zeros], axis=1)
    dn64 = jnp.concatenate([z8, sin, zeros], axis=1)
    reps = LANES // HEAD_DIM
    return jnp.stack([jnp.tile(c64, (1, reps)), jnp.tile(up64, (1, reps)), jnp.tile(dn64, (1, reps))])


NA_ROWS_PER_ITER = 16


def _na_kernel(q_ref, k_ref, v_ref, b_ref, o_ref, *, rows, wr):
    lane = lax.broadcasted_iota(jnp.int32, (GRID_W, LANES), 1)
    lo = lane < HEAD_DIM
    scale = HEAD_DIM ** -0.5

    def row_group(gi, carry):
        scores, windows = [], []
        for u in range(NA_ROWS_PER_ITER):
            r = gi * NA_ROWS_PER_ITER + u
            r0 = jnp.clip(r - wr // 2, 0, rows - wr)
            d0 = r0 - r + NA_WIN_R - 1 - (NA_WIN_R - wr)
            q = q_ref[pl.ds(pl.multiple_of(r * GRID_W, GRID_W), GRID_W), :] * scale
            ks = pl.multiple_of(r0 * GRID_W, GRID_W)
            kw = k_ref[pl.ds(ks, wr * GRID_W), :]
            windows.append(ks)
            for hh in range(2):
                qh = jnp.where(lo if hh == 0 else jnp.logical_not(lo), q, jnp.zeros_like(q))
                s = lax.dot_general(qh, kw, (((1,), (1,)), ((), ())), preferred_element_type=F32)
                scores.append(s + b_ref[hh, d0])
        probs, sums = [], []
        for s in scores:
            p = jnp.exp(s - jnp.max(s, axis=-1, keepdims=True))
            sums.append(jnp.sum(p, axis=-1, keepdims=True))
            probs.append(p.astype(BF16))
        for u in range(NA_ROWS_PER_ITER):
            r = gi * NA_ROWS_PER_ITER + u
            vw = v_ref[pl.ds(windows[u], wr * GRID_W), :]
            outs = [jnp.dot(probs[2 * u + hh], vw, preferred_element_type=F32) / sums[2 * u + hh] for hh in range(2)]
            o = jnp.where(lo, outs[0], outs[1])
            o_ref[pl.ds(pl.multiple_of(r * GRID_W, GRID_W), GRID_W), :] = o.astype(o_ref.dtype)
        return carry

    lax.fori_loop(0, rows // NA_ROWS_PER_ITER, row_group, 0)


def _na_bias_table(rpb, wr):
    qc = np.arange(GRID_W)[:, None]
    kc = np.arange(GRID_W)[None, :]
    c0 = np.clip(qc - NA_WIN_C // 2, 0, GRID_W - NA_WIN_C)
    ok = (kc >= c0) & (kc < c0 + NA_WIN_C)
    dc = np.clip(kc - qc + NA_WIN_C - 1, 0, 2 * NA_WIN_C - 2)
    onehot = (np.arange(2 * NA_WIN_C - 1)[:, None, None] == dc[None]).astype(np.float32)
    b = jnp.einsum("lhrd,dqk->lhrqk", rpb.astype(F32), onehot, precision=lax.Precision.HIGHEST)
    b = jnp.where(jnp.asarray(ok), b, NEG_INF)
    row_idx = (NA_WIN_R - wr) + np.arange(wr)[:, None] + np.arange(wr)[None, :]
    t = b[:, :, row_idx]
    n_l, n_h = rpb.shape[0], rpb.shape[1]
    return t.transpose(0, 1, 2, 4, 3, 5).reshape(n_l, n_h, wr, GRID_W, wr * GRID_W)


def _na_attention(proj, bias_tab, layer, B, T):
    rows = T // GRID_W
    wr = min(NA_WIN_R, rows)
    cq, ck, cv = OFF_QA // LANES, OFF_KA // LANES, OFF_VA // LANES
    return pl.pallas_call(
        functools.partial(_na_kernel, rows=rows, wr=wr),
        grid=(B, NA_HEADS // 2),
        in_specs=[
            pl.BlockSpec((T, LANES), lambda b, h: (b, cq + h)),
            pl.BlockSpec((T, LANES), lambda b, h: (b, ck + h)),
            pl.BlockSpec((T, LANES), lambda b, h: (b, cv + h)),
            pl.BlockSpec((None, 2, wr, GRID_W, wr * GRID_W), lambda b, h: (layer, h, 0, 0, 0)),
        ],
        out_specs=pl.BlockSpec((T, LANES), lambda b, h: (b, h)),
        out_shape=jax.ShapeDtypeStruct((B * T, A_W), BF16),
        compiler_params=_cparams("parallel", "arbitrary"),
        name="na_attn",
    )(proj, proj, proj, bias_tab)


DIFF_KEY_CHUNK = 512


def _diff_kernel(lq1_ref, lk1_ref, lq2_ref, lk2_ref, q_ref, k_ref, v_ref, g_ref, o_ref, vt_ref, *, lam_init, tq):
    lam = (jnp.exp(jnp.sum(lq1_ref[...] * lk1_ref[...], keepdims=True))
           - jnp.exp(jnp.sum(lq2_ref[...] * lk2_ref[...], keepdims=True)) + lam_init)
    T = k_ref.shape[0]
    ck = DIFF_KEY_CHUNK
    n_chunks = T // ck
    lane = lax.broadcasted_iota(jnp.int32, (tq, LANES), 1)
    lo = lane < HEAD_DIM
    nt = (((1,), (1,)), ((), ()))
    vt_ref[...] = v_ref[...].T

    def q_block(i, carry):
        rows = pl.ds(pl.multiple_of(i * tq, tq), tq)
        q = q_ref[rows, :]
        zero = jnp.zeros_like(q)
        qs = (jnp.where(lo, q, zero), jnp.where(lo, zero, q))

        def scores(c):
            kc = k_ref[c * ck:(c + 1) * ck, :]
            return [lax.dot_general(kc, qm, nt, preferred_element_type=F32) for qm in qs]

        m = [jnp.full((1, tq), NEG_INF, F32)] * 2
        l = [jnp.zeros((1, tq), F32)] * 2
        acc = [jnp.zeros((LANES, tq), F32)] * 2
        s_next = scores(0)
        for c in range(n_chunks):
            s_cur = s_next
            if c + 1 < n_chunks:
                s_next = scores(c + 1)
            vt = vt_ref[:, c * ck:(c + 1) * ck]
            for j in range(2):
                m_new = jnp.maximum(m[j], jnp.max(s_cur[j], axis=0, keepdims=True))
                alpha = jnp.exp2(m[j] - m_new)
                p = jnp.exp2(s_cur[j] - m_new)
                l[j] = alpha * l[j] + jnp.sum(p, axis=0, keepdims=True)
                acc[j] = alpha * acc[j] + jnp.dot(vt, p.astype(BF16), preferred_element_type=F32)
                m[j] = m_new
        ot = acc[0] / l[0] - lam * (acc[1] / l[1])
        o = ot.T
        ms = jnp.mean(o * o, axis=-1, keepdims=True)
        o = o * lax.rsqrt(ms + SUBLN_EPS) * g_ref[...] * (1.0 - lam_init)
        o_ref[rows, :] = o.astype(o_ref.dtype)
        return carry

    lax.fori_loop(0, q_ref.shape[0] // tq, q_block, 0, unroll=2)


def _diff_attention(proj, lq1, lk1, lq2, lk2, subln_g, layer, lam_init, B, T, tq=512):
    cq, ck, cv = OFF_QB // LANES, OFF_KB // LANES, OFF_VB // LANES
    vec = pl.BlockSpec((None, 1, HEAD_DIM), lambda b, h: (layer, 0, 0))
    return pl.pallas_call(
        functools.partial(_diff_kernel, lam_init=lam_init, tq=tq),
        grid=(B, DIFF_HEADS),
        in_specs=[
            vec, vec, vec, vec,
            pl.BlockSpec((T, LANES), lambda b, h: (b, cq + h)),
            pl.BlockSpec((T, LANES), lambda b, h: (b, ck + h)),
            pl.BlockSpec((T, LANES), lambda b, h: (b, cv + h)),
            pl.BlockSpec((None, 1, LANES), lambda b, h: (layer, 0, 0)),
        ],
        out_specs=pl.BlockSpec((T, LANES), lambda b, h: (b, h)),
        out_shape=jax.ShapeDtypeStruct((B * T, B_W), BF16),
        scratch_shapes=[pltpu.VMEM((LANES, T), BF16)],
        compiler_params=_cparams("parallel", "arbitrary"),
        name="diff_attn",
    )(lq1, lk1, lq2, lk2, proj, proj, proj, subln_g)


DIL_Q = 128


DIL_BLOCKS_PER_ITER = 4


def _dil_kernel(q_ref, k_ref, v_ref, o_ref, lse_ref, *, L, dil, radius):
    kw_len = DIL_Q + 2 * radius
    lane = lax.broadcasted_iota(jnp.int32, (DIL_Q, LANES), 1)
    lo = lane < HEAD_DIM
    rel = (lax.broadcasted_iota(jnp.int32, (DIL_Q, kw_len), 1)
           - lax.broadcasted_iota(jnp.int32, (DIL_Q, kw_len), 0))
    scale = HEAD_DIM ** -0.5
    nt = (((1,), (1,)), ((), ()))

    blocks_per_phase = L // DIL_Q

    def blocks(gi, carry):
        scores, starts, maxes = [], [], []
        for u in range(DIL_BLOCKS_PER_ITER):
            item = gi * DIL_BLOCKS_PER_ITER + u
            p = item // blocks_per_phase
            l0 = pl.multiple_of((item % blocks_per_phase) * DIL_Q, DIL_Q)
            ks = pl.multiple_of(jnp.clip(l0 - radius, 0, L - kw_len), radius)
            q = q_ref[p, pl.ds(l0, DIL_Q), :] * scale
            kw = k_ref[p, pl.ds(ks, kw_len), :]
            ok = jnp.abs(rel + (ks - l0)) <= radius
            starts.append((p, l0, ks))
            for hh in range(2):
                qh = jnp.where(lo if hh == 0 else jnp.logical_not(lo), q, jnp.zeros_like(q))
                s = lax.dot_general(qh, kw, nt, preferred_element_type=F32)
                scores.append(jnp.where(ok, s, NEG_INF))
        probs, sums = [], []
        for s in scores:
            m = jnp.max(s, axis=-1, keepdims=True)
            e = jnp.exp(s - m)
            maxes.append(m)
            sums.append(jnp.sum(e, axis=-1, keepdims=True))
            probs.append(e.astype(BF16))
        for u in range(DIL_BLOCKS_PER_ITER):
            p, l0, ks = starts[u]
            vw = v_ref[p, pl.ds(ks, kw_len), :]
            outs = [jnp.dot(probs[2 * u + hh], vw, preferred_element_type=F32) / sums[2 * u + hh] for hh in range(2)]
            lses = [maxes[2 * u + hh] + jnp.log(sums[2 * u + hh]) for hh in range(2)]
            o = jnp.where(lo, outs[0], outs[1])
            lse = jnp.where(lo, lses[0], lses[1])
            if dil == 1:
                o_ref[pl.ds(l0, DIL_Q), :] = o
                lse_ref[pl.ds(l0, DIL_Q), :] = lse
            else:
                o_ref[pl.ds(l0 * dil + p, DIL_Q, stride=dil), :] = o
                lse_ref[pl.ds(l0 * dil + p, DIL_Q, stride=dil), :] = lse
        return carry

    lax.fori_loop(0, dil * blocks_per_phase // DIL_BLOCKS_PER_ITER, blocks, 0)


def _dilated_group(cg, radius):
    B, dil, L, _ = cg.shape
    assert L >= DIL_Q + 2 * radius and L % DIL_Q == 0 and (dil * L // DIL_Q) % DIL_BLOCKS_PER_ITER == 0
    hp = GROUP_W // LANES
    in_spec = lambda c: pl.BlockSpec((None, dil, L, LANES), lambda b, h: (b, 0, 0, c * hp + h))
    out_spec = pl.BlockSpec((L * dil, LANES), lambda b, h: (b, h))
    out_shape = jax.ShapeDtypeStruct((B * L * dil, GROUP_W), F32)
    return pl.pallas_call(
        functools.partial(_dil_kernel, L=L, dil=dil, radius=radius),
        grid=(B, hp),
        in_specs=[in_spec(0), in_spec(1), in_spec(2)],
        out_specs=[out_spec, out_spec],
        out_shape=[out_shape, out_shape],
        compiler_params=_cparams("parallel", "arbitrary"),
        name="dil_attn",
    )(cg, cg, cg)


def _dilated_branch(cgs):
    outs, lses = [], []
    for cg, (window, dil) in zip(cgs, DIL_PATTERNS):
        o, lse = _dilated_group(cg, window // (2 * dil))
        outs.append(o)
        lses.append(lse)
    return outs, lses


def _merge_kernel(h_ref, ga_ref, gb_ref, gc_ref, ya_ref, yb_ref, o0_ref, o1_ref, o2_ref,
                  l0_ref, l1_ref, l2_ref, wpa_ref, wpb_ref, wpc_ref, wo_ref, out_ref):
    l0, l1, l2 = l0_ref[...], l1_ref[...], l2_ref[...]
    m = jnp.maximum(jnp.maximum(l0, l1), l2)
    e0, e1, e2 = jnp.exp(l0 - m), jnp.exp(l1 - m), jnp.exp(l2 - m)
    yc = (o0_ref[...].astype(F32) * e0 + o1_ref[...].astype(F32) * e1 + o2_ref[...].astype(F32) * e2) / (e0 + e1 + e2)
    merged = jax.nn.sigmoid(ga_ref[...].astype(F32)) * jnp.dot(ya_ref[...], wpa_ref[...], preferred_element_type=F32)
    merged += jax.nn.sigmoid(gb_ref[...].astype(F32)) * jnp.dot(yb_ref[...], wpb_ref[...], preferred_element_type=F32)
    merged += jax.nn.sigmoid(gc_ref[...].astype(F32)) * jnp.dot(yc.astype(BF16), wpc_ref[...],
                                                                preferred_element_type=F32)
    out_ref[...] = h_ref[...] + jnp.dot(merged.astype(BF16), wo_ref[...], preferred_element_type=F32)


def _merge(h2d, proj, ya, yb, outs, lses, wpa, wpb, wpc, wo, layer, tm=512):
    n = h2d.shape[0]
    gw = GROUP_W
    row = lambda w: pl.BlockSpec((tm, w), lambda i: (i, 0))
    full = lambda a: pl.BlockSpec((None,) + a.shape[1:], lambda i: (layer, 0, 0))
    gate = lambda c: pl.BlockSpec((tm, D_MODEL), lambda i: (i, OFF_GATE // D_MODEL + c))
    return pl.pallas_call(
        _merge_kernel,
        grid=(n // tm,),
        in_specs=[row(D_MODEL), gate(0), gate(1), gate(2), row(A_W), row(B_W),
                  row(gw), row(gw), row(gw), row(gw), row(gw), row(gw),
                  full(wpa), full(wpb), full(wpc), full(wo)],
        out_specs=row(D_MODEL),
        out_shape=jax.ShapeDtypeStruct((n, D_MODEL), F32),
        compiler_params=_cparams("parallel"),
        name="merge_outproj",
    )(h2d, proj, proj, proj, ya, yb, *outs, *lses, wpa, wpb, wpc, wo)


MOE_TILE = 512
MOE_TOKENS_PER_STEP = 512
ROUTE_ROWS = 8


ROUTER_ROWS = 32


def _router_kernel(h_ref, g_ref, wr_ref, br_ref, upper_ref, route_ref, wts_ref, cnt_ref, base_ref):
    i = pl.program_id(0)
    tm = h_ref.shape[0]
    lane_reps = tm // LANES
    row = lax.broadcasted_iota(jnp.int32, (ROUTER_ROWS, tm), 0)
    rowf = row.astype(F32)
    big = float(ROUTER_ROWS)
    nt = (((1,), (1,)), ((), ()))

    @pl.when(i == 0)
    def _():
        base_ref[...] = jnp.zeros_like(base_ref)

    x = h_ref[...]
    ms = jnp.mean(x * x, axis=-1, keepdims=True)
    xn = x * lax.rsqrt(ms + NORM_EPS) * g_ref[...]
    logits = lax.dot_general(wr_ref[...], xn, nt, preferred_element_type=F32, precision=lax.Precision.HIGHEST)
    logits = logits + jnp.concatenate([br_ref[...]] * lane_reps, axis=1)
    gl = jnp.where(row < N_GROUPS, logits, -jnp.inf)
    gmax = jnp.max(gl, axis=0, keepdims=True)
    g_sel = jnp.min(jnp.where(gl == gmax, rowf, big), axis=0, keepdims=True).astype(jnp.int32)
    g_gate = 1.0 / jnp.sum(jnp.exp(gl - gmax), axis=0, keepdims=True)
    eidx = row - N_GROUPS
    in_grp = (eidx >= g_sel * EXPERTS_PER_GROUP) & (eidx < (g_sel + 1) * EXPERTS_PER_GROUP)
    el = jnp.where(in_grp, logits, -jnp.inf)
    t1 = jnp.max(el, axis=0, keepdims=True)
    i1 = jnp.min(jnp.where(el == t1, rowf, big), axis=0, keepdims=True).astype(jnp.int32)
    el2 = jnp.where(row == i1, -jnp.inf, el)
    t2 = jnp.max(el2, axis=0, keepdims=True)
    i2 = jnp.min(jnp.where(el2 == t2, rowf, big), axis=0, keepdims=True).astype(jnp.int32)
    x2 = jnp.exp(t2 - t1)
    den = 1.0 + x2
    wts_ref[:, :LANES] = jnp.broadcast_to(g_gate / den, (LANES, tm)).T
    wts_ref[:, LANES:] = jnp.broadcast_to(g_gate * x2 / den, (LANES, tm)).T
    member = jnp.where(row == i1, 1.0, 0.0) + jnp.where(row == i2, 1.0, 0.0)
    earlier = jnp.dot(member.astype(BF16), upper_ref[...], preferred_element_type=F32)
    earlier = earlier + jnp.concatenate([base_ref[...]] * lane_reps, axis=1)
    rank1 = jnp.sum(jnp.where(row == i1, earlier, 0.0), axis=0, keepdims=True)
    rank2 = jnp.sum(jnp.where(row == i2, earlier, 0.0), axis=0, keepdims=True)
    base_ref[...] = base_ref[...] + jnp.sum(member, axis=1, keepdims=True)
    cnt_ref[...] = base_ref[...]
    out_row = lax.broadcasted_iota(jnp.int32, (ROUTE_ROWS, tm), 0)
    route_ref[...] = jnp.where(out_row == 0, (i1 - N_GROUPS).astype(F32),
                               jnp.where(out_row == 1, (i2 - N_GROUPS).astype(F32),
                                         jnp.where(out_row == 2, rank1, jnp.where(out_row == 3, rank2, 0.0))))


def _router(h2d, g, w_router, b_router, layer, tm=1024):
    n = h2d.shape[0]
    return pl.pallas_call(
        _router_kernel,
        grid=(n // tm,),
        in_specs=[
            pl.BlockSpec((tm, D_MODEL), lambda i: (i, 0)),
            pl.BlockSpec((None, 1, D_MODEL), lambda i: (layer, 0, 0)),
            pl.BlockSpec((None, ROUTER_ROWS, D_MODEL), lambda i: (layer, 0, 0)),
            pl.BlockSpec((None, ROUTER_ROWS, LANES), lambda i: (layer, 0, 0)),
            pl.BlockSpec((tm, tm), lambda i: (0, 0)),
        ],
        out_specs=[pl.BlockSpec((ROUTE_ROWS, tm), lambda i: (0, i)),
                   pl.BlockSpec((tm, 2 * LANES), lambda i: (i, 0)),
                   pl.BlockSpec((ROUTER_ROWS, LANES), lambda i: (0, 0))],
        out_shape=[jax.ShapeDtypeStruct((ROUTE_ROWS, n), F32), jax.ShapeDtypeStruct((n, 2 * LANES), F32),
                   jax.ShapeDtypeStruct((ROUTER_ROWS, LANES), F32)],
        scratch_shapes=[pltpu.VMEM((ROUTER_ROWS, LANES), F32)],
        compiler_params=_cparams("arbitrary"),
        name="moe_router",
    )(h2d, g, w_router, b_router, jnp.triu(jnp.ones((tm, tm), BF16), k=1))


def _dispatch_tables(route, counts, n):
    cnt = counts[N_GROUPS:N_GROUPS + N_EXPERTS, 0].astype(jnp.int32)
    padded = (cnt + MOE_TILE - 1) // MOE_TILE * MOE_TILE
    ends = jnp.cumsum(padded)
    starts = ends - padded
    eids = jnp.arange(N_EXPERTS, dtype=jnp.int32)
    pos = []
    for k in range(2):
        e = route[k].astype(jnp.int32)
        start_e = jnp.sum(jnp.where(e[:, None] == eids[None, :], starts[None, :], 0), axis=1)
        pos.append(start_e + route[2 + k].astype(jnp.int32))
    tb = MOE_TOKENS_PER_STEP
    idx = jnp.concatenate([pos[0].reshape(n // tb, 1, tb), pos[1].reshape(n // tb, 1, tb)], axis=-1)
    n_tiles = 2 * n // MOE_TILE + N_EXPERTS
    tile_ids = jnp.arange(n_tiles, dtype=jnp.int32)
    n_used = ends[-1] // MOE_TILE
    tile_e = jnp.sum((tile_ids[:, None] * MOE_TILE >= ends[None, :]).astype(jnp.int32), axis=1)
    last_e = jnp.max(jnp.where(tile_ids < n_used, tile_e, 0))
    tile_e = jnp.where(tile_ids < n_used, tile_e, last_e).astype(jnp.int32)
    zero_rows = jnp.where(padded > 0, ends - MOE_TILE, n_tiles * MOE_TILE).astype(jnp.int32)
    return idx, tile_e, n_used.reshape(1).astype(jnp.int32), zero_rows, n_tiles


def _dispatch_kernel(zero_rows_ref, n_used_ref, idx_ref, h_ref, xs_hbm, zeros_ref, sem):
    i = pl.program_id(0)
    tb = MOE_TOKENS_PER_STEP
    n_tiles = xs_hbm.shape[0] // MOE_TILE - 1

    def zero_tile(row0):
        return pltpu.make_async_copy(zeros_ref, xs_hbm.at[pl.ds(pl.multiple_of(row0, MOE_TILE), MOE_TILE)], sem)

    @pl.when(i == 0)
    def _():
        zeros_ref[...] = jnp.zeros_like(zeros_ref)
        fills = [zero_tile(zero_rows_ref[e]) for e in range(N_EXPERTS)]
        for c in fills:
            c.start()
        for c in fills:
            c.wait()

        def tail(j, carry):
            tile = n_used_ref[0] + j

            @pl.when(tile <= n_tiles)
            def _():
                c = zero_tile(tile * MOE_TILE)
                c.start()
                c.wait()
            return carry

        lax.fori_loop(0, N_EXPERTS + 1, tail, 0)

    def row(t, carry):
        src = h_ref.at[pl.ds(t, 1)]
        pltpu.make_async_copy(src, xs_hbm.at[pl.ds(idx_ref[0, 0, t], 1)], sem).start()
        pltpu.make_async_copy(src, xs_hbm.at[pl.ds(idx_ref[0, 0, tb + t], 1)], sem).start()
        return carry

    lax.fori_loop(0, tb, row, 0, unroll=True)
    for _ in range(2):
        pltpu.make_async_copy(h_ref, xs_hbm.at[pl.ds(0, tb)], sem).wait()


def _dispatch(h2d, idx, zero_rows, n_used, n_tiles):
    n = h2d.shape[0]
    tb = MOE_TOKENS_PER_STEP
    assert 2 * n // MOE_TILE + N_EXPERTS == n_tiles
    return pl.pallas_call(
        _dispatch_kernel,
        grid_spec=pltpu.PrefetchScalarGridSpec(
            num_scalar_prefetch=2,
            grid=(n // tb,),
            in_specs=[pl.BlockSpec((1, 1, 2 * tb), lambda i, z, u: (i, 0, 0), memory_space=pltpu.SMEM),
                      pl.BlockSpec((tb, D_MODEL), lambda i, z, u: (i, 0))],
            out_specs=pl.BlockSpec(memory_space=pl.ANY),
            scratch_shapes=[pltpu.VMEM((MOE_TILE, D_MODEL), F32), pltpu.SemaphoreType.DMA(())],
        ),
        out_shape=jax.ShapeDtypeStruct(((n_tiles + 1) * MOE_TILE, D_MODEL), F32),
        compiler_params=pltpu.CompilerParams(dimension_semantics=("arbitrary",), vmem_limit_bytes=VMEM_LIMIT,
                                             disable_bounds_checks=True),
        name="moe_dispatch",
    )(zero_rows, n_used, idx, h2d)


def _expert_kernel(tile_e_ref, n_used_ref, x_ref, g_ref, w1_ref, w3_ref, w2_ref, y_ref, w1b_ref, w3b_ref, w2b_ref):
    i = pl.program_id(0)
    used = i < n_used_ref[0]

    @pl.when(jnp.logical_not(used))
    def _():
        y_ref[...] = jnp.zeros_like(y_ref)

    @pl.when(used & ((i == 0) | (tile_e_ref[i] != tile_e_ref[jnp.maximum(i - 1, 0)])))
    def _():
        w1b_ref[...] = w1_ref[...].astype(BF16)
        w3b_ref[...] = w3_ref[...].astype(BF16)
        w2b_ref[...] = w2_ref[...].astype(BF16)

    @pl.when(used)
    def _():
        x = x_ref[...]
        ms = jnp.mean(x * x, axis=-1, keepdims=True)
        xn = (x * lax.rsqrt(ms + NORM_EPS) * g_ref[...]).astype(BF16)
        a = jnp.dot(xn, w1b_ref[...], preferred_element_type=F32)
        b = jnp.dot(xn, w3b_ref[...], preferred_element_type=F32)
        hmid = (a * jax.nn.sigmoid(a) * b).astype(BF16)
        y_ref[...] = jnp.dot(hmid, w2b_ref[...], preferred_element_type=F32)


def _experts(xs, g, w1, w3, w2, tile_e, n_used, layer, n_tiles):
    row_map = lambda i, te, nu: (i, 0)
    w_map = lambda i, te, nu: (layer, te[i], 0, 0)
    return pl.pallas_call(
        _expert_kernel,
        grid_spec=pltpu.PrefetchScalarGridSpec(
            num_scalar_prefetch=2,
            grid=(n_tiles,),
            in_specs=[pl.BlockSpec((MOE_TILE, D_MODEL), row_map),
                      pl.BlockSpec((None, 1, D_MODEL), lambda i, te, nu: (layer, 0, 0)),
                      pl.BlockSpec((None, None, D_MODEL, D_FF_EXPERT), w_map),
                      pl.BlockSpec((None, None, D_MODEL, D_FF_EXPERT), w_map),
                      pl.BlockSpec((None, None, D_FF_EXPERT, D_MODEL), w_map)],
            out_specs=pl.BlockSpec((MOE_TILE, D_MODEL), row_map),
            scratch_shapes=[pltpu.VMEM((D_MODEL, D_FF_EXPERT), BF16), pltpu.VMEM((D_MODEL, D_FF_EXPERT), BF16),
                            pltpu.VMEM((D_FF_EXPERT, D_MODEL), BF16)],
        ),
        out_shape=jax.ShapeDtypeStruct((n_tiles * MOE_TILE, D_MODEL), F32),
        compiler_params=_cparams("arbitrary"),
        name="moe_experts",
    )(tile_e, n_used, xs, g, w1, w3, w2)


def _combine_kernel(idx_ref, ys_hbm, h_ref, wts_ref, gf_ref, o_ref, buf_ref, sem, *, final_norm):
    tb = h_ref.shape[0]

    def row(t, carry):
        pltpu.make_async_copy(ys_hbm.at[pl.ds(idx_ref[0, 0, t], 1)], buf_ref.at[0, pl.ds(t, 1)], sem).start()
        pltpu.make_async_copy(ys_hbm.at[pl.ds(idx_ref[0, 0, tb + t], 1)], buf_ref.at[1, pl.ds(t, 1)], sem).start()
        return carry

    lax.fori_loop(0, tb, row, 0, unroll=True)
    for k in range(2):
        pltpu.make_async_copy(ys_hbm.at[pl.ds(0, tb)], buf_ref.at[k], sem).wait()
    reps = D_MODEL // LANES
    w = wts_ref[...]
    w_top1 = jnp.concatenate([w[:, :LANES]] * reps, axis=1)
    w_top2 = jnp.concatenate([w[:, LANES:]] * reps, axis=1)
    out = h_ref[...] + w_top1 * buf_ref[0] + w_top2 * buf_ref[1]
    if final_norm:
        ms = jnp.mean(out * out, axis=-1, keepdims=True)
        out = out * lax.rsqrt(ms + NORM_EPS) * gf_ref[...]
    o_ref[...] = out


def _combine(h2d, ys, idx, wts, g_final, final_norm):
    n = h2d.shape[0]
    tb = MOE_TOKENS_PER_STEP
    return pl.pallas_call(
        functools.partial(_combine_kernel, final_norm=final_norm),
        grid=(n // tb,),
        in_specs=[pl.BlockSpec((1, 1, 2 * tb), lambda i: (i, 0, 0), memory_space=pltpu.SMEM),
                  pl.BlockSpec(memory_space=pl.ANY),
                  pl.BlockSpec((tb, D_MODEL), lambda i: (i, 0)),
                  pl.BlockSpec((tb, 2 * LANES), lambda i: (i, 0)),
                  pl.BlockSpec((1, D_MODEL), lambda i: (0, 0))],
        out_specs=pl.BlockSpec((tb, D_MODEL), lambda i: (i, 0)),
        out_shape=jax.ShapeDtypeStruct((n, D_MODEL), F32),
        scratch_shapes=[pltpu.VMEM((2, tb, D_MODEL), F32), pltpu.SemaphoreType.DMA(())],
        compiler_params=pltpu.CompilerParams(dimension_semantics=("arbitrary",), vmem_limit_bytes=VMEM_LIMIT,
                                             disable_bounds_checks=True),
        name="moe_combine",
    )(idx, ys, h2d, wts, g_final)


def _moe(h2d, g, w_router, b_router, w1, w3, w2, layer, g_final, final_norm):
    n = h2d.shape[0]
    route, wts, counts = _router(h2d, g, w_router, b_router, layer)
    idx, tile_e, n_used, zero_rows, n_tiles = _dispatch_tables(route, counts, n)
    xs = _dispatch(h2d, idx, zero_rows, n_used, n_tiles)
    ys = _experts(xs, g, w1, w3, w2, tile_e, n_used, layer, n_tiles)
    return _combine(h2d, ys, idx, wts, g_final, final_norm)


def _router_params(wg, bg, we, be):
    n_l = wg.shape[0]
    pad = ROUTER_ROWS - N_GROUPS - N_EXPERTS
    w = jnp.concatenate([wg, we, jnp.zeros((n_l, D_MODEL, pad), F32)], axis=-1).astype(F32).transpose(0, 2, 1)
    b = jnp.concatenate([bg, be, jnp.zeros((n_l, pad), F32)], axis=-1).astype(F32)
    return w, jnp.broadcast_to(b[:, :, None], (n_l, ROUTER_ROWS, LANES))


def kernel(x, w_in, na_rpb, lam_q1, lam_k1, lam_q2, lam_k2, diff_subln, w_pa, w_pb, w_pc, w_o, norm_mix, norm_ffn,
           router_group_w, router_group_b, router_expert_w, router_expert_b, w1, w3, w2, norm_final):
    B, T, D = x.shape
    depth = w_in.shape[0]
    rows = T // GRID_W
    wr = min(NA_WIN_R, rows)
    rope_tab = _rope_table(T)
    row3 = lambda a: a[:, None, :]
    w_in_bf = _permute_cols(w_in).astype(BF16)
    bias_tab = _na_bias_table(na_rpb, wr)
    g_mix, g_ffn = row3(norm_mix), row3(norm_ffn)
    lams = [row3(a) for a in (lam_q1, lam_k1, lam_q2, lam_k2)]
    subln = row3(diff_subln)
    wpa, wpb, wpc, wo = (a.astype(BF16) for a in (w_pa, w_pb, w_pc, w_o))
    w_router, b_router = _router_params(router_group_w, router_group_b, router_expert_w, router_expert_b)
    h = x.reshape(B * T, D)
    for l in range(depth):
        lam_init = 0.8 - 0.6 * math.exp(-0.3 * l)
        proj, *cgs = _inproj(h, g_mix, w_in_bf, rope_tab, l, B, T)
        ya = _na_attention(proj, bias_tab, l, B, T)
        yb = _diff_attention(proj, *lams, subln, l, lam_init, B, T)
        outs, lses = _dilated_branch(cgs)
        h = _merge(h, proj, ya, yb, outs, lses, wpa, wpb, wpc, wo, l)
        h = _moe(h, g_ffn, w_router, b_router, w1, w3, w2, l, norm_final[None, :], final_norm=(l == depth - 1))
    return h.reshape(B, T, D)
```

```python
import functools
import math

import jax
import jax.numpy as jnp
import numpy as np
from jax import lax
from jax.experimental import pallas as pl
from jax.experimental.pallas import tpu as pltpu

F32 = jnp.float32
BF16 = jnp.bfloat16

D_MODEL = 1024
HEAD_DIM = 64
ROPE_DIM = 16
ROPE_THETA = 500000.0
GRID_W = 64
NA_HEADS = 8
NA_WIN_R = 8
NA_WIN_C = 16
DIFF_HEADS = 4
DIL_PATTERNS = ((128, 1), (512, 4), (2048, 16))
N_GROUPS = 4
EXPERTS_PER_GROUP = 4
N_EXPERTS = 16
D_FF_EXPERT = D_MODEL // 2
NORM_EPS = 1e-6
SUBLN_EPS = 1e-5
NEG_INF = -1e30

LANES = 128
MXU_N = 256
VMEM_LIMIT = 56 * 1024 * 1024

A_W, B_W, C_W = 512, 512, 768
N_STEPS = len(DIL_PATTERNS)
GROUP_W = C_W // N_STEPS
_MAIN_ORDER = ("qb", "kb", "va", "qa", "ga", "gb", "gc", "ka", "vb")
_REF_ORDER = ("qa", "ka", "va", "qb", "kb", "vb", "qc", "kc", "vc", "ga", "gb", "gc")
_WIDTH = dict(qa=A_W, ka=A_W, va=A_W, qb=B_W, kb=B_W, vb=B_W, qc=C_W, kc=C_W, vc=C_W,
              ga=D_MODEL, gb=D_MODEL, gc=D_MODEL)


def _offsets(order):
    off, out = 0, {}
    for name in order:
        out[name] = off
        off += _WIDTH[name]
    return out, off


_OFF, MAIN_W = _offsets(_MAIN_ORDER)
_REF_OFF, IN_W = _offsets(_REF_ORDER)
OFF_QA, OFF_KA, OFF_VA = _OFF["qa"], _OFF["ka"], _OFF["va"]
OFF_QB, OFF_KB, OFF_VB = _OFF["qb"], _OFF["kb"], _OFF["vb"]
OFF_GATE = _OFF["ga"]
MAIN_STEP_W = MAIN_W // N_STEPS
STEP_W = MAIN_STEP_W + 3 * GROUP_W
MAIN_ROPE_W = 2 * B_W
QB_SCALE = HEAD_DIM ** -0.5 * math.log2(math.e)
assert OFF_GATE % D_MODEL == 0 and OFF_QB == 0 and OFF_KB == B_W and MAIN_ROPE_W <= MAIN_STEP_W
assert MAIN_W % N_STEPS == 0 and MAIN_STEP_W % MXU_N == 0 and N_STEPS * STEP_W == IN_W


def _permute_cols(w):
    cols = {n: w[..., _REF_OFF[n]:_REF_OFF[n] + _WIDTH[n]] for n in _MAIN_ORDER}
    cols["qb"] = cols["qb"] * QB_SCALE
    main = jnp.concatenate([cols[n] for n in _MAIN_ORDER], axis=-1)
    parts = []
    for s in range(N_STEPS):
        parts.append(main[..., s * MAIN_STEP_W:(s + 1) * MAIN_STEP_W])
        for n in ("qc", "kc", "vc"):
            parts.append(w[..., _REF_OFF[n] + s * GROUP_W:_REF_OFF[n] + (s + 1) * GROUP_W])
    return jnp.concatenate(parts, axis=-1)


def _cparams(*sem):
    return pltpu.CompilerParams(dimension_semantics=sem, vmem_limit_bytes=VMEM_LIMIT)


def _inproj_kernel(x_ref, g_ref, w_ref, rope_ref, main_ref, c0_ref, c1_ref, c2_ref, xn_ref, y_ref, *, tm):
    j = pl.program_id(1)
    c_refs = (c0_ref, c1_ref, c2_ref)
    reps = MXU_N // LANES
    half = ROPE_DIM // 2
    assert GROUP_W == MXU_N

    def rope(y):
        cos = jnp.concatenate([rope_ref[0]] * reps, axis=1)
        s_up = jnp.concatenate([rope_ref[1]] * reps, axis=1)
        s_dn = jnp.concatenate([rope_ref[2]] * reps, axis=1)
        return y * cos + pltpu.roll(y, MXU_N - half, 1) * s_up + pltpu.roll(y, half, 1) * s_dn

    def step(s):
        xn = xn_ref[...]
        for c in range(MAIN_STEP_W // MXU_N):
            sl = slice(c * MXU_N, (c + 1) * MXU_N)
            y = jnp.dot(xn, w_ref[:, sl], preferred_element_type=F32)
            if s == 0 and c < MAIN_ROPE_W // MXU_N:
                y = rope(y)
            main_ref[:, sl] = y.astype(main_ref.dtype)
        dil = DIL_PATTERNS[s][1]
        for c in range(3):
            wsl = slice(MAIN_STEP_W + c * GROUP_W, MAIN_STEP_W + (c + 1) * GROUP_W)
            osl = slice(c * GROUP_W, (c + 1) * GROUP_W)
            y = jnp.dot(xn, w_ref[:, wsl], preferred_element_type=F32)
            if c < 2:
                y = rope(y)
            if dil == 1:
                c_refs[s][0, :, osl] = y.astype(BF16)
            else:
                for hb in range(reps):
                    y_ref[c, hb] = y[:, hb * LANES:(hb + 1) * LANES]
                for p in range(dil):
                    for hb in range(reps):
                        c_refs[s][p, :, c * GROUP_W + hb * LANES:c * GROUP_W + (hb + 1) * LANES] = (
                            y_ref[c, hb, pl.ds(p, tm // dil, stride=dil), :].astype(BF16))

    @pl.when(j == 0)
    def _():
        x = x_ref[...]
        ms = jnp.mean(x * x, axis=-1, keepdims=True)
        xn_ref[...] = (x * lax.rsqrt(ms + NORM_EPS) * g_ref[...]).astype(BF16)
        step(0)

    for s in range(1, N_STEPS):
        pl.when(j == s)(functools.partial(step, s))


def _inproj(h2d, g, w_bf, rope_tab, layer, B, T, tm=1024):
    n = h2d.shape[0]
    tpb = T // tm
    c_specs, c_shapes = [], []
    for _, dil in DIL_PATTERNS:
        assert tm % dil == 0
        c_specs.append(pl.BlockSpec((None, dil, tm // dil, 3 * GROUP_W), lambda i, j: (i // tpb, 0, i % tpb, 0)))
        c_shapes.append(jax.ShapeDtypeStruct((B, dil, T // dil, 3 * GROUP_W), BF16))
    return pl.pallas_call(
        functools.partial(_inproj_kernel, tm=tm),
        grid=(n // tm, N_STEPS),
        in_specs=[
            pl.BlockSpec((tm, D_MODEL), lambda i, j: (i, 0)),
            pl.BlockSpec((None, 1, D_MODEL), lambda i, j: (layer, 0, 0)),
            pl.BlockSpec((None, D_MODEL, STEP_W), lambda i, j: (layer, 0, j)),
            pl.BlockSpec((3, tm, LANES), lambda i, j: (0, i % tpb, 0)),
        ],
        out_specs=[pl.BlockSpec((tm, MAIN_STEP_W), lambda i, j: (i, j))] + c_specs,
        out_shape=[jax.ShapeDtypeStruct((n, MAIN_W), BF16)] + c_shapes,
        scratch_shapes=[pltpu.VMEM((tm, D_MODEL), BF16), pltpu.VMEM((3, MXU_N // LANES, tm, LANES), F32)],
        compiler_params=_cparams("parallel", "arbitrary"),
        name="inproj",
    )(h2d, g, w_bf, rope_tab)


def _rope_table(T):
    half = ROPE_DIM // 2
    inv = 1.0 / (ROPE_THETA ** (jnp.arange(0, ROPE_DIM, 2, dtype=F32) / ROPE_DIM))
    ang = jnp.arange(T, dtype=F32)[:, None] * inv[None, :]
    cos, sin = jnp.cos(ang), jnp.sin(ang)
    zeros = jnp.zeros((T, HEAD_DIM - ROPE_DIM), F32)
    z8 = jnp.zeros((T, half), F32)
    c64 = jnp.concatenate([cos, cos, zeros + 1.0], axis=1)
    up64 = jnp.concatenate([-sin, z8, zeros], axis=1)
    dn64 = jnp.concatenate([z8, sin, zeros], axis=1)
    reps = LANES // HEAD_DIM
    return jnp.stack([jnp.tile(c64, (1, reps)), jnp.tile(up64, (1, reps)), jnp.tile(dn64, (1, reps))])


NA_ROWS_PER_ITER = 16


def _na_kernel(q_ref, k_ref, v_ref, b_ref, o_ref, *, rows, wr):
    lane = lax.broadcasted_iota(jnp.int32, (GRID_W, LANES), 1)
    lo = lane < HEAD_DIM
    scale = HEAD_DIM ** -0.5

    def row_group(gi, carry):
        scores, windows = [], []
        for u in range(NA_ROWS_PER_ITER):
            r = gi * NA_ROWS_PER_ITER + u
            r0 = jnp.clip(r - wr // 2, 0, rows - wr)
            d0 = r0 - r + NA_WIN_R - 1 - (NA_WIN_R - wr)
            q = q_ref[pl.ds(pl.multiple_of(r * GRID_W, GRID_W), GRID_W), :] * scale
            ks = pl.multiple_of(r0 * GRID_W, GRID_W)
            kw = k_ref[pl.ds(ks, wr * GRID_W), :]
            windows.append(ks)
            for hh in range(2):
                qh = jnp.where(lo if hh == 0 else jnp.logical_not(lo), q, jnp.zeros_like(q))
                s = lax.dot_general(qh, kw, (((1,), (1,)), ((), ())), preferred_element_type=F32)
                scores.append(s + b_ref[hh, d0])
        probs, sums = [], []
        for s in scores:
            p = jnp.exp(s - jnp.max(s, axis=-1, keepdims=True))
            sums.append(jnp.sum(p, axis=-1, keepdims=True))
            probs.append(p.astype(BF16))
        for u in range(NA_ROWS_PER_ITER):
            r = gi * NA_ROWS_PER_ITER + u
            vw = v_ref[pl.ds(windows[u], wr * GRID_W), :]
            outs = [jnp.dot(probs[2 * u + hh], vw, preferred_element_type=F32) / sums[2 * u + hh] for hh in range(2)]
            o = jnp.where(lo, outs[0], outs[1])
            o_ref[pl.ds(pl.multiple_of(r * GRID_W, GRID_W), GRID_W), :] = o.astype(o_ref.dtype)
        return carry

    lax.fori_loop(0, rows // NA_ROWS_PER_ITER, row_group, 0)


def _na_bias_table(rpb, wr):
    qc = np.arange(GRID_W)[:, None]
    kc = np.arange(GRID_W)[None, :]
    c0 = np.clip(qc - NA_WIN_C // 2, 0, GRID_W - NA_WIN_C)
    ok = (kc >= c0) & (kc < c0 + NA_WIN_C)
    dc = np.clip(kc - qc + NA_WIN_C - 1, 0, 2 * NA_WIN_C - 2)
    onehot = (np.arange(2 * NA_WIN_C - 1)[:, None, None] == dc[None]).astype(np.float32)
    b = jnp.einsum("lhrd,dqk->lhrqk", rpb.astype(F32), onehot, precision=lax.Precision.HIGHEST)
    b = jnp.where(jnp.asarray(ok), b, NEG_INF)
    row_idx = (NA_WIN_R - wr) + np.arange(wr)[:, None] + np.arange(wr)[None, :]
    t = b[:, :, row_idx]
    n_l, n_h = rpb.shape[0], rpb.shape[1]
    return t.transpose(0, 1, 2, 4, 3, 5).reshape(n_l, n_h, wr, GRID_W, wr * GRID_W)


def _na_attention(proj, bias_tab, layer, B, T):
    rows = T // GRID_W
    wr = min(NA_WIN_R, rows)
    cq, ck, cv = OFF_QA // LANES, OFF_KA // LANES, OFF_VA // LANES
    return pl.pallas_call(
        functools.partial(_na_kernel, rows=rows, wr=wr),
        grid=(B, NA_HEADS // 2),
        in_specs=[
            pl.BlockSpec((T, LANES), lambda b, h: (b, cq + h)),
            pl.BlockSpec((T, LANES), lambda b, h: (b, ck + h)),
            pl.BlockSpec((T, LANES), lambda b, h: (b, cv + h)),
            pl.BlockSpec((None, 2, wr, GRID_W, wr * GRID_W), lambda b, h: (layer, h, 0, 0, 0)),
        ],
        out_specs=pl.BlockSpec((T, LANES), lambda b, h: (b, h)),
        out_shape=jax.ShapeDtypeStruct((B * T, A_W), BF16),
        compiler_params=_cparams("parallel", "arbitrary"),
        name="na_attn",
    )(proj, proj, proj, bias_tab)


DIFF_KEY_CHUNK = 512


def _diff_kernel(lq1_ref, lk1_ref, lq2_ref, lk2_ref, q_ref, k_ref, v_ref, g_ref, o_ref, vt_ref, *, lam_init, tq):
    lam = (jnp.exp(jnp.sum(lq1_ref[...] * lk1_ref[...], keepdims=True))
           - jnp.exp(jnp.sum(lq2_ref[...] * lk2_ref[...], keepdims=True)) + lam_init)
    T = k_ref.shape[0]
    ck = DIFF_KEY_CHUNK
    n_chunks = T // ck
    lane = lax.broadcasted_iota(jnp.int32, (tq, LANES), 1)
    lo = lane < HEAD_DIM
    nt = (((1,), (1,)), ((), ()))
    vt_ref[...] = v_ref[...].T

    def q_block(i, carry):
        rows = pl.ds(pl.multiple_of(i * tq, tq), tq)
        q = q_ref[rows, :]
        zero = jnp.zeros_like(q)
        qs = (jnp.where(lo, q, zero), jnp.where(lo, zero, q))

        def scores(c):
            kc = k_ref[c * ck:(c + 1) * ck, :]
            return [lax.dot_general(kc, qm, nt, preferred_element_type=F32) for qm in qs]

        m = [jnp.full((1, tq), NEG_INF, F32)] * 2
        l = [jnp.zeros((1, tq), F32)] * 2
        acc = [jnp.zeros((LANES, tq), F32)] * 2
        s_next = scores(0)
        for c in range(n_chunks):
            s_cur = s_next
            if c + 1 < n_chunks:
                s_next = scores(c + 1)
            vt = vt_ref[:, c * ck:(c + 1) * ck]
            for j in range(2):
                m_new = jnp.maximum(m[j], jnp.max(s_cur[j], axis=0, keepdims=True))
                alpha = jnp.exp2(m[j] - m_new)
                p = jnp.exp2(s_cur[j] - m_new)
                l[j] = alpha * l[j] + jnp.sum(p, axis=0, keepdims=True)
                acc[j] = alpha * acc[j] + jnp.dot(vt, p.astype(BF16), preferred_element_type=F32)
                m[j] = m_new
        ot = acc[0] / l[0] - lam * (acc[1] / l[1])
        o = ot.T
        ms = jnp.mean(o * o, axis=-1, keepdims=True)
        o = o * lax.rsqrt(ms + SUBLN_EPS) * g_ref[...] * (1.0 - lam_init)
        o_ref[rows, :] = o.astype(o_ref.dtype)
        return carry

    lax.fori_loop(0, q_ref.shape[0] // tq, q_block, 0, unroll=2)


def _diff_attention(proj, lq1, lk1, lq2, lk2, subln_g, layer, lam_init, B, T, tq=512):
    cq, ck, cv = OFF_QB // LANES, OFF_KB // LANES, OFF_VB // LANES
    vec = pl.BlockSpec((None, 1, HEAD_DIM), lambda b, h: (layer, 0, 0))
    return pl.pallas_call(
        functools.partial(_diff_kernel, lam_init=lam_init, tq=tq),
        grid=(B, DIFF_HEADS),
        in_specs=[
            vec, vec, vec, vec,
            pl.BlockSpec((T, LANES), lambda b, h: (b, cq + h)),
            pl.BlockSpec((T, LANES), lambda b, h: (b, ck + h)),
            pl.BlockSpec((T, LANES), lambda b, h: (b, cv + h)),
            pl.BlockSpec((None, 1, LANES), lambda b, h: (layer, 0, 0)),
        ],
        out_specs=pl.BlockSpec((T, LANES), lambda b, h: (b, h)),
        out_shape=jax.ShapeDtypeStruct((B * T, B_W), BF16),
        scratch_shapes=[pltpu.VMEM((LANES, T), BF16)],
        compiler_params=_cparams("parallel", "arbitrary"),
        name="diff_attn",
    )(lq1, lk1, lq2, lk2, proj, proj, proj, subln_g)


DIL_Q = 128


DIL_BLOCKS_PER_ITER = 4


def _dil_kernel(q_ref, k_ref, v_ref, o_ref, lse_ref, *, L, dil, radius):
    kw_len = DIL_Q + 2 * radius
    lane = lax.broadcasted_iota(jnp.int32, (DIL_Q, LANES), 1)
    lo = lane < HEAD_DIM
    rel = (lax.broadcasted_iota(jnp.int32, (DIL_Q, kw_len), 1)
           - lax.broadcasted_iota(jnp.int32, (DIL_Q, kw_len), 0))
    scale = HEAD_DIM ** -0.5
    nt = (((1,), (1,)), ((), ()))

    blocks_per_phase = L // DIL_Q

    def blocks(gi, carry):
        scores, starts, maxes = [], [], []
        for u in range(DIL_BLOCKS_PER_ITER):
            item = gi * DIL_BLOCKS_PER_ITER + u
            p = item // blocks_per_phase
            l0 = pl.multiple_of((item % blocks_per_phase) * DIL_Q, DIL_Q)
            ks = pl.multiple_of(jnp.clip(l0 - radius, 0, L - kw_len), radius)
            q = q_ref[p, pl.ds(l0, DIL_Q), :] * scale
            kw = k_ref[p, pl.ds(ks, kw_len), :]
            ok = jnp.abs(rel + (ks - l0)) <= radius
            starts.append((p, l0, ks))
            for hh in range(2):
                qh = jnp.where(lo if hh == 0 else jnp.logical_not(lo), q, jnp.zeros_like(q))
                s = lax.dot_general(qh, kw, nt, preferred_element_type=F32)
                scores.append(jnp.where(ok, s, NEG_INF))
        probs, sums = [], []
        for s in scores:
            m = jnp.max(s, axis=-1, keepdims=True)
            e = jnp.exp(s - m)
            maxes.append(m)
            sums.append(jnp.sum(e, axis=-1, keepdims=True))
            probs.append(e.astype(BF16))
        for u in range(DIL_BLOCKS_PER_ITER):
            p, l0, ks = starts[u]
            vw = v_ref[p, pl.ds(ks, kw_len), :]
            outs = [jnp.dot(probs[2 * u + hh], vw, preferred_element_type=F32) / sums[2 * u + hh] for hh in range(2)]
            lses = [maxes[2 * u + hh] + jnp.log(sums[2 * u + hh]) for hh in range(2)]
            o = jnp.where(lo, outs[0], outs[1])
            lse = jnp.where(lo, lses[0], lses[1])
            if dil == 1:
                o_ref[pl.ds(l0, DIL_Q), :] = o
                lse_ref[pl.ds(l0, DIL_Q), :] = lse
            else:
                o_ref[pl.ds(l0 * dil + p, DIL_Q, stride=dil), :] = o
                lse_ref[pl.ds(l0 * dil + p, DIL_Q, stride=dil), :] = lse
        return carry

    lax.fori_loop(0, dil * blocks_per_phase // DIL_BLOCKS_PER_ITER, blocks, 0)


def _dilated_group(cg, radius):
    B, dil, L, _ = cg.shape
    assert L >= DIL_Q + 2 * radius and L % DIL_Q == 0 and (dil * L // DIL_Q) % DIL_BLOCKS_PER_ITER == 0
    hp = GROUP_W // LANES
    in_spec = lambda c: pl.BlockSpec((None, dil, L, LANES), lambda b, h: (b, 0, 0, c * hp + h))
    out_spec = pl.BlockSpec((L * dil, LANES), lambda b, h: (b, h))
    out_shape = jax.ShapeDtypeStruct((B * L * dil, GROUP_W), F32)
    return pl.pallas_call(
        functools.partial(_dil_kernel, L=L, dil=dil, radius=radius),
        grid=(B, hp),
        in_specs=[in_spec(0), in_spec(1), in_spec(2)],
        out_specs=[out_spec, out_spec],
        out_shape=[out_shape, out_shape],
        compiler_params=_cparams("parallel", "arbitrary"),
        name="dil_attn",
    )(cg, cg, cg)


def _dilated_branch(cgs):
    outs, lses = [], []
    for cg, (window, dil) in zip(cgs, DIL_PATTERNS):
        o, lse = _dilated_group(cg, window // (2 * dil))
        outs.append(o)
        lses.append(lse)
    return outs, lses


def _merge_kernel(h_ref, ga_ref, gb_ref, gc_ref, ya_ref, yb_ref, o0_ref, o1_ref, o2_ref,
                  l0_ref, l1_ref, l2_ref, wpa32_ref, wpb32_ref, wpc32_ref, wo32_ref, out_ref,
                  wpa_ref, wpb_ref, wpc_ref, wo_ref):
    @pl.when(pl.program_id(0) == 0)
    def _():
        for src, dst in ((wpa32_ref, wpa_ref), (wpb32_ref, wpb_ref), (wpc32_ref, wpc_ref), (wo32_ref, wo_ref)):
            dst[...] = src[...].astype(BF16)

    l0, l1, l2 = l0_ref[...], l1_ref[...], l2_ref[...]
    m = jnp.maximum(jnp.maximum(l0, l1), l2)
    e0, e1, e2 = jnp.exp(l0 - m), jnp.exp(l1 - m), jnp.exp(l2 - m)
    yc = (o0_ref[...].astype(F32) * e0 + o1_ref[...].astype(F32) * e1 + o2_ref[...].astype(F32) * e2) / (e0 + e1 + e2)
    merged = jax.nn.sigmoid(ga_ref[...].astype(F32)) * jnp.dot(ya_ref[...], wpa_ref[...], preferred_element_type=F32)
    merged += jax.nn.sigmoid(gb_ref[...].astype(F32)) * jnp.dot(yb_ref[...], wpb_ref[...], preferred_element_type=F32)
    merged += jax.nn.sigmoid(gc_ref[...].astype(F32)) * jnp.dot(yc.astype(BF16), wpc_ref[...],
                                                                preferred_element_type=F32)
    out_ref[...] = h_ref[...] + jnp.dot(merged.astype(BF16), wo_ref[...], preferred_element_type=F32)


def _merge(h2d, proj, ya, yb, outs, lses, wpa, wpb, wpc, wo, layer, tm=512):
    n = h2d.shape[0]
    gw = GROUP_W
    row = lambda w: pl.BlockSpec((tm, w), lambda i: (i, 0))
    full = lambda a: pl.BlockSpec((None,) + a.shape[1:], lambda i: (layer, 0, 0))
    gate = lambda c: pl.BlockSpec((tm, D_MODEL), lambda i: (i, OFF_GATE // D_MODEL + c))
    return pl.pallas_call(
        _merge_kernel,
        grid=(n // tm,),
        in_specs=[row(D_MODEL), gate(0), gate(1), gate(2), row(A_W), row(B_W),
                  row(gw), row(gw), row(gw), row(gw), row(gw), row(gw),
                  full(wpa), full(wpb), full(wpc), full(wo)],
        out_specs=row(D_MODEL),
        out_shape=jax.ShapeDtypeStruct((n, D_MODEL), F32),
        scratch_shapes=[pltpu.VMEM(a.shape[1:], BF16) for a in (wpa, wpb, wpc, wo)],
        compiler_params=_cparams("arbitrary"),
        name="merge_outproj",
    )(h2d, proj, proj, proj, ya, yb, *outs, *lses, wpa, wpb, wpc, wo)


MOE_TILE = 512
MOE_TOKENS_PER_STEP = 512
ROUTE_ROWS = 8


ROUTER_ROWS = 32


def _router_kernel(h_ref, g_ref, wr_ref, br_ref, upper_ref, route_ref, wts_ref, cnt_ref, base_ref):
    i = pl.program_id(0)
    tm = h_ref.shape[0]
    lane_reps = tm // LANES
    row = lax.broadcasted_iota(jnp.int32, (ROUTER_ROWS, tm), 0)
    rowf = row.astype(F32)
    big = float(ROUTER_ROWS)
    nt = (((1,), (1,)), ((), ()))

    @pl.when(i == 0)
    def _():
        base_ref[...] = jnp.zeros_like(base_ref)

    x = h_ref[...]
    ms = jnp.mean(x * x, axis=-1, keepdims=True)
    xn = x * lax.rsqrt(ms + NORM_EPS) * g_ref[...]
    logits = lax.dot_general(wr_ref[...], xn, nt, preferred_element_type=F32, precision=lax.Precision.HIGHEST)
    logits = logits + jnp.concatenate([br_ref[...]] * lane_reps, axis=1)
    gl = jnp.where(row < N_GROUPS, logits, -jnp.inf)
    gmax = jnp.max(gl, axis=0, keepdims=True)
    g_sel = jnp.min(jnp.where(gl == gmax, rowf, big), axis=0, keepdims=True).astype(jnp.int32)
    g_gate = 1.0 / jnp.sum(jnp.exp(gl - gmax), axis=0, keepdims=True)
    eidx = row - N_GROUPS
    in_grp = (eidx >= g_sel * EXPERTS_PER_GROUP) & (eidx < (g_sel + 1) * EXPERTS_PER_GROUP)
    el = jnp.where(in_grp, logits, -jnp.inf)
    t1 = jnp.max(el, axis=0, keepdims=True)
    i1 = jnp.min(jnp.where(el == t1, rowf, big), axis=0, keepdims=True).astype(jnp.int32)
    el2 = jnp.where(row == i1, -jnp.inf, el)
    t2 = jnp.max(el2, axis=0, keepdims=True)
    i2 = jnp.min(jnp.where(el2 == t2, rowf, big), axis=0, keepdims=True).astype(jnp.int32)
    x2 = jnp.exp(t2 - t1)
    den = 1.0 + x2
    wts_ref[:, :LANES] = jnp.broadcast_to(g_gate / den, (LANES, tm)).T
    wts_ref[:, LANES:] = jnp.broadcast_to(g_gate * x2 / den, (LANES, tm)).T
    member = jnp.where(row == i1, 1.0, 0.0) + jnp.where(row == i2, 1.0, 0.0)
    earlier = jnp.dot(member.astype(BF16), upper_ref[...], preferred_element_type=F32)
    earlier = earlier + jnp.concatenate([base_ref[...]] * lane_reps, axis=1)
    rank1 = jnp.sum(jnp.where(row == i1, earlier, 0.0), axis=0, keepdims=True)
    rank2 = jnp.sum(jnp.where(row == i2, earlier, 0.0), axis=0, keepdims=True)
    base_ref[...] = base_ref[...] + jnp.sum(member, axis=1, keepdims=True)
    cnt_ref[...] = base_ref[...]
    out_row = lax.broadcasted_iota(jnp.int32, (ROUTE_ROWS, tm), 0)
    route_ref[...] = jnp.where(out_row == 0, (i1 - N_GROUPS).astype(F32),
                               jnp.where(out_row == 1, (i2 - N_GROUPS).astype(F32),
                                         jnp.where(out_row == 2, rank1, jnp.where(out_row == 3, rank2, 0.0))))


def _router(h2d, g, w_router, b_router, layer, tm=1024):
    n = h2d.shape[0]
    return pl.pallas_call(
        _router_kernel,
        grid=(n // tm,),
        in_specs=[
            pl.BlockSpec((tm, D_MODEL), lambda i: (i, 0)),
            pl.BlockSpec((None, 1, D_MODEL), lambda i: (layer, 0, 0)),
            pl.BlockSpec((None, ROUTER_ROWS, D_MODEL), lambda i: (layer, 0, 0)),
            pl.BlockSpec((None, ROUTER_ROWS, LANES), lambda i: (layer, 0, 0)),
            pl.BlockSpec((tm, tm), lambda i: (0, 0)),
        ],
        out_specs=[pl.BlockSpec((ROUTE_ROWS, tm), lambda i: (0, i)),
                   pl.BlockSpec((tm, 2 * LANES), lambda i: (i, 0)),
                   pl.BlockSpec((ROUTER_ROWS, LANES), lambda i: (0, 0))],
        out_shape=[jax.ShapeDtypeStruct((ROUTE_ROWS, n), F32), jax.ShapeDtypeStruct((n, 2 * LANES), F32),
                   jax.ShapeDtypeStruct((ROUTER_ROWS, LANES), F32)],
        scratch_shapes=[pltpu.VMEM((ROUTER_ROWS, LANES), F32)],
        compiler_params=_cparams("arbitrary"),
        name="moe_router",
    )(h2d, g, w_router, b_router, jnp.triu(jnp.ones((tm, tm), BF16), k=1))


def _dispatch_tables(route, counts, n):
    cnt = counts[N_GROUPS:N_GROUPS + N_EXPERTS, 0].astype(jnp.int32)
    padded = (cnt + MOE_TILE - 1) // MOE_TILE * MOE_TILE
    ends = jnp.cumsum(padded)
    starts = ends - padded
    eids = jnp.arange(N_EXPERTS, dtype=jnp.int32)
    pos = []
    for k in range(2):
        e = route[k].astype(jnp.int32)
        start_e = jnp.sum(jnp.where(e[:, None] == eids[None, :], starts[None, :], 0), axis=1)
        pos.append(start_e + route[2 + k].astype(jnp.int32))
    tb = MOE_TOKENS_PER_STEP
    idx = jnp.concatenate([pos[0].reshape(n // tb, 1, tb), pos[1].reshape(n // tb, 1, tb)], axis=-1)
    n_tiles = 2 * n // MOE_TILE + N_EXPERTS
    tile_ids = jnp.arange(n_tiles, dtype=jnp.int32)
    n_used = ends[-1] // MOE_TILE
    tile_e = jnp.sum((tile_ids[:, None] * MOE_TILE >= ends[None, :]).astype(jnp.int32), axis=1)
    last_e = jnp.max(jnp.where(tile_ids < n_used, tile_e, 0))
    tile_e = jnp.where(tile_ids < n_used, tile_e, last_e).astype(jnp.int32)
    zero_rows = jnp.where(padded > 0, ends - MOE_TILE, n_tiles * MOE_TILE).astype(jnp.int32)
    return idx, tile_e, n_used.reshape(1).astype(jnp.int32), zero_rows, n_tiles


def _dispatch_kernel(zero_rows_ref, n_used_ref, idx_ref, h_ref, xs_hbm, zeros_ref, sem):
    i = pl.program_id(0)
    tb = MOE_TOKENS_PER_STEP
    n_tiles = xs_hbm.shape[0] // MOE_TILE - 1

    def zero_tile(row0):
        return pltpu.make_async_copy(zeros_ref, xs_hbm.at[pl.ds(pl.multiple_of(row0, MOE_TILE), MOE_TILE)], sem)

    @pl.when(i == 0)
    def _():
        zeros_ref[...] = jnp.zeros_like(zeros_ref)
        fills = [zero_tile(zero_rows_ref[e]) for e in range(N_EXPERTS)]
        for c in fills:
            c.start()
        for c in fills:
            c.wait()

        def tail(j, carry):
            tile = n_used_ref[0] + j

            @pl.when(tile <= n_tiles)
            def _():
                c = zero_tile(tile * MOE_TILE)
                c.start()
                c.wait()
            return carry

        lax.fori_loop(0, N_EXPERTS + 1, tail, 0)

    def row(t, carry):
        src = h_ref.at[pl.ds(t, 1)]
        pltpu.make_async_copy(src, xs_hbm.at[pl.ds(idx_ref[0, 0, t], 1)], sem).start()
        pltpu.make_async_copy(src, xs_hbm.at[pl.ds(idx_ref[0, 0, tb + t], 1)], sem).start()
        return carry

    lax.fori_loop(0, tb, row, 0, unroll=True)
    for _ in range(2):
        pltpu.make_async_copy(h_ref, xs_hbm.at[pl.ds(0, tb)], sem).wait()


def _dispatch(h2d, idx, zero_rows, n_used, n_tiles):
    n = h2d.shape[0]
    tb = MOE_TOKENS_PER_STEP
    assert 2 * n // MOE_TILE + N_EXPERTS == n_tiles
    return pl.pallas_call(
        _dispatch_kernel,
        grid_spec=pltpu.PrefetchScalarGridSpec(
            num_scalar_prefetch=2,
            grid=(n // tb,),
            in_specs=[pl.BlockSpec((1, 1, 2 * tb), lambda i, z, u: (i, 0, 0), memory_space=pltpu.SMEM),
                      pl.BlockSpec((tb, D_MODEL), lambda i, z, u: (i, 0))],
            out_specs=pl.BlockSpec(memory_space=pl.ANY),
            scratch_shapes=[pltpu.VMEM((MOE_TILE, D_MODEL), F32), pltpu.SemaphoreType.DMA(())],
        ),
        out_shape=jax.ShapeDtypeStruct(((n_tiles + 1) * MOE_TILE, D_MODEL), F32),
        compiler_params=pltpu.CompilerParams(dimension_semantics=("arbitrary",), vmem_limit_bytes=VMEM_LIMIT,
                                             disable_bounds_checks=True),
        name="moe_dispatch",
    )(zero_rows, n_used, idx, h2d)


def _expert_kernel(tile_e_ref, n_used_ref, x_ref, g_ref, w1_ref, w3_ref, w2_ref, y_ref, w1b_ref, w3b_ref, w2b_ref):
    i = pl.program_id(0)
    used = i < n_used_ref[0]

    @pl.when(jnp.logical_not(used))
    def _():
        y_ref[...] = jnp.zeros_like(y_ref)

    @pl.when(used & ((i == 0) | (tile_e_ref[i] != tile_e_ref[jnp.maximum(i - 1, 0)])))
    def _():
        w1b_ref[...] = w1_ref[...].astype(BF16)
        w3b_ref[...] = w3_ref[...].astype(BF16)
        w2b_ref[...] = w2_ref[...].astype(BF16)

    @pl.when(used)
    def _():
        x = x_ref[...]
        ms = jnp.mean(x * x, axis=-1, keepdims=True)
        xn = (x * lax.rsqrt(ms + NORM_EPS) * g_ref[...]).astype(BF16)
        a = jnp.dot(xn, w1b_ref[...], preferred_element_type=F32)
        b = jnp.dot(xn, w3b_ref[...], preferred_element_type=F32)
        hmid = (a * jax.nn.sigmoid(a) * b).astype(BF16)
        y_ref[...] = jnp.dot(hmid, w2b_ref[...], preferred_element_type=F32)


def _experts(xs, g, w1, w3, w2, tile_e, n_used, layer, n_tiles):
    row_map = lambda i, te, nu: (i, 0)
    w_map = lambda i, te, nu: (layer, te[i], 0, 0)
    return pl.pallas_call(
        _expert_kernel,
        grid_spec=pltpu.PrefetchScalarGridSpec(
            num_scalar_prefetch=2,
            grid=(n_tiles,),
            in_specs=[pl.BlockSpec((MOE_TILE, D_MODEL), row_map),
                      pl.BlockSpec((None, 1, D_MODEL), lambda i, te, nu: (layer, 0, 0)),
                      pl.BlockSpec((None, None, D_MODEL, D_FF_EXPERT), w_map),
                      pl.BlockSpec((None, None, D_MODEL, D_FF_EXPERT), w_map),
                      pl.BlockSpec((None, None, D_FF_EXPERT, D_MODEL), w_map)],
            out_specs=pl.BlockSpec((MOE_TILE, D_MODEL), row_map),
            scratch_shapes=[pltpu.VMEM((D_MODEL, D_FF_EXPERT), BF16), pltpu.VMEM((D_MODEL, D_FF_EXPERT), BF16),
                            pltpu.VMEM((D_FF_EXPERT, D_MODEL), BF16)],
        ),
        out_shape=jax.ShapeDtypeStruct((n_tiles * MOE_TILE, D_MODEL), F32),
        compiler_params=_cparams("arbitrary"),
        name="moe_experts",
    )(tile_e, n_used, xs, g, w1, w3, w2)


def _combine_kernel(idx_ref, ys_hbm, h_ref, wts_ref, gf_ref, o_ref, buf_ref, sem, *, final_norm):
    tb = h_ref.shape[0]

    def row(t, carry):
        pltpu.make_async_copy(ys_hbm.at[pl.ds(idx_ref[0, 0, t], 1)], buf_ref.at[0, pl.ds(t, 1)], sem).start()
        pltpu.make_async_copy(ys_hbm.at[pl.ds(idx_ref[0, 0, tb + t], 1)], buf_ref.at[1, pl.ds(t, 1)], sem).start()
        return carry

    lax.fori_loop(0, tb, row, 0, unroll=True)
    for k in range(2):
        pltpu.make_async_copy(ys_hbm.at[pl.ds(0, tb)], buf_ref.at[k], sem).wait()
    reps = D_MODEL // LANES
    w = wts_ref[...]
    w_top1 = jnp.concatenate([w[:, :LANES]] * reps, axis=1)
    w_top2 = jnp.concatenate([w[:, LANES:]] * reps, axis=1)
    out = h_ref[...] + w_top1 * buf_ref[0] + w_top2 * buf_ref[1]
    if final_norm:
        ms = jnp.mean(out * out, axis=-1, keepdims=True)
        out = out * lax.rsqrt(ms + NORM_EPS) * gf_ref[...]
    o_ref[...] = out


def _combine(h2d, ys, idx, wts, g_final, final_norm, batch):
    n = h2d.shape[0]
    tb = MOE_TOKENS_PER_STEP
    steps_per_seq = n // batch // tb
    return pl.pallas_call(
        functools.partial(_combine_kernel, final_norm=final_norm),
        grid=(n // tb,),
        in_specs=[pl.BlockSpec((1, 1, 2 * tb), lambda i: (i, 0, 0), memory_space=pltpu.SMEM),
                  pl.BlockSpec(memory_space=pl.ANY),
                  pl.BlockSpec((tb, D_MODEL), lambda i: (i, 0)),
                  pl.BlockSpec((tb, 2 * LANES), lambda i: (i, 0)),
                  pl.BlockSpec((1, D_MODEL), lambda i: (0, 0))],
        out_specs=pl.BlockSpec((None, tb, D_MODEL), lambda i: (i // steps_per_seq, i % steps_per_seq, 0)),
        out_shape=jax.ShapeDtypeStruct((batch, n // batch, D_MODEL), F32),
        scratch_shapes=[pltpu.VMEM((2, tb, D_MODEL), F32), pltpu.SemaphoreType.DMA(())],
        compiler_params=pltpu.CompilerParams(dimension_semantics=("arbitrary",), vmem_limit_bytes=VMEM_LIMIT,
                                             disable_bounds_checks=True),
        name="moe_combine",
    )(idx, ys, h2d, wts, g_final)


def _moe(h2d, g, w_router, b_router, w1, w3, w2, layer, g_final, final_norm, batch):
    n = h2d.shape[0]
    route, wts, counts = _router(h2d, g, w_router, b_router, layer)
    idx, tile_e, n_used, zero_rows, n_tiles = _dispatch_tables(route, counts, n)
    xs = _dispatch(h2d, idx, zero_rows, n_used, n_tiles)
    ys = _experts(xs, g, w1, w3, w2, tile_e, n_used, layer, n_tiles)
    return _combine(h2d, ys, idx, wts, g_final, final_norm, batch)


def _router_params(wg, bg, we, be):
    n_l = wg.shape[0]
    pad = ROUTER_ROWS - N_GROUPS - N_EXPERTS
    w = jnp.concatenate([wg, we, jnp.zeros((n_l, D_MODEL, pad), F32)], axis=-1).astype(F32).transpose(0, 2, 1)
    b = jnp.concatenate([bg, be, jnp.zeros((n_l, pad), F32)], axis=-1).astype(F32)
    return w, jnp.broadcast_to(b[:, :, None], (n_l, ROUTER_ROWS, LANES))


def kernel(x, w_in, na_rpb, lam_q1, lam_k1, lam_q2, lam_k2, diff_subln, w_pa, w_pb, w_pc, w_o, norm_mix, norm_ffn,
           router_group_w, router_group_b, router_expert_w, router_expert_b, w1, w3, w2, norm_final):
    B, T, D = x.shape
    depth = w_in.shape[0]
    rows = T // GRID_W
    wr = min(NA_WIN_R, rows)
    rope_tab = _rope_table(T)
    row3 = lambda a: a[:, None, :]
    w_in_bf = _permute_cols(w_in).astype(BF16)
    bias_tab = _na_bias_table(na_rpb, wr)
    g_mix, g_ffn = row3(norm_mix), row3(norm_ffn)
    lams = [row3(a) for a in (lam_q1, lam_k1, lam_q2, lam_k2)]
    subln = row3(diff_subln)
    w_router, b_router = _router_params(router_group_w, router_group_b, router_expert_w, router_expert_b)
    h = x.reshape(B * T, D)
    for l in range(depth):
        lam_init = 0.8 - 0.6 * math.exp(-0.3 * l)
        proj, *cgs = _inproj(h, g_mix, w_in_bf, rope_tab, l, B, T)
        ya = _na_attention(proj, bias_tab, l, B, T)
        yb = _diff_attention(proj, *lams, subln, l, lam_init, B, T)
        outs, lses = _dilated_branch(cgs)
        h = _merge(h, proj, ya, yb, outs, lses, w_pa, w_pb, w_pc, w_o, l)
        out = _moe(h, g_ffn, w_router, b_router, w1, w3, w2, l, norm_final[None, :], l == depth - 1, B)
        h = out.reshape(B * T, D)
    return out
```

```python
import functools
import math

import jax
import jax.numpy as jnp
import numpy as np
from jax import lax
from jax.experimental import pallas as pl
from jax.experimental.pallas import tpu as pltpu

F32 = jnp.float32
BF16 = jnp.bfloat16

D_MODEL = 1024
HEAD_DIM = 64
ROPE_DIM = 16
ROPE_THETA = 500000.0
GRID_W = 64
NA_HEADS = 8
NA_WIN_R = 8
NA_WIN_C = 16
DIFF_HEADS = 4
DIL_PATTERNS = ((128, 1), (512, 4), (2048, 16))
N_GROUPS = 4
EXPERTS_PER_GROUP = 4
N_EXPERTS = 16
D_FF_EXPERT = D_MODEL // 2
NORM_EPS = 1e-6
SUBLN_EPS = 1e-5
NEG_INF = -1e30

LANES = 128
MXU_N = 256
VMEM_LIMIT = 56 * 1024 * 1024

A_W, B_W, C_W = 512, 512, 768
N_STEPS = len(DIL_PATTERNS)
GROUP_W = C_W // N_STEPS
_MAIN_ORDER = ("qb", "kb", "va", "qa", "ga", "gb", "gc", "ka", "vb")
_REF_ORDER = ("qa", "ka", "va", "qb", "kb", "vb", "qc", "kc", "vc", "ga", "gb", "gc")
_WIDTH = dict(qa=A_W, ka=A_W, va=A_W, qb=B_W, kb=B_W, vb=B_W, qc=C_W, kc=C_W, vc=C_W,
              ga=D_MODEL, gb=D_MODEL, gc=D_MODEL)


def _offsets(order):
    off, out = 0, {}
    for name in order:
        out[name] = off
        off += _WIDTH[name]
    return out, off


_OFF, MAIN_W = _offsets(_MAIN_ORDER)
_REF_OFF, IN_W = _offsets(_REF_ORDER)
OFF_QA, OFF_KA, OFF_VA = _OFF["qa"], _OFF["ka"], _OFF["va"]
OFF_QB, OFF_KB, OFF_VB = _OFF["qb"], _OFF["kb"], _OFF["vb"]
OFF_GATE = _OFF["ga"]
MAIN_STEP_W = MAIN_W // N_STEPS
STEP_W = MAIN_STEP_W + 3 * GROUP_W
MAIN_ROPE_W = 2 * B_W
QB_SCALE = HEAD_DIM ** -0.5 * math.log2(math.e)
assert OFF_GATE % D_MODEL == 0 and OFF_QB == 0 and OFF_KB == B_W and MAIN_ROPE_W <= MAIN_STEP_W
assert MAIN_W % N_STEPS == 0 and MAIN_STEP_W % MXU_N == 0 and N_STEPS * STEP_W == IN_W


def _permute_cols(w):
    def piece(name, lo, hi):
        p = w[..., _REF_OFF[name] + lo:_REF_OFF[name] + hi]
        return (p * QB_SCALE if name == "qb" else p).astype(BF16)

    parts = []
    for s in range(N_STEPS):
        for n in _MAIN_ORDER:
            lo = max(_OFF[n], s * MAIN_STEP_W) - _OFF[n]
            hi = min(_OFF[n] + _WIDTH[n], (s + 1) * MAIN_STEP_W) - _OFF[n]
            if lo < hi:
                parts.append(piece(n, lo, hi))
        for n in ("qc", "kc", "vc"):
            parts.append(piece(n, s * GROUP_W, (s + 1) * GROUP_W))
    return jnp.concatenate(parts, axis=-1)


def _cparams(*sem):
    return pltpu.CompilerParams(dimension_semantics=sem, vmem_limit_bytes=VMEM_LIMIT)


def _inproj_kernel(x_ref, g_ref, w_ref, rope_ref, main_ref, c0_ref, c1_ref, c2_ref, xn_ref, y_ref, *, tm):
    j = pl.program_id(1)
    c_refs = (c0_ref, c1_ref, c2_ref)
    reps = MXU_N // LANES
    half = ROPE_DIM // 2
    assert GROUP_W == MXU_N

    def rope(y):
        cos = jnp.concatenate([rope_ref[0]] * reps, axis=1)
        s_up = jnp.concatenate([rope_ref[1]] * reps, axis=1)
        s_dn = jnp.concatenate([rope_ref[2]] * reps, axis=1)
        return y * cos + pltpu.roll(y, MXU_N - half, 1) * s_up + pltpu.roll(y, half, 1) * s_dn

    def step(s):
        xn = xn_ref[...]
        for c in range(MAIN_STEP_W // MXU_N):
            sl = slice(c * MXU_N, (c + 1) * MXU_N)
            y = jnp.dot(xn, w_ref[:, sl], preferred_element_type=F32)
            if s == 0 and c < MAIN_ROPE_W // MXU_N:
                y = rope(y)
            main_ref[:, sl] = y.astype(main_ref.dtype)
        dil = DIL_PATTERNS[s][1]
        for c in range(3):
            wsl = slice(MAIN_STEP_W + c * GROUP_W, MAIN_STEP_W + (c + 1) * GROUP_W)
            osl = slice(c * GROUP_W, (c + 1) * GROUP_W)
            y = jnp.dot(xn, w_ref[:, wsl], preferred_element_type=F32)
            if c < 2:
                y = rope(y)
            if dil == 1:
                c_refs[s][0, :, osl] = y.astype(BF16)
            else:
                for hb in range(reps):
                    y_ref[c, hb] = y[:, hb * LANES:(hb + 1) * LANES]
                for p in range(dil):
                    for hb in range(reps):
                        c_refs[s][p, :, c * GROUP_W + hb * LANES:c * GROUP_W + (hb + 1) * LANES] = (
                            y_ref[c, hb, pl.ds(p, tm // dil, stride=dil), :].astype(BF16))

    @pl.when(j == 0)
    def _():
        x = x_ref[...]
        ms = jnp.mean(x * x, axis=-1, keepdims=True)
        xn_ref[...] = (x * lax.rsqrt(ms + NORM_EPS) * g_ref[...]).astype(BF16)
        step(0)

    for s in range(1, N_STEPS):
        pl.when(j == s)(functools.partial(step, s))


def _inproj(h2d, g, w_bf, rope_tab, layer, B, T, tm=1024):
    n = h2d.shape[0]
    tpb = T // tm
    c_specs, c_shapes = [], []
    for _, dil in DIL_PATTERNS:
        assert tm % dil == 0
        c_specs.append(pl.BlockSpec((None, dil, tm // dil, 3 * GROUP_W), lambda i, j: (i // tpb, 0, i % tpb, 0)))
        c_shapes.append(jax.ShapeDtypeStruct((B, dil, T // dil, 3 * GROUP_W), BF16))
    return pl.pallas_call(
        functools.partial(_inproj_kernel, tm=tm),
        grid=(n // tm, N_STEPS),
        in_specs=[
            pl.BlockSpec((tm, D_MODEL), lambda i, j: (i, 0)),
            pl.BlockSpec((None, 1, D_MODEL), lambda i, j: (layer, 0, 0)),
            pl.BlockSpec((None, D_MODEL, STEP_W), lambda i, j: (layer, 0, j)),
            pl.BlockSpec((3, tm, LANES), lambda i, j: (0, i % tpb, 0)),
        ],
        out_specs=[pl.BlockSpec((tm, MAIN_STEP_W), lambda i, j: (i, j))] + c_specs,
        out_shape=[jax.ShapeDtypeStruct((n, MAIN_W), BF16)] + c_shapes,
        scratch_shapes=[pltpu.VMEM((tm, D_MODEL), BF16), pltpu.VMEM((3, MXU_N // LANES, tm, LANES), F32)],
        compiler_params=_cparams("parallel", "arbitrary"),
        name="inproj",
    )(h2d, g, w_bf, rope_tab)


def _rope_table(T):
    half = ROPE_DIM // 2
    inv = 1.0 / (ROPE_THETA ** (jnp.arange(0, ROPE_DIM, 2, dtype=F32) / ROPE_DIM))
    ang = jnp.arange(T, dtype=F32)[:, None] * inv[None, :]
    cos, sin = jnp.cos(ang), jnp.sin(ang)
    zeros = jnp.zeros((T, HEAD_DIM - ROPE_DIM), F32)
    z8 = jnp.zeros((T, half), F32)
    c64 = jnp.concatenate([cos, cos, zeros + 1.0], axis=1)
    up64 = jnp.concatenate([-sin, z8, zeros], axis=1)
    dn64 = jnp.concatenate([z8, sin, zeros], axis=1)
    reps = LANES // HEAD_DIM
    return jnp.stack([jnp.tile(c64, (1, reps)), jnp.tile(up64, (1, reps)), jnp.tile(dn64, (1, reps))])


NA_ROWS_PER_ITER = 16


def _na_kernel(q_ref, k_ref, v_ref, b_ref, o_ref, *, rows, wr):
    lane = lax.broadcasted_iota(jnp.int32, (GRID_W, LANES), 1)
    lo = lane < HEAD_DIM
    scale = HEAD_DIM ** -0.5

    def row_group(gi, carry):
        scores, windows = [], []
        for u in range(NA_ROWS_PER_ITER):
            r = gi * NA_ROWS_PER_ITER + u
            r0 = jnp.clip(r - wr // 2, 0, rows - wr)
            d0 = r0 - r + NA_WIN_R - 1 - (NA_WIN_R - wr)
            q = q_ref[pl.ds(pl.multiple_of(r * GRID_W, GRID_W), GRID_W), :] * scale
            ks = pl.multiple_of(r0 * GRID_W, GRID_W)
            kw = k_ref[pl.ds(ks, wr * GRID_W), :]
            windows.append(ks)
            for hh in range(2):
                qh = jnp.where(lo if hh == 0 else jnp.logical_not(lo), q, jnp.zeros_like(q))
                s = lax.dot_general(qh, kw, (((1,), (1,)), ((), ())), preferred_element_type=F32)
                bias = jnp.concatenate([b_ref[hh, d0 + 2 * i] for i in range(wr // 2)], axis=1)
                scores.append(s + bias)
        probs, sums = [], []
        for s in scores:
            p = jnp.exp(s - jnp.max(s, axis=-1, keepdims=True))
            sums.append(jnp.sum(p, axis=-1, keepdims=True))
            probs.append(p.astype(BF16))
        for u in range(NA_ROWS_PER_ITER):
            r = gi * NA_ROWS_PER_ITER + u
            vw = v_ref[pl.ds(windows[u], wr * GRID_W), :]
            outs = [jnp.dot(probs[2 * u + hh], vw, preferred_element_type=F32) / sums[2 * u + hh] for hh in range(2)]
            o = jnp.where(lo, outs[0], outs[1])
            o_ref[pl.ds(pl.multiple_of(r * GRID_W, GRID_W), GRID_W), :] = o.astype(o_ref.dtype)
        return carry

    lax.fori_loop(0, rows // NA_ROWS_PER_ITER, row_group, 0)


def _na_bias_table(rpb, wr):
    qc = np.arange(GRID_W)[:, None]
    kc = np.arange(GRID_W)[None, :]
    c0 = np.clip(qc - NA_WIN_C // 2, 0, GRID_W - NA_WIN_C)
    ok = (kc >= c0) & (kc < c0 + NA_WIN_C)
    dc = np.clip(kc - qc + NA_WIN_C - 1, 0, 2 * NA_WIN_C - 2)
    onehot = (np.arange(2 * NA_WIN_C - 1)[:, None, None] == dc[None]).astype(np.float32)
    b = jnp.einsum("lhrd,dqk->lhrqk", rpb.astype(F32), onehot, precision=lax.Precision.HIGHEST)
    b = jnp.where(jnp.asarray(ok), b, NEG_INF)
    off = NA_WIN_R - wr
    n_pairs = 2 * wr - 2
    return jnp.concatenate([b[:, :, off:off + n_pairs], b[:, :, off + 1:off + 1 + n_pairs]], axis=-1)


def _na_attention(proj, bias_tab, layer, B, T):
    rows = T // GRID_W
    wr = min(NA_WIN_R, rows)
    cq, ck, cv = OFF_QA // LANES, OFF_KA // LANES, OFF_VA // LANES
    assert wr % 2 == 0 and 2 * GRID_W == LANES and rows % NA_ROWS_PER_ITER == 0
    return pl.pallas_call(
        functools.partial(_na_kernel, rows=rows, wr=wr),
        grid=(B, NA_HEADS // 2),
        in_specs=[
            pl.BlockSpec((T, LANES), lambda b, h: (b, cq + h)),
            pl.BlockSpec((T, LANES), lambda b, h: (b, ck + h)),
            pl.BlockSpec((T, LANES), lambda b, h: (b, cv + h)),
            pl.BlockSpec((None, 2, 2 * wr - 2, GRID_W, 2 * GRID_W), lambda b, h: (layer, h, 0, 0, 0)),
        ],
        out_specs=pl.BlockSpec((T, LANES), lambda b, h: (b, h)),
        out_shape=jax.ShapeDtypeStruct((B * T, A_W), BF16),
        compiler_params=_cparams("parallel", "arbitrary"),
        name="na_attn",
    )(proj, proj, proj, bias_tab)


DIFF_KEY_CHUNK = 512


def _diff_kernel(lq1_ref, lk1_ref, lq2_ref, lk2_ref, q_ref, k_ref, v_ref, g_ref, o_ref, vt_ref, *, lam_init, tq):
    lam = (jnp.exp(jnp.sum(lq1_ref[...] * lk1_ref[...], keepdims=True))
           - jnp.exp(jnp.sum(lq2_ref[...] * lk2_ref[...], keepdims=True)) + lam_init)
    T = k_ref.shape[0]
    ck = DIFF_KEY_CHUNK
    n_chunks = T // ck
    lane = lax.broadcasted_iota(jnp.int32, (tq, LANES), 1)
    lo = lane < HEAD_DIM
    nt = (((1,), (1,)), ((), ()))
    vt_ref[...] = v_ref[...].T

    def q_block(i, carry):
        rows = pl.ds(pl.multiple_of(i * tq, tq), tq)
        q = q_ref[rows, :]
        zero = jnp.zeros_like(q)
        qs = (jnp.where(lo, q, zero), jnp.where(lo, zero, q))

        def scores(c):
            kc = k_ref[c * ck:(c + 1) * ck, :]
            return [lax.dot_general(kc, qm, nt, preferred_element_type=F32) for qm in qs]

        m = [jnp.full((1, tq), NEG_INF, F32)] * 2
        l = [jnp.zeros((1, tq), F32)] * 2
        acc = [jnp.zeros((LANES, tq), F32)] * 2
        s_next = scores(0)
        for c in range(n_chunks):
            s_cur = s_next
            if c + 1 < n_chunks:
                s_next = scores(c + 1)
            vt = vt_ref[:, c * ck:(c + 1) * ck]
            for j in range(2):
                m_new = jnp.maximum(m[j], jnp.max(s_cur[j], axis=0, keepdims=True))
                alpha = jnp.exp2(m[j] - m_new)
                p = jnp.exp2(s_cur[j] - m_new)
                l[j] = alpha * l[j] + jnp.sum(p, axis=0, keepdims=True)
                acc[j] = alpha * acc[j] + jnp.dot(vt, p.astype(BF16), preferred_element_type=F32)
                m[j] = m_new
        ot = acc[0] / l[0] - lam * (acc[1] / l[1])
        o = ot.T
        ms = jnp.mean(o * o, axis=-1, keepdims=True)
        o = o * lax.rsqrt(ms + SUBLN_EPS) * g_ref[...] * (1.0 - lam_init)
        o_ref[rows, :] = o.astype(o_ref.dtype)
        return carry

    lax.fori_loop(0, q_ref.shape[0] // tq, q_block, 0, unroll=2)


def _diff_attention(proj, lq1, lk1, lq2, lk2, subln_g, layer, lam_init, B, T, tq=512):
    cq, ck, cv = OFF_QB // LANES, OFF_KB // LANES, OFF_VB // LANES
    vec = pl.BlockSpec((None, 1, HEAD_DIM), lambda b, h: (layer, 0, 0))
    return pl.pallas_call(
        functools.partial(_diff_kernel, lam_init=lam_init, tq=tq),
        grid=(B, DIFF_HEADS),
        in_specs=[
            vec, vec, vec, vec,
            pl.BlockSpec((T, LANES), lambda b, h: (b, cq + h)),
            pl.BlockSpec((T, LANES), lambda b, h: (b, ck + h)),
            pl.BlockSpec((T, LANES), lambda b, h: (b, cv + h)),
            pl.BlockSpec((None, 1, LANES), lambda b, h: (layer, 0, 0)),
        ],
        out_specs=pl.BlockSpec((T, LANES), lambda b, h: (b, h)),
        out_shape=jax.ShapeDtypeStruct((B * T, B_W), BF16),
        scratch_shapes=[pltpu.VMEM((LANES, T), BF16)],
        compiler_params=_cparams("parallel", "arbitrary"),
        name="diff_attn",
    )(lq1, lk1, lq2, lk2, proj, proj, proj, subln_g)


DIL_Q = 128


DIL_BLOCKS_PER_ITER = 4


def _dil_kernel(q_ref, k_ref, v_ref, o_ref, lse_ref, *, L, dil, radius):
    kw_len = DIL_Q + 2 * radius
    lane = lax.broadcasted_iota(jnp.int32, (DIL_Q, LANES), 1)
    lo = lane < HEAD_DIM
    rel = (lax.broadcasted_iota(jnp.int32, (DIL_Q, kw_len), 1)
           - lax.broadcasted_iota(jnp.int32, (DIL_Q, kw_len), 0))
    scale = HEAD_DIM ** -0.5
    nt = (((1,), (1,)), ((), ()))

    blocks_per_phase = L // DIL_Q

    def blocks(gi, carry):
        scores, starts, maxes = [], [], []
        for u in range(DIL_BLOCKS_PER_ITER):
            item = gi * DIL_BLOCKS_PER_ITER + u
            p = item // blocks_per_phase
            l0 = pl.multiple_of((item % blocks_per_phase) * DIL_Q, DIL_Q)
            ks = pl.multiple_of(jnp.clip(l0 - radius, 0, L - kw_len), radius)
            q = q_ref[p, pl.ds(l0, DIL_Q), :] * scale
            kw = k_ref[p, pl.ds(ks, kw_len), :]
            ok = jnp.abs(rel + (ks - l0)) <= radius
            starts.append((p, l0, ks))
            for hh in range(2):
                qh = jnp.where(lo if hh == 0 else jnp.logical_not(lo), q, jnp.zeros_like(q))
                s = lax.dot_general(qh, kw, nt, preferred_element_type=F32)
                scores.append(jnp.where(ok, s, NEG_INF))
        probs, sums = [], []
        for s in scores:
            m = jnp.max(s, axis=-1, keepdims=True)
            e = jnp.exp(s - m)
            maxes.append(m)
            sums.append(jnp.sum(e, axis=-1, keepdims=True))
            probs.append(e.astype(BF16))
        for u in range(DIL_BLOCKS_PER_ITER):
            p, l0, ks = starts[u]
            vw = v_ref[p, pl.ds(ks, kw_len), :]
            outs = [jnp.dot(probs[2 * u + hh], vw, preferred_element_type=F32) / sums[2 * u + hh] for hh in range(2)]
            lses = [maxes[2 * u + hh] + jnp.log(sums[2 * u + hh]) for hh in range(2)]
            o = jnp.where(lo, outs[0], outs[1])
            lse = jnp.where(lo, lses[0], lses[1])
            if dil == 1:
                o_ref[pl.ds(l0, DIL_Q), :] = o
                lse_ref[pl.ds(l0, DIL_Q), :] = lse
            else:
                o_ref[pl.ds(l0 * dil + p, DIL_Q, stride=dil), :] = o
                lse_ref[pl.ds(l0 * dil + p, DIL_Q, stride=dil), :] = lse
        return carry

    lax.fori_loop(0, dil * blocks_per_phase // DIL_BLOCKS_PER_ITER, blocks, 0)


def _dilated_group(cg, radius):
    B, dil, L, _ = cg.shape
    assert L >= DIL_Q + 2 * radius and L % DIL_Q == 0 and (dil * L // DIL_Q) % DIL_BLOCKS_PER_ITER == 0
    hp = GROUP_W // LANES
    in_spec = lambda c: pl.BlockSpec((None, dil, L, LANES), lambda b, h: (b, 0, 0, c * hp + h))
    out_spec = pl.BlockSpec((L * dil, LANES), lambda b, h: (b, h))
    out_shape = jax.ShapeDtypeStruct((B * L * dil, GROUP_W), F32)
    return pl.pallas_call(
        functools.partial(_dil_kernel, L=L, dil=dil, radius=radius),
        grid=(B, hp),
        in_specs=[in_spec(0), in_spec(1), in_spec(2)],
        out_specs=[out_spec, out_spec],
        out_shape=[out_shape, out_shape],
        compiler_params=_cparams("parallel", "arbitrary"),
        name="dil_attn",
    )(cg, cg, cg)


def _dilated_branch(cgs):
    outs, lses = [], []
    for cg, (window, dil) in zip(cgs, DIL_PATTERNS):
        o, lse = _dilated_group(cg, window // (2 * dil))
        outs.append(o)
        lses.append(lse)
    return outs, lses


def _merge_kernel(h_ref, ga_ref, gb_ref, gc_ref, ya_ref, yb_ref, o0_ref, o1_ref, o2_ref,
                  l0_ref, l1_ref, l2_ref, wpa32_ref, wpb32_ref, wpc32_ref, wo32_ref, out_ref,
                  wpa_ref, wpb_ref, wpc_ref, wo_ref):
    @pl.when(pl.program_id(0) == 0)
    def _():
        for src, dst in ((wpa32_ref, wpa_ref), (wpb32_ref, wpb_ref), (wpc32_ref, wpc_ref), (wo32_ref, wo_ref)):
            dst[...] = src[...].astype(BF16)

    l0, l1, l2 = l0_ref[...], l1_ref[...], l2_ref[...]
    m = jnp.maximum(jnp.maximum(l0, l1), l2)
    e0, e1, e2 = jnp.exp(l0 - m), jnp.exp(l1 - m), jnp.exp(l2 - m)
    yc = (o0_ref[...].astype(F32) * e0 + o1_ref[...].astype(F32) * e1 + o2_ref[...].astype(F32) * e2) / (e0 + e1 + e2)
    merged = jax.nn.sigmoid(ga_ref[...].astype(F32)) * jnp.dot(ya_ref[...], wpa_ref[...], preferred_element_type=F32)
    merged += jax.nn.sigmoid(gb_ref[...].astype(F32)) * jnp.dot(yb_ref[...], wpb_ref[...], preferred_element_type=F32)
    merged += jax.nn.sigmoid(gc_ref[...].astype(F32)) * jnp.dot(yc.astype(BF16), wpc_ref[...],
                                                                preferred_element_type=F32)
    out_ref[...] = h_ref[...] + jnp.dot(merged.astype(BF16), wo_ref[...], preferred_element_type=F32)


def _merge(h2d, proj, ya, yb, outs, lses, wpa, wpb, wpc, wo, layer, tm=512):
    n = h2d.shape[0]
    gw = GROUP_W
    row = lambda w: pl.BlockSpec((tm, w), lambda i: (i, 0))
    full = lambda a: pl.BlockSpec((None,) + a.shape[1:], lambda i: (layer, 0, 0))
    gate = lambda c: pl.BlockSpec((tm, D_MODEL), lambda i: (i, OFF_GATE // D_MODEL + c))
    return pl.pallas_call(
        _merge_kernel,
        grid=(n // tm,),
        in_specs=[row(D_MODEL), gate(0), gate(1), gate(2), row(A_W), row(B_W),
                  row(gw), row(gw), row(gw), row(gw), row(gw), row(gw),
                  full(wpa), full(wpb), full(wpc), full(wo)],
        out_specs=row(D_MODEL),
        out_shape=jax.ShapeDtypeStruct((n, D_MODEL), F32),
        scratch_shapes=[pltpu.VMEM(a.shape[1:], BF16) for a in (wpa, wpb, wpc, wo)],
        compiler_params=_cparams("arbitrary"),
        name="merge_outproj",
    )(h2d, proj, proj, proj, ya, yb, *outs, *lses, wpa, wpb, wpc, wo)


MOE_TILE = 512
MOE_TOKENS_PER_STEP = 512
ROUTE_ROWS = 8


ROUTER_ROWS = 32


def _router_kernel(h_ref, g_ref, wr_ref, br_ref, upper_ref, route_ref, wts_ref, cnt_ref, base_ref):
    i = pl.program_id(0)
    tm = h_ref.shape[0]
    lane_reps = tm // LANES
    row = lax.broadcasted_iota(jnp.int32, (ROUTER_ROWS, tm), 0)
    rowf = row.astype(F32)
    big = float(ROUTER_ROWS)
    nt = (((1,), (1,)), ((), ()))

    @pl.when(i == 0)
    def _():
        base_ref[...] = jnp.zeros_like(base_ref)

    x = h_ref[...]
    ms = jnp.mean(x * x, axis=-1, keepdims=True)
    xn = x * lax.rsqrt(ms + NORM_EPS) * g_ref[...]
    logits = lax.dot_general(wr_ref[...], xn, nt, preferred_element_type=F32, precision=lax.Precision.HIGHEST)
    logits = logits + jnp.concatenate([br_ref[...]] * lane_reps, axis=1)
    gl = jnp.where(row < N_GROUPS, logits, -jnp.inf)
    gmax = jnp.max(gl, axis=0, keepdims=True)
    g_sel = jnp.min(jnp.where(gl == gmax, rowf, big), axis=0, keepdims=True).astype(jnp.int32)
    g_gate = 1.0 / jnp.sum(jnp.exp(gl - gmax), axis=0, keepdims=True)
    eidx = row - N_GROUPS
    in_grp = (eidx >= g_sel * EXPERTS_PER_GROUP) & (eidx < (g_sel + 1) * EXPERTS_PER_GROUP)
    el = jnp.where(in_grp, logits, -jnp.inf)
    t1 = jnp.max(el, axis=0, keepdims=True)
    i1 = jnp.min(jnp.where(el == t1, rowf, big), axis=0, keepdims=True).astype(jnp.int32)
    el2 = jnp.where(row == i1, -jnp.inf, el)
    t2 = jnp.max(el2, axis=0, keepdims=True)
    i2 = jnp.min(jnp.where(el2 == t2, rowf, big), axis=0, keepdims=True).astype(jnp.int32)
    x2 = jnp.exp(t2 - t1)
    den = 1.0 + x2
    wts_ref[:, :LANES] = jnp.broadcast_to(g_gate / den, (LANES, tm)).T
    wts_ref[:, LANES:] = jnp.broadcast_to(g_gate * x2 / den, (LANES, tm)).T
    member = jnp.where(row == i1, 1.0, 0.0) + jnp.where(row == i2, 1.0, 0.0)
    earlier = jnp.dot(member.astype(BF16), upper_ref[...], preferred_element_type=F32)
    earlier = earlier + jnp.concatenate([base_ref[...]] * lane_reps, axis=1)
    rank1 = jnp.sum(jnp.where(row == i1, earlier, 0.0), axis=0, keepdims=True)
    rank2 = jnp.sum(jnp.where(row == i2, earlier, 0.0), axis=0, keepdims=True)
    base_ref[...] = base_ref[...] + jnp.sum(member, axis=1, keepdims=True)
    cnt_ref[...] = base_ref[...]
    out_row = lax.broadcasted_iota(jnp.int32, (ROUTE_ROWS, tm), 0)
    route_ref[...] = jnp.where(out_row == 0, (i1 - N_GROUPS).astype(F32),
                               jnp.where(out_row == 1, (i2 - N_GROUPS).astype(F32),
                                         jnp.where(out_row == 2, rank1, jnp.where(out_row == 3, rank2, 0.0))))


def _router(h2d, g, w_router, b_router, layer, tm=1024):
    n = h2d.shape[0]
    return pl.pallas_call(
        _router_kernel,
        grid=(n // tm,),
        in_specs=[
            pl.BlockSpec((tm, D_MODEL), lambda i: (i, 0)),
            pl.BlockSpec((None, 1, D_MODEL), lambda i: (layer, 0, 0)),
            pl.BlockSpec((None, ROUTER_ROWS, D_MODEL), lambda i: (layer, 0, 0)),
            pl.BlockSpec((None, ROUTER_ROWS, LANES), lambda i: (layer, 0, 0)),
            pl.BlockSpec((tm, tm), lambda i: (0, 0)),
        ],
        out_specs=[pl.BlockSpec((ROUTE_ROWS, tm), lambda i: (0, i)),
                   pl.BlockSpec((tm, 2 * LANES), lambda i: (i, 0)),
                   pl.BlockSpec((ROUTER_ROWS, LANES), lambda i: (0, 0))],
        out_shape=[jax.ShapeDtypeStruct((ROUTE_ROWS, n), F32), jax.ShapeDtypeStruct((n, 2 * LANES), F32),
                   jax.ShapeDtypeStruct((ROUTER_ROWS, LANES), F32)],
        scratch_shapes=[pltpu.VMEM((ROUTER_ROWS, LANES), F32)],
        compiler_params=_cparams("arbitrary"),
        name="moe_router",
    )(h2d, g, w_router, b_router, jnp.triu(jnp.ones((tm, tm), BF16), k=1))


def _dispatch_tables(route, counts, n):
    cnt = counts[N_GROUPS:N_GROUPS + N_EXPERTS, 0].astype(jnp.int32)
    padded = (cnt + MOE_TILE - 1) // MOE_TILE * MOE_TILE
    ends = jnp.cumsum(padded)
    starts = ends - padded
    eids = jnp.arange(N_EXPERTS, dtype=jnp.int32)
    pos = []
    for k in range(2):
        e = route[k].astype(jnp.int32)
        start_e = jnp.sum(jnp.where(e[:, None] == eids[None, :], starts[None, :], 0), axis=1)
        pos.append(start_e + route[2 + k].astype(jnp.int32))
    tb = MOE_TOKENS_PER_STEP
    idx = jnp.concatenate([pos[0].reshape(n // tb, 1, tb), pos[1].reshape(n // tb, 1, tb)], axis=-1)
    n_tiles = 2 * n // MOE_TILE + N_EXPERTS
    tile_ids = jnp.arange(n_tiles, dtype=jnp.int32)
    n_used = ends[-1] // MOE_TILE
    tile_e = jnp.sum((tile_ids[:, None] * MOE_TILE >= ends[None, :]).astype(jnp.int32), axis=1)
    last_e = jnp.max(jnp.where(tile_ids < n_used, tile_e, 0))
    tile_e = jnp.where(tile_ids < n_used, tile_e, last_e).astype(jnp.int32)
    zero_rows = jnp.where(padded > 0, ends - MOE_TILE, n_tiles * MOE_TILE).astype(jnp.int32)
    return idx, tile_e, n_used.reshape(1).astype(jnp.int32), zero_rows, n_tiles


def _dispatch_kernel(zero_rows_ref, n_used_ref, idx_ref, h_ref, xs_hbm, zeros_ref, sem):
    i = pl.program_id(0)
    tb = MOE_TOKENS_PER_STEP
    n_tiles = xs_hbm.shape[0] // MOE_TILE - 1

    def zero_tile(row0):
        return pltpu.make_async_copy(zeros_ref, xs_hbm.at[pl.ds(pl.multiple_of(row0, MOE_TILE), MOE_TILE)], sem)

    @pl.when(i == 0)
    def _():
        zeros_ref[...] = jnp.zeros_like(zeros_ref)
        fills = [zero_tile(zero_rows_ref[e]) for e in range(N_EXPERTS)]
        for c in fills:
            c.start()
        for c in fills:
            c.wait()

        def tail(j, carry):
            tile = n_used_ref[0] + j

            @pl.when(tile <= n_tiles)
            def _():
                c = zero_tile(tile * MOE_TILE)
                c.start()
                c.wait()
            return carry

        lax.fori_loop(0, N_EXPERTS + 1, tail, 0)

    def row(t, carry):
        src = h_ref.at[pl.ds(t, 1)]
        pltpu.make_async_copy(src, xs_hbm.at[pl.ds(idx_ref[0, 0, t], 1)], sem).start()
        pltpu.make_async_copy(src, xs_hbm.at[pl.ds(idx_ref[0, 0, tb + t], 1)], sem).start()
        return carry

    lax.fori_loop(0, tb, row, 0, unroll=True)
    for _ in range(2):
        pltpu.make_async_copy(h_ref, xs_hbm.at[pl.ds(0, tb)], sem).wait()


def _dispatch(h2d, idx, zero_rows, n_used, n_tiles):
    n = h2d.shape[0]
    tb = MOE_TOKENS_PER_STEP
    assert 2 * n // MOE_TILE + N_EXPERTS == n_tiles
    return pl.pallas_call(
        _dispatch_kernel,
        grid_spec=pltpu.PrefetchScalarGridSpec(
            num_scalar_prefetch=2,
            grid=(n // tb,),
            in_specs=[pl.BlockSpec((1, 1, 2 * tb), lambda i, z, u: (i, 0, 0), memory_space=pltpu.SMEM),
                      pl.BlockSpec((tb, D_MODEL), lambda i, z, u: (i, 0))],
            out_specs=pl.BlockSpec(memory_space=pl.ANY),
            scratch_shapes=[pltpu.VMEM((MOE_TILE, D_MODEL), F32), pltpu.SemaphoreType.DMA(())],
        ),
        out_shape=jax.ShapeDtypeStruct(((n_tiles + 1) * MOE_TILE, D_MODEL), F32),
        compiler_params=pltpu.CompilerParams(dimension_semantics=("arbitrary",), vmem_limit_bytes=VMEM_LIMIT,
                                             disable_bounds_checks=True),
        name="moe_dispatch",
    )(zero_rows, n_used, idx, h2d)


def _expert_kernel(tile_e_ref, n_used_ref, x_ref, g_ref, w1_ref, w3_ref, w2_ref, y_ref, w1b_ref, w3b_ref, w2b_ref):
    i = pl.program_id(0)
    used = i < n_used_ref[0]

    @pl.when(jnp.logical_not(used))
    def _():
        y_ref[...] = jnp.zeros_like(y_ref)

    @pl.when(used & ((i == 0) | (tile_e_ref[i] != tile_e_ref[jnp.maximum(i - 1, 0)])))
    def _():
        w1b_ref[...] = w1_ref[...].astype(BF16)
        w3b_ref[...] = w3_ref[...].astype(BF16)
        w2b_ref[...] = w2_ref[...].astype(BF16)

    @pl.when(used)
    def _():
        x = x_ref[...]
        ms = jnp.mean(x * x, axis=-1, keepdims=True)
        xn = (x * lax.rsqrt(ms + NORM_EPS) * g_ref[...]).astype(BF16)
        a = jnp.dot(xn, w1b_ref[...], preferred_element_type=F32)
        b = jnp.dot(xn, w3b_ref[...], preferred_element_type=F32)
        hmid = (a * jax.nn.sigmoid(a) * b).astype(BF16)
        y_ref[...] = jnp.dot(hmid, w2b_ref[...], preferred_element_type=F32)


def _experts(xs, g, w1, w3, w2, tile_e, n_used, layer, n_tiles):
    row_map = lambda i, te, nu: (i, 0)
    w_map = lambda i, te, nu: (layer, te[i], 0, 0)
    return pl.pallas_call(
        _expert_kernel,
        grid_spec=pltpu.PrefetchScalarGridSpec(
            num_scalar_prefetch=2,
            grid=(n_tiles,),
            in_specs=[pl.BlockSpec((MOE_TILE, D_MODEL), row_map),
                      pl.BlockSpec((None, 1, D_MODEL), lambda i, te, nu: (layer, 0, 0)),
                      pl.BlockSpec((None, None, D_MODEL, D_FF_EXPERT), w_map),
                      pl.BlockSpec((None, None, D_MODEL, D_FF_EXPERT), w_map),
                      pl.BlockSpec((None, None, D_FF_EXPERT, D_MODEL), w_map)],
            out_specs=pl.BlockSpec((MOE_TILE, D_MODEL), row_map),
            scratch_shapes=[pltpu.VMEM((D_MODEL, D_FF_EXPERT), BF16), pltpu.VMEM((D_MODEL, D_FF_EXPERT), BF16),
                            pltpu.VMEM((D_FF_EXPERT, D_MODEL), BF16)],
        ),
        out_shape=jax.ShapeDtypeStruct((n_tiles * MOE_TILE, D_MODEL), F32),
        compiler_params=_cparams("arbitrary"),
        name="moe_experts",
    )(tile_e, n_used, xs, g, w1, w3, w2)


def _combine_kernel(idx_ref, ys_hbm, h_ref, wts_ref, gf_ref, o_ref, buf_ref, sem, *, final_norm):
    tb = h_ref.shape[0]

    def row(t, carry):
        pltpu.make_async_copy(ys_hbm.at[pl.ds(idx_ref[0, 0, t], 1)], buf_ref.at[0, pl.ds(t, 1)], sem).start()
        pltpu.make_async_copy(ys_hbm.at[pl.ds(idx_ref[0, 0, tb + t], 1)], buf_ref.at[1, pl.ds(t, 1)], sem).start()
        return carry

    lax.fori_loop(0, tb, row, 0, unroll=True)
    for k in range(2):
        pltpu.make_async_copy(ys_hbm.at[pl.ds(0, tb)], buf_ref.at[k], sem).wait()
    reps = D_MODEL // LANES
    w = wts_ref[...]
    w_top1 = jnp.concatenate([w[:, :LANES]] * reps, axis=1)
    w_top2 = jnp.concatenate([w[:, LANES:]] * reps, axis=1)
    out = h_ref[...] + w_top1 * buf_ref[0] + w_top2 * buf_ref[1]
    if final_norm:
        ms = jnp.mean(out * out, axis=-1, keepdims=True)
        out = out * lax.rsqrt(ms + NORM_EPS) * gf_ref[...]
    o_ref[...] = out


def _combine(h2d, ys, idx, wts, g_final, final_norm, batch):
    n = h2d.shape[0]
    tb = MOE_TOKENS_PER_STEP
    steps_per_seq = n // batch // tb
    return pl.pallas_call(
        functools.partial(_combine_kernel, final_norm=final_norm),
        grid=(n // tb,),
        in_specs=[pl.BlockSpec((1, 1, 2 * tb), lambda i: (i, 0, 0), memory_space=pltpu.SMEM),
                  pl.BlockSpec(memory_space=pl.ANY),
                  pl.BlockSpec((tb, D_MODEL), lambda i: (i, 0)),
                  pl.BlockSpec((tb, 2 * LANES), lambda i: (i, 0)),
                  pl.BlockSpec((1, D_MODEL), lambda i: (0, 0))],
        out_specs=pl.BlockSpec((None, tb, D_MODEL), lambda i: (i // steps_per_seq, i % steps_per_seq, 0)),
        out_shape=jax.ShapeDtypeStruct((batch, n // batch, D_MODEL), F32),
        scratch_shapes=[pltpu.VMEM((2, tb, D_MODEL), F32), pltpu.SemaphoreType.DMA(())],
        compiler_params=pltpu.CompilerParams(dimension_semantics=("arbitrary",), vmem_limit_bytes=VMEM_LIMIT,
                                             disable_bounds_checks=True),
        name="moe_combine",
    )(idx, ys, h2d, wts, g_final)


def _moe(h2d, g, w_router, b_router, w1, w3, w2, layer, g_final, final_norm, batch):
    n = h2d.shape[0]
    route, wts, counts = _router(h2d, g, w_router, b_router, layer)
    idx, tile_e, n_used, zero_rows, n_tiles = _dispatch_tables(route, counts, n)
    xs = _dispatch(h2d, idx, zero_rows, n_used, n_tiles)
    ys = _experts(xs, g, w1, w3, w2, tile_e, n_used, layer, n_tiles)
    return _combine(h2d, ys, idx, wts, g_final, final_norm, batch)


def _router_params(wg, bg, we, be):
    n_l = wg.shape[0]
    pad = ROUTER_ROWS - N_GROUPS - N_EXPERTS
    w = jnp.concatenate([wg, we, jnp.zeros((n_l, D_MODEL, pad), F32)], axis=-1).astype(F32).transpose(0, 2, 1)
    b = jnp.concatenate([bg, be, jnp.zeros((n_l, pad), F32)], axis=-1).astype(F32)
    return w, jnp.broadcast_to(b[:, :, None], (n_l, ROUTER_ROWS, LANES))


def kernel(x, w_in, na_rpb, lam_q1, lam_k1, lam_q2, lam_k2, diff_subln, w_pa, w_pb, w_pc, w_o, norm_mix, norm_ffn,
           router_group_w, router_group_b, router_expert_w, router_expert_b, w1, w3, w2, norm_final):
    B, T, D = x.shape
    depth = w_in.shape[0]
    rows = T // GRID_W
    wr = min(NA_WIN_R, rows)
    rope_tab = _rope_table(T)
    row3 = lambda a: a[:, None, :]
    w_in_bf = _permute_cols(w_in)
    bias_tab = _na_bias_table(na_rpb, wr)
    g_mix, g_ffn = row3(norm_mix), row3(norm_ffn)
    lams = [row3(a) for a in (lam_q1, lam_k1, lam_q2, lam_k2)]
    subln = row3(diff_subln)
    w_router, b_router = _router_params(router_group_w, router_group_b, router_expert_w, router_expert_b)
    h = x.reshape(B * T, D)
    for l in range(depth):
        lam_init = 0.8 - 0.6 * math.exp(-0.3 * l)
        proj, *cgs = _inproj(h, g_mix, w_in_bf, rope_tab, l, B, T)
        ya = _na_attention(proj, bias_tab, l, B, T)
        yb = _diff_attention(proj, *lams, subln, l, lam_init, B, T)
        outs, lses = _dilated_branch(cgs)
        h = _merge(h, proj, ya, yb, outs, lses, w_pa, w_pb, w_pc, w_o, l)
        out = _moe(h, g_ffn, w_router, b_router, w1, w3, w2, l, norm_final[None, :], l == depth - 1, B)
        h = out.reshape(B * T, D)
    return out
```

```python
import functools
import math

import jax
import jax.numpy as jnp
import numpy as np
from jax import lax
from jax.experimental import pallas as pl
from jax.experimental.pallas import tpu as pltpu

F32 = jnp.float32
BF16 = jnp.bfloat16

D_MODEL = 1024
HEAD_DIM = 64
ROPE_DIM = 16
ROPE_THETA = 500000.0
GRID_W = 64
NA_HEADS = 8
NA_WIN_R = 8
NA_WIN_C = 16
DIFF_HEADS = 4
DIL_PATTERNS = ((128, 1), (512, 4), (2048, 16))
N_GROUPS = 4
EXPERTS_PER_GROUP = 4
N_EXPERTS = 16
D_FF_EXPERT = D_MODEL // 2
NORM_EPS = 1e-6
SUBLN_EPS = 1e-5
NEG_INF = -1e30

LANES = 128
MXU_N = 256
VMEM_LIMIT = 56 * 1024 * 1024

A_W, B_W, C_W = 512, 512, 768
N_STEPS = len(DIL_PATTERNS)
GROUP_W = C_W // N_STEPS
_MAIN_ORDER = ("qb", "kb", "va", "qa", "ga", "gb", "gc", "ka", "vb")
_REF_ORDER = ("qa", "ka", "va", "qb", "kb", "vb", "qc", "kc", "vc", "ga", "gb", "gc")
_WIDTH = dict(qa=A_W, ka=A_W, va=A_W, qb=B_W, kb=B_W, vb=B_W, qc=C_W, kc=C_W, vc=C_W,
              ga=D_MODEL, gb=D_MODEL, gc=D_MODEL)


def _offsets(order):
    off, out = 0, {}
    for name in order:
        out[name] = off
        off += _WIDTH[name]
    return out, off


_OFF, MAIN_W = _offsets(_MAIN_ORDER)
_REF_OFF, IN_W = _offsets(_REF_ORDER)
OFF_QA, OFF_KA, OFF_VA = _OFF["qa"], _OFF["ka"], _OFF["va"]
OFF_QB, OFF_KB, OFF_VB = _OFF["qb"], _OFF["kb"], _OFF["vb"]
OFF_GATE = _OFF["ga"]
MAIN_STEP_W = MAIN_W // N_STEPS
STEP_W = MAIN_STEP_W + 3 * GROUP_W
MAIN_ROPE_W = 2 * B_W
QB_SCALE = HEAD_DIM ** -0.5 * math.log2(math.e)
assert OFF_GATE % D_MODEL == 0 and OFF_QB == 0 and OFF_KB == B_W and MAIN_ROPE_W <= MAIN_STEP_W
assert MAIN_W % N_STEPS == 0 and MAIN_STEP_W % MXU_N == 0 and N_STEPS * STEP_W == IN_W


def _permute_cols(w):
    def piece(name, lo, hi):
        p = w[..., _REF_OFF[name] + lo:_REF_OFF[name] + hi]
        return (p * QB_SCALE if name == "qb" else p).astype(BF16)

    parts = []
    for s in range(N_STEPS):
        for n in _MAIN_ORDER:
            lo = max(_OFF[n], s * MAIN_STEP_W) - _OFF[n]
            hi = min(_OFF[n] + _WIDTH[n], (s + 1) * MAIN_STEP_W) - _OFF[n]
            if lo < hi:
                parts.append(piece(n, lo, hi))
        for n in ("qc", "kc", "vc"):
            parts.append(piece(n, s * GROUP_W, (s + 1) * GROUP_W))
    return jnp.concatenate(parts, axis=-1)


def _cparams(*sem):
    return pltpu.CompilerParams(dimension_semantics=sem, vmem_limit_bytes=VMEM_LIMIT)


def _inproj_kernel(x_ref, g_ref, w_ref, rope_ref, main_ref, c0_ref, c1_ref, c2_ref, xn_ref, y_ref, *, tm):
    j = pl.program_id(1)
    c_refs = (c0_ref, c1_ref, c2_ref)
    reps = MXU_N // LANES
    half = ROPE_DIM // 2
    assert GROUP_W == MXU_N

    def rope(y):
        cos = jnp.concatenate([rope_ref[0]] * reps, axis=1)
        s_up = jnp.concatenate([rope_ref[1]] * reps, axis=1)
        s_dn = jnp.concatenate([rope_ref[2]] * reps, axis=1)
        return y * cos + pltpu.roll(y, MXU_N - half, 1) * s_up + pltpu.roll(y, half, 1) * s_dn

    def step(s):
        xn = xn_ref[...]
        for c in range(MAIN_STEP_W // MXU_N):
            sl = slice(c * MXU_N, (c + 1) * MXU_N)
            y = jnp.dot(xn, w_ref[:, sl], preferred_element_type=F32)
            if s == 0 and c < MAIN_ROPE_W // MXU_N:
                y = rope(y)
            main_ref[:, sl] = y.astype(main_ref.dtype)
        dil = DIL_PATTERNS[s][1]
        for c in range(3):
            wsl = slice(MAIN_STEP_W + c * GROUP_W, MAIN_STEP_W + (c + 1) * GROUP_W)
            osl = slice(c * GROUP_W, (c + 1) * GROUP_W)
            y = jnp.dot(xn, w_ref[:, wsl], preferred_element_type=F32)
            if c < 2:
                y = rope(y)
            if dil == 1:
                c_refs[s][0, :, osl] = y.astype(BF16)
            else:
                for hb in range(reps):
                    y_ref[c, hb] = y[:, hb * LANES:(hb + 1) * LANES]
                for p in range(dil):
                    for hb in range(reps):
                        c_refs[s][p, :, c * GROUP_W + hb * LANES:c * GROUP_W + (hb + 1) * LANES] = (
                            y_ref[c, hb, pl.ds(p, tm // dil, stride=dil), :].astype(BF16))

    @pl.when(j == 0)
    def _():
        x = x_ref[...]
        ms = jnp.mean(x * x, axis=-1, keepdims=True)
        xn_ref[...] = (x * lax.rsqrt(ms + NORM_EPS) * g_ref[...]).astype(BF16)
        step(0)

    for s in range(1, N_STEPS):
        pl.when(j == s)(functools.partial(step, s))


def _inproj(h2d, g, w_bf, rope_tab, layer, B, T, tm=1024):
    n = h2d.shape[0]
    tpb = T // tm
    c_specs, c_shapes = [], []
    for _, dil in DIL_PATTERNS:
        assert tm % dil == 0
        c_specs.append(pl.BlockSpec((None, dil, tm // dil, 3 * GROUP_W), lambda i, j: (i // tpb, 0, i % tpb, 0)))
        c_shapes.append(jax.ShapeDtypeStruct((B, dil, T // dil, 3 * GROUP_W), BF16))
    return pl.pallas_call(
        functools.partial(_inproj_kernel, tm=tm),
        grid=(n // tm, N_STEPS),
        in_specs=[
            pl.BlockSpec((tm, D_MODEL), lambda i, j: (i, 0)),
            pl.BlockSpec((None, 1, D_MODEL), lambda i, j: (layer, 0, 0)),
            pl.BlockSpec((None, D_MODEL, STEP_W), lambda i, j: (layer, 0, j)),
            pl.BlockSpec((3, tm, LANES), lambda i, j: (0, i % tpb, 0)),
        ],
        out_specs=[pl.BlockSpec((tm, MAIN_STEP_W), lambda i, j: (i, j))] + c_specs,
        out_shape=[jax.ShapeDtypeStruct((n, MAIN_W), BF16)] + c_shapes,
        scratch_shapes=[pltpu.VMEM((tm, D_MODEL), BF16), pltpu.VMEM((3, MXU_N // LANES, tm, LANES), F32)],
        compiler_params=_cparams("parallel", "arbitrary"),
        name="inproj",
    )(h2d, g, w_bf, rope_tab)


def _rope_table(T):
    half = ROPE_DIM // 2
    inv = 1.0 / (ROPE_THETA ** (jnp.arange(0, ROPE_DIM, 2, dtype=F32) / ROPE_DIM))
    ang = jnp.arange(T, dtype=F32)[:, None] * inv[None, :]
    cos, sin = jnp.cos(ang), jnp.sin(ang)
    zeros = jnp.zeros((T, HEAD_DIM - ROPE_DIM), F32)
    z8 = jnp.zeros((T, half), F32)
    c64 = jnp.concatenate([cos, cos, zeros + 1.0], axis=1)
    up64 = jnp.concatenate([-sin, z8, zeros], axis=1)
    dn64 = jnp.concatenate([z8, sin, zeros], axis=1)
    reps = LANES // HEAD_DIM
    return jnp.stack([jnp.tile(c64, (1, reps)), jnp.tile(up64, (1, reps)), jnp.tile(dn64, (1, reps))])


NA_ROWS_PER_ITER = 16


def _na_kernel(q_ref, k_ref, v_ref, b_ref, o_ref, *, rows, wr):
    lane = lax.broadcasted_iota(jnp.int32, (GRID_W, LANES), 1)
    lo = lane < HEAD_DIM
    scale = HEAD_DIM ** -0.5

    def row_group(gi, carry):
        scores, windows = [], []
        for u in range(NA_ROWS_PER_ITER):
            r = gi * NA_ROWS_PER_ITER + u
            r0 = jnp.clip(r - wr // 2, 0, rows - wr)
            d0 = r0 - r + NA_WIN_R - 1 - (NA_WIN_R - wr)
            q = q_ref[pl.ds(pl.multiple_of(r * GRID_W, GRID_W), GRID_W), :] * scale
            ks = pl.multiple_of(r0 * GRID_W, GRID_W)
            kw = k_ref[pl.ds(ks, wr * GRID_W), :]
            windows.append(ks)
            for hh in range(2):
                qh = jnp.where(lo if hh == 0 else jnp.logical_not(lo), q, jnp.zeros_like(q))
                s = lax.dot_general(qh, kw, (((1,), (1,)), ((), ())), preferred_element_type=F32)
                bias = jnp.concatenate([b_ref[hh, d0 + 2 * i] for i in range(wr // 2)], axis=1)
                scores.append(s + bias)
        probs, sums = [], []
        for s in scores:
            p = jnp.exp(s - jnp.max(s, axis=-1, keepdims=True))
            sums.append(jnp.sum(p, axis=-1, keepdims=True))
            probs.append(p.astype(BF16))
        for u in range(NA_ROWS_PER_ITER):
            r = gi * NA_ROWS_PER_ITER + u
            vw = v_ref[pl.ds(windows[u], wr * GRID_W), :]
            outs = [jnp.dot(probs[2 * u + hh], vw, preferred_element_type=F32) / sums[2 * u + hh] for hh in range(2)]
            o = jnp.where(lo, outs[0], outs[1])
            o_ref[pl.ds(pl.multiple_of(r * GRID_W, GRID_W), GRID_W), :] = o.astype(o_ref.dtype)
        return carry

    lax.fori_loop(0, rows // NA_ROWS_PER_ITER, row_group, 0)


def _na_bias_table(rpb, wr):
    qc = np.arange(GRID_W)[:, None]
    kc = np.arange(GRID_W)[None, :]
    c0 = np.clip(qc - NA_WIN_C // 2, 0, GRID_W - NA_WIN_C)
    ok = (kc >= c0) & (kc < c0 + NA_WIN_C)
    dc = np.clip(kc - qc + NA_WIN_C - 1, 0, 2 * NA_WIN_C - 2)
    onehot = (np.arange(2 * NA_WIN_C - 1)[:, None, None] == dc[None]).astype(np.float32)
    b = jnp.einsum("lhrd,dqk->lhrqk", rpb.astype(F32), onehot, precision=lax.Precision.HIGHEST)
    b = jnp.where(jnp.asarray(ok), b, NEG_INF)
    off = NA_WIN_R - wr
    n_pairs = 2 * wr - 2
    return jnp.concatenate([b[:, :, off:off + n_pairs], b[:, :, off + 1:off + 1 + n_pairs]], axis=-1)


def _na_attention(proj, bias_tab, layer, B, T):
    rows = T // GRID_W
    wr = min(NA_WIN_R, rows)
    cq, ck, cv = OFF_QA // LANES, OFF_KA // LANES, OFF_VA // LANES
    assert wr % 2 == 0 and 2 * GRID_W == LANES and rows % NA_ROWS_PER_ITER == 0
    return pl.pallas_call(
        functools.partial(_na_kernel, rows=rows, wr=wr),
        grid=(B, NA_HEADS // 2),
        in_specs=[
            pl.BlockSpec((T, LANES), lambda b, h: (b, cq + h)),
            pl.BlockSpec((T, LANES), lambda b, h: (b, ck + h)),
            pl.BlockSpec((T, LANES), lambda b, h: (b, cv + h)),
            pl.BlockSpec((None, 2, 2 * wr - 2, GRID_W, 2 * GRID_W), lambda b, h: (layer, h, 0, 0, 0)),
        ],
        out_specs=pl.BlockSpec((T, LANES), lambda b, h: (b, h)),
        out_shape=jax.ShapeDtypeStruct((B * T, A_W), BF16),
        compiler_params=_cparams("parallel", "arbitrary"),
        name="na_attn",
    )(proj, proj, proj, bias_tab)


DIFF_KEY_CHUNK = 512


def _diff_kernel(lq1_ref, lk1_ref, lq2_ref, lk2_ref, q_ref, k_ref, v_ref, g_ref, o_ref, vt_ref, *, lam_init, tq):
    lam = (jnp.exp(jnp.sum(lq1_ref[...] * lk1_ref[...], keepdims=True))
           - jnp.exp(jnp.sum(lq2_ref[...] * lk2_ref[...], keepdims=True)) + lam_init)
    T = k_ref.shape[0]
    ck = DIFF_KEY_CHUNK
    n_chunks = T // ck
    lane = lax.broadcasted_iota(jnp.int32, (tq, LANES), 1)
    lo = lane < HEAD_DIM
    nt = (((1,), (1,)), ((), ()))
    vt_ref[...] = v_ref[...].T

    def q_block(i, carry):
        rows = pl.ds(pl.multiple_of(i * tq, tq), tq)
        q = q_ref[rows, :]
        zero = jnp.zeros_like(q)
        qs = (jnp.where(lo, q, zero), jnp.where(lo, zero, q))

        def scores(c):
            kc = k_ref[c * ck:(c + 1) * ck, :]
            return [lax.dot_general(kc, qm, nt, preferred_element_type=F32) for qm in qs]

        m = [jnp.full((1, tq), NEG_INF, F32)] * 2
        l = [jnp.zeros((1, tq), F32)] * 2
        acc = [jnp.zeros((LANES, tq), F32)] * 2
        s_next = scores(0)
        for c in range(n_chunks):
            s_cur = s_next
            if c + 1 < n_chunks:
                s_next = scores(c + 1)
            vt = vt_ref[:, c * ck:(c + 1) * ck]
            for j in range(2):
                m_new = jnp.maximum(m[j], jnp.max(s_cur[j], axis=0, keepdims=True))
                alpha = jnp.exp2(m[j] - m_new)
                p = jnp.exp2(s_cur[j] - m_new)
                l[j] = alpha * l[j] + jnp.sum(p, axis=0, keepdims=True)
                acc[j] = alpha * acc[j] + jnp.dot(vt, p.astype(BF16), preferred_element_type=F32)
                m[j] = m_new
        ot = acc[0] / l[0] - lam * (acc[1] / l[1])
        o = ot.T
        ms = jnp.mean(o * o, axis=-1, keepdims=True)
        o = o * lax.rsqrt(ms + SUBLN_EPS) * g_ref[...] * (1.0 - lam_init)
        o_ref[rows, :] = o.astype(o_ref.dtype)
        return carry

    lax.fori_loop(0, q_ref.shape[0] // tq, q_block, 0, unroll=2)


def _diff_attention(proj, lq1, lk1, lq2, lk2, subln_g, layer, lam_init, B, T, tq=512):
    cq, ck, cv = OFF_QB // LANES, OFF_KB // LANES, OFF_VB // LANES
    vec = pl.BlockSpec((None, 1, HEAD_DIM), lambda b, h: (layer, 0, 0))
    return pl.pallas_call(
        functools.partial(_diff_kernel, lam_init=lam_init, tq=tq),
        grid=(B, DIFF_HEADS),
        in_specs=[
            vec, vec, vec, vec,
            pl.BlockSpec((T, LANES), lambda b, h: (b, cq + h)),
            pl.BlockSpec((T, LANES), lambda b, h: (b, ck + h)),
            pl.BlockSpec((T, LANES), lambda b, h: (b, cv + h)),
            pl.BlockSpec((None, 1, LANES), lambda b, h: (layer, 0, 0)),
        ],
        out_specs=pl.BlockSpec((T, LANES), lambda b, h: (b, h)),
        out_shape=jax.ShapeDtypeStruct((B * T, B_W), BF16),
        scratch_shapes=[pltpu.VMEM((LANES, T), BF16)],
        compiler_params=_cparams("parallel", "arbitrary"),
        name="diff_attn",
    )(lq1, lk1, lq2, lk2, proj, proj, proj, subln_g)


DIL_Q = 128


DIL_BLOCKS_PER_ITER = 4


def _dil_kernel(q_ref, k_ref, v_ref, o_ref, lse_ref, *, L, dil, radius):
    kw_len = DIL_Q + 2 * radius
    lane = lax.broadcasted_iota(jnp.int32, (DIL_Q, LANES), 1)
    lo = lane < HEAD_DIM
    rel = (lax.broadcasted_iota(jnp.int32, (DIL_Q, kw_len), 1)
           - lax.broadcasted_iota(jnp.int32, (DIL_Q, kw_len), 0))
    scale = HEAD_DIM ** -0.5
    nt = (((1,), (1,)), ((), ()))

    blocks_per_phase = L // DIL_Q

    def blocks(gi, carry):
        scores, starts, maxes = [], [], []
        for u in range(DIL_BLOCKS_PER_ITER):
            item = gi * DIL_BLOCKS_PER_ITER + u
            p = item // blocks_per_phase
            l0 = pl.multiple_of((item % blocks_per_phase) * DIL_Q, DIL_Q)
            ks = pl.multiple_of(jnp.clip(l0 - radius, 0, L - kw_len), radius)
            q = q_ref[p, pl.ds(l0, DIL_Q), :] * scale
            kw = k_ref[p, pl.ds(ks, kw_len), :]
            ok = jnp.abs(rel + (ks - l0)) <= radius
            starts.append((p, l0, ks))
            for hh in range(2):
                qh = jnp.where(lo if hh == 0 else jnp.logical_not(lo), q, jnp.zeros_like(q))
                s = lax.dot_general(qh, kw, nt, preferred_element_type=F32)
                scores.append(jnp.where(ok, s, NEG_INF))
        probs, sums = [], []
        for s in scores:
            m = jnp.max(s, axis=-1, keepdims=True)
            e = jnp.exp(s - m)
            maxes.append(m)
            sums.append(jnp.sum(e, axis=-1, keepdims=True))
            probs.append(e.astype(BF16))
        for u in range(DIL_BLOCKS_PER_ITER):
            p, l0, ks = starts[u]
            vw = v_ref[p, pl.ds(ks, kw_len), :]
            outs = [jnp.dot(probs[2 * u + hh], vw, preferred_element_type=F32) / sums[2 * u + hh] for hh in range(2)]
            lses = [maxes[2 * u + hh] + jnp.log(sums[2 * u + hh]) for hh in range(2)]
            o = jnp.where(lo, outs[0], outs[1])
            lse = jnp.where(lo, lses[0], lses[1])
            if dil == 1:
                o_ref[pl.ds(l0, DIL_Q), :] = o
                lse_ref[pl.ds(l0, DIL_Q), :] = lse
            else:
                o_ref[pl.ds(l0 * dil + p, DIL_Q, stride=dil), :] = o
                lse_ref[pl.ds(l0 * dil + p, DIL_Q, stride=dil), :] = lse
        return carry

    lax.fori_loop(0, dil * blocks_per_phase // DIL_BLOCKS_PER_ITER, blocks, 0)


def _dil_merge_kernel(*refs, geoms):
    n = len(geoms)
    out_ref = refs[3 * n]
    scratch = refs[3 * n + 1:]
    for g, (L, dil, radius) in enumerate(geoms):
        _dil_kernel(refs[3 * g], refs[3 * g + 1], refs[3 * g + 2], scratch[2 * g], scratch[2 * g + 1],
                    L=L, dil=dil, radius=radius)
    lses = [scratch[2 * g + 1][...] for g in range(n)]
    m = functools.reduce(jnp.maximum, lses)
    es = [jnp.exp(l - m) for l in lses]
    num = sum(scratch[2 * g][...] * es[g] for g in range(n))
    out_ref[...] = (num / sum(es)).astype(out_ref.dtype)


def _dilated_branch(cgs):
    B = cgs[0].shape[0]
    hp = GROUP_W // LANES
    geoms, in_specs, operands = [], [], []
    for cg, (window, dil) in zip(cgs, DIL_PATTERNS):
        _, d, L, _ = cg.shape
        radius = window // (2 * dil)
        assert d == dil and L >= DIL_Q + 2 * radius and L % DIL_Q == 0
        assert (dil * L // DIL_Q) % DIL_BLOCKS_PER_ITER == 0
        geoms.append((L, dil, radius))
        for c in range(3):
            in_specs.append(pl.BlockSpec((None, dil, L, LANES), lambda b, h, c=c: (b, 0, 0, c * hp + h)))
            operands.append(cg)
    T = geoms[0][0] * geoms[0][1]
    return pl.pallas_call(
        functools.partial(_dil_merge_kernel, geoms=tuple(geoms)),
        grid=(B, hp),
        in_specs=in_specs,
        out_specs=pl.BlockSpec((T, LANES), lambda b, h: (b, h)),
        out_shape=jax.ShapeDtypeStruct((B * T, GROUP_W), BF16),
        scratch_shapes=[pltpu.VMEM((T, LANES), F32)] * (2 * len(geoms)),
        compiler_params=_cparams("parallel", "arbitrary"),
        name="dil_attn",
    )(*operands)


def _merge_kernel(h_ref, ga_ref, gb_ref, gc_ref, ya_ref, yb_ref, yc_ref,
                  wpa32_ref, wpb32_ref, wpc32_ref, wo32_ref, out_ref, wpa_ref, wpb_ref, wpc_ref, wo_ref):
    @pl.when(pl.program_id(0) == 0)
    def _():
        for src, dst in ((wpa32_ref, wpa_ref), (wpb32_ref, wpb_ref), (wpc32_ref, wpc_ref), (wo32_ref, wo_ref)):
            dst[...] = src[...].astype(BF16)

    merged = jax.nn.sigmoid(ga_ref[...].astype(F32)) * jnp.dot(ya_ref[...], wpa_ref[...], preferred_element_type=F32)
    merged += jax.nn.sigmoid(gb_ref[...].astype(F32)) * jnp.dot(yb_ref[...], wpb_ref[...], preferred_element_type=F32)
    merged += jax.nn.sigmoid(gc_ref[...].astype(F32)) * jnp.dot(yc_ref[...], wpc_ref[...], preferred_element_type=F32)
    out_ref[...] = h_ref[...] + jnp.dot(merged.astype(BF16), wo_ref[...], preferred_element_type=F32)


def _merge(h2d, proj, ya, yb, yc, wpa, wpb, wpc, wo, layer, tm=512):
    n = h2d.shape[0]
    gw = GROUP_W
    row = lambda w: pl.BlockSpec((tm, w), lambda i: (i, 0))
    full = lambda a: pl.BlockSpec((None,) + a.shape[1:], lambda i: (layer, 0, 0))
    gate = lambda c: pl.BlockSpec((tm, D_MODEL), lambda i: (i, OFF_GATE // D_MODEL + c))
    return pl.pallas_call(
        _merge_kernel,
        grid=(n // tm,),
        in_specs=[row(D_MODEL), gate(0), gate(1), gate(2), row(A_W), row(B_W),
                  row(gw),
                  full(wpa), full(wpb), full(wpc), full(wo)],
        out_specs=row(D_MODEL),
        out_shape=jax.ShapeDtypeStruct((n, D_MODEL), F32),
        scratch_shapes=[pltpu.VMEM(a.shape[1:], BF16) for a in (wpa, wpb, wpc, wo)],
        compiler_params=_cparams("arbitrary"),
        name="merge_outproj",
    )(h2d, proj, proj, proj, ya, yb, yc, wpa, wpb, wpc, wo)


MOE_TILE = 512
MOE_TOKENS_PER_STEP = 512
ROUTE_ROWS = 8


ROUTER_ROWS = 32


def _router_kernel(h_ref, g_ref, wr_ref, br_ref, upper_ref, route_ref, wts_ref, cnt_ref, base_ref):
    i = pl.program_id(0)
    tm = h_ref.shape[0]
    lane_reps = tm // LANES
    row = lax.broadcasted_iota(jnp.int32, (ROUTER_ROWS, tm), 0)
    rowf = row.astype(F32)
    big = float(ROUTER_ROWS)
    nt = (((1,), (1,)), ((), ()))

    @pl.when(i == 0)
    def _():
        base_ref[...] = jnp.zeros_like(base_ref)

    x = h_ref[...]
    ms = jnp.mean(x * x, axis=-1, keepdims=True)
    xn = x * lax.rsqrt(ms + NORM_EPS) * g_ref[...]
    logits = lax.dot_general(wr_ref[...], xn, nt, preferred_element_type=F32, precision=lax.Precision.HIGHEST)
    logits = logits + jnp.concatenate([br_ref[...]] * lane_reps, axis=1)
    gl = jnp.where(row < N_GROUPS, logits, -jnp.inf)
    gmax = jnp.max(gl, axis=0, keepdims=True)
    g_sel = jnp.min(jnp.where(gl == gmax, rowf, big), axis=0, keepdims=True).astype(jnp.int32)
    g_gate = 1.0 / jnp.sum(jnp.exp(gl - gmax), axis=0, keepdims=True)
    eidx = row - N_GROUPS
    in_grp = (eidx >= g_sel * EXPERTS_PER_GROUP) & (eidx < (g_sel + 1) * EXPERTS_PER_GROUP)
    el = jnp.where(in_grp, logits, -jnp.inf)
    t1 = jnp.max(el, axis=0, keepdims=True)
    i1 = jnp.min(jnp.where(el == t1, rowf, big), axis=0, keepdims=True).astype(jnp.int32)
    el2 = jnp.where(row == i1, -jnp.inf, el)
    t2 = jnp.max(el2, axis=0, keepdims=True)
    i2 = jnp.min(jnp.where(el2 == t2, rowf, big), axis=0, keepdims=True).astype(jnp.int32)
    x2 = jnp.exp(t2 - t1)
    den = 1.0 + x2
    wts_ref[:, :LANES] = jnp.broadcast_to(g_gate / den, (LANES, tm)).T
    wts_ref[:, LANES:] = jnp.broadcast_to(g_gate * x2 / den, (LANES, tm)).T
    member = jnp.where(row == i1, 1.0, 0.0) + jnp.where(row == i2, 1.0, 0.0)
    earlier = jnp.dot(member.astype(BF16), upper_ref[...], preferred_element_type=F32)
    earlier = earlier + jnp.concatenate([base_ref[...]] * lane_reps, axis=1)
    rank1 = jnp.sum(jnp.where(row == i1, earlier, 0.0), axis=0, keepdims=True)
    rank2 = jnp.sum(jnp.where(row == i2, earlier, 0.0), axis=0, keepdims=True)
    base_ref[...] = base_ref[...] + jnp.sum(member, axis=1, keepdims=True)
    cnt_ref[...] = base_ref[...]
    out_row = lax.broadcasted_iota(jnp.int32, (ROUTE_ROWS, tm), 0)
    route_ref[...] = jnp.where(out_row == 0, (i1 - N_GROUPS).astype(F32),
                               jnp.where(out_row == 1, (i2 - N_GROUPS).astype(F32),
                                         jnp.where(out_row == 2, rank1, jnp.where(out_row == 3, rank2, 0.0))))


def _router(h2d, g, w_router, b_router, layer, tm=1024):
    n = h2d.shape[0]
    return pl.pallas_call(
        _router_kernel,
        grid=(n // tm,),
        in_specs=[
            pl.BlockSpec((tm, D_MODEL), lambda i: (i, 0)),
            pl.BlockSpec((None, 1, D_MODEL), lambda i: (layer, 0, 0)),
            pl.BlockSpec((None, ROUTER_ROWS, D_MODEL), lambda i: (layer, 0, 0)),
            pl.BlockSpec((None, ROUTER_ROWS, LANES), lambda i: (layer, 0, 0)),
            pl.BlockSpec((tm, tm), lambda i: (0, 0)),
        ],
        out_specs=[pl.BlockSpec((ROUTE_ROWS, tm), lambda i: (0, i)),
                   pl.BlockSpec((tm, 2 * LANES), lambda i: (i, 0)),
                   pl.BlockSpec((ROUTER_ROWS, LANES), lambda i: (0, 0))],
        out_shape=[jax.ShapeDtypeStruct((ROUTE_ROWS, n), F32), jax.ShapeDtypeStruct((n, 2 * LANES), F32),
                   jax.ShapeDtypeStruct((ROUTER_ROWS, LANES), F32)],
        scratch_shapes=[pltpu.VMEM((ROUTER_ROWS, LANES), F32)],
        compiler_params=_cparams("arbitrary"),
        name="moe_router",
    )(h2d, g, w_router, b_router, jnp.triu(jnp.ones((tm, tm), BF16), k=1))


def _dispatch_tables(route, counts, n):
    cnt = counts[N_GROUPS:N_GROUPS + N_EXPERTS, 0].astype(jnp.int32)
    padded = (cnt + MOE_TILE - 1) // MOE_TILE * MOE_TILE
    ends = jnp.cumsum(padded)
    starts = ends - padded
    eids = jnp.arange(N_EXPERTS, dtype=jnp.int32)
    pos = []
    for k in range(2):
        e = route[k].astype(jnp.int32)
        start_e = jnp.sum(jnp.where(e[:, None] == eids[None, :], starts[None, :], 0), axis=1)
        pos.append(start_e + route[2 + k].astype(jnp.int32))
    tb = MOE_TOKENS_PER_STEP
    idx = jnp.concatenate([pos[0].reshape(n // tb, 1, tb), pos[1].reshape(n // tb, 1, tb)], axis=-1)
    n_tiles = 2 * n // MOE_TILE + N_EXPERTS
    tile_ids = jnp.arange(n_tiles, dtype=jnp.int32)
    n_used = ends[-1] // MOE_TILE
    tile_e = jnp.sum((tile_ids[:, None] * MOE_TILE >= ends[None, :]).astype(jnp.int32), axis=1)
    last_e = jnp.max(jnp.where(tile_ids < n_used, tile_e, 0))
    tile_e = jnp.where(tile_ids < n_used, tile_e, last_e).astype(jnp.int32)
    zero_rows = jnp.where(padded > 0, ends - MOE_TILE, n_tiles * MOE_TILE).astype(jnp.int32)
    return idx, tile_e, n_used.reshape(1).astype(jnp.int32), zero_rows, n_tiles


def _dispatch_kernel(zero_rows_ref, n_used_ref, idx_ref, h_ref, xs_hbm, zeros_ref, sem):
    i = pl.program_id(0)
    tb = MOE_TOKENS_PER_STEP
    n_tiles = xs_hbm.shape[0] // MOE_TILE - 1

    def zero_tile(row0):
        return pltpu.make_async_copy(zeros_ref, xs_hbm.at[pl.ds(pl.multiple_of(row0, MOE_TILE), MOE_TILE)], sem)

    @pl.when(i == 0)
    def _():
        zeros_ref[...] = jnp.zeros_like(zeros_ref)
        fills = [zero_tile(zero_rows_ref[e]) for e in range(N_EXPERTS)]
        for c in fills:
            c.start()
        for c in fills:
            c.wait()

        def tail(j, carry):
            tile = n_used_ref[0] + j

            @pl.when(tile <= n_tiles)
            def _():
                c = zero_tile(tile * MOE_TILE)
                c.start()
                c.wait()
            return carry

        lax.fori_loop(0, N_EXPERTS + 1, tail, 0)

    def row(t, carry):
        src = h_ref.at[pl.ds(t, 1)]
        pltpu.make_async_copy(src, xs_hbm.at[pl.ds(idx_ref[0, 0, t], 1)], sem).start()
        pltpu.make_async_copy(src, xs_hbm.at[pl.ds(idx_ref[0, 0, tb + t], 1)], sem).start()
        return carry

    lax.fori_loop(0, tb, row, 0, unroll=True)
    for _ in range(2):
        pltpu.make_async_copy(h_ref, xs_hbm.at[pl.ds(0, tb)], sem).wait()


def _dispatch(h2d, idx, zero_rows, n_used, n_tiles):
    n = h2d.shape[0]
    tb = MOE_TOKENS_PER_STEP
    assert 2 * n // MOE_TILE + N_EXPERTS == n_tiles
    return pl.pallas_call(
        _dispatch_kernel,
        grid_spec=pltpu.PrefetchScalarGridSpec(
            num_scalar_prefetch=2,
            grid=(n // tb,),
            in_specs=[pl.BlockSpec((1, 1, 2 * tb), lambda i, z, u: (i, 0, 0), memory_space=pltpu.SMEM),
                      pl.BlockSpec((tb, D_MODEL), lambda i, z, u: (i, 0))],
            out_specs=pl.BlockSpec(memory_space=pl.ANY),
            scratch_shapes=[pltpu.VMEM((MOE_TILE, D_MODEL), F32), pltpu.SemaphoreType.DMA(())],
        ),
        out_shape=jax.ShapeDtypeStruct(((n_tiles + 1) * MOE_TILE, D_MODEL), F32),
        compiler_params=pltpu.CompilerParams(dimension_semantics=("arbitrary",), vmem_limit_bytes=VMEM_LIMIT,
                                             disable_bounds_checks=True),
        name="moe_dispatch",
    )(zero_rows, n_used, idx, h2d)


def _expert_kernel(tile_e_ref, n_used_ref, x_ref, g_ref, w1_ref, w3_ref, w2_ref, y_ref, w1b_ref, w3b_ref, w2b_ref):
    i = pl.program_id(0)
    used = i < n_used_ref[0]

    @pl.when(jnp.logical_not(used))
    def _():
        y_ref[...] = jnp.zeros_like(y_ref)

    @pl.when(used & ((i == 0) | (tile_e_ref[i] != tile_e_ref[jnp.maximum(i - 1, 0)])))
    def _():
        w1b_ref[...] = w1_ref[...].astype(BF16)
        w3b_ref[...] = w3_ref[...].astype(BF16)
        w2b_ref[...] = w2_ref[...].astype(BF16)

    @pl.when(used)
    def _():
        x = x_ref[...]
        ms = jnp.mean(x * x, axis=-1, keepdims=True)
        xn = (x * lax.rsqrt(ms + NORM_EPS) * g_ref[...]).astype(BF16)
        a = jnp.dot(xn, w1b_ref[...], preferred_element_type=F32)
        b = jnp.dot(xn, w3b_ref[...], preferred_element_type=F32)
        hmid = (a * jax.nn.sigmoid(a) * b).astype(BF16)
        y_ref[...] = jnp.dot(hmid, w2b_ref[...], preferred_element_type=F32)


def _experts(xs, g, w1, w3, w2, tile_e, n_used, layer, n_tiles):
    row_map = lambda i, te, nu: (i, 0)
    w_map = lambda i, te, nu: (layer, te[i], 0, 0)
    return pl.pallas_call(
        _expert_kernel,
        grid_spec=pltpu.PrefetchScalarGridSpec(
            num_scalar_prefetch=2,
            grid=(n_tiles,),
            in_specs=[pl.BlockSpec((MOE_TILE, D_MODEL), row_map),
                      pl.BlockSpec((None, 1, D_MODEL), lambda i, te, nu: (layer, 0, 0)),
                      pl.BlockSpec((None, None, D_MODEL, D_FF_EXPERT), w_map),
                      pl.BlockSpec((None, None, D_MODEL, D_FF_EXPERT), w_map),
                      pl.BlockSpec((None, None, D_FF_EXPERT, D_MODEL), w_map)],
            out_specs=pl.BlockSpec((MOE_TILE, D_MODEL), row_map),
            scratch_shapes=[pltpu.VMEM((D_MODEL, D_FF_EXPERT), BF16), pltpu.VMEM((D_MODEL, D_FF_EXPERT), BF16),
                            pltpu.VMEM((D_FF_EXPERT, D_MODEL), BF16)],
        ),
        out_shape=jax.ShapeDtypeStruct((n_tiles * MOE_TILE, D_MODEL), F32),
        compiler_params=_cparams("arbitrary"),
        name="moe_experts",
    )(tile_e, n_used, xs, g, w1, w3, w2)


def _combine_kernel(idx_ref, ys_hbm, h_ref, wts_ref, gf_ref, o_ref, buf_ref, sem, *, final_norm):
    tb = h_ref.shape[0]

    def row(t, carry):
        pltpu.make_async_copy(ys_hbm.at[pl.ds(idx_ref[0, 0, t], 1)], buf_ref.at[0, pl.ds(t, 1)], sem).start()
        pltpu.make_async_copy(ys_hbm.at[pl.ds(idx_ref[0, 0, tb + t], 1)], buf_ref.at[1, pl.ds(t, 1)], sem).start()
        return carry

    lax.fori_loop(0, tb, row, 0, unroll=True)
    for k in range(2):
        pltpu.make_async_copy(ys_hbm.at[pl.ds(0, tb)], buf_ref.at[k], sem).wait()
    reps = D_MODEL // LANES
    w = wts_ref[...]
    w_top1 = jnp.concatenate([w[:, :LANES]] * reps, axis=1)
    w_top2 = jnp.concatenate([w[:, LANES:]] * reps, axis=1)
    out = h_ref[...] + w_top1 * buf_ref[0] + w_top2 * buf_ref[1]
    if final_norm:
        ms = jnp.mean(out * out, axis=-1, keepdims=True)
        out = out * lax.rsqrt(ms + NORM_EPS) * gf_ref[...]
    o_ref[...] = out


def _combine(h2d, ys, idx, wts, g_final, final_norm, batch):
    n = h2d.shape[0]
    tb = MOE_TOKENS_PER_STEP
    steps_per_seq = n // batch // tb
    return pl.pallas_call(
        functools.partial(_combine_kernel, final_norm=final_norm),
        grid=(n // tb,),
        in_specs=[pl.BlockSpec((1, 1, 2 * tb), lambda i: (i, 0, 0), memory_space=pltpu.SMEM),
                  pl.BlockSpec(memory_space=pl.ANY),
                  pl.BlockSpec((tb, D_MODEL), lambda i: (i, 0)),
                  pl.BlockSpec((tb, 2 * LANES), lambda i: (i, 0)),
                  pl.BlockSpec((1, D_MODEL), lambda i: (0, 0))],
        out_specs=pl.BlockSpec((None, tb, D_MODEL), lambda i: (i // steps_per_seq, i % steps_per_seq, 0)),
        out_shape=jax.ShapeDtypeStruct((batch, n // batch, D_MODEL), F32),
        scratch_shapes=[pltpu.VMEM((2, tb, D_MODEL), F32), pltpu.SemaphoreType.DMA(())],
        compiler_params=pltpu.CompilerParams(dimension_semantics=("arbitrary",), vmem_limit_bytes=VMEM_LIMIT,
                                             disable_bounds_checks=True),
        name="moe_combine",
    )(idx, ys, h2d, wts, g_final)


def _moe(h2d, g, w_router, b_router, w1, w3, w2, layer, g_final, final_norm, batch):
    n = h2d.shape[0]
    route, wts, counts = _router(h2d, g, w_router, b_router, layer)
    idx, tile_e, n_used, zero_rows, n_tiles = _dispatch_tables(route, counts, n)
    xs = _dispatch(h2d, idx, zero_rows, n_used, n_tiles)
    ys = _experts(xs, g, w1, w3, w2, tile_e, n_used, layer, n_tiles)
    return _combine(h2d, ys, idx, wts, g_final, final_norm, batch)


def _router_params(wg, bg, we, be):
    n_l = wg.shape[0]
    pad = ROUTER_ROWS - N_GROUPS - N_EXPERTS
    w = jnp.concatenate([wg, we, jnp.zeros((n_l, D_MODEL, pad), F32)], axis=-1).astype(F32).transpose(0, 2, 1)
    b = jnp.concatenate([bg, be, jnp.zeros((n_l, pad), F32)], axis=-1).astype(F32)
    return w, jnp.broadcast_to(b[:, :, None], (n_l, ROUTER_ROWS, LANES))


def kernel(x, w_in, na_rpb, lam_q1, lam_k1, lam_q2, lam_k2, diff_subln, w_pa, w_pb, w_pc, w_o, norm_mix, norm_ffn,
           router_group_w, router_group_b, router_expert_w, router_expert_b, w1, w3, w2, norm_final):
    B, T, D = x.shape
    depth = w_in.shape[0]
    rows = T // GRID_W
    wr = min(NA_WIN_R, rows)
    rope_tab = _rope_table(T)
    row3 = lambda a: a[:, None, :]
    w_in_bf = _permute_cols(w_in)
    bias_tab = _na_bias_table(na_rpb, wr)
    g_mix, g_ffn = row3(norm_mix), row3(norm_ffn)
    lams = [row3(a) for a in (lam_q1, lam_k1, lam_q2, lam_k2)]
    subln = row3(diff_subln)
    w_router, b_router = _router_params(router_group_w, router_group_b, router_expert_w, router_expert_b)
    h = x.reshape(B * T, D)
    for l in range(depth):
        lam_init = 0.8 - 0.6 * math.exp(-0.3 * l)
        proj, *cgs = _inproj(h, g_mix, w_in_bf, rope_tab, l, B, T)
        ya = _na_attention(proj, bias_tab, l, B, T)
        yb = _diff_attention(proj, *lams, subln, l, lam_init, B, T)
        yc = _dilated_branch(cgs)
        h = _merge(h, proj, ya, yb, yc, w_pa, w_pb, w_pc, w_o, l)
        out = _moe(h, g_ffn, w_router, b_router, w1, w3, w2, l, norm_final[None, :], l == depth - 1, B)
        h = out.reshape(B * T, D)
    return out
```

```python
import functools
import math

import jax
import jax.numpy as jnp
import numpy as np
from jax import lax
from jax.experimental import pallas as pl
from jax.experimental.pallas import tpu as pltpu

F32 = jnp.float32
BF16 = jnp.bfloat16

D_MODEL = 1024
HEAD_DIM = 64
ROPE_DIM = 16
ROPE_THETA = 500000.0
GRID_W = 64
NA_HEADS = 8
NA_WIN_R = 8
NA_WIN_C = 16
DIFF_HEADS = 4
DIL_PATTERNS = ((128, 1), (512, 4), (2048, 16))
N_GROUPS = 4
EXPERTS_PER_GROUP = 4
N_EXPERTS = 16
D_FF_EXPERT = D_MODEL // 2
NORM_EPS = 1e-6
SUBLN_EPS = 1e-5
NEG_INF = -1e30

LANES = 128
MXU_N = 256
VMEM_LIMIT = 56 * 1024 * 1024

A_W, B_W, C_W = 512, 512, 768
N_STEPS = len(DIL_PATTERNS)
GROUP_W = C_W // N_STEPS
_MAIN_ORDER = ("qb", "kb", "va", "qa", "ga", "gb", "gc", "ka", "vb")
_REF_ORDER = ("qa", "ka", "va", "qb", "kb", "vb", "qc", "kc", "vc", "ga", "gb", "gc")
_WIDTH = dict(qa=A_W, ka=A_W, va=A_W, qb=B_W, kb=B_W, vb=B_W, qc=C_W, kc=C_W, vc=C_W,
              ga=D_MODEL, gb=D_MODEL, gc=D_MODEL)


def _offsets(order):
    off, out = 0, {}
    for name in order:
        out[name] = off
        off += _WIDTH[name]
    return out, off


_OFF, MAIN_W = _offsets(_MAIN_ORDER)
_REF_OFF, IN_W = _offsets(_REF_ORDER)
OFF_QA, OFF_KA, OFF_VA = _OFF["qa"], _OFF["ka"], _OFF["va"]
OFF_QB, OFF_KB, OFF_VB = _OFF["qb"], _OFF["kb"], _OFF["vb"]
OFF_GATE = _OFF["ga"]
MAIN_STEP_W = MAIN_W // N_STEPS
STEP_W = MAIN_STEP_W + 3 * GROUP_W
MAIN_ROPE_W = 2 * B_W
QB_SCALE = HEAD_DIM ** -0.5 * math.log2(math.e)
assert OFF_GATE % D_MODEL == 0 and OFF_QB == 0 and OFF_KB == B_W and MAIN_ROPE_W <= MAIN_STEP_W
assert MAIN_W % N_STEPS == 0 and MAIN_STEP_W % MXU_N == 0 and N_STEPS * STEP_W == IN_W


def _permute_cols(w):
    def piece(name, lo, hi):
        p = w[..., _REF_OFF[name] + lo:_REF_OFF[name] + hi]
        return (p * QB_SCALE if name == "qb" else p).astype(BF16)

    parts = []
    for s in range(N_STEPS):
        for n in _MAIN_ORDER:
            lo = max(_OFF[n], s * MAIN_STEP_W) - _OFF[n]
            hi = min(_OFF[n] + _WIDTH[n], (s + 1) * MAIN_STEP_W) - _OFF[n]
            if lo < hi:
                parts.append(piece(n, lo, hi))
        for n in ("qc", "kc", "vc"):
            parts.append(piece(n, s * GROUP_W, (s + 1) * GROUP_W))
    return jnp.concatenate(parts, axis=-1)


def _cparams(*sem):
    return pltpu.CompilerParams(dimension_semantics=sem, vmem_limit_bytes=VMEM_LIMIT)


def _inproj_kernel(x_ref, g_ref, w_ref, rope_ref, main_ref, c0_ref, c1_ref, c2_ref, xn_ref, y_ref, *, tm):
    j = pl.program_id(1)
    c_refs = (c0_ref, c1_ref, c2_ref)
    reps = MXU_N // LANES
    half = ROPE_DIM // 2
    assert GROUP_W == MXU_N

    def rope(y):
        cos = jnp.concatenate([rope_ref[0]] * reps, axis=1)
        s_up = jnp.concatenate([rope_ref[1]] * reps, axis=1)
        s_dn = jnp.concatenate([rope_ref[2]] * reps, axis=1)
        return y * cos + pltpu.roll(y, MXU_N - half, 1) * s_up + pltpu.roll(y, half, 1) * s_dn

    def step(s):
        xn = xn_ref[...]
        for c in range(MAIN_STEP_W // MXU_N):
            sl = slice(c * MXU_N, (c + 1) * MXU_N)
            y = jnp.dot(xn, w_ref[:, sl], preferred_element_type=F32)
            if s == 0 and c < MAIN_ROPE_W // MXU_N:
                y = rope(y)
            main_ref[:, sl] = y.astype(main_ref.dtype)
        dil = DIL_PATTERNS[s][1]
        for c in range(3):
            wsl = slice(MAIN_STEP_W + c * GROUP_W, MAIN_STEP_W + (c + 1) * GROUP_W)
            osl = slice(c * GROUP_W, (c + 1) * GROUP_W)
            y = jnp.dot(xn, w_ref[:, wsl], preferred_element_type=F32)
            if c < 2:
                y = rope(y)
            if dil == 1:
                c_refs[s][0, :, osl] = y.astype(BF16)
            else:
                for hb in range(reps):
                    y_ref[c, hb] = y[:, hb * LANES:(hb + 1) * LANES]
                for p in range(dil):
                    for hb in range(reps):
                        c_refs[s][p, :, c * GROUP_W + hb * LANES:c * GROUP_W + (hb + 1) * LANES] = (
                            y_ref[c, hb, pl.ds(p, tm // dil, stride=dil), :].astype(BF16))

    @pl.when(j == 0)
    def _():
        x = x_ref[...]
        ms = jnp.mean(x * x, axis=-1, keepdims=True)
        xn_ref[...] = (x * lax.rsqrt(ms + NORM_EPS) * g_ref[...]).astype(BF16)
        step(0)

    for s in range(1, N_STEPS):
        pl.when(j == s)(functools.partial(step, s))


def _inproj(h2d, g, w_bf, rope_tab, layer, B, T, tm=1024):
    n = h2d.shape[0]
    tpb = T // tm
    c_specs, c_shapes = [], []
    for _, dil in DIL_PATTERNS:
        assert tm % dil == 0
        c_specs.append(pl.BlockSpec((None, dil, tm // dil, 3 * GROUP_W), lambda i, j: (i // tpb, 0, i % tpb, 0)))
        c_shapes.append(jax.ShapeDtypeStruct((B, dil, T // dil, 3 * GROUP_W), BF16))
    return pl.pallas_call(
        functools.partial(_inproj_kernel, tm=tm),
        grid=(n // tm, N_STEPS),
        in_specs=[
            pl.BlockSpec((tm, D_MODEL), lambda i, j: (i, 0)),
            pl.BlockSpec((None, 1, D_MODEL), lambda i, j: (layer, 0, 0)),
            pl.BlockSpec((None, D_MODEL, STEP_W), lambda i, j: (layer, 0, j)),
            pl.BlockSpec((3, tm, LANES), lambda i, j: (0, i % tpb, 0)),
        ],
        out_specs=[pl.BlockSpec((tm, MAIN_STEP_W), lambda i, j: (i, j))] + c_specs,
        out_shape=[jax.ShapeDtypeStruct((n, MAIN_W), BF16)] + c_shapes,
        scratch_shapes=[pltpu.VMEM((tm, D_MODEL), BF16), pltpu.VMEM((3, MXU_N // LANES, tm, LANES), F32)],
        compiler_params=_cparams("parallel", "arbitrary"),
        name="inproj",
    )(h2d, g, w_bf, rope_tab)


def _rope_table(T):
    half = ROPE_DIM // 2
    inv = 1.0 / (ROPE_THETA ** (jnp.arange(0, ROPE_DIM, 2, dtype=F32) / ROPE_DIM))
    ang = jnp.arange(T, dtype=F32)[:, None] * inv[None, :]
    cos, sin = jnp.cos(ang), jnp.sin(ang)
    zeros = jnp.zeros((T, HEAD_DIM - ROPE_DIM), F32)
    z8 = jnp.zeros((T, half), F32)
    c64 = jnp.concatenate([cos, cos, zeros + 1.0], axis=1)
    up64 = jnp.concatenate([-sin, z8, zeros], axis=1)
    dn64 = jnp.concatenate([z8, sin, zeros], axis=1)
    reps = LANES // HEAD_DIM
    return jnp.stack([jnp.tile(c64, (1, reps)), jnp.tile(up64, (1, reps)), jnp.tile(dn64, (1, reps))])


NA_ROWS_PER_ITER = 32


def _na_kernel(q_ref, k_ref, v_ref, b_ref, o_ref, *, rows, wr):
    lane = lax.broadcasted_iota(jnp.int32, (GRID_W, LANES), 1)
    lo = lane < HEAD_DIM
    scale = HEAD_DIM ** -0.5

    def row_group(gi, carry):
        scores, windows = [], []
        for u in range(NA_ROWS_PER_ITER):
            r = gi * NA_ROWS_PER_ITER + u
            r0 = jnp.clip(r - wr // 2, 0, rows - wr)
            d0 = r0 - r + NA_WIN_R - 1 - (NA_WIN_R - wr)
            q = q_ref[pl.ds(pl.multiple_of(r * GRID_W, GRID_W), GRID_W), :] * scale
            ks = pl.multiple_of(r0 * GRID_W, GRID_W)
            kw = k_ref[pl.ds(ks, wr * GRID_W), :]
            windows.append(ks)
            for hh in range(2):
                qh = jnp.where(lo if hh == 0 else jnp.logical_not(lo), q, jnp.zeros_like(q))
                s = lax.dot_general(qh, kw, (((1,), (1,)), ((), ())), preferred_element_type=F32)
                bias = jnp.concatenate([b_ref[hh, d0 + 2 * i] for i in range(wr // 2)], axis=1)
                scores.append(s + bias)
        probs, sums = [], []
        for s in scores:
            p = jnp.exp(s - jnp.max(s, axis=-1, keepdims=True))
            sums.append(jnp.sum(p, axis=-1, keepdims=True))
            probs.append(p.astype(BF16))
        for u in range(NA_ROWS_PER_ITER):
            r = gi * NA_ROWS_PER_ITER + u
            vw = v_ref[pl.ds(windows[u], wr * GRID_W), :]
            outs = [jnp.dot(probs[2 * u + hh], vw, preferred_element_type=F32) / sums[2 * u + hh] for hh in range(2)]
            o = jnp.where(lo, outs[0], outs[1])
            o_ref[pl.ds(pl.multiple_of(r * GRID_W, GRID_W), GRID_W), :] = o.astype(o_ref.dtype)
        return carry

    lax.fori_loop(0, rows // NA_ROWS_PER_ITER, row_group, 0)


def _na_bias_table(rpb, wr):
    qc = np.arange(GRID_W)[:, None]
    kc = np.arange(GRID_W)[None, :]
    c0 = np.clip(qc - NA_WIN_C // 2, 0, GRID_W - NA_WIN_C)
    ok = (kc >= c0) & (kc < c0 + NA_WIN_C)
    dc = np.clip(kc - qc + NA_WIN_C - 1, 0, 2 * NA_WIN_C - 2)
    onehot = (np.arange(2 * NA_WIN_C - 1)[:, None, None] == dc[None]).astype(np.float32)
    b = jnp.einsum("lhrd,dqk->lhrqk", rpb.astype(F32), onehot, precision=lax.Precision.HIGHEST)
    b = jnp.where(jnp.asarray(ok), b, NEG_INF)
    off = NA_WIN_R - wr
    n_pairs = 2 * wr - 2
    return jnp.concatenate([b[:, :, off:off + n_pairs], b[:, :, off + 1:off + 1 + n_pairs]], axis=-1)


def _na_attention(proj, bias_tab, layer, B, T):
    rows = T // GRID_W
    wr = min(NA_WIN_R, rows)
    cq, ck, cv = OFF_QA // LANES, OFF_KA // LANES, OFF_VA // LANES
    assert wr % 2 == 0 and 2 * GRID_W == LANES and rows % NA_ROWS_PER_ITER == 0
    return pl.pallas_call(
        functools.partial(_na_kernel, rows=rows, wr=wr),
        grid=(B, NA_HEADS // 2),
        in_specs=[
            pl.BlockSpec((T, LANES), lambda b, h: (b, cq + h)),
            pl.BlockSpec((T, LANES), lambda b, h: (b, ck + h)),
            pl.BlockSpec((T, LANES), lambda b, h: (b, cv + h)),
            pl.BlockSpec((None, 2, 2 * wr - 2, GRID_W, 2 * GRID_W), lambda b, h: (layer, h, 0, 0, 0)),
        ],
        out_specs=pl.BlockSpec((T, LANES), lambda b, h: (b, h)),
        out_shape=jax.ShapeDtypeStruct((B * T, A_W), BF16),
        compiler_params=_cparams("parallel", "arbitrary"),
        name="na_attn",
    )(proj, proj, proj, bias_tab)


DIFF_KEY_CHUNK = 512


def _diff_kernel(lq1_ref, lk1_ref, lq2_ref, lk2_ref, q_ref, k_ref, v_ref, g_ref, o_ref, vt_ref, *, lam_init, tq):
    lam = (jnp.exp(jnp.sum(lq1_ref[...] * lk1_ref[...], keepdims=True))
           - jnp.exp(jnp.sum(lq2_ref[...] * lk2_ref[...], keepdims=True)) + lam_init)
    T = k_ref.shape[0]
    ck = DIFF_KEY_CHUNK
    n_chunks = T // ck
    lane = lax.broadcasted_iota(jnp.int32, (tq, LANES), 1)
    lo = lane < HEAD_DIM
    nt = (((1,), (1,)), ((), ()))
    vt_ref[...] = v_ref[...].T

    def q_block(i, carry):
        rows = pl.ds(pl.multiple_of(i * tq, tq), tq)
        q = q_ref[rows, :]
        zero = jnp.zeros_like(q)
        qs = (jnp.where(lo, q, zero), jnp.where(lo, zero, q))

        def scores(c):
            kc = k_ref[c * ck:(c + 1) * ck, :]
            return [lax.dot_general(kc, qm, nt, preferred_element_type=F32) for qm in qs]

        m = [jnp.full((1, tq), NEG_INF, F32)] * 2
        l = [jnp.zeros((1, tq), F32)] * 2
        acc = [jnp.zeros((LANES, tq), F32)] * 2
        s_next = scores(0)
        for c in range(n_chunks):
            s_cur = s_next
            if c + 1 < n_chunks:
                s_next = scores(c + 1)
            vt = vt_ref[:, c * ck:(c + 1) * ck]
            for j in range(2):
                m_new = jnp.maximum(m[j], jnp.max(s_cur[j], axis=0, keepdims=True))
                alpha = jnp.exp2(m[j] - m_new)
                p = jnp.exp2(s_cur[j] - m_new)
                l[j] = alpha * l[j] + jnp.sum(p, axis=0, keepdims=True)
                acc[j] = alpha * acc[j] + jnp.dot(vt, p.astype(BF16), preferred_element_type=F32)
                m[j] = m_new
        ot = acc[0] / l[0] - lam * (acc[1] / l[1])
        o = ot.T
        ms = jnp.mean(o * o, axis=-1, keepdims=True)
        o = o * lax.rsqrt(ms + SUBLN_EPS) * g_ref[...] * (1.0 - lam_init)
        o_ref[rows, :] = o.astype(o_ref.dtype)
        return carry

    lax.fori_loop(0, q_ref.shape[0] // tq, q_block, 0, unroll=2)


def _diff_attention(proj, lq1, lk1, lq2, lk2, subln_g, layer, lam_init, B, T, tq=512):
    cq, ck, cv = OFF_QB // LANES, OFF_KB // LANES, OFF_VB // LANES
    vec = pl.BlockSpec((None, 1, HEAD_DIM), lambda b, h: (layer, 0, 0))
    return pl.pallas_call(
        functools.partial(_diff_kernel, lam_init=lam_init, tq=tq),
        grid=(B, DIFF_HEADS),
        in_specs=[
            vec, vec, vec, vec,
            pl.BlockSpec((T, LANES), lambda b, h: (b, cq + h)),
            pl.BlockSpec((T, LANES), lambda b, h: (b, ck + h)),
            pl.BlockSpec((T, LANES), lambda b, h: (b, cv + h)),
            pl.BlockSpec((None, 1, LANES), lambda b, h: (layer, 0, 0)),
        ],
        out_specs=pl.BlockSpec((T, LANES), lambda b, h: (b, h)),
        out_shape=jax.ShapeDtypeStruct((B * T, B_W), BF16),
        scratch_shapes=[pltpu.VMEM((LANES, T), BF16)],
        compiler_params=_cparams("parallel", "arbitrary"),
        name="diff_attn",
    )(lq1, lk1, lq2, lk2, proj, proj, proj, subln_g)


DIL_Q = 128


DIL_BLOCKS_PER_ITER = 4


def _dil_kernel(q_ref, k_ref, v_ref, o_ref, lse_ref, *, L, dil, radius):
    kw_len = DIL_Q + 2 * radius
    lane = lax.broadcasted_iota(jnp.int32, (DIL_Q, LANES), 1)
    lo = lane < HEAD_DIM
    rel = (lax.broadcasted_iota(jnp.int32, (DIL_Q, kw_len), 1)
           - lax.broadcasted_iota(jnp.int32, (DIL_Q, kw_len), 0))
    scale = HEAD_DIM ** -0.5
    nt = (((1,), (1,)), ((), ()))

    blocks_per_phase = L // DIL_Q

    def blocks(gi, carry):
        scores, starts, maxes = [], [], []
        for u in range(DIL_BLOCKS_PER_ITER):
            item = gi * DIL_BLOCKS_PER_ITER + u
            p = item // blocks_per_phase
            l0 = pl.multiple_of((item % blocks_per_phase) * DIL_Q, DIL_Q)
            ks = pl.multiple_of(jnp.clip(l0 - radius, 0, L - kw_len), radius)
            q = q_ref[p, pl.ds(l0, DIL_Q), :] * scale
            kw = k_ref[p, pl.ds(ks, kw_len), :]
            ok = jnp.abs(rel + (ks - l0)) <= radius
            starts.append((p, l0, ks))
            for hh in range(2):
                qh = jnp.where(lo if hh == 0 else jnp.logical_not(lo), q, jnp.zeros_like(q))
                s = lax.dot_general(qh, kw, nt, preferred_element_type=F32)
                scores.append(jnp.where(ok, s, NEG_INF))
        probs, sums = [], []
        for s in scores:
            m = jnp.max(s, axis=-1, keepdims=True)
            e = jnp.exp(s - m)
            maxes.append(m)
            sums.append(jnp.sum(e, axis=-1, keepdims=True))
            probs.append(e.astype(BF16))
        for u in range(DIL_BLOCKS_PER_ITER):
            p, l0, ks = starts[u]
            vw = v_ref[p, pl.ds(ks, kw_len), :]
            outs = [jnp.dot(probs[2 * u + hh], vw, preferred_element_type=F32) / sums[2 * u + hh] for hh in range(2)]
            lses = [maxes[2 * u + hh] + jnp.log(sums[2 * u + hh]) for hh in range(2)]
            o = jnp.where(lo, outs[0], outs[1])
            lse = jnp.where(lo, lses[0], lses[1])
            if dil == 1:
                o_ref[pl.ds(l0, DIL_Q), :] = o
                lse_ref[pl.ds(l0, DIL_Q), :] = lse
            else:
                o_ref[pl.ds(l0 * dil + p, DIL_Q, stride=dil), :] = o
                lse_ref[pl.ds(l0 * dil + p, DIL_Q, stride=dil), :] = lse
        return carry

    lax.fori_loop(0, dil * blocks_per_phase // DIL_BLOCKS_PER_ITER, blocks, 0)


def _dil_merge_kernel(*refs, geoms):
    n = len(geoms)
    out_ref = refs[3 * n]
    scratch = refs[3 * n + 1:]
    for g, (L, dil, radius) in enumerate(geoms):
        _dil_kernel(refs[3 * g], refs[3 * g + 1], refs[3 * g + 2], scratch[2 * g], scratch[2 * g + 1],
                    L=L, dil=dil, radius=radius)
    lses = [scratch[2 * g + 1][...] for g in range(n)]
    m = functools.reduce(jnp.maximum, lses)
    es = [jnp.exp(l - m) for l in lses]
    num = sum(scratch[2 * g][...] * es[g] for g in range(n))
    out_ref[...] = (num / sum(es)).astype(out_ref.dtype)


def _dilated_branch(cgs):
    B = cgs[0].shape[0]
    hp = GROUP_W // LANES
    geoms, in_specs, operands = [], [], []
    for cg, (window, dil) in zip(cgs, DIL_PATTERNS):
        _, d, L, _ = cg.shape
        radius = window // (2 * dil)
        assert d == dil and L >= DIL_Q + 2 * radius and L % DIL_Q == 0
        assert (dil * L // DIL_Q) % DIL_BLOCKS_PER_ITER == 0
        geoms.append((L, dil, radius))
        for c in range(3):
            in_specs.append(pl.BlockSpec((None, dil, L, LANES), lambda b, h, c=c: (b, 0, 0, c * hp + h)))
            operands.append(cg)
    T = geoms[0][0] * geoms[0][1]
    return pl.pallas_call(
        functools.partial(_dil_merge_kernel, geoms=tuple(geoms)),
        grid=(B, hp),
        in_specs=in_specs,
        out_specs=pl.BlockSpec((T, LANES), lambda b, h: (b, h)),
        out_shape=jax.ShapeDtypeStruct((B * T, GROUP_W), BF16),
        scratch_shapes=[pltpu.VMEM((T, LANES), F32)] * (2 * len(geoms)),
        compiler_params=_cparams("parallel", "arbitrary"),
        name="dil_attn",
    )(*operands)


def _merge_kernel(h_ref, ga_ref, gb_ref, gc_ref, ya_ref, yb_ref, yc_ref,
                  wpa32_ref, wpb32_ref, wpc32_ref, wo32_ref, out_ref, wpa_ref, wpb_ref, wpc_ref, wo_ref):
    @pl.when(pl.program_id(0) == 0)
    def _():
        for src, dst in ((wpa32_ref, wpa_ref), (wpb32_ref, wpb_ref), (wpc32_ref, wpc_ref), (wo32_ref, wo_ref)):
            dst[...] = src[...].astype(BF16)

    merged = jax.nn.sigmoid(ga_ref[...].astype(F32)) * jnp.dot(ya_ref[...], wpa_ref[...], preferred_element_type=F32)
    merged += jax.nn.sigmoid(gb_ref[...].astype(F32)) * jnp.dot(yb_ref[...], wpb_ref[...], preferred_element_type=F32)
    merged += jax.nn.sigmoid(gc_ref[...].astype(F32)) * jnp.dot(yc_ref[...], wpc_ref[...], preferred_element_type=F32)
    out_ref[...] = h_ref[...] + jnp.dot(merged.astype(BF16), wo_ref[...], preferred_element_type=F32)


def _merge(h2d, proj, ya, yb, yc, wpa, wpb, wpc, wo, layer, tm=512):
    n = h2d.shape[0]
    gw = GROUP_W
    row = lambda w: pl.BlockSpec((tm, w), lambda i: (i, 0))
    full = lambda a: pl.BlockSpec((None,) + a.shape[1:], lambda i: (layer, 0, 0))
    gate = lambda c: pl.BlockSpec((tm, D_MODEL), lambda i: (i, OFF_GATE // D_MODEL + c))
    return pl.pallas_call(
        _merge_kernel,
        grid=(n // tm,),
        in_specs=[row(D_MODEL), gate(0), gate(1), gate(2), row(A_W), row(B_W),
                  row(gw),
                  full(wpa), full(wpb), full(wpc), full(wo)],
        out_specs=row(D_MODEL),
        out_shape=jax.ShapeDtypeStruct((n, D_MODEL), F32),
        scratch_shapes=[pltpu.VMEM(a.shape[1:], BF16) for a in (wpa, wpb, wpc, wo)],
        compiler_params=_cparams("arbitrary"),
        name="merge_outproj",
    )(h2d, proj, proj, proj, ya, yb, yc, wpa, wpb, wpc, wo)


MOE_TILE = 512
MOE_TOKENS_PER_STEP = 512
ROUTE_ROWS = 8


ROUTER_ROWS = 32


def _router_kernel(h_ref, g_ref, wr_ref, br_ref, upper_ref, route_ref, wts_ref, cnt_ref, base_ref):
    i = pl.program_id(0)
    tm = h_ref.shape[0]
    lane_reps = tm // LANES
    row = lax.broadcasted_iota(jnp.int32, (ROUTER_ROWS, tm), 0)
    rowf = row.astype(F32)
    big = float(ROUTER_ROWS)
    nt = (((1,), (1,)), ((), ()))

    @pl.when(i == 0)
    def _():
        base_ref[...] = jnp.zeros_like(base_ref)

    x = h_ref[...]
    ms = jnp.mean(x * x, axis=-1, keepdims=True)
    xn = x * lax.rsqrt(ms + NORM_EPS) * g_ref[...]
    logits = lax.dot_general(wr_ref[...], xn, nt, preferred_element_type=F32, precision=lax.Precision.HIGHEST)
    logits = logits + jnp.concatenate([br_ref[...]] * lane_reps, axis=1)
    gl = jnp.where(row < N_GROUPS, logits, -jnp.inf)
    gmax = jnp.max(gl, axis=0, keepdims=True)
    g_sel = jnp.min(jnp.where(gl == gmax, rowf, big), axis=0, keepdims=True).astype(jnp.int32)
    g_gate = 1.0 / jnp.sum(jnp.exp(gl - gmax), axis=0, keepdims=True)
    eidx = row - N_GROUPS
    in_grp = (eidx >= g_sel * EXPERTS_PER_GROUP) & (eidx < (g_sel + 1) * EXPERTS_PER_GROUP)
    el = jnp.where(in_grp, logits, -jnp.inf)
    t1 = jnp.max(el, axis=0, keepdims=True)
    i1 = jnp.min(jnp.where(el == t1, rowf, big), axis=0, keepdims=True).astype(jnp.int32)
    el2 = jnp.where(row == i1, -jnp.inf, el)
    t2 = jnp.max(el2, axis=0, keepdims=True)
    i2 = jnp.min(jnp.where(el2 == t2, rowf, big), axis=0, keepdims=True).astype(jnp.int32)
    x2 = jnp.exp(t2 - t1)
    den = 1.0 + x2
    wts_ref[:, :LANES] = jnp.broadcast_to(g_gate / den, (LANES, tm)).T
    wts_ref[:, LANES:] = jnp.broadcast_to(g_gate * x2 / den, (LANES, tm)).T
    member = jnp.where(row == i1, 1.0, 0.0) + jnp.where(row == i2, 1.0, 0.0)
    earlier = jnp.dot(member.astype(BF16), upper_ref[...], preferred_element_type=F32)
    earlier = earlier + jnp.concatenate([base_ref[...]] * lane_reps, axis=1)
    rank1 = jnp.sum(jnp.where(row == i1, earlier, 0.0), axis=0, keepdims=True)
    rank2 = jnp.sum(jnp.where(row == i2, earlier, 0.0), axis=0, keepdims=True)
    base_ref[...] = base_ref[...] + jnp.sum(member, axis=1, keepdims=True)
    cnt_ref[...] = base_ref[...]
    out_row = lax.broadcasted_iota(jnp.int32, (ROUTE_ROWS, tm), 0)
    route_ref[...] = jnp.where(out_row == 0, (i1 - N_GROUPS).astype(F32),
                               jnp.where(out_row == 1, (i2 - N_GROUPS).astype(F32),
                                         jnp.where(out_row == 2, rank1, jnp.where(out_row == 3, rank2, 0.0))))


def _router(h2d, g, w_router, b_router, layer, tm=1024):
    n = h2d.shape[0]
    return pl.pallas_call(
        _router_kernel,
        grid=(n // tm,),
        in_specs=[
            pl.BlockSpec((tm, D_MODEL), lambda i: (i, 0)),
            pl.BlockSpec((None, 1, D_MODEL), lambda i: (layer, 0, 0)),
            pl.BlockSpec((None, ROUTER_ROWS, D_MODEL), lambda i: (layer, 0, 0)),
            pl.BlockSpec((None, ROUTER_ROWS, LANES), lambda i: (layer, 0, 0)),
            pl.BlockSpec((tm, tm), lambda i: (0, 0)),
        ],
        out_specs=[pl.BlockSpec((ROUTE_ROWS, tm), lambda i: (0, i)),
                   pl.BlockSpec((tm, 2 * LANES), lambda i: (i, 0)),
                   pl.BlockSpec((ROUTER_ROWS, LANES), lambda i: (0, 0))],
        out_shape=[jax.ShapeDtypeStruct((ROUTE_ROWS, n), F32), jax.ShapeDtypeStruct((n, 2 * LANES), F32),
                   jax.ShapeDtypeStruct((ROUTER_ROWS, LANES), F32)],
        scratch_shapes=[pltpu.VMEM((ROUTER_ROWS, LANES), F32)],
        compiler_params=_cparams("arbitrary"),
        name="moe_router",
    )(h2d, g, w_router, b_router, jnp.triu(jnp.ones((tm, tm), BF16), k=1))


def _dispatch_tables(route, counts, n):
    cnt = counts[N_GROUPS:N_GROUPS + N_EXPERTS, 0].astype(jnp.int32)
    padded = (cnt + MOE_TILE - 1) // MOE_TILE * MOE_TILE
    ends = jnp.cumsum(padded)
    starts = ends - padded
    eids = jnp.arange(N_EXPERTS, dtype=jnp.int32)
    pos = []
    for k in range(2):
        e = route[k].astype(jnp.int32)
        start_e = jnp.sum(jnp.where(e[:, None] == eids[None, :], starts[None, :], 0), axis=1)
        pos.append(start_e + route[2 + k].astype(jnp.int32))
    tb = MOE_TOKENS_PER_STEP
    idx = jnp.concatenate([pos[0].reshape(n // tb, 1, tb), pos[1].reshape(n // tb, 1, tb)], axis=-1)
    n_tiles = 2 * n // MOE_TILE + N_EXPERTS
    tile_ids = jnp.arange(n_tiles, dtype=jnp.int32)
    n_used = ends[-1] // MOE_TILE
    tile_e = jnp.sum((tile_ids[:, None] * MOE_TILE >= ends[None, :]).astype(jnp.int32), axis=1)
    last_e = jnp.max(jnp.where(tile_ids < n_used, tile_e, 0))
    tile_e = jnp.where(tile_ids < n_used, tile_e, last_e).astype(jnp.int32)
    zero_rows = jnp.where(padded > 0, ends - MOE_TILE, n_tiles * MOE_TILE).astype(jnp.int32)
    return idx, tile_e, n_used.reshape(1).astype(jnp.int32), zero_rows, n_tiles


def _dispatch_kernel(zero_rows_ref, n_used_ref, idx_ref, h_ref, xs_hbm, zeros_ref, sem):
    i = pl.program_id(0)
    tb = MOE_TOKENS_PER_STEP
    n_tiles = xs_hbm.shape[0] // MOE_TILE - 1

    def zero_tile(row0):
        return pltpu.make_async_copy(zeros_ref, xs_hbm.at[pl.ds(pl.multiple_of(row0, MOE_TILE), MOE_TILE)], sem)

    @pl.when(i == 0)
    def _():
        zeros_ref[...] = jnp.zeros_like(zeros_ref)
        fills = [zero_tile(zero_rows_ref[e]) for e in range(N_EXPERTS)]
        for c in fills:
            c.start()
        for c in fills:
            c.wait()

        def tail(j, carry):
            tile = n_used_ref[0] + j

            @pl.when(tile <= n_tiles)
            def _():
                c = zero_tile(tile * MOE_TILE)
                c.start()
                c.wait()
            return carry

        lax.fori_loop(0, N_EXPERTS + 1, tail, 0)

    def row(t, carry):
        src = h_ref.at[pl.ds(t, 1)]
        pltpu.make_async_copy(src, xs_hbm.at[pl.ds(idx_ref[0, 0, t], 1)], sem).start()
        pltpu.make_async_copy(src, xs_hbm.at[pl.ds(idx_ref[0, 0, tb + t], 1)], sem).start()
        return carry

    lax.fori_loop(0, tb, row, 0, unroll=True)
    for _ in range(2):
        pltpu.make_async_copy(h_ref, xs_hbm.at[pl.ds(0, tb)], sem).wait()


def _dispatch(h2d, idx, zero_rows, n_used, n_tiles):
    n = h2d.shape[0]
    tb = MOE_TOKENS_PER_STEP
    assert 2 * n // MOE_TILE + N_EXPERTS == n_tiles
    return pl.pallas_call(
        _dispatch_kernel,
        grid_spec=pltpu.PrefetchScalarGridSpec(
            num_scalar_prefetch=2,
            grid=(n // tb,),
            in_specs=[pl.BlockSpec((1, 1, 2 * tb), lambda i, z, u: (i, 0, 0), memory_space=pltpu.SMEM),
                      pl.BlockSpec((tb, D_MODEL), lambda i, z, u: (i, 0))],
            out_specs=pl.BlockSpec(memory_space=pl.ANY),
            scratch_shapes=[pltpu.VMEM((MOE_TILE, D_MODEL), F32), pltpu.SemaphoreType.DMA(())],
        ),
        out_shape=jax.ShapeDtypeStruct(((n_tiles + 1) * MOE_TILE, D_MODEL), F32),
        compiler_params=pltpu.CompilerParams(dimension_semantics=("arbitrary",), vmem_limit_bytes=VMEM_LIMIT,
                                             disable_bounds_checks=True),
        name="moe_dispatch",
    )(zero_rows, n_used, idx, h2d)


def _expert_kernel(tile_e_ref, n_used_ref, x_ref, g_ref, w1_ref, w3_ref, w2_ref, y_ref, w1b_ref, w3b_ref, w2b_ref):
    i = pl.program_id(0)
    used = i < n_used_ref[0]

    @pl.when(jnp.logical_not(used))
    def _():
        y_ref[...] = jnp.zeros_like(y_ref)

    @pl.when(used & ((i == 0) | (tile_e_ref[i] != tile_e_ref[jnp.maximum(i - 1, 0)])))
    def _():
        w1b_ref[...] = w1_ref[...].astype(BF16)
        w3b_ref[...] = w3_ref[...].astype(BF16)
        w2b_ref[...] = w2_ref[...].astype(BF16)

    @pl.when(used)
    def _():
        x = x_ref[...]
        ms = jnp.mean(x * x, axis=-1, keepdims=True)
        xn = (x * lax.rsqrt(ms + NORM_EPS) * g_ref[...]).astype(BF16)
        a = jnp.dot(xn, w1b_ref[...], preferred_element_type=F32)
        b = jnp.dot(xn, w3b_ref[...], preferred_element_type=F32)
        hmid = (a * jax.nn.sigmoid(a) * b).astype(BF16)
        y_ref[...] = jnp.dot(hmid, w2b_ref[...], preferred_element_type=F32)


def _experts(xs, g, w1, w3, w2, tile_e, n_used, layer, n_tiles):
    row_map = lambda i, te, nu: (i, 0)
    in_row_map = lambda i, te, nu: (jnp.minimum(i, nu[0] - 1), 0)
    w_map = lambda i, te, nu: (layer, te[i], 0, 0)
    return pl.pallas_call(
        _expert_kernel,
        grid_spec=pltpu.PrefetchScalarGridSpec(
            num_scalar_prefetch=2,
            grid=(n_tiles,),
            in_specs=[pl.BlockSpec((MOE_TILE, D_MODEL), in_row_map),
                      pl.BlockSpec((None, 1, D_MODEL), lambda i, te, nu: (layer, 0, 0)),
                      pl.BlockSpec((None, None, D_MODEL, D_FF_EXPERT), w_map),
                      pl.BlockSpec((None, None, D_MODEL, D_FF_EXPERT), w_map),
                      pl.BlockSpec((None, None, D_FF_EXPERT, D_MODEL), w_map)],
            out_specs=pl.BlockSpec((MOE_TILE, D_MODEL), row_map),
            scratch_shapes=[pltpu.VMEM((D_MODEL, D_FF_EXPERT), BF16), pltpu.VMEM((D_MODEL, D_FF_EXPERT), BF16),
                            pltpu.VMEM((D_FF_EXPERT, D_MODEL), BF16)],
        ),
        out_shape=jax.ShapeDtypeStruct((n_tiles * MOE_TILE, D_MODEL), F32),
        compiler_params=_cparams("arbitrary"),
        name="moe_experts",
    )(tile_e, n_used, xs, g, w1, w3, w2)


def _combine_kernel(idx_ref, ys_hbm, h_ref, wts_ref, gf_ref, o_ref, buf_ref, sem, *, final_norm):
    tb = h_ref.shape[0]

    def row(t, carry):
        pltpu.make_async_copy(ys_hbm.at[pl.ds(idx_ref[0, 0, t], 1)], buf_ref.at[0, pl.ds(t, 1)], sem).start()
        pltpu.make_async_copy(ys_hbm.at[pl.ds(idx_ref[0, 0, tb + t], 1)], buf_ref.at[1, pl.ds(t, 1)], sem).start()
        return carry

    lax.fori_loop(0, tb, row, 0, unroll=True)
    for k in range(2):
        pltpu.make_async_copy(ys_hbm.at[pl.ds(0, tb)], buf_ref.at[k], sem).wait()
    reps = D_MODEL // LANES
    w = wts_ref[...]
    w_top1 = jnp.concatenate([w[:, :LANES]] * reps, axis=1)
    w_top2 = jnp.concatenate([w[:, LANES:]] * reps, axis=1)
    out = h_ref[...] + w_top1 * buf_ref[0] + w_top2 * buf_ref[1]
    if final_norm:
        ms = jnp.mean(out * out, axis=-1, keepdims=True)
        out = out * lax.rsqrt(ms + NORM_EPS) * gf_ref[...]
    o_ref[...] = out


def _combine(h2d, ys, idx, wts, g_final, final_norm, batch):
    n = h2d.shape[0]
    tb = MOE_TOKENS_PER_STEP
    steps_per_seq = n // batch // tb
    return pl.pallas_call(
        functools.partial(_combine_kernel, final_norm=final_norm),
        grid=(n // tb,),
        in_specs=[pl.BlockSpec((1, 1, 2 * tb), lambda i: (i, 0, 0), memory_space=pltpu.SMEM),
                  pl.BlockSpec(memory_space=pl.ANY),
                  pl.BlockSpec((tb, D_MODEL), lambda i: (i, 0)),
                  pl.BlockSpec((tb, 2 * LANES), lambda i: (i, 0)),
                  pl.BlockSpec((1, D_MODEL), lambda i: (0, 0))],
        out_specs=pl.BlockSpec((None, tb, D_MODEL), lambda i: (i // steps_per_seq, i % steps_per_seq, 0)),
        out_shape=jax.ShapeDtypeStruct((batch, n // batch, D_MODEL), F32),
        scratch_shapes=[pltpu.VMEM((2, tb, D_MODEL), F32), pltpu.SemaphoreType.DMA(())],
        compiler_params=pltpu.CompilerParams(dimension_semantics=("arbitrary",), vmem_limit_bytes=VMEM_LIMIT,
                                             disable_bounds_checks=True),
        name="moe_combine",
    )(idx, ys, h2d, wts, g_final)


def _moe(h2d, g, w_router, b_router, w1, w3, w2, layer, g_final, final_norm, batch):
    n = h2d.shape[0]
    route, wts, counts = _router(h2d, g, w_router, b_router, layer)
    idx, tile_e, n_used, zero_rows, n_tiles = _dispatch_tables(route, counts, n)
    xs = _dispatch(h2d, idx, zero_rows, n_used, n_tiles)
    ys = _experts(xs, g, w1, w3, w2, tile_e, n_used, layer, n_tiles)
    return _combine(h2d, ys, idx, wts, g_final, final_norm, batch)


def _router_params(wg, bg, we, be):
    n_l = wg.shape[0]
    pad = ROUTER_ROWS - N_GROUPS - N_EXPERTS
    w = jnp.concatenate([wg, we, jnp.zeros((n_l, D_MODEL, pad), F32)], axis=-1).astype(F32).transpose(0, 2, 1)
    b = jnp.concatenate([bg, be, jnp.zeros((n_l, pad), F32)], axis=-1).astype(F32)
    return w, jnp.broadcast_to(b[:, :, None], (n_l, ROUTER_ROWS, LANES))


def kernel(x, w_in, na_rpb, lam_q1, lam_k1, lam_q2, lam_k2, diff_subln, w_pa, w_pb, w_pc, w_o, norm_mix, norm_ffn,
           router_group_w, router_group_b, router_expert_w, router_expert_b, w1, w3, w2, norm_final):
    B, T, D = x.shape
    depth = w_in.shape[0]
    rows = T // GRID_W
    wr = min(NA_WIN_R, rows)
    rope_tab = _rope_table(T)
    row3 = lambda a: a[:, None, :]
    w_in_bf = _permute_cols(w_in)
    bias_tab = _na_bias_table(na_rpb, wr)
    g_mix, g_ffn = row3(norm_mix), row3(norm_ffn)
    lams = [row3(a) for a in (lam_q1, lam_k1, lam_q2, lam_k2)]
    subln = row3(diff_subln)
    w_router, b_router = _router_params(router_group_w, router_group_b, router_expert_w, router_expert_b)
    h = x.reshape(B * T, D)
    for l in range(depth):
        lam_init = 0.8 - 0.6 * math.exp(-0.3 * l)
        proj, *cgs = _inproj(h, g_mix, w_in_bf, rope_tab, l, B, T)
        ya = _na_attention(proj, bias_tab, l, B, T)
        yb = _diff_attention(proj, *lams, subln, l, lam_init, B, T)
        yc = _dilated_branch(cgs)
        h = _merge(h, proj, ya, yb, yc, w_pa, w_pb, w_pc, w_o, l)
        out = _moe(h, g_ffn, w_router, b_router, w1, w3, w2, l, norm_final[None, :], l == depth - 1, B)
        h = out.reshape(B * T, D)
    return out
```

```python
import functools
import math

import jax
import jax.numpy as jnp
import numpy as np
from jax import lax
from jax.experimental import pallas as pl
from jax.experimental.pallas import tpu as pltpu

F32 = jnp.float32
BF16 = jnp.bfloat16

D_MODEL = 1024
HEAD_DIM = 64
ROPE_DIM = 16
ROPE_THETA = 500000.0
GRID_W = 64
NA_HEADS = 8
NA_WIN_R = 8
NA_WIN_C = 16
DIFF_HEADS = 4
DIL_PATTERNS = ((128, 1), (512, 4), (2048, 16))
N_GROUPS = 4
EXPERTS_PER_GROUP = 4
N_EXPERTS = 16
D_FF_EXPERT = D_MODEL // 2
NORM_EPS = 1e-6
SUBLN_EPS = 1e-5
NEG_INF = -1e30

LANES = 128
MXU_N = 256
VMEM_LIMIT = 56 * 1024 * 1024

A_W, B_W, C_W = 512, 512, 768
N_STEPS = len(DIL_PATTERNS)
GROUP_W = C_W // N_STEPS
_MAIN_ORDER = ("qb", "kb", "va", "qa", "ga", "gb", "gc", "ka", "vb")
_REF_ORDER = ("qa", "ka", "va", "qb", "kb", "vb", "qc", "kc", "vc", "ga", "gb", "gc")
_WIDTH = dict(qa=A_W, ka=A_W, va=A_W, qb=B_W, kb=B_W, vb=B_W, qc=C_W, kc=C_W, vc=C_W,
              ga=D_MODEL, gb=D_MODEL, gc=D_MODEL)


def _offsets(order):
    off, out = 0, {}
    for name in order:
        out[name] = off
        off += _WIDTH[name]
    return out, off


_OFF, MAIN_W = _offsets(_MAIN_ORDER)
_REF_OFF, IN_W = _offsets(_REF_ORDER)
OFF_QA, OFF_KA, OFF_VA = _OFF["qa"], _OFF["ka"], _OFF["va"]
OFF_QB, OFF_KB, OFF_VB = _OFF["qb"], _OFF["kb"], _OFF["vb"]
OFF_GATE = _OFF["ga"]
MAIN_STEP_W = MAIN_W // N_STEPS
STEP_W = MAIN_STEP_W + 3 * GROUP_W
MAIN_ROPE_W = 2 * B_W
QB_SCALE = HEAD_DIM ** -0.5 * math.log2(math.e)
assert OFF_GATE % D_MODEL == 0 and OFF_QB == 0 and OFF_KB == B_W and MAIN_ROPE_W <= MAIN_STEP_W
assert MAIN_W % N_STEPS == 0 and MAIN_STEP_W % MXU_N == 0 and N_STEPS * STEP_W == IN_W


def _permute_cols(w):
    def piece(name, lo, hi):
        p = w[..., _REF_OFF[name] + lo:_REF_OFF[name] + hi]
        return (p * QB_SCALE if name == "qb" else p).astype(BF16)

    parts = []
    for s in range(N_STEPS):
        for n in _MAIN_ORDER:
            lo = max(_OFF[n], s * MAIN_STEP_W) - _OFF[n]
            hi = min(_OFF[n] + _WIDTH[n], (s + 1) * MAIN_STEP_W) - _OFF[n]
            if lo < hi:
                parts.append(piece(n, lo, hi))
        for n in ("qc", "kc", "vc"):
            parts.append(piece(n, s * GROUP_W, (s + 1) * GROUP_W))
    return jnp.concatenate(parts, axis=-1)


def _cparams(*sem):
    return pltpu.CompilerParams(dimension_semantics=sem, vmem_limit_bytes=VMEM_LIMIT)


def _inproj_kernel(x_ref, g_ref, w_ref, rope_ref, main_ref, c0_ref, c1_ref, c2_ref, xn_ref, y_ref, *, tm):
    j = pl.program_id(1)
    c_refs = (c0_ref, c1_ref, c2_ref)
    reps = MXU_N // LANES
    half = ROPE_DIM // 2
    assert GROUP_W == MXU_N

    def rope(y):
        cos = jnp.concatenate([rope_ref[0]] * reps, axis=1)
        s_up = jnp.concatenate([rope_ref[1]] * reps, axis=1)
        s_dn = jnp.concatenate([rope_ref[2]] * reps, axis=1)
        return y * cos + pltpu.roll(y, MXU_N - half, 1) * s_up + pltpu.roll(y, half, 1) * s_dn

    def step(s):
        xn = xn_ref[...]
        for c in range(MAIN_STEP_W // MXU_N):
            sl = slice(c * MXU_N, (c + 1) * MXU_N)
            y = jnp.dot(xn, w_ref[:, sl], preferred_element_type=F32)
            if s == 0 and c < MAIN_ROPE_W // MXU_N:
                y = rope(y)
            main_ref[:, sl] = y.astype(main_ref.dtype)
        dil = DIL_PATTERNS[s][1]
        for c in range(3):
            wsl = slice(MAIN_STEP_W + c * GROUP_W, MAIN_STEP_W + (c + 1) * GROUP_W)
            osl = slice(c * GROUP_W, (c + 1) * GROUP_W)
            y = jnp.dot(xn, w_ref[:, wsl], preferred_element_type=F32)
            if c < 2:
                y = rope(y)
            if dil == 1:
                c_refs[s][0, :, osl] = y.astype(BF16)
            else:
                for hb in range(reps):
                    y_ref[c, hb] = y[:, hb * LANES:(hb + 1) * LANES]
                for p in range(dil):
                    for hb in range(reps):
                        c_refs[s][p, :, c * GROUP_W + hb * LANES:c * GROUP_W + (hb + 1) * LANES] = (
                            y_ref[c, hb, pl.ds(p, tm // dil, stride=dil), :].astype(BF16))

    @pl.when(j == 0)
    def _():
        x = x_ref[...]
        ms = jnp.mean(x * x, axis=-1, keepdims=True)
        xn_ref[...] = (x * lax.rsqrt(ms + NORM_EPS) * g_ref[...]).astype(BF16)
        step(0)

    for s in range(1, N_STEPS):
        pl.when(j == s)(functools.partial(step, s))


def _inproj(h2d, g, w_bf, rope_tab, layer, B, T, tm=1024):
    n = h2d.shape[0]
    tpb = T // tm
    c_specs, c_shapes = [], []
    for _, dil in DIL_PATTERNS:
        assert tm % dil == 0
        c_specs.append(pl.BlockSpec((None, dil, tm // dil, 3 * GROUP_W), lambda i, j: (i // tpb, 0, i % tpb, 0)))
        c_shapes.append(jax.ShapeDtypeStruct((B, dil, T // dil, 3 * GROUP_W), BF16))
    return pl.pallas_call(
        functools.partial(_inproj_kernel, tm=tm),
        grid=(n // tm, N_STEPS),
        in_specs=[
            pl.BlockSpec((tm, D_MODEL), lambda i, j: (i, 0)),
            pl.BlockSpec((None, 1, D_MODEL), lambda i, j: (layer, 0, 0)),
            pl.BlockSpec((None, D_MODEL, STEP_W), lambda i, j: (layer, 0, j)),
            pl.BlockSpec((3, tm, LANES), lambda i, j: (0, i % tpb, 0)),
        ],
        out_specs=[pl.BlockSpec((tm, MAIN_STEP_W), lambda i, j: (i, j))] + c_specs,
        out_shape=[jax.ShapeDtypeStruct((n, MAIN_W), BF16)] + c_shapes,
        scratch_shapes=[pltpu.VMEM((tm, D_MODEL), BF16), pltpu.VMEM((3, MXU_N // LANES, tm, LANES), F32)],
        compiler_params=_cparams("parallel", "arbitrary"),
        name="inproj",
    )(h2d, g, w_bf, rope_tab)


def _rope_table(T):
    half = ROPE_DIM // 2
    inv = 1.0 / (ROPE_THETA ** (jnp.arange(0, ROPE_DIM, 2, dtype=F32) / ROPE_DIM))
    ang = jnp.arange(T, dtype=F32)[:, None] * inv[None, :]
    cos, sin = jnp.cos(ang), jnp.sin(ang)
    zeros = jnp.zeros((T, HEAD_DIM - ROPE_DIM), F32)
    z8 = jnp.zeros((T, half), F32)
    c64 = jnp.concatenate([cos, cos, zeros + 1.0], axis=1)
    up64 = jnp.concatenate([-sin, z8, zeros], axis=1)
    dn64 = jnp.concatenate([z8, sin, zeros], axis=1)
    reps = LANES // HEAD_DIM
    return jnp.stack([jnp.tile(c64, (1, reps)), jnp.tile(up64, (1, reps)), jnp.tile(dn64, (1, reps))])


NA_ROWS_PER_ITER = 32


def _na_kernel(q_ref, k_ref, v_ref, b_ref, o_ref, *, rows, wr):
    lane = lax.broadcasted_iota(jnp.int32, (GRID_W, LANES), 1)
    lo = lane < HEAD_DIM
    scale = HEAD_DIM ** -0.5

    def row_group(gi, carry):
        scores, windows = [], []
        for u in range(NA_ROWS_PER_ITER):
            r = gi * NA_ROWS_PER_ITER + u
            r0 = jnp.clip(r - wr // 2, 0, rows - wr)
            d0 = r0 - r + NA_WIN_R - 1 - (NA_WIN_R - wr)
            q = q_ref[pl.ds(pl.multiple_of(r * GRID_W, GRID_W), GRID_W), :] * scale
            ks = pl.multiple_of(r0 * GRID_W, GRID_W)
            kw = k_ref[pl.ds(ks, wr * GRID_W), :]
            windows.append(ks)
            for hh in range(2):
                qh = jnp.where(lo if hh == 0 else jnp.logical_not(lo), q, jnp.zeros_like(q))
                s = lax.dot_general(qh, kw, (((1,), (1,)), ((), ())), preferred_element_type=F32)
                bias = jnp.concatenate([b_ref[hh, d0 + 2 * i] for i in range(wr // 2)], axis=1)
                scores.append(s + bias)
        probs, sums = [], []
        for s in scores:
            p = jnp.exp(s - jnp.max(s, axis=-1, keepdims=True))
            sums.append(jnp.sum(p, axis=-1, keepdims=True))
            probs.append(p.astype(BF16))
        for u in range(NA_ROWS_PER_ITER):
            r = gi * NA_ROWS_PER_ITER + u
            vw = v_ref[pl.ds(windows[u], wr * GRID_W), :]
            outs = [jnp.dot(probs[2 * u + hh], vw, preferred_element_type=F32) / sums[2 * u + hh] for hh in range(2)]
            o = jnp.where(lo, outs[0], outs[1])
            o_ref[pl.ds(pl.multiple_of(r * GRID_W, GRID_W), GRID_W), :] = o.astype(o_ref.dtype)
        return carry

    lax.fori_loop(0, rows // NA_ROWS_PER_ITER, row_group, 0)


def _na_bias_table(rpb, wr):
    qc = np.arange(GRID_W)[:, None]
    kc = np.arange(GRID_W)[None, :]
    c0 = np.clip(qc - NA_WIN_C // 2, 0, GRID_W - NA_WIN_C)
    ok = (kc >= c0) & (kc < c0 + NA_WIN_C)
    dc = np.clip(kc - qc + NA_WIN_C - 1, 0, 2 * NA_WIN_C - 2)
    onehot = (np.arange(2 * NA_WIN_C - 1)[:, None, None] == dc[None]).astype(np.float32)
    b = jnp.einsum("lhrd,dqk->lhrqk", rpb.astype(F32), onehot, precision=lax.Precision.HIGHEST)
    b = jnp.where(jnp.asarray(ok), b, NEG_INF)
    off = NA_WIN_R - wr
    n_pairs = 2 * wr - 2
    return jnp.concatenate([b[:, :, off:off + n_pairs], b[:, :, off + 1:off + 1 + n_pairs]], axis=-1)


def _na_attention(proj, bias_tab, layer, B, T):
    rows = T // GRID_W
    wr = min(NA_WIN_R, rows)
    cq, ck, cv = OFF_QA // LANES, OFF_KA // LANES, OFF_VA // LANES
    assert wr % 2 == 0 and 2 * GRID_W == LANES and rows % NA_ROWS_PER_ITER == 0
    return pl.pallas_call(
        functools.partial(_na_kernel, rows=rows, wr=wr),
        grid=(B, NA_HEADS // 2),
        in_specs=[
            pl.BlockSpec((T, LANES), lambda b, h: (b, cq + h)),
            pl.BlockSpec((T, LANES), lambda b, h: (b, ck + h)),
            pl.BlockSpec((T, LANES), lambda b, h: (b, cv + h)),
            pl.BlockSpec((None, 2, 2 * wr - 2, GRID_W, 2 * GRID_W), lambda b, h: (layer, h, 0, 0, 0)),
        ],
        out_specs=pl.BlockSpec((T, LANES), lambda b, h: (b, h)),
        out_shape=jax.ShapeDtypeStruct((B * T, A_W), BF16),
        compiler_params=_cparams("parallel", "arbitrary"),
        name="na_attn",
    )(proj, proj, proj, bias_tab)


DIFF_KEY_CHUNK = 512


def _diff_kernel(lq1_ref, lk1_ref, lq2_ref, lk2_ref, q_ref, k_ref, v_ref, g_ref, o_ref, vt_ref, *, lam_init, tq):
    lam = (jnp.exp(jnp.sum(lq1_ref[...] * lk1_ref[...], keepdims=True))
           - jnp.exp(jnp.sum(lq2_ref[...] * lk2_ref[...], keepdims=True)) + lam_init)
    T = k_ref.shape[0]
    ck = DIFF_KEY_CHUNK
    n_chunks = T // ck
    lane = lax.broadcasted_iota(jnp.int32, (tq, LANES), 1)
    lo = lane < HEAD_DIM
    nt = (((1,), (1,)), ((), ()))
    vt_ref[...] = v_ref[...].T

    def q_block(i, carry):
        rows = pl.ds(pl.multiple_of(i * tq, tq), tq)
        q = q_ref[rows, :]
        zero = jnp.zeros_like(q)
        qs = (jnp.where(lo, q, zero), jnp.where(lo, zero, q))

        def scores(c):
            kc = k_ref[c * ck:(c + 1) * ck, :]
            return [lax.dot_general(kc, qm, nt, preferred_element_type=F32) for qm in qs]

        m = [jnp.full((1, tq), NEG_INF, F32)] * 2
        l = [jnp.zeros((1, tq), F32)] * 2
        acc = [jnp.zeros((LANES, tq), F32)] * 2
        s_next = scores(0)
        for c in range(n_chunks):
            s_cur = s_next
            if c + 1 < n_chunks:
                s_next = scores(c + 1)
            vt = vt_ref[:, c * ck:(c + 1) * ck]
            for j in range(2):
                m_new = jnp.maximum(m[j], jnp.max(s_cur[j], axis=0, keepdims=True))
                alpha = jnp.exp2(m[j] - m_new)
                p = jnp.exp2(s_cur[j] - m_new)
                l[j] = alpha * l[j] + jnp.sum(p, axis=0, keepdims=True)
                acc[j] = alpha * acc[j] + jnp.dot(vt, p.astype(BF16), preferred_element_type=F32)
                m[j] = m_new
        ot = acc[0] / l[0] - lam * (acc[1] / l[1])
        o = ot.T
        ms = jnp.mean(o * o, axis=-1, keepdims=True)
        o = o * lax.rsqrt(ms + SUBLN_EPS) * g_ref[...] * (1.0 - lam_init)
        o_ref[rows, :] = o.astype(o_ref.dtype)
        return carry

    lax.fori_loop(0, q_ref.shape[0] // tq, q_block, 0, unroll=2)


def _diff_attention(proj, lq1, lk1, lq2, lk2, subln_g, layer, lam_init, B, T, tq=512):
    cq, ck, cv = OFF_QB // LANES, OFF_KB // LANES, OFF_VB // LANES
    vec = pl.BlockSpec((None, 1, HEAD_DIM), lambda b, h: (layer, 0, 0))
    return pl.pallas_call(
        functools.partial(_diff_kernel, lam_init=lam_init, tq=tq),
        grid=(B, DIFF_HEADS),
        in_specs=[
            vec, vec, vec, vec,
            pl.BlockSpec((T, LANES), lambda b, h: (b, cq + h)),
            pl.BlockSpec((T, LANES), lambda b, h: (b, ck + h)),
            pl.BlockSpec((T, LANES), lambda b, h: (b, cv + h)),
            pl.BlockSpec((None, 1, LANES), lambda b, h: (layer, 0, 0)),
        ],
        out_specs=pl.BlockSpec((T, LANES), lambda b, h: (b, h)),
        out_shape=jax.ShapeDtypeStruct((B * T, B_W), BF16),
        scratch_shapes=[pltpu.VMEM((LANES, T), BF16)],
        compiler_params=_cparams("parallel", "arbitrary"),
        name="diff_attn",
    )(lq1, lk1, lq2, lk2, proj, proj, proj, subln_g)


DIL_Q = 128


DIL_BLOCKS_PER_ITER = 4


def _dil_kernel(q_ref, k_ref, v_ref, o_ref, lse_ref, *, L, dil, radius):
    kw_len = DIL_Q + 2 * radius
    lane = lax.broadcasted_iota(jnp.int32, (DIL_Q, LANES), 1)
    lo = lane < HEAD_DIM
    rel = (lax.broadcasted_iota(jnp.int32, (DIL_Q, kw_len), 1)
           - lax.broadcasted_iota(jnp.int32, (DIL_Q, kw_len), 0))
    scale = HEAD_DIM ** -0.5
    nt = (((1,), (1,)), ((), ()))

    blocks_per_phase = L // DIL_Q

    def blocks(gi, carry):
        scores, starts, maxes = [], [], []
        for u in range(DIL_BLOCKS_PER_ITER):
            item = gi * DIL_BLOCKS_PER_ITER + u
            p = item // blocks_per_phase
            l0 = pl.multiple_of((item % blocks_per_phase) * DIL_Q, DIL_Q)
            ks = pl.multiple_of(jnp.clip(l0 - radius, 0, L - kw_len), radius)
            q = q_ref[p, pl.ds(l0, DIL_Q), :] * scale
            kw = k_ref[p, pl.ds(ks, kw_len), :]
            ok = jnp.abs(rel + (ks - l0)) <= radius
            starts.append((p, l0, ks))
            for hh in range(2):
                qh = jnp.where(lo if hh == 0 else jnp.logical_not(lo), q, jnp.zeros_like(q))
                s = lax.dot_general(qh, kw, nt, preferred_element_type=F32)
                scores.append(jnp.where(ok, s, NEG_INF))
        probs, sums = [], []
        for s in scores:
            m = jnp.max(s, axis=-1, keepdims=True)
            e = jnp.exp(s - m)
            maxes.append(m)
            sums.append(jnp.sum(e, axis=-1, keepdims=True))
            probs.append(e.astype(BF16))
        for u in range(DIL_BLOCKS_PER_ITER):
            p, l0, ks = starts[u]
            vw = v_ref[p, pl.ds(ks, kw_len), :]
            outs = [jnp.dot(probs[2 * u + hh], vw, preferred_element_type=F32) / sums[2 * u + hh] for hh in range(2)]
            lses = [maxes[2 * u + hh] + jnp.log(sums[2 * u + hh]) for hh in range(2)]
            o = jnp.where(lo, outs[0], outs[1])
            lse = jnp.where(lo, lses[0], lses[1])
            if dil == 1:
                o_ref[pl.ds(l0, DIL_Q), :] = o
                lse_ref[pl.ds(l0, DIL_Q), :] = lse
            else:
                o_ref[pl.ds(l0 * dil + p, DIL_Q, stride=dil), :] = o
                lse_ref[pl.ds(l0 * dil + p, DIL_Q, stride=dil), :] = lse
        return carry

    lax.fori_loop(0, dil * blocks_per_phase // DIL_BLOCKS_PER_ITER, blocks, 0)


def _dil_merge_kernel(*refs, geoms):
    n = len(geoms)
    out_ref = refs[3 * n]
    scratch = refs[3 * n + 1:]
    for g, (L, dil, radius) in enumerate(geoms):
        _dil_kernel(refs[3 * g], refs[3 * g + 1], refs[3 * g + 2], scratch[2 * g], scratch[2 * g + 1],
                    L=L, dil=dil, radius=radius)
    lses = [scratch[2 * g + 1][...] for g in range(n)]
    m = functools.reduce(jnp.maximum, lses)
    es = [jnp.exp(l - m) for l in lses]
    num = sum(scratch[2 * g][...] * es[g] for g in range(n))
    out_ref[...] = (num / sum(es)).astype(out_ref.dtype)


def _dilated_branch(cgs):
    B = cgs[0].shape[0]
    hp = GROUP_W // LANES
    geoms, in_specs, operands = [], [], []
    for cg, (window, dil) in zip(cgs, DIL_PATTERNS):
        _, d, L, _ = cg.shape
        radius = window // (2 * dil)
        assert d == dil and L >= DIL_Q + 2 * radius and L % DIL_Q == 0
        assert (dil * L // DIL_Q) % DIL_BLOCKS_PER_ITER == 0
        geoms.append((L, dil, radius))
        for c in range(3):
            in_specs.append(pl.BlockSpec((None, dil, L, LANES), lambda b, h, c=c: (b, 0, 0, c * hp + h)))
            operands.append(cg)
    T = geoms[0][0] * geoms[0][1]
    return pl.pallas_call(
        functools.partial(_dil_merge_kernel, geoms=tuple(geoms)),
        grid=(B, hp),
        in_specs=in_specs,
        out_specs=pl.BlockSpec((T, LANES), lambda b, h: (b, h)),
        out_shape=jax.ShapeDtypeStruct((B * T, GROUP_W), BF16),
        scratch_shapes=[pltpu.VMEM((T, LANES), F32)] * (2 * len(geoms)),
        compiler_params=_cparams("parallel", "arbitrary"),
        name="dil_attn",
    )(*operands)


def _merge_kernel(h_ref, ga_ref, gb_ref, gc_ref, ya_ref, yb_ref, yc_ref,
                  wpa32_ref, wpb32_ref, wpc32_ref, wo32_ref, out_ref, wpa_ref, wpb_ref, wpc_ref, wo_ref):
    @pl.when(pl.program_id(0) == 0)
    def _():
        for src, dst in ((wpa32_ref, wpa_ref), (wpb32_ref, wpb_ref), (wpc32_ref, wpc_ref), (wo32_ref, wo_ref)):
            dst[...] = src[...].astype(BF16)

    merged = jax.nn.sigmoid(ga_ref[...].astype(F32)) * jnp.dot(ya_ref[...], wpa_ref[...], preferred_element_type=F32)
    merged += jax.nn.sigmoid(gb_ref[...].astype(F32)) * jnp.dot(yb_ref[...], wpb_ref[...], preferred_element_type=F32)
    merged += jax.nn.sigmoid(gc_ref[...].astype(F32)) * jnp.dot(yc_ref[...], wpc_ref[...], preferred_element_type=F32)
    out_ref[...] = h_ref[...] + jnp.dot(merged.astype(BF16), wo_ref[...], preferred_element_type=F32)


def _merge(h2d, proj, ya, yb, yc, wpa, wpb, wpc, wo, layer, tm=512):
    n = h2d.shape[0]
    gw = GROUP_W
    row = lambda w: pl.BlockSpec((tm, w), lambda i: (i, 0))
    full = lambda a: pl.BlockSpec((None,) + a.shape[1:], lambda i: (layer, 0, 0))
    gate = lambda c: pl.BlockSpec((tm, D_MODEL), lambda i: (i, OFF_GATE // D_MODEL + c))
    return pl.pallas_call(
        _merge_kernel,
        grid=(n // tm,),
        in_specs=[row(D_MODEL), gate(0), gate(1), gate(2), row(A_W), row(B_W),
                  row(gw),
                  full(wpa), full(wpb), full(wpc), full(wo)],
        out_specs=row(D_MODEL),
        out_shape=jax.ShapeDtypeStruct((n, D_MODEL), F32),
        scratch_shapes=[pltpu.VMEM(a.shape[1:], BF16) for a in (wpa, wpb, wpc, wo)],
        compiler_params=_cparams("arbitrary"),
        name="merge_outproj",
    )(h2d, proj, proj, proj, ya, yb, yc, wpa, wpb, wpc, wo)


MOE_TILE = 512
MOE_TOKENS_PER_STEP = 512
ROUTE_ROWS = 8
ROW_SLABS = D_MODEL // LANES


def _to_slabs(x, slab_ref):
    rows = x.shape[0]
    for j in range(ROW_SLABS):
        slab_ref[pl.ds(j, rows, stride=ROW_SLABS), :] = x[:, j * LANES:(j + 1) * LANES]


def _from_slabs(slab_ref, rows):
    return [slab_ref[pl.ds(j, rows, stride=ROW_SLABS), :] for j in range(ROW_SLABS)]


ROUTER_ROWS = 32


def _router_kernel(h_ref, g_ref, wr_ref, br_ref, upper_ref, route_ref, wts_ref, cnt_ref, base_ref):
    i = pl.program_id(0)
    tm = h_ref.shape[0]
    lane_reps = tm // LANES
    row = lax.broadcasted_iota(jnp.int32, (ROUTER_ROWS, tm), 0)
    rowf = row.astype(F32)
    big = float(ROUTER_ROWS)
    nt = (((1,), (1,)), ((), ()))

    @pl.when(i == 0)
    def _():
        base_ref[...] = jnp.zeros_like(base_ref)

    x = h_ref[...]
    ms = jnp.mean(x * x, axis=-1, keepdims=True)
    xn = x * lax.rsqrt(ms + NORM_EPS) * g_ref[...]
    logits = lax.dot_general(wr_ref[...], xn, nt, preferred_element_type=F32, precision=lax.Precision.HIGHEST)
    logits = logits + jnp.concatenate([br_ref[...]] * lane_reps, axis=1)
    gl = jnp.where(row < N_GROUPS, logits, -jnp.inf)
    gmax = jnp.max(gl, axis=0, keepdims=True)
    g_sel = jnp.min(jnp.where(gl == gmax, rowf, big), axis=0, keepdims=True).astype(jnp.int32)
    g_gate = 1.0 / jnp.sum(jnp.exp(gl - gmax), axis=0, keepdims=True)
    eidx = row - N_GROUPS
    in_grp = (eidx >= g_sel * EXPERTS_PER_GROUP) & (eidx < (g_sel + 1) * EXPERTS_PER_GROUP)
    el = jnp.where(in_grp, logits, -jnp.inf)
    t1 = jnp.max(el, axis=0, keepdims=True)
    i1 = jnp.min(jnp.where(el == t1, rowf, big), axis=0, keepdims=True).astype(jnp.int32)
    el2 = jnp.where(row == i1, -jnp.inf, el)
    t2 = jnp.max(el2, axis=0, keepdims=True)
    i2 = jnp.min(jnp.where(el2 == t2, rowf, big), axis=0, keepdims=True).astype(jnp.int32)
    x2 = jnp.exp(t2 - t1)
    den = 1.0 + x2
    wts_ref[:, :LANES] = jnp.broadcast_to(g_gate / den, (LANES, tm)).T
    wts_ref[:, LANES:] = jnp.broadcast_to(g_gate * x2 / den, (LANES, tm)).T
    member = jnp.where(row == i1, 1.0, 0.0) + jnp.where(row == i2, 1.0, 0.0)
    earlier = jnp.dot(member.astype(BF16), upper_ref[...], preferred_element_type=F32)
    earlier = earlier + jnp.concatenate([base_ref[...]] * lane_reps, axis=1)
    rank1 = jnp.sum(jnp.where(row == i1, earlier, 0.0), axis=0, keepdims=True)
    rank2 = jnp.sum(jnp.where(row == i2, earlier, 0.0), axis=0, keepdims=True)
    base_ref[...] = base_ref[...] + jnp.sum(member, axis=1, keepdims=True)
    cnt_ref[...] = base_ref[...]
    out_row = lax.broadcasted_iota(jnp.int32, (ROUTE_ROWS, tm), 0)
    route_ref[...] = jnp.where(out_row == 0, (i1 - N_GROUPS).astype(F32),
                               jnp.where(out_row == 1, (i2 - N_GROUPS).astype(F32),
                                         jnp.where(out_row == 2, rank1, jnp.where(out_row == 3, rank2, 0.0))))


def _router(h2d, g, w_router, b_router, layer, tm=1024):
    n = h2d.shape[0]
    return pl.pallas_call(
        _router_kernel,
        grid=(n // tm,),
        in_specs=[
            pl.BlockSpec((tm, D_MODEL), lambda i: (i, 0)),
            pl.BlockSpec((None, 1, D_MODEL), lambda i: (layer, 0, 0)),
            pl.BlockSpec((None, ROUTER_ROWS, D_MODEL), lambda i: (layer, 0, 0)),
            pl.BlockSpec((None, ROUTER_ROWS, LANES), lambda i: (layer, 0, 0)),
            pl.BlockSpec((tm, tm), lambda i: (0, 0)),
        ],
        out_specs=[pl.BlockSpec((ROUTE_ROWS, tm), lambda i: (0, i)),
                   pl.BlockSpec((tm, 2 * LANES), lambda i: (i, 0)),
                   pl.BlockSpec((ROUTER_ROWS, LANES), lambda i: (0, 0))],
        out_shape=[jax.ShapeDtypeStruct((ROUTE_ROWS, n), F32), jax.ShapeDtypeStruct((n, 2 * LANES), F32),
                   jax.ShapeDtypeStruct((ROUTER_ROWS, LANES), F32)],
        scratch_shapes=[pltpu.VMEM((ROUTER_ROWS, LANES), F32)],
        compiler_params=_cparams("arbitrary"),
        name="moe_router",
    )(h2d, g, w_router, b_router, jnp.triu(jnp.ones((tm, tm), BF16), k=1))


def _dispatch_tables(route, counts, n):
    cnt = counts[N_GROUPS:N_GROUPS + N_EXPERTS, 0].astype(jnp.int32)
    padded = (cnt + MOE_TILE - 1) // MOE_TILE * MOE_TILE
    ends = jnp.cumsum(padded)
    starts = ends - padded
    eids = jnp.arange(N_EXPERTS, dtype=jnp.int32)
    pos = []
    for k in range(2):
        e = route[k].astype(jnp.int32)
        start_e = jnp.sum(jnp.where(e[:, None] == eids[None, :], starts[None, :], 0), axis=1)
        pos.append(start_e + route[2 + k].astype(jnp.int32))
    tb = MOE_TOKENS_PER_STEP
    idx = jnp.concatenate([pos[0].reshape(n // tb, 1, tb), pos[1].reshape(n // tb, 1, tb)], axis=-1)
    n_tiles = 2 * n // MOE_TILE + N_EXPERTS
    tile_ids = jnp.arange(n_tiles, dtype=jnp.int32)
    n_used = ends[-1] // MOE_TILE
    tile_e = jnp.sum((tile_ids[:, None] * MOE_TILE >= ends[None, :]).astype(jnp.int32), axis=1)
    last_e = jnp.max(jnp.where(tile_ids < n_used, tile_e, 0))
    tile_e = jnp.where(tile_ids < n_used, tile_e, last_e).astype(jnp.int32)
    zero_rows = jnp.where(padded > 0, ends - MOE_TILE, n_tiles * MOE_TILE).astype(jnp.int32)
    return idx, tile_e, n_used.reshape(1).astype(jnp.int32), zero_rows, n_tiles


def _dispatch_kernel(zero_rows_ref, n_used_ref, idx_ref, h_ref, xs_hbm, zeros_ref, hs_ref, sem):
    i = pl.program_id(0)
    tb = MOE_TOKENS_PER_STEP
    tile_rows = MOE_TILE * ROW_SLABS
    n_tiles = xs_hbm.shape[0] // tile_rows - 1

    def zero_tile(row0):
        start = pl.multiple_of(row0 * ROW_SLABS, tile_rows)
        return pltpu.make_async_copy(zeros_ref, xs_hbm.at[pl.ds(start, tile_rows)], sem)

    @pl.when(i == 0)
    def _():
        zeros_ref[...] = jnp.zeros_like(zeros_ref)
        fills = [zero_tile(zero_rows_ref[e]) for e in range(N_EXPERTS)]
        for c in fills:
            c.start()
        for c in fills:
            c.wait()

        def tail(j, carry):
            tile = n_used_ref[0] + j

            @pl.when(tile <= n_tiles)
            def _():
                c = zero_tile(tile * MOE_TILE)
                c.start()
                c.wait()
            return carry

        lax.fori_loop(0, N_EXPERTS + 1, tail, 0)

    _to_slabs(h_ref[...], hs_ref)

    def row(t, carry):
        src = hs_ref.at[pl.ds(t * ROW_SLABS, ROW_SLABS)]
        for k in range(2):
            dst = pl.multiple_of(idx_ref[0, 0, k * tb + t] * ROW_SLABS, ROW_SLABS)
            pltpu.make_async_copy(src, xs_hbm.at[pl.ds(dst, ROW_SLABS)], sem).start()
        return carry

    lax.fori_loop(0, tb, row, 0, unroll=True)
    for _ in range(2):
        pltpu.make_async_copy(hs_ref, xs_hbm.at[pl.ds(0, tb * ROW_SLABS)], sem).wait()


def _dispatch(h2d, idx, zero_rows, n_used, n_tiles):
    n = h2d.shape[0]
    tb = MOE_TOKENS_PER_STEP
    assert 2 * n // MOE_TILE + N_EXPERTS == n_tiles
    return pl.pallas_call(
        _dispatch_kernel,
        grid_spec=pltpu.PrefetchScalarGridSpec(
            num_scalar_prefetch=2,
            grid=(n // tb,),
            in_specs=[pl.BlockSpec((1, 1, 2 * tb), lambda i, z, u: (i, 0, 0), memory_space=pltpu.SMEM),
                      pl.BlockSpec((tb, D_MODEL), lambda i, z, u: (i, 0))],
            out_specs=pl.BlockSpec(memory_space=pl.ANY),
            scratch_shapes=[pltpu.VMEM((MOE_TILE * ROW_SLABS, LANES), F32), pltpu.VMEM((tb * ROW_SLABS, LANES), F32),
                            pltpu.SemaphoreType.DMA(())],
        ),
        out_shape=jax.ShapeDtypeStruct(((n_tiles + 1) * MOE_TILE * ROW_SLABS, LANES), F32),
        compiler_params=pltpu.CompilerParams(dimension_semantics=("arbitrary",), vmem_limit_bytes=VMEM_LIMIT,
                                             disable_bounds_checks=True),
        name="moe_dispatch",
    )(zero_rows, n_used, idx, h2d)


def _expert_kernel(tile_e_ref, n_used_ref, x_ref, g_ref, w1_ref, w3_ref, w2_ref, y_ref, w1b_ref, w3b_ref, w2b_ref):
    i = pl.program_id(0)
    used = i < n_used_ref[0]

    @pl.when(jnp.logical_not(used))
    def _():
        y_ref[...] = jnp.zeros_like(y_ref)

    @pl.when(used & ((i == 0) | (tile_e_ref[i] != tile_e_ref[jnp.maximum(i - 1, 0)])))
    def _():
        w1b_ref[...] = w1_ref[...].astype(BF16)
        w3b_ref[...] = w3_ref[...].astype(BF16)
        w2b_ref[...] = w2_ref[...].astype(BF16)

    @pl.when(used)
    def _():
        x = jnp.concatenate(_from_slabs(x_ref, MOE_TILE), axis=1)
        ms = jnp.mean(x * x, axis=-1, keepdims=True)
        xn = (x * lax.rsqrt(ms + NORM_EPS) * g_ref[...]).astype(BF16)
        a = jnp.dot(xn, w1b_ref[...], preferred_element_type=F32)
        b = jnp.dot(xn, w3b_ref[...], preferred_element_type=F32)
        hmid = (a * jax.nn.sigmoid(a) * b).astype(BF16)
        _to_slabs(jnp.dot(hmid, w2b_ref[...], preferred_element_type=F32), y_ref)


def _experts(xs, g, w1, w3, w2, tile_e, n_used, layer, n_tiles):
    row_map = lambda i, te, nu: (i, 0)
    in_row_map = lambda i, te, nu: (jnp.minimum(i, nu[0] - 1), 0)
    w_map = lambda i, te, nu: (layer, te[i], 0, 0)
    return pl.pallas_call(
        _expert_kernel,
        grid_spec=pltpu.PrefetchScalarGridSpec(
            num_scalar_prefetch=2,
            grid=(n_tiles,),
            in_specs=[pl.BlockSpec((MOE_TILE * ROW_SLABS, LANES), in_row_map),
                      pl.BlockSpec((None, 1, D_MODEL), lambda i, te, nu: (layer, 0, 0)),
                      pl.BlockSpec((None, None, D_MODEL, D_FF_EXPERT), w_map),
                      pl.BlockSpec((None, None, D_MODEL, D_FF_EXPERT), w_map),
                      pl.BlockSpec((None, None, D_FF_EXPERT, D_MODEL), w_map)],
            out_specs=pl.BlockSpec((MOE_TILE * ROW_SLABS, LANES), row_map),
            scratch_shapes=[pltpu.VMEM((D_MODEL, D_FF_EXPERT), BF16), pltpu.VMEM((D_MODEL, D_FF_EXPERT), BF16),
                            pltpu.VMEM((D_FF_EXPERT, D_MODEL), BF16)],
        ),
        out_shape=jax.ShapeDtypeStruct((n_tiles * MOE_TILE * ROW_SLABS, LANES), F32),
        compiler_params=_cparams("arbitrary"),
        name="moe_experts",
    )(tile_e, n_used, xs, g, w1, w3, w2)


def _combine_kernel(idx_ref, ys_hbm, h_ref, wts_ref, gf_ref, o_ref, buf_ref, sem, *, final_norm):
    tb = h_ref.shape[0]

    def row(t, carry):
        for k in range(2):
            src = pl.multiple_of(idx_ref[0, 0, k * tb + t] * ROW_SLABS, ROW_SLABS)
            pltpu.make_async_copy(ys_hbm.at[pl.ds(src, ROW_SLABS)],
                                  buf_ref.at[k, pl.ds(t * ROW_SLABS, ROW_SLABS)], sem).start()
        return carry

    lax.fori_loop(0, tb, row, 0, unroll=True)
    for k in range(2):
        pltpu.make_async_copy(ys_hbm.at[pl.ds(0, tb * ROW_SLABS)], buf_ref.at[k], sem).wait()
    w = wts_ref[...]
    w_top1, w_top2 = w[:, :LANES], w[:, LANES:]
    y1, y2 = _from_slabs(buf_ref.at[0], tb), _from_slabs(buf_ref.at[1], tb)
    out = [h_ref[:, j * LANES:(j + 1) * LANES] + w_top1 * y1[j] + w_top2 * y2[j] for j in range(ROW_SLABS)]
    if final_norm:
        ms = sum(jnp.sum(o * o, axis=-1, keepdims=True) for o in out) * (1.0 / D_MODEL)
        scale = lax.rsqrt(ms + NORM_EPS)
        out = [o * scale * gf_ref[:, j * LANES:(j + 1) * LANES] for j, o in enumerate(out)]
    for j, o in enumerate(out):
        o_ref[:, j * LANES:(j + 1) * LANES] = o


def _combine(h2d, ys, idx, wts, g_final, final_norm, batch):
    n = h2d.shape[0]
    tb = MOE_TOKENS_PER_STEP
    steps_per_seq = n // batch // tb
    return pl.pallas_call(
        functools.partial(_combine_kernel, final_norm=final_norm),
        grid=(n // tb,),
        in_specs=[pl.BlockSpec((1, 1, 2 * tb), lambda i: (i, 0, 0), memory_space=pltpu.SMEM),
                  pl.BlockSpec(memory_space=pl.ANY),
                  pl.BlockSpec((tb, D_MODEL), lambda i: (i, 0)),
                  pl.BlockSpec((tb, 2 * LANES), lambda i: (i, 0)),
                  pl.BlockSpec((1, D_MODEL), lambda i: (0, 0))],
        out_specs=pl.BlockSpec((None, tb, D_MODEL), lambda i: (i // steps_per_seq, i % steps_per_seq, 0)),
        out_shape=jax.ShapeDtypeStruct((batch, n // batch, D_MODEL), F32),
        scratch_shapes=[pltpu.VMEM((2, tb * ROW_SLABS, LANES), F32), pltpu.SemaphoreType.DMA(())],
        compiler_params=pltpu.CompilerParams(dimension_semantics=("arbitrary",), vmem_limit_bytes=VMEM_LIMIT,
                                             disable_bounds_checks=True),
        name="moe_combine",
    )(idx, ys, h2d, wts, g_final)


def _moe(h2d, g, w_router, b_router, w1, w3, w2, layer, g_final, final_norm, batch):
    n = h2d.shape[0]
    route, wts, counts = _router(h2d, g, w_router, b_router, layer)
    idx, tile_e, n_used, zero_rows, n_tiles = _dispatch_tables(route, counts, n)
    xs = _dispatch(h2d, idx, zero_rows, n_used, n_tiles)
    ys = _experts(xs, g, w1, w3, w2, tile_e, n_used, layer, n_tiles)
    return _combine(h2d, ys, idx, wts, g_final, final_norm, batch)


def _router_params(wg, bg, we, be):
    n_l = wg.shape[0]
    pad = ROUTER_ROWS - N_GROUPS - N_EXPERTS
    w = jnp.concatenate([wg, we, jnp.zeros((n_l, D_MODEL, pad), F32)], axis=-1).astype(F32).transpose(0, 2, 1)
    b = jnp.concatenate([bg, be, jnp.zeros((n_l, pad), F32)], axis=-1).astype(F32)
    return w, jnp.broadcast_to(b[:, :, None], (n_l, ROUTER_ROWS, LANES))


def kernel(x, w_in, na_rpb, lam_q1, lam_k1, lam_q2, lam_k2, diff_subln, w_pa, w_pb, w_pc, w_o, norm_mix, norm_ffn,
           router_group_w, router_group_b, router_expert_w, router_expert_b, w1, w3, w2, norm_final):
    B, T, D = x.shape
    depth = w_in.shape[0]
    rows = T // GRID_W
    wr = min(NA_WIN_R, rows)
    rope_tab = _rope_table(T)
    row3 = lambda a: a[:, None, :]
    w_in_bf = _permute_cols(w_in)
    bias_tab = _na_bias_table(na_rpb, wr)
    g_mix, g_ffn = row3(norm_mix), row3(norm_ffn)
    lams = [row3(a) for a in (lam_q1, lam_k1, lam_q2, lam_k2)]
    subln = row3(diff_subln)
    w_router, b_router = _router_params(router_group_w, router_group_b, router_expert_w, router_expert_b)
    h = x.reshape(B * T, D)
    for l in range(depth):
        lam_init = 0.8 - 0.6 * math.exp(-0.3 * l)
        proj, *cgs = _inproj(h, g_mix, w_in_bf, rope_tab, l, B, T)
        ya = _na_attention(proj, bias_tab, l, B, T)
        yb = _diff_attention(proj, *lams, subln, l, lam_init, B, T)
        yc = _dilated_branch(cgs)
        h = _merge(h, proj, ya, yb, yc, w_pa, w_pb, w_pc, w_o, l)
        out = _moe(h, g_ffn, w_router, b_router, w1, w3, w2, l, norm_final[None, :], l == depth - 1, B)
        h = out.reshape(B * T, D)
    return out
```

```python
import functools
import math

import jax
import jax.numpy as jnp
import numpy as np
from jax import lax
from jax.experimental import pallas as pl
from jax.experimental.pallas import tpu as pltpu

F32 = jnp.float32
BF16 = jnp.bfloat16

D_MODEL = 1024
HEAD_DIM = 64
ROPE_DIM = 16
ROPE_THETA = 500000.0
GRID_W = 64
NA_HEADS = 8
NA_WIN_R = 8
NA_WIN_C = 16
DIFF_HEADS = 4
DIL_PATTERNS = ((128, 1), (512, 4), (2048, 16))
N_GROUPS = 4
EXPERTS_PER_GROUP = 4
N_EXPERTS = 16
D_FF_EXPERT = D_MODEL // 2
NORM_EPS = 1e-6
SUBLN_EPS = 1e-5
NEG_INF = -1e30

LANES = 128
MXU_N = 256
VMEM_LIMIT = 56 * 1024 * 1024

A_W, B_W, C_W = 512, 512, 768
N_STEPS = len(DIL_PATTERNS)
GROUP_W = C_W // N_STEPS
_MAIN_ORDER = ("qb", "kb", "va", "qa", "ga", "gb", "gc", "ka", "vb")
_REF_ORDER = ("qa", "ka", "va", "qb", "kb", "vb", "qc", "kc", "vc", "ga", "gb", "gc")
_WIDTH = dict(qa=A_W, ka=A_W, va=A_W, qb=B_W, kb=B_W, vb=B_W, qc=C_W, kc=C_W, vc=C_W,
              ga=D_MODEL, gb=D_MODEL, gc=D_MODEL)


def _offsets(order):
    off, out = 0, {}
    for name in order:
        out[name] = off
        off += _WIDTH[name]
    return out, off


_OFF, MAIN_W = _offsets(_MAIN_ORDER)
_REF_OFF, IN_W = _offsets(_REF_ORDER)
OFF_QA, OFF_KA, OFF_VA = _OFF["qa"], _OFF["ka"], _OFF["va"]
OFF_QB, OFF_KB, OFF_VB = _OFF["qb"], _OFF["kb"], _OFF["vb"]
OFF_GATE = _OFF["ga"]
MAIN_STEP_W = MAIN_W // N_STEPS
STEP_W = MAIN_STEP_W + 3 * GROUP_W
MAIN_ROPE_W = 2 * B_W
QB_SCALE = HEAD_DIM ** -0.5 * math.log2(math.e)
assert OFF_GATE % D_MODEL == 0 and OFF_QB == 0 and OFF_KB == B_W and MAIN_ROPE_W <= MAIN_STEP_W
assert MAIN_W % N_STEPS == 0 and MAIN_STEP_W % MXU_N == 0 and N_STEPS * STEP_W == IN_W


def _permute_cols(w):
    def piece(name, lo, hi):
        p = w[..., _REF_OFF[name] + lo:_REF_OFF[name] + hi]
        return (p * QB_SCALE if name == "qb" else p).astype(BF16)

    parts = []
    for s in range(N_STEPS):
        for n in _MAIN_ORDER:
            lo = max(_OFF[n], s * MAIN_STEP_W) - _OFF[n]
            hi = min(_OFF[n] + _WIDTH[n], (s + 1) * MAIN_STEP_W) - _OFF[n]
            if lo < hi:
                parts.append(piece(n, lo, hi))
        for n in ("qc", "kc", "vc"):
            parts.append(piece(n, s * GROUP_W, (s + 1) * GROUP_W))
    return jnp.concatenate(parts, axis=-1)


def _cparams(*sem):
    return pltpu.CompilerParams(dimension_semantics=sem, vmem_limit_bytes=VMEM_LIMIT)


def _inproj_kernel(x_ref, g_ref, w_ref, rope_ref, main_ref, c0_ref, c1_ref, c2_ref, xn_ref, y_ref, *, tm):
    j = pl.program_id(1)
    c_refs = (c0_ref, c1_ref, c2_ref)
    reps = MXU_N // LANES
    half = ROPE_DIM // 2
    assert GROUP_W == MXU_N

    def rope(y):
        cos = jnp.concatenate([rope_ref[0]] * reps, axis=1)
        s_up = jnp.concatenate([rope_ref[1]] * reps, axis=1)
        s_dn = jnp.concatenate([rope_ref[2]] * reps, axis=1)
        return y * cos + pltpu.roll(y, MXU_N - half, 1) * s_up + pltpu.roll(y, half, 1) * s_dn

    def step(s):
        xn = xn_ref[...]
        for c in range(MAIN_STEP_W // MXU_N):
            sl = slice(c * MXU_N, (c + 1) * MXU_N)
            y = jnp.dot(xn, w_ref[:, sl], preferred_element_type=F32)
            if s == 0 and c < MAIN_ROPE_W // MXU_N:
                y = rope(y)
            main_ref[:, sl] = y.astype(main_ref.dtype)
        dil = DIL_PATTERNS[s][1]
        for c in range(3):
            wsl = slice(MAIN_STEP_W + c * GROUP_W, MAIN_STEP_W + (c + 1) * GROUP_W)
            osl = slice(c * GROUP_W, (c + 1) * GROUP_W)
            y = jnp.dot(xn, w_ref[:, wsl], preferred_element_type=F32)
            if c < 2:
                y = rope(y)
            if dil == 1:
                c_refs[s][0, :, osl] = y.astype(BF16)
            else:
                for hb in range(reps):
                    y_ref[c, hb] = y[:, hb * LANES:(hb + 1) * LANES]
                for p in range(dil):
                    for hb in range(reps):
                        c_refs[s][p, :, c * GROUP_W + hb * LANES:c * GROUP_W + (hb + 1) * LANES] = (
                            y_ref[c, hb, pl.ds(p, tm // dil, stride=dil), :].astype(BF16))

    @pl.when(j == 0)
    def _():
        x = x_ref[...]
        ms = jnp.mean(x * x, axis=-1, keepdims=True)
        xn_ref[...] = (x * lax.rsqrt(ms + NORM_EPS) * g_ref[...]).astype(BF16)
        step(0)

    for s in range(1, N_STEPS):
        pl.when(j == s)(functools.partial(step, s))


def _inproj(h2d, g, w_bf, rope_tab, layer, B, T, tm=1024):
    n = h2d.shape[0]
    tpb = T // tm
    c_specs, c_shapes = [], []
    for _, dil in DIL_PATTERNS:
        assert tm % dil == 0
        c_specs.append(pl.BlockSpec((None, dil, tm // dil, 3 * GROUP_W), lambda i, j: (i // tpb, 0, i % tpb, 0)))
        c_shapes.append(jax.ShapeDtypeStruct((B, dil, T // dil, 3 * GROUP_W), BF16))
    return pl.pallas_call(
        functools.partial(_inproj_kernel, tm=tm),
        grid=(n // tm, N_STEPS),
        in_specs=[
            pl.BlockSpec((tm, D_MODEL), lambda i, j: (i, 0)),
            pl.BlockSpec((None, 1, D_MODEL), lambda i, j: (layer, 0, 0)),
            pl.BlockSpec((None, D_MODEL, STEP_W), lambda i, j: (layer, 0, j)),
            pl.BlockSpec((3, tm, LANES), lambda i, j: (0, i % tpb, 0)),
        ],
        out_specs=[pl.BlockSpec((tm, MAIN_STEP_W), lambda i, j: (i, j))] + c_specs,
        out_shape=[jax.ShapeDtypeStruct((n, MAIN_W), BF16)] + c_shapes,
        scratch_shapes=[pltpu.VMEM((tm, D_MODEL), BF16), pltpu.VMEM((3, MXU_N // LANES, tm, LANES), F32)],
        compiler_params=_cparams("parallel", "arbitrary"),
        name="inproj",
    )(h2d, g, w_bf, rope_tab)


def _rope_table(T):
    half = ROPE_DIM // 2
    inv = 1.0 / (ROPE_THETA ** (jnp.arange(0, ROPE_DIM, 2, dtype=F32) / ROPE_DIM))
    ang = jnp.arange(T, dtype=F32)[:, None] * inv[None, :]
    cos, sin = jnp.cos(ang), jnp.sin(ang)
    zeros = jnp.zeros((T, HEAD_DIM - ROPE_DIM), F32)
    z8 = jnp.zeros((T, half), F32)
    c64 = jnp.concatenate([cos, cos, zeros + 1.0], axis=1)
    up64 = jnp.concatenate([-sin, z8, zeros], axis=1)
    dn64 = jnp.concatenate([z8, sin, zeros], axis=1)
    reps = LANES // HEAD_DIM
    return jnp.stack([jnp.tile(c64, (1, reps)), jnp.tile(up64, (1, reps)), jnp.tile(dn64, (1, reps))])


NA_ROWS_PER_ITER = 32


def _na_kernel(q_ref, k_ref, v_ref, b_ref, o_ref, *, rows, wr):
    lane = lax.broadcasted_iota(jnp.int32, (GRID_W, LANES), 1)
    lo = lane < HEAD_DIM
    scale = HEAD_DIM ** -0.5

    def row_group(gi, carry):
        scores, windows = [], []
        for u in range(NA_ROWS_PER_ITER):
            r = gi * NA_ROWS_PER_ITER + u
            r0 = jnp.clip(r - wr // 2, 0, rows - wr)
            d0 = r0 - r + NA_WIN_R - 1 - (NA_WIN_R - wr)
            q = q_ref[pl.ds(pl.multiple_of(r * GRID_W, GRID_W), GRID_W), :] * scale
            ks = pl.multiple_of(r0 * GRID_W, GRID_W)
            kw = k_ref[pl.ds(ks, wr * GRID_W), :]
            windows.append(ks)
            for hh in range(2):
                qh = jnp.where(lo if hh == 0 else jnp.logical_not(lo), q, jnp.zeros_like(q))
                s = lax.dot_general(qh, kw, (((1,), (1,)), ((), ())), preferred_element_type=F32)
                bias = jnp.concatenate([b_ref[hh, d0 + 2 * i] for i in range(wr // 2)], axis=1)
                scores.append(s + bias)
        probs, sums = [], []
        for s in scores:
            p = jnp.exp(s - jnp.max(s, axis=-1, keepdims=True))
            sums.append(jnp.sum(p, axis=-1, keepdims=True))
            probs.append(p.astype(BF16))
        for u in range(NA_ROWS_PER_ITER):
            r = gi * NA_ROWS_PER_ITER + u
            vw = v_ref[pl.ds(windows[u], wr * GRID_W), :]
            outs = [jnp.dot(probs[2 * u + hh], vw, preferred_element_type=F32) / sums[2 * u + hh] for hh in range(2)]
            o = jnp.where(lo, outs[0], outs[1])
            o_ref[pl.ds(pl.multiple_of(r * GRID_W, GRID_W), GRID_W), :] = o.astype(o_ref.dtype)
        return carry

    lax.fori_loop(0, rows // NA_ROWS_PER_ITER, row_group, 0)


def _na_bias_table(rpb, wr):
    qc = np.arange(GRID_W)[:, None]
    kc = np.arange(GRID_W)[None, :]
    c0 = np.clip(qc - NA_WIN_C // 2, 0, GRID_W - NA_WIN_C)
    ok = (kc >= c0) & (kc < c0 + NA_WIN_C)
    dc = np.clip(kc - qc + NA_WIN_C - 1, 0, 2 * NA_WIN_C - 2)
    onehot = (np.arange(2 * NA_WIN_C - 1)[:, None, None] == dc[None]).astype(np.float32)
    b = jnp.einsum("lhrd,dqk->lhrqk", rpb.astype(F32), onehot, precision=lax.Precision.HIGHEST)
    b = jnp.where(jnp.asarray(ok), b, NEG_INF)
    off = NA_WIN_R - wr
    n_pairs = 2 * wr - 2
    return jnp.concatenate([b[:, :, off:off + n_pairs], b[:, :, off + 1:off + 1 + n_pairs]], axis=-1)


def _na_attention(proj, bias_tab, layer, B, T):
    rows = T // GRID_W
    wr = min(NA_WIN_R, rows)
    cq, ck, cv = OFF_QA // LANES, OFF_KA // LANES, OFF_VA // LANES
    assert wr % 2 == 0 and 2 * GRID_W == LANES and rows % NA_ROWS_PER_ITER == 0
    return pl.pallas_call(
        functools.partial(_na_kernel, rows=rows, wr=wr),
        grid=(B, NA_HEADS // 2),
        in_specs=[
            pl.BlockSpec((T, LANES), lambda b, h: (b, cq + h)),
            pl.BlockSpec((T, LANES), lambda b, h: (b, ck + h)),
            pl.BlockSpec((T, LANES), lambda b, h: (b, cv + h)),
            pl.BlockSpec((None, 2, 2 * wr - 2, GRID_W, 2 * GRID_W), lambda b, h: (layer, h, 0, 0, 0)),
        ],
        out_specs=pl.BlockSpec((T, LANES), lambda b, h: (b, h)),
        out_shape=jax.ShapeDtypeStruct((B * T, A_W), BF16),
        compiler_params=_cparams("parallel", "arbitrary"),
        name="na_attn",
    )(proj, proj, proj, bias_tab)


DIFF_KEY_CHUNK = 512


def _diff_kernel(lq1_ref, lk1_ref, lq2_ref, lk2_ref, q_ref, k_ref, v_ref, g_ref, o_ref, vt_ref, *, lam_init, tq):
    lam = (jnp.exp(jnp.sum(lq1_ref[...] * lk1_ref[...], keepdims=True))
           - jnp.exp(jnp.sum(lq2_ref[...] * lk2_ref[...], keepdims=True)) + lam_init)
    T = k_ref.shape[0]
    ck = DIFF_KEY_CHUNK
    n_chunks = T // ck
    lane = lax.broadcasted_iota(jnp.int32, (tq, LANES), 1)
    lo = lane < HEAD_DIM
    nt = (((1,), (1,)), ((), ()))
    vt_ref[...] = v_ref[...].T

    def q_block(i, carry):
        rows = pl.ds(pl.multiple_of(i * tq, tq), tq)
        q = q_ref[rows, :]
        zero = jnp.zeros_like(q)
        qs = (jnp.where(lo, q, zero), jnp.where(lo, zero, q))

        def scores(c):
            kc = k_ref[c * ck:(c + 1) * ck, :]
            return [lax.dot_general(kc, qm, nt, preferred_element_type=F32) for qm in qs]

        m = [jnp.full((1, tq), NEG_INF, F32)] * 2
        l = [jnp.zeros((1, tq), F32)] * 2
        acc = [jnp.zeros((LANES, tq), F32)] * 2
        s_next = scores(0)
        for c in range(n_chunks):
            s_cur = s_next
            if c + 1 < n_chunks:
                s_next = scores(c + 1)
            vt = vt_ref[:, c * ck:(c + 1) * ck]
            for j in range(2):
                m_new = jnp.maximum(m[j], jnp.max(s_cur[j], axis=0, keepdims=True))
                alpha = jnp.exp2(m[j] - m_new)
                p = jnp.exp2(s_cur[j] - m_new)
                l[j] = alpha * l[j] + jnp.sum(p, axis=0, keepdims=True)
                acc[j] = alpha * acc[j] + jnp.dot(vt, p.astype(BF16), preferred_element_type=F32)
                m[j] = m_new
        ot = acc[0] / l[0] - lam * (acc[1] / l[1])
        o = ot.T
        ms = jnp.mean(o * o, axis=-1, keepdims=True)
        o = o * lax.rsqrt(ms + SUBLN_EPS) * g_ref[...] * (1.0 - lam_init)
        o_ref[rows, :] = o.astype(o_ref.dtype)
        return carry

    lax.fori_loop(0, q_ref.shape[0] // tq, q_block, 0, unroll=2)


def _diff_attention(proj, lq1, lk1, lq2, lk2, subln_g, layer, lam_init, B, T, tq=512):
    cq, ck, cv = OFF_QB // LANES, OFF_KB // LANES, OFF_VB // LANES
    vec = pl.BlockSpec((None, 1, HEAD_DIM), lambda b, h: (layer, 0, 0))
    return pl.pallas_call(
        functools.partial(_diff_kernel, lam_init=lam_init, tq=tq),
        grid=(B, DIFF_HEADS),
        in_specs=[
            vec, vec, vec, vec,
            pl.BlockSpec((T, LANES), lambda b, h: (b, cq + h)),
            pl.BlockSpec((T, LANES), lambda b, h: (b, ck + h)),
            pl.BlockSpec((T, LANES), lambda b, h: (b, cv + h)),
            pl.BlockSpec((None, 1, LANES), lambda b, h: (layer, 0, 0)),
        ],
        out_specs=pl.BlockSpec((T, LANES), lambda b, h: (b, h)),
        out_shape=jax.ShapeDtypeStruct((B * T, B_W), BF16),
        scratch_shapes=[pltpu.VMEM((LANES, T), BF16)],
        compiler_params=_cparams("parallel", "arbitrary"),
        name="diff_attn",
    )(lq1, lk1, lq2, lk2, proj, proj, proj, subln_g)


DIL_Q = 128


DIL_BLOCKS_PER_ITER = 4


def _dil_kernel(q_ref, k_ref, v_ref, o_ref, lse_ref, *, L, dil, radius):
    kw_len = DIL_Q + 2 * radius
    lane = lax.broadcasted_iota(jnp.int32, (DIL_Q, LANES), 1)
    lo = lane < HEAD_DIM
    rel = (lax.broadcasted_iota(jnp.int32, (DIL_Q, kw_len), 1)
           - lax.broadcasted_iota(jnp.int32, (DIL_Q, kw_len), 0))
    scale = HEAD_DIM ** -0.5
    nt = (((1,), (1,)), ((), ()))

    blocks_per_phase = L // DIL_Q

    def blocks(gi, carry):
        scores, starts, maxes = [], [], []
        for u in range(DIL_BLOCKS_PER_ITER):
            item = gi * DIL_BLOCKS_PER_ITER + u
            p = item // blocks_per_phase
            l0 = pl.multiple_of((item % blocks_per_phase) * DIL_Q, DIL_Q)
            ks = pl.multiple_of(jnp.clip(l0 - radius, 0, L - kw_len), radius)
            q = q_ref[p, pl.ds(l0, DIL_Q), :] * scale
            kw = k_ref[p, pl.ds(ks, kw_len), :]
            ok = jnp.abs(rel + (ks - l0)) <= radius
            starts.append((p, l0, ks))
            for hh in range(2):
                qh = jnp.where(lo if hh == 0 else jnp.logical_not(lo), q, jnp.zeros_like(q))
                s = lax.dot_general(qh, kw, nt, preferred_element_type=F32)
                scores.append(jnp.where(ok, s, NEG_INF))
        probs, sums = [], []
        for s in scores:
            m = jnp.max(s, axis=-1, keepdims=True)
            e = jnp.exp(s - m)
            maxes.append(m)
            sums.append(jnp.sum(e, axis=-1, keepdims=True))
            probs.append(e.astype(BF16))
        for u in range(DIL_BLOCKS_PER_ITER):
            p, l0, ks = starts[u]
            vw = v_ref[p, pl.ds(ks, kw_len), :]
            outs = [jnp.dot(probs[2 * u + hh], vw, preferred_element_type=F32) / sums[2 * u + hh] for hh in range(2)]
            lses = [maxes[2 * u + hh] + jnp.log(sums[2 * u + hh]) for hh in range(2)]
            o = jnp.where(lo, outs[0], outs[1])
            lse = jnp.where(lo, lses[0], lses[1])
            if dil == 1:
                o_ref[pl.ds(l0, DIL_Q), :] = o
                lse_ref[pl.ds(l0, DIL_Q), :] = lse
            else:
                o_ref[pl.ds(l0 * dil + p, DIL_Q, stride=dil), :] = o
                lse_ref[pl.ds(l0 * dil + p, DIL_Q, stride=dil), :] = lse
        return carry

    lax.fori_loop(0, dil * blocks_per_phase // DIL_BLOCKS_PER_ITER, blocks, 0)


def _dil_merge_kernel(*refs, geoms):
    n = len(geoms)
    out_ref = refs[3 * n]
    scratch = refs[3 * n + 1:]
    for g, (L, dil, radius) in enumerate(geoms):
        _dil_kernel(refs[3 * g], refs[3 * g + 1], refs[3 * g + 2], scratch[2 * g], scratch[2 * g + 1],
                    L=L, dil=dil, radius=radius)
    lses = [scratch[2 * g + 1][...] for g in range(n)]
    m = functools.reduce(jnp.maximum, lses)
    es = [jnp.exp(l - m) for l in lses]
    num = sum(scratch[2 * g][...] * es[g] for g in range(n))
    out_ref[...] = (num / sum(es)).astype(out_ref.dtype)


def _dilated_branch(cgs):
    B = cgs[0].shape[0]
    hp = GROUP_W // LANES
    geoms, in_specs, operands = [], [], []
    for cg, (window, dil) in zip(cgs, DIL_PATTERNS):
        _, d, L, _ = cg.shape
        radius = window // (2 * dil)
        assert d == dil and L >= DIL_Q + 2 * radius and L % DIL_Q == 0
        assert (dil * L // DIL_Q) % DIL_BLOCKS_PER_ITER == 0
        geoms.append((L, dil, radius))
        for c in range(3):
            in_specs.append(pl.BlockSpec((None, dil, L, LANES), lambda b, h, c=c: (b, 0, 0, c * hp + h)))
            operands.append(cg)
    T = geoms[0][0] * geoms[0][1]
    return pl.pallas_call(
        functools.partial(_dil_merge_kernel, geoms=tuple(geoms)),
        grid=(B, hp),
        in_specs=in_specs,
        out_specs=pl.BlockSpec((T, LANES), lambda b, h: (b, h)),
        out_shape=jax.ShapeDtypeStruct((B * T, GROUP_W), BF16),
        scratch_shapes=[pltpu.VMEM((T, LANES), F32)] * (2 * len(geoms)),
        compiler_params=_cparams("parallel", "arbitrary"),
        name="dil_attn",
    )(*operands)


def _merge_kernel(h_ref, ga_ref, gb_ref, gc_ref, ya_ref, yb_ref, yc_ref,
                  wpa32_ref, wpb32_ref, wpc32_ref, wo32_ref, out_ref, wpa_ref, wpb_ref, wpc_ref, wo_ref):
    @pl.when(pl.program_id(0) == 0)
    def _():
        for src, dst in ((wpa32_ref, wpa_ref), (wpb32_ref, wpb_ref), (wpc32_ref, wpc_ref), (wo32_ref, wo_ref)):
            dst[...] = src[...].astype(BF16)

    merged = jax.nn.sigmoid(ga_ref[...].astype(F32)) * jnp.dot(ya_ref[...], wpa_ref[...], preferred_element_type=F32)
    merged += jax.nn.sigmoid(gb_ref[...].astype(F32)) * jnp.dot(yb_ref[...], wpb_ref[...], preferred_element_type=F32)
    merged += jax.nn.sigmoid(gc_ref[...].astype(F32)) * jnp.dot(yc_ref[...], wpc_ref[...], preferred_element_type=F32)
    out_ref[...] = h_ref[...] + jnp.dot(merged.astype(BF16), wo_ref[...], preferred_element_type=F32)


def _merge(h2d, proj, ya, yb, yc, wpa, wpb, wpc, wo, layer, tm=512):
    n = h2d.shape[0]
    gw = GROUP_W
    row = lambda w: pl.BlockSpec((tm, w), lambda i: (i, 0))
    full = lambda a: pl.BlockSpec((None,) + a.shape[1:], lambda i: (layer, 0, 0))
    gate = lambda c: pl.BlockSpec((tm, D_MODEL), lambda i: (i, OFF_GATE // D_MODEL + c))
    return pl.pallas_call(
        _merge_kernel,
        grid=(n // tm,),
        in_specs=[row(D_MODEL), gate(0), gate(1), gate(2), row(A_W), row(B_W),
                  row(gw),
                  full(wpa), full(wpb), full(wpc), full(wo)],
        out_specs=row(D_MODEL),
        out_shape=jax.ShapeDtypeStruct((n, D_MODEL), F32),
        scratch_shapes=[pltpu.VMEM(a.shape[1:], BF16) for a in (wpa, wpb, wpc, wo)],
        compiler_params=_cparams("arbitrary"),
        name="merge_outproj",
    )(h2d, proj, proj, proj, ya, yb, yc, wpa, wpb, wpc, wo)


MOE_TILE = 512
MOE_TOKENS_PER_STEP = 1024
ROUTE_ROWS = 8


ROUTER_ROWS = 32


def _router_kernel(h_ref, g_ref, wr_ref, br_ref, upper_ref, route_ref, wts_ref, cnt_ref, base_ref):
    i = pl.program_id(0)
    tm = h_ref.shape[0]
    lane_reps = tm // LANES
    row = lax.broadcasted_iota(jnp.int32, (ROUTER_ROWS, tm), 0)
    rowf = row.astype(F32)
    big = float(ROUTER_ROWS)
    nt = (((1,), (1,)), ((), ()))

    @pl.when(i == 0)
    def _():
        base_ref[...] = jnp.zeros_like(base_ref)

    x = h_ref[...]
    ms = jnp.mean(x * x, axis=-1, keepdims=True)
    xn = x * lax.rsqrt(ms + NORM_EPS) * g_ref[...]
    logits = lax.dot_general(wr_ref[...], xn, nt, preferred_element_type=F32, precision=lax.Precision.HIGHEST)
    logits = logits + jnp.concatenate([br_ref[...]] * lane_reps, axis=1)
    gl = jnp.where(row < N_GROUPS, logits, -jnp.inf)
    gmax = jnp.max(gl, axis=0, keepdims=True)
    g_sel = jnp.min(jnp.where(gl == gmax, rowf, big), axis=0, keepdims=True).astype(jnp.int32)
    g_gate = 1.0 / jnp.sum(jnp.exp(gl - gmax), axis=0, keepdims=True)
    eidx = row - N_GROUPS
    in_grp = (eidx >= g_sel * EXPERTS_PER_GROUP) & (eidx < (g_sel + 1) * EXPERTS_PER_GROUP)
    el = jnp.where(in_grp, logits, -jnp.inf)
    t1 = jnp.max(el, axis=0, keepdims=True)
    i1 = jnp.min(jnp.where(el == t1, rowf, big), axis=0, keepdims=True).astype(jnp.int32)
    el2 = jnp.where(row == i1, -jnp.inf, el)
    t2 = jnp.max(el2, axis=0, keepdims=True)
    i2 = jnp.min(jnp.where(el2 == t2, rowf, big), axis=0, keepdims=True).astype(jnp.int32)
    x2 = jnp.exp(t2 - t1)
    den = 1.0 + x2
    wts_ref[:, :LANES] = jnp.broadcast_to(g_gate / den, (LANES, tm)).T
    wts_ref[:, LANES:] = jnp.broadcast_to(g_gate * x2 / den, (LANES, tm)).T
    member = jnp.where(row == i1, 1.0, 0.0) + jnp.where(row == i2, 1.0, 0.0)
    earlier = jnp.dot(member.astype(BF16), upper_ref[...], preferred_element_type=F32)
    earlier = earlier + jnp.concatenate([base_ref[...]] * lane_reps, axis=1)
    rank1 = jnp.sum(jnp.where(row == i1, earlier, 0.0), axis=0, keepdims=True)
    rank2 = jnp.sum(jnp.where(row == i2, earlier, 0.0), axis=0, keepdims=True)
    base_ref[...] = base_ref[...] + jnp.sum(member, axis=1, keepdims=True)
    cnt_ref[...] = base_ref[...]
    out_row = lax.broadcasted_iota(jnp.int32, (ROUTE_ROWS, tm), 0)
    route_ref[...] = jnp.where(out_row == 0, (i1 - N_GROUPS).astype(F32),
                               jnp.where(out_row == 1, (i2 - N_GROUPS).astype(F32),
                                         jnp.where(out_row == 2, rank1, jnp.where(out_row == 3, rank2, 0.0))))


def _router(h2d, g, w_router, b_router, layer, tm=1024):
    n = h2d.shape[0]
    return pl.pallas_call(
        _router_kernel,
        grid=(n // tm,),
        in_specs=[
            pl.BlockSpec((tm, D_MODEL), lambda i: (i, 0)),
            pl.BlockSpec((None, 1, D_MODEL), lambda i: (layer, 0, 0)),
            pl.BlockSpec((None, ROUTER_ROWS, D_MODEL), lambda i: (layer, 0, 0)),
            pl.BlockSpec((None, ROUTER_ROWS, LANES), lambda i: (layer, 0, 0)),
            pl.BlockSpec((tm, tm), lambda i: (0, 0)),
        ],
        out_specs=[pl.BlockSpec((ROUTE_ROWS, tm), lambda i: (0, i)),
                   pl.BlockSpec((tm, 2 * LANES), lambda i: (i, 0)),
                   pl.BlockSpec((ROUTER_ROWS, LANES), lambda i: (0, 0))],
        out_shape=[jax.ShapeDtypeStruct((ROUTE_ROWS, n), F32), jax.ShapeDtypeStruct((n, 2 * LANES), F32),
                   jax.ShapeDtypeStruct((ROUTER_ROWS, LANES), F32)],
        scratch_shapes=[pltpu.VMEM((ROUTER_ROWS, LANES), F32)],
        compiler_params=_cparams("arbitrary"),
        name="moe_router",
    )(h2d, g, w_router, b_router, jnp.triu(jnp.ones((tm, tm), BF16), k=1))


def _dispatch_tables(route, counts, n):
    cnt = counts[N_GROUPS:N_GROUPS + N_EXPERTS, 0].astype(jnp.int32)
    padded = (cnt + MOE_TILE - 1) // MOE_TILE * MOE_TILE
    ends = jnp.cumsum(padded)
    starts = ends - padded
    eids = jnp.arange(N_EXPERTS, dtype=jnp.int32)
    pos = []
    for k in range(2):
        e = route[k].astype(jnp.int32)
        start_e = jnp.sum(jnp.where(e[:, None] == eids[None, :], starts[None, :], 0), axis=1)
        pos.append(start_e + route[2 + k].astype(jnp.int32))
    tb = MOE_TOKENS_PER_STEP
    idx = jnp.concatenate([pos[0].reshape(n // tb, 1, tb), pos[1].reshape(n // tb, 1, tb)], axis=-1)
    n_tiles = 2 * n // MOE_TILE + N_EXPERTS
    tile_ids = jnp.arange(n_tiles, dtype=jnp.int32)
    n_used = ends[-1] // MOE_TILE
    tile_e = jnp.sum((tile_ids[:, None] * MOE_TILE >= ends[None, :]).astype(jnp.int32), axis=1)
    last_e = jnp.max(jnp.where(tile_ids < n_used, tile_e, 0))
    tile_e = jnp.where(tile_ids < n_used, tile_e, last_e).astype(jnp.int32)
    zero_rows = jnp.where(padded > 0, ends - MOE_TILE, n_tiles * MOE_TILE).astype(jnp.int32)
    return idx, tile_e, n_used.reshape(1).astype(jnp.int32), zero_rows, n_tiles


def _dispatch_kernel(zero_rows_ref, n_used_ref, idx_ref, h_ref, xs_hbm, zeros_ref, sem):
    i = pl.program_id(0)
    tb = MOE_TOKENS_PER_STEP
    n_tiles = xs_hbm.shape[0] // MOE_TILE - 1

    def zero_tile(row0):
        return pltpu.make_async_copy(zeros_ref, xs_hbm.at[pl.ds(pl.multiple_of(row0, MOE_TILE), MOE_TILE)], sem)

    @pl.when(i == 0)
    def _():
        zeros_ref[...] = jnp.zeros_like(zeros_ref)
        fills = [zero_tile(zero_rows_ref[e]) for e in range(N_EXPERTS)]
        for c in fills:
            c.start()
        for c in fills:
            c.wait()

        def tail(j, carry):
            tile = n_used_ref[0] + j

            @pl.when(tile <= n_tiles)
            def _():
                c = zero_tile(tile * MOE_TILE)
                c.start()
                c.wait()
            return carry

        lax.fori_loop(0, N_EXPERTS + 1, tail, 0)

    def row(t, carry):
        src = h_ref.at[pl.ds(t, 1)]
        pltpu.make_async_copy(src, xs_hbm.at[pl.ds(idx_ref[0, 0, t], 1)], sem).start()
        pltpu.make_async_copy(src, xs_hbm.at[pl.ds(idx_ref[0, 0, tb + t], 1)], sem).start()
        return carry

    lax.fori_loop(0, tb, row, 0, unroll=True)
    for _ in range(2):
        pltpu.make_async_copy(h_ref, xs_hbm.at[pl.ds(0, tb)], sem).wait()


def _dispatch(h2d, idx, zero_rows, n_used, n_tiles):
    n = h2d.shape[0]
    tb = MOE_TOKENS_PER_STEP
    assert 2 * n // MOE_TILE + N_EXPERTS == n_tiles
    return pl.pallas_call(
        _dispatch_kernel,
        grid_spec=pltpu.PrefetchScalarGridSpec(
            num_scalar_prefetch=2,
            grid=(n // tb,),
            in_specs=[pl.BlockSpec((1, 1, 2 * tb), lambda i, z, u: (i, 0, 0), memory_space=pltpu.SMEM),
                      pl.BlockSpec((tb, D_MODEL), lambda i, z, u: (i, 0))],
            out_specs=pl.BlockSpec(memory_space=pl.ANY),
            scratch_shapes=[pltpu.VMEM((MOE_TILE, D_MODEL), F32), pltpu.SemaphoreType.DMA(())],
        ),
        out_shape=jax.ShapeDtypeStruct(((n_tiles + 1) * MOE_TILE, D_MODEL), F32),
        compiler_params=pltpu.CompilerParams(dimension_semantics=("arbitrary",), vmem_limit_bytes=VMEM_LIMIT,
                                             disable_bounds_checks=True),
        name="moe_dispatch",
    )(zero_rows, n_used, idx, h2d)


def _expert_kernel(tile_e_ref, n_used_ref, x_ref, g_ref, w1_ref, w3_ref, w2_ref, y_ref, w1b_ref, w3b_ref, w2b_ref):
    i = pl.program_id(0)
    used = i < n_used_ref[0]

    @pl.when(jnp.logical_not(used))
    def _():
        y_ref[...] = jnp.zeros_like(y_ref)

    @pl.when(used & ((i == 0) | (tile_e_ref[i] != tile_e_ref[jnp.maximum(i - 1, 0)])))
    def _():
        w1b_ref[...] = w1_ref[...].astype(BF16)
        w3b_ref[...] = w3_ref[...].astype(BF16)
        w2b_ref[...] = w2_ref[...].astype(BF16)

    @pl.when(used)
    def _():
        x = x_ref[...]
        ms = jnp.mean(x * x, axis=-1, keepdims=True)
        xn = (x * lax.rsqrt(ms + NORM_EPS) * g_ref[...]).astype(BF16)
        a = jnp.dot(xn, w1b_ref[...], preferred_element_type=F32)
        b = jnp.dot(xn, w3b_ref[...], preferred_element_type=F32)
        hmid = (a * jax.nn.sigmoid(a) * b).astype(BF16)
        y_ref[...] = jnp.dot(hmid, w2b_ref[...], preferred_element_type=F32)


def _experts(xs, g, w1, w3, w2, tile_e, n_used, layer, n_tiles):
    row_map = lambda i, te, nu: (i, 0)
    in_row_map = lambda i, te, nu: (jnp.minimum(i, nu[0] - 1), 0)
    w_map = lambda i, te, nu: (layer, te[i], 0, 0)
    return pl.pallas_call(
        _expert_kernel,
        grid_spec=pltpu.PrefetchScalarGridSpec(
            num_scalar_prefetch=2,
            grid=(n_tiles,),
            in_specs=[pl.BlockSpec((MOE_TILE, D_MODEL), in_row_map),
                      pl.BlockSpec((None, 1, D_MODEL), lambda i, te, nu: (layer, 0, 0)),
                      pl.BlockSpec((None, None, D_MODEL, D_FF_EXPERT), w_map),
                      pl.BlockSpec((None, None, D_MODEL, D_FF_EXPERT), w_map),
                      pl.BlockSpec((None, None, D_FF_EXPERT, D_MODEL), w_map)],
            out_specs=pl.BlockSpec((MOE_TILE, D_MODEL), row_map),
            scratch_shapes=[pltpu.VMEM((D_MODEL, D_FF_EXPERT), BF16), pltpu.VMEM((D_MODEL, D_FF_EXPERT), BF16),
                            pltpu.VMEM((D_FF_EXPERT, D_MODEL), BF16)],
        ),
        out_shape=jax.ShapeDtypeStruct((n_tiles * MOE_TILE, D_MODEL), F32),
        compiler_params=_cparams("arbitrary"),
        name="moe_experts",
    )(tile_e, n_used, xs, g, w1, w3, w2)


def _combine_kernel(idx_ref, ys_hbm, h_ref, wts_ref, gf_ref, o_ref, buf_ref, sem, *, final_norm):
    tb = h_ref.shape[0]

    def row(t, carry):
        pltpu.make_async_copy(ys_hbm.at[pl.ds(idx_ref[0, 0, t], 1)], buf_ref.at[0, pl.ds(t, 1)], sem).start()
        pltpu.make_async_copy(ys_hbm.at[pl.ds(idx_ref[0, 0, tb + t], 1)], buf_ref.at[1, pl.ds(t, 1)], sem).start()
        return carry

    lax.fori_loop(0, tb, row, 0, unroll=True)
    for k in range(2):
        pltpu.make_async_copy(ys_hbm.at[pl.ds(0, tb)], buf_ref.at[k], sem).wait()
    reps = D_MODEL // LANES
    w = wts_ref[...]
    w_top1 = jnp.concatenate([w[:, :LANES]] * reps, axis=1)
    w_top2 = jnp.concatenate([w[:, LANES:]] * reps, axis=1)
    out = h_ref[...] + w_top1 * buf_ref[0] + w_top2 * buf_ref[1]
    if final_norm:
        ms = jnp.mean(out * out, axis=-1, keepdims=True)
        out = out * lax.rsqrt(ms + NORM_EPS) * gf_ref[...]
    o_ref[...] = out


def _combine(h2d, ys, idx, wts, g_final, final_norm, batch):
    n = h2d.shape[0]
    tb = MOE_TOKENS_PER_STEP
    steps_per_seq = n // batch // tb
    return pl.pallas_call(
        functools.partial(_combine_kernel, final_norm=final_norm),
        grid=(n // tb,),
        in_specs=[pl.BlockSpec((1, 1, 2 * tb), lambda i: (i, 0, 0), memory_space=pltpu.SMEM),
                  pl.BlockSpec(memory_space=pl.ANY),
                  pl.BlockSpec((tb, D_MODEL), lambda i: (i, 0)),
                  pl.BlockSpec((tb, 2 * LANES), lambda i: (i, 0)),
                  pl.BlockSpec((1, D_MODEL), lambda i: (0, 0))],
        out_specs=pl.BlockSpec((None, tb, D_MODEL), lambda i: (i // steps_per_seq, i % steps_per_seq, 0)),
        out_shape=jax.ShapeDtypeStruct((batch, n // batch, D_MODEL), F32),
        scratch_shapes=[pltpu.VMEM((2, tb, D_MODEL), F32), pltpu.SemaphoreType.DMA(())],
        compiler_params=pltpu.CompilerParams(dimension_semantics=("arbitrary",), vmem_limit_bytes=VMEM_LIMIT,
                                             disable_bounds_checks=True),
        name="moe_combine",
    )(idx, ys, h2d, wts, g_final)


def _moe(h2d, g, w_router, b_router, w1, w3, w2, layer, g_final, final_norm, batch):
    n = h2d.shape[0]
    route, wts, counts = _router(h2d, g, w_router, b_router, layer)
    idx, tile_e, n_used, zero_rows, n_tiles = _dispatch_tables(route, counts, n)
    xs = _dispatch(h2d, idx, zero_rows, n_used, n_tiles)
    ys = _experts(xs, g, w1, w3, w2, tile_e, n_used, layer, n_tiles)
    return _combine(h2d, ys, idx, wts, g_final, final_norm, batch)


def _router_params(wg, bg, we, be):
    n_l = wg.shape[0]
    pad = ROUTER_ROWS - N_GROUPS - N_EXPERTS
    w = jnp.concatenate([wg, we, jnp.zeros((n_l, D_MODEL, pad), F32)], axis=-1).astype(F32).transpose(0, 2, 1)
    b = jnp.concatenate([bg, be, jnp.zeros((n_l, pad), F32)], axis=-1).astype(F32)
    return w, jnp.broadcast_to(b[:, :, None], (n_l, ROUTER_ROWS, LANES))


def kernel(x, w_in, na_rpb, lam_q1, lam_k1, lam_q2, lam_k2, diff_subln, w_pa, w_pb, w_pc, w_o, norm_mix, norm_ffn,
           router_group_w, router_group_b, router_expert_w, router_expert_b, w1, w3, w2, norm_final):
    B, T, D = x.shape
    depth = w_in.shape[0]
    rows = T // GRID_W
    wr = min(NA_WIN_R, rows)
    rope_tab = _rope_table(T)
    row3 = lambda a: a[:, None, :]
    w_in_bf = _permute_cols(w_in)
    bias_tab = _na_bias_table(na_rpb, wr)
    g_mix, g_ffn = row3(norm_mix), row3(norm_ffn)
    lams = [row3(a) for a in (lam_q1, lam_k1, lam_q2, lam_k2)]
    subln = row3(diff_subln)
    w_router, b_router = _router_params(router_group_w, router_group_b, router_expert_w, router_expert_b)
    h = x.reshape(B * T, D)
    for l in range(depth):
        lam_init = 0.8 - 0.6 * math.exp(-0.3 * l)
        proj, *cgs = _inproj(h, g_mix, w_in_bf, rope_tab, l, B, T)
        ya = _na_attention(proj, bias_tab, l, B, T)
        yb = _diff_attention(proj, *lams, subln, l, lam_init, B, T)
        yc = _dilated_branch(cgs)
        h = _merge(h, proj, ya, yb, yc, w_pa, w_pb, w_pc, w_o, l)
        out = _moe(h, g_ffn, w_router, b_router, w1, w3, w2, l, norm_final[None, :], l == depth - 1, B)
        h = out.reshape(B * T, D)
    return out
```

```python
import functools
import math

import jax
import jax.numpy as jnp
import numpy as np
from jax import lax
from jax.experimental import pallas as pl
from jax.experimental.pallas import tpu as pltpu

F32 = jnp.float32
BF16 = jnp.bfloat16

D_MODEL = 1024
HEAD_DIM = 64
ROPE_DIM = 16
ROPE_THETA = 500000.0
GRID_W = 64
NA_HEADS = 8
NA_WIN_R = 8
NA_WIN_C = 16
DIFF_HEADS = 4
DIL_PATTERNS = ((128, 1), (512, 4), (2048, 16))
N_GROUPS = 4
EXPERTS_PER_GROUP = 4
N_EXPERTS = 16
D_FF_EXPERT = D_MODEL // 2
NORM_EPS = 1e-6
SUBLN_EPS = 1e-5
NEG_INF = -1e30

LANES = 128
MXU_N = 256
VMEM_LIMIT = 56 * 1024 * 1024

A_W, B_W, C_W = 512, 512, 768
N_STEPS = len(DIL_PATTERNS)
GROUP_W = C_W // N_STEPS
_MAIN_ORDER = ("qb", "kb", "va", "qa", "ga", "gb", "gc", "ka", "vb")
_REF_ORDER = ("qa", "ka", "va", "qb", "kb", "vb", "qc", "kc", "vc", "ga", "gb", "gc")
_WIDTH = dict(qa=A_W, ka=A_W, va=A_W, qb=B_W, kb=B_W, vb=B_W, qc=C_W, kc=C_W, vc=C_W,
              ga=D_MODEL, gb=D_MODEL, gc=D_MODEL)


def _offsets(order):
    off, out = 0, {}
    for name in order:
        out[name] = off
        off += _WIDTH[name]
    return out, off


_OFF, MAIN_W = _offsets(_MAIN_ORDER)
_REF_OFF, IN_W = _offsets(_REF_ORDER)
OFF_QA, OFF_KA, OFF_VA = _OFF["qa"], _OFF["ka"], _OFF["va"]
OFF_QB, OFF_KB, OFF_VB = _OFF["qb"], _OFF["kb"], _OFF["vb"]
OFF_GATE = _OFF["ga"]
MAIN_STEP_W = MAIN_W // N_STEPS
STEP_W = MAIN_STEP_W + 3 * GROUP_W
MAIN_ROPE_W = 2 * B_W
QB_SCALE = HEAD_DIM ** -0.5 * math.log2(math.e)
assert OFF_GATE % D_MODEL == 0 and OFF_QB == 0 and OFF_KB == B_W and MAIN_ROPE_W <= MAIN_STEP_W
assert MAIN_W % N_STEPS == 0 and MAIN_STEP_W % MXU_N == 0 and N_STEPS * STEP_W == IN_W


def _permute_cols(w):
    def piece(name, lo, hi):
        p = w[..., _REF_OFF[name] + lo:_REF_OFF[name] + hi]
        return (p * QB_SCALE if name == "qb" else p).astype(BF16)

    parts = []
    for s in range(N_STEPS):
        for n in _MAIN_ORDER:
            lo = max(_OFF[n], s * MAIN_STEP_W) - _OFF[n]
            hi = min(_OFF[n] + _WIDTH[n], (s + 1) * MAIN_STEP_W) - _OFF[n]
            if lo < hi:
                parts.append(piece(n, lo, hi))
        for n in ("qc", "kc", "vc"):
            parts.append(piece(n, s * GROUP_W, (s + 1) * GROUP_W))
    return jnp.concatenate(parts, axis=-1)


def _cparams(*sem):
    return pltpu.CompilerParams(dimension_semantics=sem, vmem_limit_bytes=VMEM_LIMIT)


def _inproj_kernel(x_ref, g_ref, w_ref, rope_ref, main_ref, c0_ref, c1_ref, c2_ref, xn_ref, y_ref, *, tm):
    j = pl.program_id(1)
    c_refs = (c0_ref, c1_ref, c2_ref)
    reps = MXU_N // LANES
    half = ROPE_DIM // 2
    assert GROUP_W == MXU_N

    def rope(y):
        cos = jnp.concatenate([rope_ref[0]] * reps, axis=1)
        s_up = jnp.concatenate([rope_ref[1]] * reps, axis=1)
        s_dn = jnp.concatenate([rope_ref[2]] * reps, axis=1)
        return y * cos + pltpu.roll(y, MXU_N - half, 1) * s_up + pltpu.roll(y, half, 1) * s_dn

    def step(s):
        xn = xn_ref[...]
        for c in range(MAIN_STEP_W // MXU_N):
            sl = slice(c * MXU_N, (c + 1) * MXU_N)
            y = jnp.dot(xn, w_ref[:, sl], preferred_element_type=F32)
            if s == 0 and c < MAIN_ROPE_W // MXU_N:
                y = rope(y)
            main_ref[:, sl] = y.astype(main_ref.dtype)
        dil = DIL_PATTERNS[s][1]
        for c in range(3):
            wsl = slice(MAIN_STEP_W + c * GROUP_W, MAIN_STEP_W + (c + 1) * GROUP_W)
            osl = slice(c * GROUP_W, (c + 1) * GROUP_W)
            y = jnp.dot(xn, w_ref[:, wsl], preferred_element_type=F32)
            if c < 2:
                y = rope(y)
            if dil == 1:
                c_refs[s][0, :, osl] = y.astype(BF16)
            else:
                for hb in range(reps):
                    y_ref[c, hb] = y[:, hb * LANES:(hb + 1) * LANES]
                for p in range(dil):
                    for hb in range(reps):
                        c_refs[s][p, :, c * GROUP_W + hb * LANES:c * GROUP_W + (hb + 1) * LANES] = (
                            y_ref[c, hb, pl.ds(p, tm // dil, stride=dil), :].astype(BF16))

    @pl.when(j == 0)
    def _():
        x = x_ref[...]
        ms = jnp.mean(x * x, axis=-1, keepdims=True)
        xn_ref[...] = (x * lax.rsqrt(ms + NORM_EPS) * g_ref[...]).astype(BF16)
        step(0)

    for s in range(1, N_STEPS):
        pl.when(j == s)(functools.partial(step, s))


def _inproj(h2d, g, w_bf, rope_tab, layer, B, T, tm=1024):
    n = h2d.shape[0]
    tpb = T // tm
    c_specs, c_shapes = [], []
    for _, dil in DIL_PATTERNS:
        assert tm % dil == 0
        c_specs.append(pl.BlockSpec((None, dil, tm // dil, 3 * GROUP_W), lambda i, j: (i // tpb, 0, i % tpb, 0)))
        c_shapes.append(jax.ShapeDtypeStruct((B, dil, T // dil, 3 * GROUP_W), BF16))
    return pl.pallas_call(
        functools.partial(_inproj_kernel, tm=tm),
        grid=(n // tm, N_STEPS),
        in_specs=[
            pl.BlockSpec((tm, D_MODEL), lambda i, j: (i, 0)),
            pl.BlockSpec((None, 1, D_MODEL), lambda i, j: (layer, 0, 0)),
            pl.BlockSpec((None, D_MODEL, STEP_W), lambda i, j: (layer, 0, j)),
            pl.BlockSpec((3, tm, LANES), lambda i, j: (0, i % tpb, 0)),
        ],
        out_specs=[pl.BlockSpec((tm, MAIN_STEP_W), lambda i, j: (i, j))] + c_specs,
        out_shape=[jax.ShapeDtypeStruct((n, MAIN_W), BF16)] + c_shapes,
        scratch_shapes=[pltpu.VMEM((tm, D_MODEL), BF16), pltpu.VMEM((3, MXU_N // LANES, tm, LANES), F32)],
        compiler_params=_cparams("parallel", "arbitrary"),
        name="inproj",
    )(h2d, g, w_bf, rope_tab)


def _rope_table(T):
    half = ROPE_DIM // 2
    inv = 1.0 / (ROPE_THETA ** (jnp.arange(0, ROPE_DIM, 2, dtype=F32) / ROPE_DIM))
    ang = jnp.arange(T, dtype=F32)[:, None] * inv[None, :]
    cos, sin = jnp.cos(ang), jnp.sin(ang)
    zeros = jnp.zeros((T, HEAD_DIM - ROPE_DIM), F32)
    z8 = jnp.zeros((T, half), F32)
    c64 = jnp.concatenate([cos, cos, zeros + 1.0], axis=1)
    up64 = jnp.concatenate([-sin, z8, zeros], axis=1)
    dn64 = jnp.concatenate([z8, sin, zeros], axis=1)
    reps = LANES // HEAD_DIM
    return jnp.stack([jnp.tile(c64, (1, reps)), jnp.tile(up64, (1, reps)), jnp.tile(dn64, (1, reps))])


NA_ROWS_PER_ITER = 32


def _na_kernel(q_ref, k_ref, v_ref, b_ref, o_ref, *, rows, wr):
    lane = lax.broadcasted_iota(jnp.int32, (GRID_W, LANES), 1)
    lo = lane < HEAD_DIM
    scale = HEAD_DIM ** -0.5

    def row_group(gi, carry):
        scores, windows = [], []
        for u in range(NA_ROWS_PER_ITER):
            r = gi * NA_ROWS_PER_ITER + u
            r0 = jnp.clip(r - wr // 2, 0, rows - wr)
            d0 = r0 - r + NA_WIN_R - 1 - (NA_WIN_R - wr)
            q = q_ref[pl.ds(pl.multiple_of(r * GRID_W, GRID_W), GRID_W), :] * scale
            ks = pl.multiple_of(r0 * GRID_W, GRID_W)
            kw = k_ref[pl.ds(ks, wr * GRID_W), :]
            windows.append(ks)
            for hh in range(2):
                qh = jnp.where(lo if hh == 0 else jnp.logical_not(lo), q, jnp.zeros_like(q))
                s = lax.dot_general(qh, kw, (((1,), (1,)), ((), ())), preferred_element_type=F32)
                bias = jnp.concatenate([b_ref[hh, d0 + 2 * i] for i in range(wr // 2)], axis=1)
                scores.append(s + bias)
        probs, sums = [], []
        for s in scores:
            p = jnp.exp(s - jnp.max(s, axis=-1, keepdims=True))
            sums.append(jnp.sum(p, axis=-1, keepdims=True))
            probs.append(p.astype(BF16))
        for u in range(NA_ROWS_PER_ITER):
            r = gi * NA_ROWS_PER_ITER + u
            vw = v_ref[pl.ds(windows[u], wr * GRID_W), :]
            outs = [jnp.dot(probs[2 * u + hh], vw, preferred_element_type=F32) / sums[2 * u + hh] for hh in range(2)]
            o = jnp.where(lo, outs[0], outs[1])
            o_ref[pl.ds(pl.multiple_of(r * GRID_W, GRID_W), GRID_W), :] = o.astype(o_ref.dtype)
        return carry

    lax.fori_loop(0, rows // NA_ROWS_PER_ITER, row_group, 0)


def _na_bias_table(rpb, wr):
    qc = np.arange(GRID_W)[:, None]
    kc = np.arange(GRID_W)[None, :]
    c0 = np.clip(qc - NA_WIN_C // 2, 0, GRID_W - NA_WIN_C)
    ok = (kc >= c0) & (kc < c0 + NA_WIN_C)
    dc = np.clip(kc - qc + NA_WIN_C - 1, 0, 2 * NA_WIN_C - 2)
    onehot = (np.arange(2 * NA_WIN_C - 1)[:, None, None] == dc[None]).astype(np.float32)
    b = jnp.einsum("lhrd,dqk->lhrqk", rpb.astype(F32), onehot, precision=lax.Precision.HIGHEST)
    b = jnp.where(jnp.asarray(ok), b, NEG_INF)
    off = NA_WIN_R - wr
    n_pairs = 2 * wr - 2
    return jnp.concatenate([b[:, :, off:off + n_pairs], b[:, :, off + 1:off + 1 + n_pairs]], axis=-1)


def _na_attention(proj, bias_tab, layer, B, T):
    rows = T // GRID_W
    wr = min(NA_WIN_R, rows)
    cq, ck, cv = OFF_QA // LANES, OFF_KA // LANES, OFF_VA // LANES
    assert wr % 2 == 0 and 2 * GRID_W == LANES and rows % NA_ROWS_PER_ITER == 0
    return pl.pallas_call(
        functools.partial(_na_kernel, rows=rows, wr=wr),
        grid=(B, NA_HEADS // 2),
        in_specs=[
            pl.BlockSpec((T, LANES), lambda b, h: (b, cq + h)),
            pl.BlockSpec((T, LANES), lambda b, h: (b, ck + h)),
            pl.BlockSpec((T, LANES), lambda b, h: (b, cv + h)),
            pl.BlockSpec((None, 2, 2 * wr - 2, GRID_W, 2 * GRID_W), lambda b, h: (layer, h, 0, 0, 0)),
        ],
        out_specs=pl.BlockSpec((T, LANES), lambda b, h: (b, h)),
        out_shape=jax.ShapeDtypeStruct((B * T, A_W), BF16),
        compiler_params=_cparams("parallel", "arbitrary"),
        name="na_attn",
    )(proj, proj, proj, bias_tab)


DIFF_KEY_CHUNK = 512


def _diff_kernel(lq1_ref, lk1_ref, lq2_ref, lk2_ref, q_ref, k_ref, v_ref, g_ref, o_ref, vt_ref, *, lam_init, tq):
    lam = (jnp.exp(jnp.sum(lq1_ref[...] * lk1_ref[...], keepdims=True))
           - jnp.exp(jnp.sum(lq2_ref[...] * lk2_ref[...], keepdims=True)) + lam_init)
    T = k_ref.shape[0]
    ck = DIFF_KEY_CHUNK
    n_chunks = T // ck
    lane = lax.broadcasted_iota(jnp.int32, (tq, LANES), 1)
    lo = lane < HEAD_DIM
    nt = (((1,), (1,)), ((), ()))
    vt_ref[...] = v_ref[...].T

    def q_block(i, carry):
        rows = pl.ds(pl.multiple_of(i * tq, tq), tq)
        q = q_ref[rows, :]
        zero = jnp.zeros_like(q)
        qs = (jnp.where(lo, q, zero), jnp.where(lo, zero, q))

        def scores(c):
            kc = k_ref[c * ck:(c + 1) * ck, :]
            return [lax.dot_general(kc, qm, nt, preferred_element_type=F32) for qm in qs]

        m = [jnp.full((1, tq), NEG_INF, F32)] * 2
        l = [jnp.zeros((1, tq), F32)] * 2
        acc = [jnp.zeros((LANES, tq), F32)] * 2
        s_next = scores(0)
        for c in range(n_chunks):
            s_cur = s_next
            if c + 1 < n_chunks:
                s_next = scores(c + 1)
            vt = vt_ref[:, c * ck:(c + 1) * ck]
            for j in range(2):
                m_new = jnp.maximum(m[j], jnp.max(s_cur[j], axis=0, keepdims=True))
                alpha = jnp.exp2(m[j] - m_new)
                p = jnp.exp2(s_cur[j] - m_new)
                l[j] = alpha * l[j] + jnp.sum(p, axis=0, keepdims=True)
                acc[j] = alpha * acc[j] + jnp.dot(vt, p.astype(BF16), preferred_element_type=F32)
                m[j] = m_new
        ot = acc[0] / l[0] - lam * (acc[1] / l[1])
        o = ot.T
        ms = jnp.mean(o * o, axis=-1, keepdims=True)
        o = o * lax.rsqrt(ms + SUBLN_EPS) * g_ref[...] * (1.0 - lam_init)
        o_ref[rows, :] = o.astype(o_ref.dtype)
        return carry

    lax.fori_loop(0, q_ref.shape[0] // tq, q_block, 0, unroll=2)


def _diff_attention(proj, lq1, lk1, lq2, lk2, subln_g, layer, lam_init, B, T, tq=512):
    cq, ck, cv = OFF_QB // LANES, OFF_KB // LANES, OFF_VB // LANES
    vec = pl.BlockSpec((None, 1, HEAD_DIM), lambda b, h: (layer, 0, 0))
    return pl.pallas_call(
        functools.partial(_diff_kernel, lam_init=lam_init, tq=tq),
        grid=(B, DIFF_HEADS),
        in_specs=[
            vec, vec, vec, vec,
            pl.BlockSpec((T, LANES), lambda b, h: (b, cq + h)),
            pl.BlockSpec((T, LANES), lambda b, h: (b, ck + h)),
            pl.BlockSpec((T, LANES), lambda b, h: (b, cv + h)),
            pl.BlockSpec((None, 1, LANES), lambda b, h: (layer, 0, 0)),
        ],
        out_specs=pl.BlockSpec((T, LANES), lambda b, h: (b, h)),
        out_shape=jax.ShapeDtypeStruct((B * T, B_W), BF16),
        scratch_shapes=[pltpu.VMEM((LANES, T), BF16)],
        compiler_params=_cparams("parallel", "arbitrary"),
        name="diff_attn",
    )(lq1, lk1, lq2, lk2, proj, proj, proj, subln_g)


DIL_Q = 128


DIL_BLOCKS_PER_ITER = 4


def _dil_kernel(q_ref, k_ref, v_ref, o_ref, lse_ref, *, L, dil, radius):
    kw_len = DIL_Q + 2 * radius
    lane = lax.broadcasted_iota(jnp.int32, (DIL_Q, LANES), 1)
    lo = lane < HEAD_DIM
    rel = (lax.broadcasted_iota(jnp.int32, (DIL_Q, kw_len), 1)
           - lax.broadcasted_iota(jnp.int32, (DIL_Q, kw_len), 0))
    scale = HEAD_DIM ** -0.5
    nt = (((1,), (1,)), ((), ()))

    blocks_per_phase = L // DIL_Q

    def blocks(gi, carry):
        scores, starts, maxes = [], [], []
        for u in range(DIL_BLOCKS_PER_ITER):
            item = gi * DIL_BLOCKS_PER_ITER + u
            p = item // blocks_per_phase
            l0 = pl.multiple_of((item % blocks_per_phase) * DIL_Q, DIL_Q)
            ks = pl.multiple_of(jnp.clip(l0 - radius, 0, L - kw_len), radius)
            q = q_ref[p, pl.ds(l0, DIL_Q), :] * scale
            kw = k_ref[p, pl.ds(ks, kw_len), :]
            ok = jnp.abs(rel + (ks - l0)) <= radius
            starts.append((p, l0, ks))
            for hh in range(2):
                qh = jnp.where(lo if hh == 0 else jnp.logical_not(lo), q, jnp.zeros_like(q))
                s = lax.dot_general(qh, kw, nt, preferred_element_type=F32)
                scores.append(jnp.where(ok, s, NEG_INF))
        probs, sums = [], []
        for s in scores:
            m = jnp.max(s, axis=-1, keepdims=True)
            e = jnp.exp(s - m)
            maxes.append(m)
            sums.append(jnp.sum(e, axis=-1, keepdims=True))
            probs.append(e.astype(BF16))
        for u in range(DIL_BLOCKS_PER_ITER):
            p, l0, ks = starts[u]
            vw = v_ref[p, pl.ds(ks, kw_len), :]
            outs = [jnp.dot(probs[2 * u + hh], vw, preferred_element_type=F32) / sums[2 * u + hh] for hh in range(2)]
            lses = [maxes[2 * u + hh] + jnp.log(sums[2 * u + hh]) for hh in range(2)]
            o = jnp.where(lo, outs[0], outs[1])
            lse = jnp.where(lo, lses[0], lses[1])
            if dil == 1:
                o_ref[pl.ds(l0, DIL_Q), :] = o
                lse_ref[pl.ds(l0, DIL_Q), :] = lse
            else:
                o_ref[pl.ds(l0 * dil + p, DIL_Q, stride=dil), :] = o
                lse_ref[pl.ds(l0 * dil + p, DIL_Q, stride=dil), :] = lse
        return carry

    lax.fori_loop(0, dil * blocks_per_phase // DIL_BLOCKS_PER_ITER, blocks, 0)


def _dil_merge_kernel(*refs, geoms):
    n = len(geoms)
    out_ref = refs[3 * n]
    scratch = refs[3 * n + 1:]
    for g, (L, dil, radius) in enumerate(geoms):
        _dil_kernel(refs[3 * g], refs[3 * g + 1], refs[3 * g + 2], scratch[2 * g], scratch[2 * g + 1],
                    L=L, dil=dil, radius=radius)
    lses = [scratch[2 * g + 1][...] for g in range(n)]
    m = functools.reduce(jnp.maximum, lses)
    es = [jnp.exp(l - m) for l in lses]
    num = sum(scratch[2 * g][...] * es[g] for g in range(n))
    out_ref[...] = (num / sum(es)).astype(out_ref.dtype)


def _dilated_branch(cgs):
    B = cgs[0].shape[0]
    hp = GROUP_W // LANES
    geoms, in_specs, operands = [], [], []
    for cg, (window, dil) in zip(cgs, DIL_PATTERNS):
        _, d, L, _ = cg.shape
        radius = window // (2 * dil)
        assert d == dil and L >= DIL_Q + 2 * radius and L % DIL_Q == 0
        assert (dil * L // DIL_Q) % DIL_BLOCKS_PER_ITER == 0
        geoms.append((L, dil, radius))
        for c in range(3):
            in_specs.append(pl.BlockSpec((None, dil, L, LANES), lambda b, h, c=c: (b, 0, 0, c * hp + h)))
            operands.append(cg)
    T = geoms[0][0] * geoms[0][1]
    return pl.pallas_call(
        functools.partial(_dil_merge_kernel, geoms=tuple(geoms)),
        grid=(B, hp),
        in_specs=in_specs,
        out_specs=pl.BlockSpec((T, LANES), lambda b, h: (b, h)),
        out_shape=jax.ShapeDtypeStruct((B * T, GROUP_W), BF16),
        scratch_shapes=[pltpu.VMEM((T, LANES), F32)] * (2 * len(geoms)),
        compiler_params=_cparams("parallel", "arbitrary"),
        name="dil_attn",
    )(*operands)


def _merge_kernel(h_ref, ga_ref, gb_ref, gc_ref, ya_ref, yb_ref, yc_ref,
                  wpa32_ref, wpb32_ref, wpc32_ref, wo32_ref, out_ref, wpa_ref, wpb_ref, wpc_ref, wo_ref):
    @pl.when(pl.program_id(0) == 0)
    def _():
        for src, dst in ((wpa32_ref, wpa_ref), (wpb32_ref, wpb_ref), (wpc32_ref, wpc_ref), (wo32_ref, wo_ref)):
            dst[...] = src[...].astype(BF16)

    merged = jax.nn.sigmoid(ga_ref[...].astype(F32)) * jnp.dot(ya_ref[...], wpa_ref[...], preferred_element_type=F32)
    merged += jax.nn.sigmoid(gb_ref[...].astype(F32)) * jnp.dot(yb_ref[...], wpb_ref[...], preferred_element_type=F32)
    merged += jax.nn.sigmoid(gc_ref[...].astype(F32)) * jnp.dot(yc_ref[...], wpc_ref[...], preferred_element_type=F32)
    out_ref[...] = h_ref[...] + jnp.dot(merged.astype(BF16), wo_ref[...], preferred_element_type=F32)


def _merge(h2d, proj, ya, yb, yc, wpa, wpb, wpc, wo, layer, tm=512):
    n = h2d.shape[0]
    gw = GROUP_W
    row = lambda w: pl.BlockSpec((tm, w), lambda i: (i, 0))
    full = lambda a: pl.BlockSpec((None,) + a.shape[1:], lambda i: (layer, 0, 0))
    gate = lambda c: pl.BlockSpec((tm, D_MODEL), lambda i: (i, OFF_GATE // D_MODEL + c))
    return pl.pallas_call(
        _merge_kernel,
        grid=(n // tm,),
        in_specs=[row(D_MODEL), gate(0), gate(1), gate(2), row(A_W), row(B_W),
                  row(gw),
                  full(wpa), full(wpb), full(wpc), full(wo)],
        out_specs=row(D_MODEL),
        out_shape=jax.ShapeDtypeStruct((n, D_MODEL), F32),
        scratch_shapes=[pltpu.VMEM(a.shape[1:], BF16) for a in (wpa, wpb, wpc, wo)],
        compiler_params=_cparams("arbitrary"),
        name="merge_outproj",
    )(h2d, proj, proj, proj, ya, yb, yc, wpa, wpb, wpc, wo)


MOE_TILE = 512
MOE_TOKENS_PER_STEP = 1024
ROUTE_ROWS = 8


ROUTER_ROWS = 32


def _router_kernel(h_ref, g_ref, wr_ref, br_ref, upper_ref, route_ref, wts_ref, cnt_ref, base_ref):
    i = pl.program_id(0)
    tm = h_ref.shape[0]
    lane_reps = tm // LANES
    row = lax.broadcasted_iota(jnp.int32, (ROUTER_ROWS, tm), 0)
    rowf = row.astype(F32)
    big = float(ROUTER_ROWS)
    nt = (((1,), (1,)), ((), ()))

    @pl.when(i == 0)
    def _():
        base_ref[...] = jnp.zeros_like(base_ref)

    x = h_ref[...]
    ms = jnp.mean(x * x, axis=-1, keepdims=True)
    xn = x * lax.rsqrt(ms + NORM_EPS) * g_ref[...]
    logits = lax.dot_general(wr_ref[...], xn, nt, preferred_element_type=F32, precision=lax.Precision.HIGHEST)
    logits = logits + jnp.concatenate([br_ref[...]] * lane_reps, axis=1)
    gl = jnp.where(row < N_GROUPS, logits, -jnp.inf)
    gmax = jnp.max(gl, axis=0, keepdims=True)
    g_sel = jnp.min(jnp.where(gl == gmax, rowf, big), axis=0, keepdims=True).astype(jnp.int32)
    g_gate = 1.0 / jnp.sum(jnp.exp(gl - gmax), axis=0, keepdims=True)
    eidx = row - N_GROUPS
    in_grp = (eidx >= g_sel * EXPERTS_PER_GROUP) & (eidx < (g_sel + 1) * EXPERTS_PER_GROUP)
    el = jnp.where(in_grp, logits, -jnp.inf)
    t1 = jnp.max(el, axis=0, keepdims=True)
    i1 = jnp.min(jnp.where(el == t1, rowf, big), axis=0, keepdims=True).astype(jnp.int32)
    el2 = jnp.where(row == i1, -jnp.inf, el)
    t2 = jnp.max(el2, axis=0, keepdims=True)
    i2 = jnp.min(jnp.where(el2 == t2, rowf, big), axis=0, keepdims=True).astype(jnp.int32)
    x2 = jnp.exp(t2 - t1)
    den = 1.0 + x2
    wts_ref[:, :LANES] = jnp.broadcast_to(g_gate / den, (LANES, tm)).T
    wts_ref[:, LANES:] = jnp.broadcast_to(g_gate * x2 / den, (LANES, tm)).T
    member = jnp.where(row == i1, 1.0, 0.0) + jnp.where(row == i2, 1.0, 0.0)
    earlier = jnp.dot(member.astype(BF16), upper_ref[...], preferred_element_type=F32)
    earlier = earlier + jnp.concatenate([base_ref[...]] * lane_reps, axis=1)
    rank1 = jnp.sum(jnp.where(row == i1, earlier, 0.0), axis=0, keepdims=True)
    rank2 = jnp.sum(jnp.where(row == i2, earlier, 0.0), axis=0, keepdims=True)
    base_ref[...] = base_ref[...] + jnp.sum(member, axis=1, keepdims=True)
    cnt_ref[...] = base_ref[...]
    out_row = lax.broadcasted_iota(jnp.int32, (ROUTE_ROWS, tm), 0)
    route_ref[...] = jnp.where(out_row == 0, (i1 - N_GROUPS).astype(F32),
                               jnp.where(out_row == 1, (i2 - N_GROUPS).astype(F32),
                                         jnp.where(out_row == 2, rank1, jnp.where(out_row == 3, rank2, 0.0))))


def _router(h2d, g, w_router, b_router, layer, tm=1024):
    n = h2d.shape[0]
    return pl.pallas_call(
        _router_kernel,
        grid=(n // tm,),
        in_specs=[
            pl.BlockSpec((tm, D_MODEL), lambda i: (i, 0)),
            pl.BlockSpec((None, 1, D_MODEL), lambda i: (layer, 0, 0)),
            pl.BlockSpec((None, ROUTER_ROWS, D_MODEL), lambda i: (layer, 0, 0)),
            pl.BlockSpec((None, ROUTER_ROWS, LANES), lambda i: (layer, 0, 0)),
            pl.BlockSpec((tm, tm), lambda i: (0, 0)),
        ],
        out_specs=[pl.BlockSpec((ROUTE_ROWS, tm), lambda i: (0, i)),
                   pl.BlockSpec((tm, 2 * LANES), lambda i: (i, 0)),
                   pl.BlockSpec((ROUTER_ROWS, LANES), lambda i: (0, 0))],
        out_shape=[jax.ShapeDtypeStruct((ROUTE_ROWS, n), F32), jax.ShapeDtypeStruct((n, 2 * LANES), F32),
                   jax.ShapeDtypeStruct((ROUTER_ROWS, LANES), F32)],
        scratch_shapes=[pltpu.VMEM((ROUTER_ROWS, LANES), F32)],
        compiler_params=_cparams("arbitrary"),
        name="moe_router",
    )(h2d, g, w_router, b_router, jnp.triu(jnp.ones((tm, tm), BF16), k=1))


def _dispatch_tables(route, counts, n):
    cnt = counts[N_GROUPS:N_GROUPS + N_EXPERTS, 0].astype(jnp.int32)
    padded = (cnt + MOE_TILE - 1) // MOE_TILE * MOE_TILE
    ends = jnp.cumsum(padded)
    starts = ends - padded
    eids = jnp.arange(N_EXPERTS, dtype=jnp.int32)
    pos = []
    for k in range(2):
        e = route[k].astype(jnp.int32)
        start_e = jnp.sum(jnp.where(e[:, None] == eids[None, :], starts[None, :], 0), axis=1)
        pos.append(start_e + route[2 + k].astype(jnp.int32))
    tb = MOE_TOKENS_PER_STEP
    idx = jnp.concatenate([pos[0].reshape(n // tb, 1, tb), pos[1].reshape(n // tb, 1, tb)], axis=-1)
    n_tiles = 2 * n // MOE_TILE + N_EXPERTS
    tile_ids = jnp.arange(n_tiles, dtype=jnp.int32)
    n_used = ends[-1] // MOE_TILE
    tile_e = jnp.sum((tile_ids[:, None] * MOE_TILE >= ends[None, :]).astype(jnp.int32), axis=1)
    last_e = jnp.max(jnp.where(tile_ids < n_used, tile_e, 0))
    tile_e = jnp.where(tile_ids < n_used, tile_e, last_e).astype(jnp.int32)
    zero_rows = jnp.where(padded > 0, ends - MOE_TILE, n_tiles * MOE_TILE).astype(jnp.int32)
    return idx, tile_e, n_used.reshape(1).astype(jnp.int32), zero_rows, n_tiles


def _dispatch_kernel(zero_rows_ref, n_used_ref, idx_ref, h_ref, xs_hbm, zeros_ref, sem):
    i = pl.program_id(0)
    tb = MOE_TOKENS_PER_STEP
    n_tiles = xs_hbm.shape[0] // MOE_TILE - 1

    def zero_tile(row0):
        return pltpu.make_async_copy(zeros_ref, xs_hbm.at[pl.ds(pl.multiple_of(row0, MOE_TILE), MOE_TILE)], sem)

    @pl.when(i == 0)
    def _():
        zeros_ref[...] = jnp.zeros_like(zeros_ref)
        fills = [zero_tile(zero_rows_ref[e]) for e in range(N_EXPERTS)]
        for c in fills:
            c.start()
        for c in fills:
            c.wait()

        def tail(j, carry):
            tile = n_used_ref[0] + j

            @pl.when(tile <= n_tiles)
            def _():
                c = zero_tile(tile * MOE_TILE)
                c.start()
                c.wait()
            return carry

        lax.fori_loop(0, N_EXPERTS + 1, tail, 0)

    def row(t, carry):
        src = h_ref.at[pl.ds(t, 1)]
        pltpu.make_async_copy(src, xs_hbm.at[pl.ds(idx_ref[0, 0, t], 1)], sem).start()
        pltpu.make_async_copy(src, xs_hbm.at[pl.ds(idx_ref[0, 0, tb + t], 1)], sem).start()
        return carry

    lax.fori_loop(0, tb, row, 0, unroll=True)
    for _ in range(2):
        pltpu.make_async_copy(h_ref, xs_hbm.at[pl.ds(0, tb)], sem).wait()


def _dispatch(h2d, idx, zero_rows, n_used, n_tiles):
    n = h2d.shape[0]
    tb = MOE_TOKENS_PER_STEP
    assert 2 * n // MOE_TILE + N_EXPERTS == n_tiles
    return pl.pallas_call(
        _dispatch_kernel,
        grid_spec=pltpu.PrefetchScalarGridSpec(
            num_scalar_prefetch=2,
            grid=(n // tb,),
            in_specs=[pl.BlockSpec((1, 1, 2 * tb), lambda i, z, u: (i, 0, 0), memory_space=pltpu.SMEM),
                      pl.BlockSpec((tb, D_MODEL), lambda i, z, u: (i, 0))],
            out_specs=pl.BlockSpec(memory_space=pl.ANY),
            scratch_shapes=[pltpu.VMEM((MOE_TILE, D_MODEL), F32), pltpu.SemaphoreType.DMA(())],
        ),
        out_shape=jax.ShapeDtypeStruct(((n_tiles + 1) * MOE_TILE, D_MODEL), F32),
        compiler_params=pltpu.CompilerParams(dimension_semantics=("arbitrary",), vmem_limit_bytes=VMEM_LIMIT,
                                             disable_bounds_checks=True),
        name="moe_dispatch",
    )(zero_rows, n_used, idx, h2d)


def _expert_kernel(tile_e_ref, n_used_ref, x_ref, g_ref, w1_ref, w3_ref, w2_ref, y_ref, w1b_ref, w3b_ref, w2b_ref):
    i = pl.program_id(0)
    used = i < n_used_ref[0]

    @pl.when(jnp.logical_not(used))
    def _():
        y_ref[...] = jnp.zeros_like(y_ref)

    @pl.when(used & ((i == 0) | (tile_e_ref[i] != tile_e_ref[jnp.maximum(i - 1, 0)])))
    def _():
        w1b_ref[...] = w1_ref[...].astype(BF16)
        w3b_ref[...] = w3_ref[...].astype(BF16)
        w2b_ref[...] = w2_ref[...].astype(BF16)

    @pl.when(used)
    def _():
        x = x_ref[...]
        ms = jnp.mean(x * x, axis=-1, keepdims=True)
        xn = (x * lax.rsqrt(ms + NORM_EPS) * g_ref[...]).astype(BF16)
        a = jnp.dot(xn, w1b_ref[...], preferred_element_type=F32)
        b = jnp.dot(xn, w3b_ref[...], preferred_element_type=F32)
        hmid = (a * jax.nn.sigmoid(a) * b).astype(BF16)
        y_ref[...] = jnp.dot(hmid, w2b_ref[...], preferred_element_type=F32)


def _experts(xs, g, w1, w3, w2, tile_e, n_used, layer, n_tiles):
    row_map = lambda i, te, nu: (i, 0)
    in_row_map = lambda i, te, nu: (jnp.minimum(i, nu[0] - 1), 0)
    w_map = lambda i, te, nu: (layer, te[i], 0, 0)
    return pl.pallas_call(
        _expert_kernel,
        grid_spec=pltpu.PrefetchScalarGridSpec(
            num_scalar_prefetch=2,
            grid=(n_tiles,),
            in_specs=[pl.BlockSpec((MOE_TILE, D_MODEL), in_row_map),
                      pl.BlockSpec((None, 1, D_MODEL), lambda i, te, nu: (layer, 0, 0)),
                      pl.BlockSpec((None, None, D_MODEL, D_FF_EXPERT), w_map),
                      pl.BlockSpec((None, None, D_MODEL, D_FF_EXPERT), w_map),
                      pl.BlockSpec((None, None, D_FF_EXPERT, D_MODEL), w_map)],
            out_specs=pl.BlockSpec((MOE_TILE, D_MODEL), row_map),
            scratch_shapes=[pltpu.VMEM((D_MODEL, D_FF_EXPERT), BF16), pltpu.VMEM((D_MODEL, D_FF_EXPERT), BF16),
                            pltpu.VMEM((D_FF_EXPERT, D_MODEL), BF16)],
        ),
        out_shape=jax.ShapeDtypeStruct((n_tiles * MOE_TILE, D_MODEL), F32),
        compiler_params=_cparams("arbitrary"),
        name="moe_experts",
    )(tile_e, n_used, xs, g, w1, w3, w2)


def _combine_kernel(idx_ref, ys_hbm, h_ref, wts_ref, gf_ref, o_ref, buf_ref, sem, *, final_norm, n_blocks):
    i = pl.program_id(0)
    tb = h_ref.shape[0]

    @pl.when(i < n_blocks)
    def _():
        slot = i % 2

        def row(t, carry):
            for k in range(2):
                pltpu.make_async_copy(ys_hbm.at[pl.ds(idx_ref[0, 0, k * tb + t], 1)],
                                      buf_ref.at[slot, k, pl.ds(t, 1)], sem.at[slot]).start()
            return carry

        lax.fori_loop(0, tb, row, 0, unroll=True)

    @pl.when(i > 0)
    def _():
        slot = (i - 1) % 2
        for k in range(2):
            pltpu.make_async_copy(ys_hbm.at[pl.ds(0, tb)], buf_ref.at[slot, k], sem.at[slot]).wait()
        reps = D_MODEL // LANES
        w = wts_ref[...]
        w_top1 = jnp.concatenate([w[:, :LANES]] * reps, axis=1)
        w_top2 = jnp.concatenate([w[:, LANES:]] * reps, axis=1)
        out = h_ref[...] + w_top1 * buf_ref[slot, 0] + w_top2 * buf_ref[slot, 1]
        if final_norm:
            ms = jnp.mean(out * out, axis=-1, keepdims=True)
            out = out * lax.rsqrt(ms + NORM_EPS) * gf_ref[...]
        o_ref[...] = out


def _combine(h2d, ys, idx, wts, g_final, final_norm, batch):
    n = h2d.shape[0]
    tb = MOE_TOKENS_PER_STEP
    steps_per_seq = n // batch // tb
    n_blocks = n // tb
    prev = lambda i: jnp.maximum(i - 1, 0)
    return pl.pallas_call(
        functools.partial(_combine_kernel, final_norm=final_norm, n_blocks=n_blocks),
        grid=(n_blocks + 1,),
        in_specs=[pl.BlockSpec((1, 1, 2 * tb), lambda i: (jnp.minimum(i, n_blocks - 1), 0, 0),
                               memory_space=pltpu.SMEM),
                  pl.BlockSpec(memory_space=pl.ANY),
                  pl.BlockSpec((tb, D_MODEL), lambda i: (prev(i), 0)),
                  pl.BlockSpec((tb, 2 * LANES), lambda i: (prev(i), 0)),
                  pl.BlockSpec((1, D_MODEL), lambda i: (0, 0))],
        out_specs=pl.BlockSpec((None, tb, D_MODEL),
                               lambda i: (prev(i) // steps_per_seq, prev(i) % steps_per_seq, 0)),
        out_shape=jax.ShapeDtypeStruct((batch, n // batch, D_MODEL), F32),
        scratch_shapes=[pltpu.VMEM((2, 2, tb, D_MODEL), F32), pltpu.SemaphoreType.DMA((2,))],
        compiler_params=pltpu.CompilerParams(dimension_semantics=("arbitrary",), vmem_limit_bytes=VMEM_LIMIT,
                                             disable_bounds_checks=True),
        name="moe_combine",
    )(idx, ys, h2d, wts, g_final)


def _moe(h2d, g, w_router, b_router, w1, w3, w2, layer, g_final, final_norm, batch):
    n = h2d.shape[0]
    route, wts, counts = _router(h2d, g, w_router, b_router, layer)
    idx, tile_e, n_used, zero_rows, n_tiles = _dispatch_tables(route, counts, n)
    xs = _dispatch(h2d, idx, zero_rows, n_used, n_tiles)
    ys = _experts(xs, g, w1, w3, w2, tile_e, n_used, layer, n_tiles)
    return _combine(h2d, ys, idx, wts, g_final, final_norm, batch)


def _router_params(wg, bg, we, be):
    n_l = wg.shape[0]
    pad = ROUTER_ROWS - N_GROUPS - N_EXPERTS
    w = jnp.concatenate([wg, we, jnp.zeros((n_l, D_MODEL, pad), F32)], axis=-1).astype(F32).transpose(0, 2, 1)
    b = jnp.concatenate([bg, be, jnp.zeros((n_l, pad), F32)], axis=-1).astype(F32)
    return w, jnp.broadcast_to(b[:, :, None], (n_l, ROUTER_ROWS, LANES))


def kernel(x, w_in, na_rpb, lam_q1, lam_k1, lam_q2, lam_k2, diff_subln, w_pa, w_pb, w_pc, w_o, norm_mix, norm_ffn,
           router_group_w, router_group_b, router_expert_w, router_expert_b, w1, w3, w2, norm_final):
    B, T, D = x.shape
    depth = w_in.shape[0]
    rows = T // GRID_W
    wr = min(NA_WIN_R, rows)
    rope_tab = _rope_table(T)
    row3 = lambda a: a[:, None, :]
    w_in_bf = _permute_cols(w_in)
    bias_tab = _na_bias_table(na_rpb, wr)
    g_mix, g_ffn = row3(norm_mix), row3(norm_ffn)
    lams = [row3(a) for a in (lam_q1, lam_k1, lam_q2, lam_k2)]
    subln = row3(diff_subln)
    w_router, b_router = _router_params(router_group_w, router_group_b, router_expert_w, router_expert_b)
    h = x.reshape(B * T, D)
    for l in range(depth):
        lam_init = 0.8 - 0.6 * math.exp(-0.3 * l)
        proj, *cgs = _inproj(h, g_mix, w_in_bf, rope_tab, l, B, T)
        ya = _na_attention(proj, bias_tab, l, B, T)
        yb = _diff_attention(proj, *lams, subln, l, lam_init, B, T)
        yc = _dilated_branch(cgs)
        h = _merge(h, proj, ya, yb, yc, w_pa, w_pb, w_pc, w_o, l)
        out = _moe(h, g_ffn, w_router, b_router, w1, w3, w2, l, norm_final[None, :], l == depth - 1, B)
        h = out.reshape(B * T, D)
    return out
```

```python
import functools
import math

import jax
import jax.numpy as jnp
import numpy as np
from jax import lax
from jax.experimental import pallas as pl
from jax.experimental.pallas import tpu as pltpu

F32 = jnp.float32
BF16 = jnp.bfloat16

D_MODEL = 1024
HEAD_DIM = 64
ROPE_DIM = 16
ROPE_THETA = 500000.0
GRID_W = 64
NA_HEADS = 8
NA_WIN_R = 8
NA_WIN_C = 16
DIFF_HEADS = 4
DIL_PATTERNS = ((128, 1), (512, 4), (2048, 16))
N_GROUPS = 4
EXPERTS_PER_GROUP = 4
N_EXPERTS = 16
D_FF_EXPERT = D_MODEL // 2
NORM_EPS = 1e-6
SUBLN_EPS = 1e-5
NEG_INF = -1e30

LANES = 128
MXU_N = 256
VMEM_LIMIT = 56 * 1024 * 1024

A_W, B_W, C_W = 512, 512, 768
N_STEPS = len(DIL_PATTERNS)
GROUP_W = C_W // N_STEPS
_MAIN_ORDER = ("qb", "kb", "va", "qa", "ga", "gb", "gc", "ka", "vb")
_REF_ORDER = ("qa", "ka", "va", "qb", "kb", "vb", "qc", "kc", "vc", "ga", "gb", "gc")
_WIDTH = dict(qa=A_W, ka=A_W, va=A_W, qb=B_W, kb=B_W, vb=B_W, qc=C_W, kc=C_W, vc=C_W,
              ga=D_MODEL, gb=D_MODEL, gc=D_MODEL)


def _offsets(order):
    off, out = 0, {}
    for name in order:
        out[name] = off
        off += _WIDTH[name]
    return out, off


_OFF, MAIN_W = _offsets(_MAIN_ORDER)
_REF_OFF, IN_W = _offsets(_REF_ORDER)
OFF_QA, OFF_KA, OFF_VA = _OFF["qa"], _OFF["ka"], _OFF["va"]
OFF_QB, OFF_KB, OFF_VB = _OFF["qb"], _OFF["kb"], _OFF["vb"]
OFF_GATE = _OFF["ga"]
MAIN_STEP_W = MAIN_W // N_STEPS
STEP_W = MAIN_STEP_W + 3 * GROUP_W
MAIN_ROPE_W = 2 * B_W
QB_SCALE = HEAD_DIM ** -0.5 * math.log2(math.e)
assert OFF_GATE % D_MODEL == 0 and OFF_QB == 0 and OFF_KB == B_W and MAIN_ROPE_W <= MAIN_STEP_W
assert MAIN_W % N_STEPS == 0 and MAIN_STEP_W % MXU_N == 0 and N_STEPS * STEP_W == IN_W


def _permute_cols(w):
    def piece(name, lo, hi):
        p = w[..., _REF_OFF[name] + lo:_REF_OFF[name] + hi]
        return (p * QB_SCALE if name == "qb" else p).astype(BF16)

    parts = []
    for s in range(N_STEPS):
        for n in _MAIN_ORDER:
            lo = max(_OFF[n], s * MAIN_STEP_W) - _OFF[n]
            hi = min(_OFF[n] + _WIDTH[n], (s + 1) * MAIN_STEP_W) - _OFF[n]
            if lo < hi:
                parts.append(piece(n, lo, hi))
        for n in ("qc", "kc", "vc"):
            parts.append(piece(n, s * GROUP_W, (s + 1) * GROUP_W))
    return jnp.concatenate(parts, axis=-1)


def _cparams(*sem):
    return pltpu.CompilerParams(dimension_semantics=sem, vmem_limit_bytes=VMEM_LIMIT)


def _inproj_kernel(x_ref, g_ref, w_ref, rope_ref, main_ref, c0_ref, c1_ref, c2_ref, xn_ref, y_ref, *, tm):
    j = pl.program_id(1)
    c_refs = (c0_ref, c1_ref, c2_ref)
    reps = MXU_N // LANES
    half = ROPE_DIM // 2
    assert GROUP_W == MXU_N

    def rope(y):
        cos = jnp.concatenate([rope_ref[0]] * reps, axis=1)
        s_up = jnp.concatenate([rope_ref[1]] * reps, axis=1)
        s_dn = jnp.concatenate([rope_ref[2]] * reps, axis=1)
        return y * cos + pltpu.roll(y, MXU_N - half, 1) * s_up + pltpu.roll(y, half, 1) * s_dn

    def step(s):
        xn = xn_ref[...]
        for c in range(MAIN_STEP_W // MXU_N):
            sl = slice(c * MXU_N, (c + 1) * MXU_N)
            y = jnp.dot(xn, w_ref[:, sl], preferred_element_type=F32)
            if s == 0 and c < MAIN_ROPE_W // MXU_N:
                y = rope(y)
            main_ref[:, sl] = y.astype(main_ref.dtype)
        dil = DIL_PATTERNS[s][1]
        for c in range(3):
            wsl = slice(MAIN_STEP_W + c * GROUP_W, MAIN_STEP_W + (c + 1) * GROUP_W)
            osl = slice(c * GROUP_W, (c + 1) * GROUP_W)
            y = jnp.dot(xn, w_ref[:, wsl], preferred_element_type=F32)
            if c < 2:
                y = rope(y)
            if dil == 1:
                c_refs[s][0, :, osl] = y.astype(BF16)
            else:
                for hb in range(reps):
                    y_ref[c, hb] = y[:, hb * LANES:(hb + 1) * LANES]
                for p in range(dil):
                    for hb in range(reps):
                        c_refs[s][p, :, c * GROUP_W + hb * LANES:c * GROUP_W + (hb + 1) * LANES] = (
                            y_ref[c, hb, pl.ds(p, tm // dil, stride=dil), :].astype(BF16))

    @pl.when(j == 0)
    def _():
        x = x_ref[...]
        ms = jnp.mean(x * x, axis=-1, keepdims=True)
        xn_ref[...] = (x * lax.rsqrt(ms + NORM_EPS) * g_ref[...]).astype(BF16)
        step(0)

    for s in range(1, N_STEPS):
        pl.when(j == s)(functools.partial(step, s))


def _inproj(h2d, g, w_bf, rope_tab, layer, B, T, tm=1024):
    n = h2d.shape[0]
    tpb = T // tm
    c_specs, c_shapes = [], []
    for _, dil in DIL_PATTERNS:
        assert tm % dil == 0
        c_specs.append(pl.BlockSpec((None, dil, tm // dil, 3 * GROUP_W), lambda i, j: (i // tpb, 0, i % tpb, 0)))
        c_shapes.append(jax.ShapeDtypeStruct((B, dil, T // dil, 3 * GROUP_W), BF16))
    return pl.pallas_call(
        functools.partial(_inproj_kernel, tm=tm),
        grid=(n // tm, N_STEPS),
        in_specs=[
            pl.BlockSpec((tm, D_MODEL), lambda i, j: (i, 0)),
            pl.BlockSpec((None, 1, D_MODEL), lambda i, j: (layer, 0, 0)),
            pl.BlockSpec((None, D_MODEL, STEP_W), lambda i, j: (layer, 0, j)),
            pl.BlockSpec((3, tm, LANES), lambda i, j: (0, i % tpb, 0)),
        ],
        out_specs=[pl.BlockSpec((tm, MAIN_STEP_W), lambda i, j: (i, j))] + c_specs,
        out_shape=[jax.ShapeDtypeStruct((n, MAIN_W), BF16)] + c_shapes,
        scratch_shapes=[pltpu.VMEM((tm, D_MODEL), BF16), pltpu.VMEM((3, MXU_N // LANES, tm, LANES), F32)],
        compiler_params=_cparams("parallel", "arbitrary"),
        name="inproj",
    )(h2d, g, w_bf, rope_tab)


def _rope_table(T):
    half = ROPE_DIM // 2
    inv = 1.0 / (ROPE_THETA ** (jnp.arange(0, ROPE_DIM, 2, dtype=F32) / ROPE_DIM))
    ang = jnp.arange(T, dtype=F32)[:, None] * inv[None, :]
    cos, sin = jnp.cos(ang), jnp.sin(ang)
    zeros = jnp.zeros((T, HEAD_DIM - ROPE_DIM), F32)
    z8 = jnp.zeros((T, half), F32)
    c64 = jnp.concatenate([cos, cos, zeros + 1.0], axis=1)
    up64 = jnp.concatenate([-sin, z8, zeros], axis=1)
    dn64 = jnp.concatenate([z8, sin, zeros], axis=1)
    reps = LANES // HEAD_DIM
    return jnp.stack([jnp.tile(c64, (1, reps)), jnp.tile(up64, (1, reps)), jnp.tile(dn64, (1, reps))])


NA_ROWS_PER_ITER = 32


def _na_kernel(q_ref, k_ref, v_ref, b_ref, o_ref, *, rows, wr):
    lane = lax.broadcasted_iota(jnp.int32, (GRID_W, LANES), 1)
    lo = lane < HEAD_DIM
    scale = HEAD_DIM ** -0.5

    def row_group(gi, carry):
        scores, windows = [], []
        for u in range(NA_ROWS_PER_ITER):
            r = gi * NA_ROWS_PER_ITER + u
            r0 = jnp.clip(r - wr // 2, 0, rows - wr)
            d0 = r0 - r + NA_WIN_R - 1 - (NA_WIN_R - wr)
            q = q_ref[pl.ds(pl.multiple_of(r * GRID_W, GRID_W), GRID_W), :] * scale
            ks = pl.multiple_of(r0 * GRID_W, GRID_W)
            kw = k_ref[pl.ds(ks, wr * GRID_W), :]
            windows.append(ks)
            for hh in range(2):
                qh = jnp.where(lo if hh == 0 else jnp.logical_not(lo), q, jnp.zeros_like(q))
                s = lax.dot_general(qh, kw, (((1,), (1,)), ((), ())), preferred_element_type=F32)
                bias = jnp.concatenate([b_ref[hh, d0 + 2 * i] for i in range(wr // 2)], axis=1)
                scores.append(s + bias)
        probs, sums = [], []
        for s in scores:
            p = jnp.exp(s - jnp.max(s, axis=-1, keepdims=True))
            sums.append(jnp.sum(p, axis=-1, keepdims=True))
            probs.append(p.astype(BF16))
        for u in range(NA_ROWS_PER_ITER):
            r = gi * NA_ROWS_PER_ITER + u
            vw = v_ref[pl.ds(windows[u], wr * GRID_W), :]
            outs = [jnp.dot(probs[2 * u + hh], vw, preferred_element_type=F32) / sums[2 * u + hh] for hh in range(2)]
            o = jnp.where(lo, outs[0], outs[1])
            o_ref[pl.ds(pl.multiple_of(r * GRID_W, GRID_W), GRID_W), :] = o.astype(o_ref.dtype)
        return carry

    lax.fori_loop(0, rows // NA_ROWS_PER_ITER, row_group, 0)


def _na_bias_table(rpb, wr):
    qc = np.arange(GRID_W)[:, None]
    kc = np.arange(GRID_W)[None, :]
    c0 = np.clip(qc - NA_WIN_C // 2, 0, GRID_W - NA_WIN_C)
    ok = (kc >= c0) & (kc < c0 + NA_WIN_C)
    dc = np.clip(kc - qc + NA_WIN_C - 1, 0, 2 * NA_WIN_C - 2)
    onehot = (np.arange(2 * NA_WIN_C - 1)[:, None, None] == dc[None]).astype(np.float32)
    b = jnp.einsum("lhrd,dqk->lhrqk", rpb.astype(F32), onehot, precision=lax.Precision.HIGHEST)
    b = jnp.where(jnp.asarray(ok), b, NEG_INF)
    off = NA_WIN_R - wr
    n_pairs = 2 * wr - 2
    return jnp.concatenate([b[:, :, off:off + n_pairs], b[:, :, off + 1:off + 1 + n_pairs]], axis=-1)


def _na_attention(proj, bias_tab, layer, B, T):
    rows = T // GRID_W
    wr = min(NA_WIN_R, rows)
    cq, ck, cv = OFF_QA // LANES, OFF_KA // LANES, OFF_VA // LANES
    assert wr % 2 == 0 and 2 * GRID_W == LANES and rows % NA_ROWS_PER_ITER == 0
    return pl.pallas_call(
        functools.partial(_na_kernel, rows=rows, wr=wr),
        grid=(B, NA_HEADS // 2),
        in_specs=[
            pl.BlockSpec((T, LANES), lambda b, h: (b, cq + h)),
            pl.BlockSpec((T, LANES), lambda b, h: (b, ck + h)),
            pl.BlockSpec((T, LANES), lambda b, h: (b, cv + h)),
            pl.BlockSpec((None, 2, 2 * wr - 2, GRID_W, 2 * GRID_W), lambda b, h: (layer, h, 0, 0, 0)),
        ],
        out_specs=pl.BlockSpec((T, LANES), lambda b, h: (b, h)),
        out_shape=jax.ShapeDtypeStruct((B * T, A_W), BF16),
        compiler_params=_cparams("parallel", "arbitrary"),
        name="na_attn",
    )(proj, proj, proj, bias_tab)


DIFF_KEY_CHUNK = 512


def _diff_kernel(lq1_ref, lk1_ref, lq2_ref, lk2_ref, q_ref, k_ref, v_ref, g_ref, o_ref, vt_ref, *, lam_init, tq):
    lam = (jnp.exp(jnp.sum(lq1_ref[...] * lk1_ref[...], keepdims=True))
           - jnp.exp(jnp.sum(lq2_ref[...] * lk2_ref[...], keepdims=True)) + lam_init)
    T = k_ref.shape[0]
    ck = DIFF_KEY_CHUNK
    n_chunks = T // ck
    lane = lax.broadcasted_iota(jnp.int32, (tq, LANES), 1)
    lo = lane < HEAD_DIM
    nt = (((1,), (1,)), ((), ()))
    vt_ref[...] = v_ref[...].T

    def q_block(i, carry):
        rows = pl.ds(pl.multiple_of(i * tq, tq), tq)
        q = q_ref[rows, :]
        zero = jnp.zeros_like(q)
        qs = (jnp.where(lo, q, zero), jnp.where(lo, zero, q))

        def scores(c):
            kc = k_ref[c * ck:(c + 1) * ck, :]
            return [lax.dot_general(kc, qm, nt, preferred_element_type=F32) for qm in qs]

        m = [jnp.full((1, tq), NEG_INF, F32)] * 2
        l = [jnp.zeros((1, tq), F32)] * 2
        acc = [jnp.zeros((LANES, tq), F32)] * 2
        s_next = scores(0)
        for c in range(n_chunks):
            s_cur = s_next
            if c + 1 < n_chunks:
                s_next = scores(c + 1)
            vt = vt_ref[:, c * ck:(c + 1) * ck]
            for j in range(2):
                m_new = jnp.maximum(m[j], jnp.max(s_cur[j], axis=0, keepdims=True))
                alpha = jnp.exp2(m[j] - m_new)
                p = jnp.exp2(s_cur[j] - m_new)
                l[j] = alpha * l[j] + jnp.sum(p, axis=0, keepdims=True)
                acc[j] = alpha * acc[j] + jnp.dot(vt, p.astype(BF16), preferred_element_type=F32)
                m[j] = m_new
        ot = acc[0] / l[0] - lam * (acc[1] / l[1])
        o = ot.T
        ms = jnp.mean(o * o, axis=-1, keepdims=True)
        o = o * lax.rsqrt(ms + SUBLN_EPS) * g_ref[...] * (1.0 - lam_init)
        o_ref[rows, :] = o.astype(o_ref.dtype)
        return carry

    lax.fori_loop(0, q_ref.shape[0] // tq, q_block, 0, unroll=2)


def _diff_attention(proj, lq1, lk1, lq2, lk2, subln_g, layer, lam_init, B, T, tq=512):
    cq, ck, cv = OFF_QB // LANES, OFF_KB // LANES, OFF_VB // LANES
    vec = pl.BlockSpec((None, 1, HEAD_DIM), lambda b, h: (layer, 0, 0))
    return pl.pallas_call(
        functools.partial(_diff_kernel, lam_init=lam_init, tq=tq),
        grid=(B, DIFF_HEADS),
        in_specs=[
            vec, vec, vec, vec,
            pl.BlockSpec((T, LANES), lambda b, h: (b, cq + h)),
            pl.BlockSpec((T, LANES), lambda b, h: (b, ck + h)),
            pl.BlockSpec((T, LANES), lambda b, h: (b, cv + h)),
            pl.BlockSpec((None, 1, LANES), lambda b, h: (layer, 0, 0)),
        ],
        out_specs=pl.BlockSpec((T, LANES), lambda b, h: (b, h)),
        out_shape=jax.ShapeDtypeStruct((B * T, B_W), BF16),
        scratch_shapes=[pltpu.VMEM((LANES, T), BF16)],
        compiler_params=_cparams("parallel", "arbitrary"),
        name="diff_attn",
    )(lq1, lk1, lq2, lk2, proj, proj, proj, subln_g)


DIL_Q = 128


DIL_BLOCKS_PER_ITER = 4


def _dil_kernel(q_ref, k_ref, v_ref, o_ref, lse_ref, *, L, dil, radius):
    kw_len = DIL_Q + 2 * radius
    lane = lax.broadcasted_iota(jnp.int32, (DIL_Q, LANES), 1)
    lo = lane < HEAD_DIM
    rel = (lax.broadcasted_iota(jnp.int32, (DIL_Q, kw_len), 1)
           - lax.broadcasted_iota(jnp.int32, (DIL_Q, kw_len), 0))
    scale = HEAD_DIM ** -0.5
    nt = (((1,), (1,)), ((), ()))

    blocks_per_phase = L // DIL_Q

    def blocks(gi, carry):
        scores, starts, maxes = [], [], []
        for u in range(DIL_BLOCKS_PER_ITER):
            item = gi * DIL_BLOCKS_PER_ITER + u
            p = item // blocks_per_phase
            l0 = pl.multiple_of((item % blocks_per_phase) * DIL_Q, DIL_Q)
            ks = pl.multiple_of(jnp.clip(l0 - radius, 0, L - kw_len), radius)
            q = q_ref[p, pl.ds(l0, DIL_Q), :] * scale
            kw = k_ref[p, pl.ds(ks, kw_len), :]
            ok = jnp.abs(rel + (ks - l0)) <= radius
            starts.append((p, l0, ks))
            for hh in range(2):
                qh = jnp.where(lo if hh == 0 else jnp.logical_not(lo), q, jnp.zeros_like(q))
                s = lax.dot_general(qh, kw, nt, preferred_element_type=F32)
                scores.append(jnp.where(ok, s, NEG_INF))
        probs, sums = [], []
        for s in scores:
            m = jnp.max(s, axis=-1, keepdims=True)
            e = jnp.exp(s - m)
            maxes.append(m)
            sums.append(jnp.sum(e, axis=-1, keepdims=True))
            probs.append(e.astype(BF16))
        for u in range(DIL_BLOCKS_PER_ITER):
            p, l0, ks = starts[u]
            vw = v_ref[p, pl.ds(ks, kw_len), :]
            outs = [jnp.dot(probs[2 * u + hh], vw, preferred_element_type=F32) / sums[2 * u + hh] for hh in range(2)]
            lses = [maxes[2 * u + hh] + jnp.log(sums[2 * u + hh]) for hh in range(2)]
            o = jnp.where(lo, outs[0], outs[1])
            lse = jnp.where(lo, lses[0], lses[1])
            if dil == 1:
                o_ref[pl.ds(l0, DIL_Q), :] = o
                lse_ref[pl.ds(l0, DIL_Q), :] = lse
            else:
                o_ref[pl.ds(l0 * dil + p, DIL_Q, stride=dil), :] = o
                lse_ref[pl.ds(l0 * dil + p, DIL_Q, stride=dil), :] = lse
        return carry

    lax.fori_loop(0, dil * blocks_per_phase // DIL_BLOCKS_PER_ITER, blocks, 0)


def _dil_merge_kernel(*refs, geoms):
    n = len(geoms)
    out_ref = refs[3 * n]
    scratch = refs[3 * n + 1:]
    for g, (L, dil, radius) in enumerate(geoms):
        _dil_kernel(refs[3 * g], refs[3 * g + 1], refs[3 * g + 2], scratch[2 * g], scratch[2 * g + 1],
                    L=L, dil=dil, radius=radius)
    lses = [scratch[2 * g + 1][...] for g in range(n)]
    m = functools.reduce(jnp.maximum, lses)
    es = [jnp.exp(l - m) for l in lses]
    num = sum(scratch[2 * g][...] * es[g] for g in range(n))
    out_ref[...] = (num / sum(es)).astype(out_ref.dtype)


def _dilated_branch(cgs):
    B = cgs[0].shape[0]
    hp = GROUP_W // LANES
    geoms, in_specs, operands = [], [], []
    for cg, (window, dil) in zip(cgs, DIL_PATTERNS):
        _, d, L, _ = cg.shape
        radius = window // (2 * dil)
        assert d == dil and L >= DIL_Q + 2 * radius and L % DIL_Q == 0
        assert (dil * L // DIL_Q) % DIL_BLOCKS_PER_ITER == 0
        geoms.append((L, dil, radius))
        for c in range(3):
            in_specs.append(pl.BlockSpec((None, dil, L, LANES), lambda b, h, c=c: (b, 0, 0, c * hp + h)))
            operands.append(cg)
    T = geoms[0][0] * geoms[0][1]
    return pl.pallas_call(
        functools.partial(_dil_merge_kernel, geoms=tuple(geoms)),
        grid=(B, hp),
        in_specs=in_specs,
        out_specs=pl.BlockSpec((T, LANES), lambda b, h: (b, h)),
        out_shape=jax.ShapeDtypeStruct((B * T, GROUP_W), BF16),
        scratch_shapes=[pltpu.VMEM((T, LANES), F32)] * (2 * len(geoms)),
        compiler_params=_cparams("parallel", "arbitrary"),
        name="dil_attn",
    )(*operands)


def _merge_kernel(h_ref, ga_ref, gb_ref, gc_ref, ya_ref, yb_ref, yc_ref,
                  wpa32_ref, wpb32_ref, wpc32_ref, wo32_ref, out_ref, wpa_ref, wpb_ref, wpc_ref, wo_ref):
    @pl.when(pl.program_id(0) == 0)
    def _():
        for src, dst in ((wpa32_ref, wpa_ref), (wpb32_ref, wpb_ref), (wpc32_ref, wpc_ref), (wo32_ref, wo_ref)):
            dst[...] = src[...].astype(BF16)

    merged = jax.nn.sigmoid(ga_ref[...].astype(F32)) * jnp.dot(ya_ref[...], wpa_ref[...], preferred_element_type=F32)
    merged += jax.nn.sigmoid(gb_ref[...].astype(F32)) * jnp.dot(yb_ref[...], wpb_ref[...], preferred_element_type=F32)
    merged += jax.nn.sigmoid(gc_ref[...].astype(F32)) * jnp.dot(yc_ref[...], wpc_ref[...], preferred_element_type=F32)
    out_ref[...] = h_ref[...] + jnp.dot(merged.astype(BF16), wo_ref[...], preferred_element_type=F32)


def _merge(h2d, proj, ya, yb, yc, wpa, wpb, wpc, wo, layer, tm=512):
    n = h2d.shape[0]
    gw = GROUP_W
    row = lambda w: pl.BlockSpec((tm, w), lambda i: (i, 0))
    full = lambda a: pl.BlockSpec((None,) + a.shape[1:], lambda i: (layer, 0, 0))
    gate = lambda c: pl.BlockSpec((tm, D_MODEL), lambda i: (i, OFF_GATE // D_MODEL + c))
    return pl.pallas_call(
        _merge_kernel,
        grid=(n // tm,),
        in_specs=[row(D_MODEL), gate(0), gate(1), gate(2), row(A_W), row(B_W),
                  row(gw),
                  full(wpa), full(wpb), full(wpc), full(wo)],
        out_specs=row(D_MODEL),
        out_shape=jax.ShapeDtypeStruct((n, D_MODEL), F32),
        scratch_shapes=[pltpu.VMEM(a.shape[1:], BF16) for a in (wpa, wpb, wpc, wo)],
        compiler_params=_cparams("arbitrary"),
        name="merge_outproj",
    )(h2d, proj, proj, proj, ya, yb, yc, wpa, wpb, wpc, wo)


MOE_TILE = 512
MOE_TOKENS_PER_STEP = 1024
ROUTE_ROWS = 8


ROUTER_ROWS = 32


def _router_kernel(h_ref, g_ref, wr_ref, br_ref, upper_ref, route_ref, wts_ref, cnt_ref, base_ref):
    i = pl.program_id(0)
    tm = h_ref.shape[0]
    lane_reps = tm // LANES
    row = lax.broadcasted_iota(jnp.int32, (ROUTER_ROWS, tm), 0)
    rowf = row.astype(F32)
    big = float(ROUTER_ROWS)
    nt = (((1,), (1,)), ((), ()))

    @pl.when(i == 0)
    def _():
        base_ref[...] = jnp.zeros_like(base_ref)

    x = h_ref[...]
    ms = jnp.mean(x * x, axis=-1, keepdims=True)
    xn = x * lax.rsqrt(ms + NORM_EPS) * g_ref[...]
    logits = lax.dot_general(wr_ref[...], xn, nt, preferred_element_type=F32, precision=lax.Precision.HIGHEST)
    logits = logits + jnp.concatenate([br_ref[...]] * lane_reps, axis=1)
    gl = jnp.where(row < N_GROUPS, logits, -jnp.inf)
    gmax = jnp.max(gl, axis=0, keepdims=True)
    g_sel = jnp.min(jnp.where(gl == gmax, rowf, big), axis=0, keepdims=True).astype(jnp.int32)
    g_gate = 1.0 / jnp.sum(jnp.exp(gl - gmax), axis=0, keepdims=True)
    eidx = row - N_GROUPS
    in_grp = (eidx >= g_sel * EXPERTS_PER_GROUP) & (eidx < (g_sel + 1) * EXPERTS_PER_GROUP)
    el = jnp.where(in_grp, logits, -jnp.inf)
    t1 = jnp.max(el, axis=0, keepdims=True)
    i1 = jnp.min(jnp.where(el == t1, rowf, big), axis=0, keepdims=True).astype(jnp.int32)
    el2 = jnp.where(row == i1, -jnp.inf, el)
    t2 = jnp.max(el2, axis=0, keepdims=True)
    i2 = jnp.min(jnp.where(el2 == t2, rowf, big), axis=0, keepdims=True).astype(jnp.int32)
    x2 = jnp.exp(t2 - t1)
    den = 1.0 + x2
    wts_ref[:, :LANES] = jnp.broadcast_to(g_gate / den, (LANES, tm)).T
    wts_ref[:, LANES:] = jnp.broadcast_to(g_gate * x2 / den, (LANES, tm)).T
    member = jnp.where(row == i1, 1.0, 0.0) + jnp.where(row == i2, 1.0, 0.0)
    earlier = jnp.dot(member.astype(BF16), upper_ref[...], preferred_element_type=F32)
    earlier = earlier + jnp.concatenate([base_ref[...]] * lane_reps, axis=1)
    rank1 = jnp.sum(jnp.where(row == i1, earlier, 0.0), axis=0, keepdims=True)
    rank2 = jnp.sum(jnp.where(row == i2, earlier, 0.0), axis=0, keepdims=True)
    base_ref[...] = base_ref[...] + jnp.sum(member, axis=1, keepdims=True)
    cnt_ref[...] = base_ref[...]
    out_row = lax.broadcasted_iota(jnp.int32, (ROUTE_ROWS, tm), 0)
    route_ref[...] = jnp.where(out_row == 0, (i1 - N_GROUPS).astype(F32),
                               jnp.where(out_row == 1, (i2 - N_GROUPS).astype(F32),
                                         jnp.where(out_row == 2, rank1, jnp.where(out_row == 3, rank2, 0.0))))


def _router(h2d, g, w_router, b_router, layer, tm=1024):
    n = h2d.shape[0]
    return pl.pallas_call(
        _router_kernel,
        grid=(n // tm,),
        in_specs=[
            pl.BlockSpec((tm, D_MODEL), lambda i: (i, 0)),
            pl.BlockSpec((None, 1, D_MODEL), lambda i: (layer, 0, 0)),
            pl.BlockSpec((None, ROUTER_ROWS, D_MODEL), lambda i: (layer, 0, 0)),
            pl.BlockSpec((None, ROUTER_ROWS, LANES), lambda i: (layer, 0, 0)),
            pl.BlockSpec((tm, tm), lambda i: (0, 0)),
        ],
        out_specs=[pl.BlockSpec((ROUTE_ROWS, tm), lambda i: (0, i)),
                   pl.BlockSpec((tm, 2 * LANES), lambda i: (i, 0)),
                   pl.BlockSpec((ROUTER_ROWS, LANES), lambda i: (0, 0))],
        out_shape=[jax.ShapeDtypeStruct((ROUTE_ROWS, n), F32), jax.ShapeDtypeStruct((n, 2 * LANES), F32),
                   jax.ShapeDtypeStruct((ROUTER_ROWS, LANES), F32)],
        scratch_shapes=[pltpu.VMEM((ROUTER_ROWS, LANES), F32)],
        compiler_params=_cparams("arbitrary"),
        name="moe_router",
    )(h2d, g, w_router, b_router, jnp.triu(jnp.ones((tm, tm), BF16), k=1))


def _dispatch_tables(route, counts, n):
    cnt = counts[N_GROUPS:N_GROUPS + N_EXPERTS, 0].astype(jnp.int32)
    padded = (cnt + MOE_TILE - 1) // MOE_TILE * MOE_TILE
    ends = jnp.cumsum(padded)
    starts = ends - padded
    eids = jnp.arange(N_EXPERTS, dtype=jnp.int32)
    pos = []
    for k in range(2):
        e = route[k].astype(jnp.int32)
        start_e = jnp.sum(jnp.where(e[:, None] == eids[None, :], starts[None, :], 0), axis=1)
        pos.append(start_e + route[2 + k].astype(jnp.int32))
    tb = MOE_TOKENS_PER_STEP
    idx = jnp.concatenate([pos[0].reshape(n // tb, 1, tb), pos[1].reshape(n // tb, 1, tb)], axis=-1)
    n_tiles = 2 * n // MOE_TILE + N_EXPERTS
    tile_ids = jnp.arange(n_tiles, dtype=jnp.int32)
    n_used = ends[-1] // MOE_TILE
    tile_e = jnp.sum((tile_ids[:, None] * MOE_TILE >= ends[None, :]).astype(jnp.int32), axis=1)
    last_e = jnp.max(jnp.where(tile_ids < n_used, tile_e, 0))
    tile_e = jnp.where(tile_ids < n_used, tile_e, last_e).astype(jnp.int32)
    zero_rows = jnp.where(padded > 0, ends - MOE_TILE, n_tiles * MOE_TILE).astype(jnp.int32)
    return idx, tile_e, n_used.reshape(1).astype(jnp.int32), zero_rows, n_tiles


def _dispatch_kernel(zero_rows_ref, n_used_ref, idx_ref, h_ref, xs_hbm, zeros_ref, stage_ref, sem, row_sem):
    i = pl.program_id(0)
    tb = MOE_TOKENS_PER_STEP
    n_tiles = xs_hbm.shape[0] // MOE_TILE - 1

    def zero_tile(row0):
        return pltpu.make_async_copy(zeros_ref, xs_hbm.at[pl.ds(pl.multiple_of(row0, MOE_TILE), MOE_TILE)], sem)

    @pl.when(i == 0)
    def _():
        zeros_ref[...] = jnp.zeros_like(zeros_ref)
        fills = [zero_tile(zero_rows_ref[e]) for e in range(N_EXPERTS)]
        for c in fills:
            c.start()
        for c in fills:
            c.wait()

        def tail(j, carry):
            tile = n_used_ref[0] + j

            @pl.when(tile <= n_tiles)
            def _():
                c = zero_tile(tile * MOE_TILE)
                c.start()
                c.wait()
            return carry

        lax.fori_loop(0, N_EXPERTS + 1, tail, 0)

    slot = i % 2
    stage_ref[slot] = h_ref[...]

    def row(t, carry):
        src = stage_ref.at[slot, pl.ds(t, 1)]
        for k in range(2):
            pltpu.make_async_copy(src, xs_hbm.at[pl.ds(idx_ref[0, 0, k * tb + t], 1)], row_sem.at[slot]).start()
        return carry

    lax.fori_loop(0, tb, row, 0, unroll=True)

    def drain(s):
        for _ in range(2):
            pltpu.make_async_copy(stage_ref.at[s], xs_hbm.at[pl.ds(0, tb)], row_sem.at[s]).wait()

    @pl.when(i > 0)
    def _():
        drain(1 - slot)

    @pl.when(i == pl.num_programs(0) - 1)
    def _():
        drain(slot)


def _dispatch(h2d, idx, zero_rows, n_used, n_tiles):
    n = h2d.shape[0]
    tb = MOE_TOKENS_PER_STEP
    assert 2 * n // MOE_TILE + N_EXPERTS == n_tiles
    return pl.pallas_call(
        _dispatch_kernel,
        grid_spec=pltpu.PrefetchScalarGridSpec(
            num_scalar_prefetch=2,
            grid=(n // tb,),
            in_specs=[pl.BlockSpec((1, 1, 2 * tb), lambda i, z, u: (i, 0, 0), memory_space=pltpu.SMEM),
                      pl.BlockSpec((tb, D_MODEL), lambda i, z, u: (i, 0))],
            out_specs=pl.BlockSpec(memory_space=pl.ANY),
            scratch_shapes=[pltpu.VMEM((MOE_TILE, D_MODEL), F32), pltpu.VMEM((2, tb, D_MODEL), F32),
                            pltpu.SemaphoreType.DMA(()), pltpu.SemaphoreType.DMA((2,))],
        ),
        out_shape=jax.ShapeDtypeStruct(((n_tiles + 1) * MOE_TILE, D_MODEL), F32),
        compiler_params=pltpu.CompilerParams(dimension_semantics=("arbitrary",), vmem_limit_bytes=VMEM_LIMIT,
                                             disable_bounds_checks=True),
        name="moe_dispatch",
    )(zero_rows, n_used, idx, h2d)


def _expert_kernel(tile_e_ref, n_used_ref, x_ref, g_ref, w1_ref, w3_ref, w2_ref, y_ref, w1b_ref, w3b_ref, w2b_ref):
    i = pl.program_id(0)
    used = i < n_used_ref[0]

    @pl.when(jnp.logical_not(used))
    def _():
        y_ref[...] = jnp.zeros_like(y_ref)

    @pl.when(used & ((i == 0) | (tile_e_ref[i] != tile_e_ref[jnp.maximum(i - 1, 0)])))
    def _():
        w1b_ref[...] = w1_ref[...].astype(BF16)
        w3b_ref[...] = w3_ref[...].astype(BF16)
        w2b_ref[...] = w2_ref[...].astype(BF16)

    @pl.when(used)
    def _():
        x = x_ref[...]
        ms = jnp.mean(x * x, axis=-1, keepdims=True)
        xn = (x * lax.rsqrt(ms + NORM_EPS) * g_ref[...]).astype(BF16)
        a = jnp.dot(xn, w1b_ref[...], preferred_element_type=F32)
        b = jnp.dot(xn, w3b_ref[...], preferred_element_type=F32)
        hmid = (a * jax.nn.sigmoid(a) * b).astype(BF16)
        y_ref[...] = jnp.dot(hmid, w2b_ref[...], preferred_element_type=F32)


def _experts(xs, g, w1, w3, w2, tile_e, n_used, layer, n_tiles):
    row_map = lambda i, te, nu: (i, 0)
    in_row_map = lambda i, te, nu: (jnp.minimum(i, nu[0] - 1), 0)
    w_map = lambda i, te, nu: (layer, te[i], 0, 0)
    return pl.pallas_call(
        _expert_kernel,
        grid_spec=pltpu.PrefetchScalarGridSpec(
            num_scalar_prefetch=2,
            grid=(n_tiles,),
            in_specs=[pl.BlockSpec((MOE_TILE, D_MODEL), in_row_map),
                      pl.BlockSpec((None, 1, D_MODEL), lambda i, te, nu: (layer, 0, 0)),
                      pl.BlockSpec((None, None, D_MODEL, D_FF_EXPERT), w_map),
                      pl.BlockSpec((None, None, D_MODEL, D_FF_EXPERT), w_map),
                      pl.BlockSpec((None, None, D_FF_EXPERT, D_MODEL), w_map)],
            out_specs=pl.BlockSpec((MOE_TILE, D_MODEL), row_map),
            scratch_shapes=[pltpu.VMEM((D_MODEL, D_FF_EXPERT), BF16), pltpu.VMEM((D_MODEL, D_FF_EXPERT), BF16),
                            pltpu.VMEM((D_FF_EXPERT, D_MODEL), BF16)],
        ),
        out_shape=jax.ShapeDtypeStruct((n_tiles * MOE_TILE, D_MODEL), F32),
        compiler_params=_cparams("arbitrary"),
        name="moe_experts",
    )(tile_e, n_used, xs, g, w1, w3, w2)


def _combine_kernel(idx_ref, ys_hbm, h_ref, wts_ref, gf_ref, o_ref, buf_ref, sem, *, final_norm, n_blocks):
    i = pl.program_id(0)
    tb = h_ref.shape[0]

    @pl.when(i < n_blocks)
    def _():
        slot = i % 2

        def row(t, carry):
            for k in range(2):
                pltpu.make_async_copy(ys_hbm.at[pl.ds(idx_ref[0, 0, k * tb + t], 1)],
                                      buf_ref.at[slot, k, pl.ds(t, 1)], sem.at[slot]).start()
            return carry

        lax.fori_loop(0, tb, row, 0, unroll=True)

    @pl.when(i > 0)
    def _():
        slot = (i - 1) % 2
        for k in range(2):
            pltpu.make_async_copy(ys_hbm.at[pl.ds(0, tb)], buf_ref.at[slot, k], sem.at[slot]).wait()
        reps = D_MODEL // LANES
        w = wts_ref[...]
        w_top1 = jnp.concatenate([w[:, :LANES]] * reps, axis=1)
        w_top2 = jnp.concatenate([w[:, LANES:]] * reps, axis=1)
        out = h_ref[...] + w_top1 * buf_ref[slot, 0] + w_top2 * buf_ref[slot, 1]
        if final_norm:
            ms = jnp.mean(out * out, axis=-1, keepdims=True)
            out = out * lax.rsqrt(ms + NORM_EPS) * gf_ref[...]
        o_ref[...] = out


def _combine(h2d, ys, idx, wts, g_final, final_norm, batch):
    n = h2d.shape[0]
    tb = MOE_TOKENS_PER_STEP
    steps_per_seq = n // batch // tb
    n_blocks = n // tb
    prev = lambda i: jnp.maximum(i - 1, 0)
    return pl.pallas_call(
        functools.partial(_combine_kernel, final_norm=final_norm, n_blocks=n_blocks),
        grid=(n_blocks + 1,),
        in_specs=[pl.BlockSpec((1, 1, 2 * tb), lambda i: (jnp.minimum(i, n_blocks - 1), 0, 0),
                               memory_space=pltpu.SMEM),
                  pl.BlockSpec(memory_space=pl.ANY),
                  pl.BlockSpec((tb, D_MODEL), lambda i: (prev(i), 0)),
                  pl.BlockSpec((tb, 2 * LANES), lambda i: (prev(i), 0)),
                  pl.BlockSpec((1, D_MODEL), lambda i: (0, 0))],
        out_specs=pl.BlockSpec((None, tb, D_MODEL),
                               lambda i: (prev(i) // steps_per_seq, prev(i) % steps_per_seq, 0)),
        out_shape=jax.ShapeDtypeStruct((batch, n // batch, D_MODEL), F32),
        scratch_shapes=[pltpu.VMEM((2, 2, tb, D_MODEL), F32), pltpu.SemaphoreType.DMA((2,))],
        compiler_params=pltpu.CompilerParams(dimension_semantics=("arbitrary",), vmem_limit_bytes=VMEM_LIMIT,
                                             disable_bounds_checks=True),
        name="moe_combine",
    )(idx, ys, h2d, wts, g_final)


def _moe(h2d, g, w_router, b_router, w1, w3, w2, layer, g_final, final_norm, batch):
    n = h2d.shape[0]
    route, wts, counts = _router(h2d, g, w_router, b_router, layer)
    idx, tile_e, n_used, zero_rows, n_tiles = _dispatch_tables(route, counts, n)
    xs = _dispatch(h2d, idx, zero_rows, n_used, n_tiles)
    ys = _experts(xs, g, w1, w3, w2, tile_e, n_used, layer, n_tiles)
    return _combine(h2d, ys, idx, wts, g_final, final_norm, batch)


def _router_params(wg, bg, we, be):
    n_l = wg.shape[0]
    pad = ROUTER_ROWS - N_GROUPS - N_EXPERTS
    w = jnp.concatenate([wg, we, jnp.zeros((n_l, D_MODEL, pad), F32)], axis=-1).astype(F32).transpose(0, 2, 1)
    b = jnp.concatenate([bg, be, jnp.zeros((n_l, pad), F32)], axis=-1).astype(F32)
    return w, jnp.broadcast_to(b[:, :, None], (n_l, ROUTER_ROWS, LANES))


def kernel(x, w_in, na_rpb, lam_q1, lam_k1, lam_q2, lam_k2, diff_subln, w_pa, w_pb, w_pc, w_o, norm_mix, norm_ffn,
           router_group_w, router_group_b, router_expert_w, router_expert_b, w1, w3, w2, norm_final):
    B, T, D = x.shape
    depth = w_in.shape[0]
    rows = T // GRID_W
    wr = min(NA_WIN_R, rows)
    rope_tab = _rope_table(T)
    row3 = lambda a: a[:, None, :]
    w_in_bf = _permute_cols(w_in)
    bias_tab = _na_bias_table(na_rpb, wr)
    g_mix, g_ffn = row3(norm_mix), row3(norm_ffn)
    lams = [row3(a) for a in (lam_q1, lam_k1, lam_q2, lam_k2)]
    subln = row3(diff_subln)
    w_router, b_router = _router_params(router_group_w, router_group_b, router_expert_w, router_expert_b)
    h = x.reshape(B * T, D)
    for l in range(depth):
        lam_init = 0.8 - 0.6 * math.exp(-0.3 * l)
        proj, *cgs = _inproj(h, g_mix, w_in_bf, rope_tab, l, B, T)
        ya = _na_attention(proj, bias_tab, l, B, T)
        yb = _diff_attention(proj, *lams, subln, l, lam_init, B, T)
        yc = _dilated_branch(cgs)
        h = _merge(h, proj, ya, yb, yc, w_pa, w_pb, w_pc, w_o, l)
        out = _moe(h, g_ffn, w_router, b_router, w1, w3, w2, l, norm_final[None, :], l == depth - 1, B)
        h = out.reshape(B * T, D)
    return out
```

```python
import functools
import math

import jax
import jax.numpy as jnp
import numpy as np
from jax import lax
from jax.experimental import pallas as pl
from jax.experimental.pallas import tpu as pltpu

F32 = jnp.float32
BF16 = jnp.bfloat16

D_MODEL = 1024
HEAD_DIM = 64
ROPE_DIM = 16
ROPE_THETA = 500000.0
GRID_W = 64
NA_HEADS = 8
NA_WIN_R = 8
NA_WIN_C = 16
DIFF_HEADS = 4
DIL_PATTERNS = ((128, 1), (512, 4), (2048, 16))
N_GROUPS = 4
EXPERTS_PER_GROUP = 4
N_EXPERTS = 16
D_FF_EXPERT = D_MODEL // 2
NORM_EPS = 1e-6
SUBLN_EPS = 1e-5
NEG_INF = -1e30

LANES = 128
MXU_N = 256
VMEM_LIMIT = 56 * 1024 * 1024

A_W, B_W, C_W = 512, 512, 768
N_STEPS = len(DIL_PATTERNS)
GROUP_W = C_W // N_STEPS
_MAIN_ORDER = ("qb", "kb", "va", "qa", "ga", "gb", "gc", "ka", "vb")
_REF_ORDER = ("qa", "ka", "va", "qb", "kb", "vb", "qc", "kc", "vc", "ga", "gb", "gc")
_WIDTH = dict(qa=A_W, ka=A_W, va=A_W, qb=B_W, kb=B_W, vb=B_W, qc=C_W, kc=C_W, vc=C_W,
              ga=D_MODEL, gb=D_MODEL, gc=D_MODEL)


def _offsets(order):
    off, out = 0, {}
    for name in order:
        out[name] = off
        off += _WIDTH[name]
    return out, off


_OFF, MAIN_W = _offsets(_MAIN_ORDER)
_REF_OFF, IN_W = _offsets(_REF_ORDER)
OFF_QA, OFF_KA, OFF_VA = _OFF["qa"], _OFF["ka"], _OFF["va"]
OFF_QB, OFF_KB, OFF_VB = _OFF["qb"], _OFF["kb"], _OFF["vb"]
OFF_GATE = _OFF["ga"]
MAIN_STEP_W = MAIN_W // N_STEPS
STEP_W = MAIN_STEP_W + 3 * GROUP_W
MAIN_ROPE_W = 2 * B_W
QB_SCALE = HEAD_DIM ** -0.5 * math.log2(math.e)
assert OFF_GATE % D_MODEL == 0 and OFF_QB == 0 and OFF_KB == B_W and MAIN_ROPE_W <= MAIN_STEP_W
assert MAIN_W % N_STEPS == 0 and MAIN_STEP_W % MXU_N == 0 and N_STEPS * STEP_W == IN_W


def _permute_cols(w):
    def piece(name, lo, hi):
        p = w[..., _REF_OFF[name] + lo:_REF_OFF[name] + hi]
        return (p * QB_SCALE if name == "qb" else p).astype(BF16)

    parts = []
    for s in range(N_STEPS):
        for n in _MAIN_ORDER:
            lo = max(_OFF[n], s * MAIN_STEP_W) - _OFF[n]
            hi = min(_OFF[n] + _WIDTH[n], (s + 1) * MAIN_STEP_W) - _OFF[n]
            if lo < hi:
                parts.append(piece(n, lo, hi))
        for n in ("qc", "kc", "vc"):
            parts.append(piece(n, s * GROUP_W, (s + 1) * GROUP_W))
    return jnp.concatenate(parts, axis=-1)


def _cparams(*sem):
    return pltpu.CompilerParams(dimension_semantics=sem, vmem_limit_bytes=VMEM_LIMIT)


def _inproj_kernel(x_ref, g_ref, w_ref, rope_ref, main_ref, c0_ref, c1_ref, c2_ref, xn_ref, y_ref, *, tm):
    j = pl.program_id(1)
    c_refs = (c0_ref, c1_ref, c2_ref)
    reps = MXU_N // LANES
    half = ROPE_DIM // 2
    assert GROUP_W == MXU_N

    def rope(y):
        cos = jnp.concatenate([rope_ref[0]] * reps, axis=1)
        s_up = jnp.concatenate([rope_ref[1]] * reps, axis=1)
        s_dn = jnp.concatenate([rope_ref[2]] * reps, axis=1)
        return y * cos + pltpu.roll(y, MXU_N - half, 1) * s_up + pltpu.roll(y, half, 1) * s_dn

    def step(s):
        xn = xn_ref[...]
        for c in range(MAIN_STEP_W // MXU_N):
            sl = slice(c * MXU_N, (c + 1) * MXU_N)
            y = jnp.dot(xn, w_ref[:, sl], preferred_element_type=F32)
            if s == 0 and c < MAIN_ROPE_W // MXU_N:
                y = rope(y)
            main_ref[:, sl] = y.astype(main_ref.dtype)
        dil = DIL_PATTERNS[s][1]
        for c in range(3):
            wsl = slice(MAIN_STEP_W + c * GROUP_W, MAIN_STEP_W + (c + 1) * GROUP_W)
            osl = slice(c * GROUP_W, (c + 1) * GROUP_W)
            y = jnp.dot(xn, w_ref[:, wsl], preferred_element_type=F32)
            if c < 2:
                y = rope(y)
            if dil == 1:
                c_refs[s][0, :, osl] = y.astype(BF16)
            else:
                for hb in range(reps):
                    y_ref[c, hb] = y[:, hb * LANES:(hb + 1) * LANES]
                for p in range(dil):
                    for hb in range(reps):
                        c_refs[s][p, :, c * GROUP_W + hb * LANES:c * GROUP_W + (hb + 1) * LANES] = (
                            y_ref[c, hb, pl.ds(p, tm // dil, stride=dil), :].astype(BF16))

    @pl.when(j == 0)
    def _():
        x = x_ref[...]
        ms = jnp.mean(x * x, axis=-1, keepdims=True)
        xn_ref[...] = (x * lax.rsqrt(ms + NORM_EPS) * g_ref[...]).astype(BF16)
        step(0)

    for s in range(1, N_STEPS):
        pl.when(j == s)(functools.partial(step, s))


def _inproj(h2d, g, w_bf, rope_tab, layer, B, T, tm=1024):
    n = h2d.shape[0]
    tpb = T // tm
    c_specs, c_shapes = [], []
    for _, dil in DIL_PATTERNS:
        assert tm % dil == 0
        c_specs.append(pl.BlockSpec((None, dil, tm // dil, 3 * GROUP_W), lambda i, j: (i // tpb, 0, i % tpb, 0)))
        c_shapes.append(jax.ShapeDtypeStruct((B, dil, T // dil, 3 * GROUP_W), BF16))
    return pl.pallas_call(
        functools.partial(_inproj_kernel, tm=tm),
        grid=(n // tm, N_STEPS),
        in_specs=[
            pl.BlockSpec((tm, D_MODEL), lambda i, j: (i, 0)),
            pl.BlockSpec((None, 1, D_MODEL), lambda i, j: (layer, 0, 0)),
            pl.BlockSpec((None, D_MODEL, STEP_W), lambda i, j: (layer, 0, j)),
            pl.BlockSpec((3, tm, LANES), lambda i, j: (0, i % tpb, 0)),
        ],
        out_specs=[pl.BlockSpec((tm, MAIN_STEP_W), lambda i, j: (i, j))] + c_specs,
        out_shape=[jax.ShapeDtypeStruct((n, MAIN_W), BF16)] + c_shapes,
        scratch_shapes=[pltpu.VMEM((tm, D_MODEL), BF16), pltpu.VMEM((3, MXU_N // LANES, tm, LANES), F32)],
        compiler_params=_cparams("parallel", "arbitrary"),
        name="inproj",
    )(h2d, g, w_bf, rope_tab)


def _rope_table(T):
    half = ROPE_DIM // 2
    inv = 1.0 / (ROPE_THETA ** (jnp.arange(0, ROPE_DIM, 2, dtype=F32) / ROPE_DIM))
    ang = jnp.arange(T, dtype=F32)[:, None] * inv[None, :]
    cos, sin = jnp.cos(ang), jnp.sin(ang)
    zeros = jnp.zeros((T, HEAD_DIM - ROPE_DIM), F32)
    z8 = jnp.zeros((T, half), F32)
    c64 = jnp.concatenate([cos, cos, zeros + 1.0], axis=1)
    up64 = jnp.concatenate([-sin, z8, zeros], axis=1)
    dn64 = jnp.concatenate([z8, sin, zeros], axis=1)
    reps = LANES // HEAD_DIM
    return jnp.stack([jnp.tile(c64, (1, reps)), jnp.tile(up64, (1, reps)), jnp.tile(dn64, (1, reps))])


NA_ROWS_PER_ITER = 32


def _na_kernel(q_ref, k_ref, v_ref, b_ref, o_ref, *, rows, wr):
    lane = lax.broadcasted_iota(jnp.int32, (GRID_W, LANES), 1)
    lo = lane < HEAD_DIM
    scale = HEAD_DIM ** -0.5

    def row_group(gi, carry):
        scores, windows = [], []
        for u in range(NA_ROWS_PER_ITER):
            r = gi * NA_ROWS_PER_ITER + u
            r0 = jnp.clip(r - wr // 2, 0, rows - wr)
            d0 = r0 - r + NA_WIN_R - 1 - (NA_WIN_R - wr)
            q = q_ref[pl.ds(pl.multiple_of(r * GRID_W, GRID_W), GRID_W), :] * scale
            ks = pl.multiple_of(r0 * GRID_W, GRID_W)
            kw = k_ref[pl.ds(ks, wr * GRID_W), :]
            windows.append(ks)
            for hh in range(2):
                qh = jnp.where(lo if hh == 0 else jnp.logical_not(lo), q, jnp.zeros_like(q))
                s = lax.dot_general(qh, kw, (((1,), (1,)), ((), ())), preferred_element_type=F32)
                bias = jnp.concatenate([b_ref[hh, d0 + 2 * i] for i in range(wr // 2)], axis=1)
                scores.append(s + bias)
        probs, sums = [], []
        for s in scores:
            p = jnp.exp(s - jnp.max(s, axis=-1, keepdims=True))
            sums.append(jnp.sum(p, axis=-1, keepdims=True))
            probs.append(p.astype(BF16))
        for u in range(NA_ROWS_PER_ITER):
            r = gi * NA_ROWS_PER_ITER + u
            vw = v_ref[pl.ds(windows[u], wr * GRID_W), :]
            outs = [jnp.dot(probs[2 * u + hh], vw, preferred_element_type=F32) / sums[2 * u + hh] for hh in range(2)]
            o = jnp.where(lo, outs[0], outs[1])
            o_ref[pl.ds(pl.multiple_of(r * GRID_W, GRID_W), GRID_W), :] = o.astype(o_ref.dtype)
        return carry

    lax.fori_loop(0, rows // NA_ROWS_PER_ITER, row_group, 0)


def _na_bias_table(rpb, wr):
    qc = np.arange(GRID_W)[:, None]
    kc = np.arange(GRID_W)[None, :]
    c0 = np.clip(qc - NA_WIN_C // 2, 0, GRID_W - NA_WIN_C)
    ok = (kc >= c0) & (kc < c0 + NA_WIN_C)
    dc = np.clip(kc - qc + NA_WIN_C - 1, 0, 2 * NA_WIN_C - 2)
    onehot = (np.arange(2 * NA_WIN_C - 1)[:, None, None] == dc[None]).astype(np.float32)
    b = jnp.einsum("lhrd,dqk->lhrqk", rpb.astype(F32), onehot, precision=lax.Precision.HIGHEST)
    b = jnp.where(jnp.asarray(ok), b, NEG_INF)
    off = NA_WIN_R - wr
    n_pairs = 2 * wr - 2
    return jnp.concatenate([b[:, :, off:off + n_pairs], b[:, :, off + 1:off + 1 + n_pairs]], axis=-1)


def _na_attention(proj, bias_tab, layer, B, T):
    rows = T // GRID_W
    wr = min(NA_WIN_R, rows)
    cq, ck, cv = OFF_QA // LANES, OFF_KA // LANES, OFF_VA // LANES
    assert wr % 2 == 0 and 2 * GRID_W == LANES and rows % NA_ROWS_PER_ITER == 0
    return pl.pallas_call(
        functools.partial(_na_kernel, rows=rows, wr=wr),
        grid=(B, NA_HEADS // 2),
        in_specs=[
            pl.BlockSpec((T, LANES), lambda b, h: (b, cq + h)),
            pl.BlockSpec((T, LANES), lambda b, h: (b, ck + h)),
            pl.BlockSpec((T, LANES), lambda b, h: (b, cv + h)),
            pl.BlockSpec((None, 2, 2 * wr - 2, GRID_W, 2 * GRID_W), lambda b, h: (layer, h, 0, 0, 0)),
        ],
        out_specs=pl.BlockSpec((T, LANES), lambda b, h: (b, h)),
        out_shape=jax.ShapeDtypeStruct((B * T, A_W), BF16),
        compiler_params=_cparams("parallel", "arbitrary"),
        name="na_attn",
    )(proj, proj, proj, bias_tab)


DIFF_KEY_CHUNK = 512


def _diff_kernel(lq1_ref, lk1_ref, lq2_ref, lk2_ref, q_ref, k_ref, v_ref, g_ref, o_ref, vt_ref, *, lam_init, tq):
    lam = (jnp.exp(jnp.sum(lq1_ref[...] * lk1_ref[...], keepdims=True))
           - jnp.exp(jnp.sum(lq2_ref[...] * lk2_ref[...], keepdims=True)) + lam_init)
    T = k_ref.shape[0]
    ck = DIFF_KEY_CHUNK
    n_chunks = T // ck
    lane = lax.broadcasted_iota(jnp.int32, (tq, LANES), 1)
    lo = lane < HEAD_DIM
    nt = (((1,), (1,)), ((), ()))
    vt_ref[...] = v_ref[...].T

    def q_block(i, carry):
        rows = pl.ds(pl.multiple_of(i * tq, tq), tq)
        q = q_ref[rows, :]
        zero = jnp.zeros_like(q)
        qs = (jnp.where(lo, q, zero), jnp.where(lo, zero, q))

        def scores(c):
            kc = k_ref[c * ck:(c + 1) * ck, :]
            return [lax.dot_general(kc, qm, nt, preferred_element_type=F32) for qm in qs]

        m = [jnp.full((1, tq), NEG_INF, F32)] * 2
        l = [jnp.zeros((1, tq), F32)] * 2
        acc = [jnp.zeros((LANES, tq), F32)] * 2
        s_next = scores(0)
        for c in range(n_chunks):
            s_cur = s_next
            if c + 1 < n_chunks:
                s_next = scores(c + 1)
            vt = vt_ref[:, c * ck:(c + 1) * ck]
            for j in range(2):
                m_new = jnp.maximum(m[j], jnp.max(s_cur[j], axis=0, keepdims=True))
                alpha = jnp.exp2(m[j] - m_new)
                p = jnp.exp2(s_cur[j] - m_new)
                l[j] = alpha * l[j] + jnp.sum(p, axis=0, keepdims=True)
                acc[j] = alpha * acc[j] + jnp.dot(vt, p.astype(BF16), preferred_element_type=F32)
                m[j] = m_new
        ot = acc[0] / l[0] - lam * (acc[1] / l[1])
        o = ot.T
        ms = jnp.mean(o * o, axis=-1, keepdims=True)
        o = o * lax.rsqrt(ms + SUBLN_EPS) * g_ref[...] * (1.0 - lam_init)
        o_ref[rows, :] = o.astype(o_ref.dtype)
        return carry

    lax.fori_loop(0, q_ref.shape[0] // tq, q_block, 0, unroll=2)


def _diff_attention(proj, lq1, lk1, lq2, lk2, subln_g, layer, lam_init, B, T, tq=512):
    cq, ck, cv = OFF_QB // LANES, OFF_KB // LANES, OFF_VB // LANES
    vec = pl.BlockSpec((None, 1, HEAD_DIM), lambda b, h: (layer, 0, 0))
    return pl.pallas_call(
        functools.partial(_diff_kernel, lam_init=lam_init, tq=tq),
        grid=(B, DIFF_HEADS),
        in_specs=[
            vec, vec, vec, vec,
            pl.BlockSpec((T, LANES), lambda b, h: (b, cq + h)),
            pl.BlockSpec((T, LANES), lambda b, h: (b, ck + h)),
            pl.BlockSpec((T, LANES), lambda b, h: (b, cv + h)),
            pl.BlockSpec((None, 1, LANES), lambda b, h: (layer, 0, 0)),
        ],
        out_specs=pl.BlockSpec((T, LANES), lambda b, h: (b, h)),
        out_shape=jax.ShapeDtypeStruct((B * T, B_W), BF16),
        scratch_shapes=[pltpu.VMEM((LANES, T), BF16)],
        compiler_params=_cparams("parallel", "arbitrary"),
        name="diff_attn",
    )(lq1, lk1, lq2, lk2, proj, proj, proj, subln_g)


DIL_Q = 128


DIL_BLOCKS_PER_ITER = 4


def _dil_kernel(q_ref, k_ref, v_ref, o_ref, lse_ref, *, L, dil, radius):
    kw_len = DIL_Q + 2 * radius
    lane = lax.broadcasted_iota(jnp.int32, (DIL_Q, LANES), 1)
    lo = lane < HEAD_DIM
    rel = (lax.broadcasted_iota(jnp.int32, (DIL_Q, kw_len), 1)
           - lax.broadcasted_iota(jnp.int32, (DIL_Q, kw_len), 0))
    scale = HEAD_DIM ** -0.5
    nt = (((1,), (1,)), ((), ()))

    blocks_per_phase = L // DIL_Q

    def blocks(gi, carry):
        scores, starts, maxes = [], [], []
        for u in range(DIL_BLOCKS_PER_ITER):
            item = gi * DIL_BLOCKS_PER_ITER + u
            p = item // blocks_per_phase
            l0 = pl.multiple_of((item % blocks_per_phase) * DIL_Q, DIL_Q)
            ks = pl.multiple_of(jnp.clip(l0 - radius, 0, L - kw_len), radius)
            q = q_ref[p, pl.ds(l0, DIL_Q), :] * scale
            kw = k_ref[p, pl.ds(ks, kw_len), :]
            ok = jnp.abs(rel + (ks - l0)) <= radius
            starts.append((p, l0, ks))
            for hh in range(2):
                qh = jnp.where(lo if hh == 0 else jnp.logical_not(lo), q, jnp.zeros_like(q))
                s = lax.dot_general(qh, kw, nt, preferred_element_type=F32)
                scores.append(jnp.where(ok, s, NEG_INF))
        probs, sums = [], []
        for s in scores:
            m = jnp.max(s, axis=-1, keepdims=True)
            e = jnp.exp(s - m)
            maxes.append(m)
            sums.append(jnp.sum(e, axis=-1, keepdims=True))
            probs.append(e.astype(BF16))
        for u in range(DIL_BLOCKS_PER_ITER):
            p, l0, ks = starts[u]
            vw = v_ref[p, pl.ds(ks, kw_len), :]
            outs = [jnp.dot(probs[2 * u + hh], vw, preferred_element_type=F32) / sums[2 * u + hh] for hh in range(2)]
            lses = [maxes[2 * u + hh] + jnp.log(sums[2 * u + hh]) for hh in range(2)]
            o = jnp.where(lo, outs[0], outs[1])
            lse = jnp.where(lo, lses[0], lses[1])
            if dil == 1:
                o_ref[pl.ds(l0, DIL_Q), :] = o
                lse_ref[pl.ds(l0, DIL_Q), :] = lse
            else:
                o_ref[pl.ds(l0 * dil + p, DIL_Q, stride=dil), :] = o
                lse_ref[pl.ds(l0 * dil + p, DIL_Q, stride=dil), :] = lse
        return carry

    lax.fori_loop(0, dil * blocks_per_phase // DIL_BLOCKS_PER_ITER, blocks, 0)


def _dil_merge_kernel(*refs, geoms):
    n = len(geoms)
    out_ref = refs[3 * n]
    scratch = refs[3 * n + 1:]
    for g, (L, dil, radius) in enumerate(geoms):
        _dil_kernel(refs[3 * g], refs[3 * g + 1], refs[3 * g + 2], scratch[2 * g], scratch[2 * g + 1],
                    L=L, dil=dil, radius=radius)
    lses = [scratch[2 * g + 1][...] for g in range(n)]
    m = functools.reduce(jnp.maximum, lses)
    es = [jnp.exp(l - m) for l in lses]
    num = sum(scratch[2 * g][...] * es[g] for g in range(n))
    out_ref[...] = (num / sum(es)).astype(out_ref.dtype)


def _dilated_branch(cgs):
    B = cgs[0].shape[0]
    hp = GROUP_W // LANES
    geoms, in_specs, operands = [], [], []
    for cg, (window, dil) in zip(cgs, DIL_PATTERNS):
        _, d, L, _ = cg.shape
        radius = window // (2 * dil)
        assert d == dil and L >= DIL_Q + 2 * radius and L % DIL_Q == 0
        assert (dil * L // DIL_Q) % DIL_BLOCKS_PER_ITER == 0
        geoms.append((L, dil, radius))
        for c in range(3):
            in_specs.append(pl.BlockSpec((None, dil, L, LANES), lambda b, h, c=c: (b, 0, 0, c * hp + h)))
            operands.append(cg)
    T = geoms[0][0] * geoms[0][1]
    return pl.pallas_call(
        functools.partial(_dil_merge_kernel, geoms=tuple(geoms)),
        grid=(B, hp),
        in_specs=in_specs,
        out_specs=pl.BlockSpec((T, LANES), lambda b, h: (b, h)),
        out_shape=jax.ShapeDtypeStruct((B * T, GROUP_W), BF16),
        scratch_shapes=[pltpu.VMEM((T, LANES), F32)] * (2 * len(geoms)),
        compiler_params=_cparams("parallel", "arbitrary"),
        name="dil_attn",
    )(*operands)


def _merge_kernel(h_ref, ga_ref, gb_ref, gc_ref, ya_ref, yb_ref, yc_ref,
                  wpa32_ref, wpb32_ref, wpc32_ref, wo32_ref, out_ref, wpa_ref, wpb_ref, wpc_ref, wo_ref):
    @pl.when(pl.program_id(0) == 0)
    def _():
        for src, dst in ((wpa32_ref, wpa_ref), (wpb32_ref, wpb_ref), (wpc32_ref, wpc_ref), (wo32_ref, wo_ref)):
            dst[...] = src[...].astype(BF16)

    merged = jax.nn.sigmoid(ga_ref[...].astype(F32)) * jnp.dot(ya_ref[...], wpa_ref[...], preferred_element_type=F32)
    merged += jax.nn.sigmoid(gb_ref[...].astype(F32)) * jnp.dot(yb_ref[...], wpb_ref[...], preferred_element_type=F32)
    merged += jax.nn.sigmoid(gc_ref[...].astype(F32)) * jnp.dot(yc_ref[...], wpc_ref[...], preferred_element_type=F32)
    out_ref[...] = h_ref[...] + jnp.dot(merged.astype(BF16), wo_ref[...], preferred_element_type=F32)


def _merge(h2d, proj, ya, yb, yc, wpa, wpb, wpc, wo, layer, tm=512):
    n = h2d.shape[0]
    gw = GROUP_W
    row = lambda w: pl.BlockSpec((tm, w), lambda i: (i, 0))
    full = lambda a: pl.BlockSpec((None,) + a.shape[1:], lambda i: (layer, 0, 0))
    gate = lambda c: pl.BlockSpec((tm, D_MODEL), lambda i: (i, OFF_GATE // D_MODEL + c))
    return pl.pallas_call(
        _merge_kernel,
        grid=(n // tm,),
        in_specs=[row(D_MODEL), gate(0), gate(1), gate(2), row(A_W), row(B_W),
                  row(gw),
                  full(wpa), full(wpb), full(wpc), full(wo)],
        out_specs=row(D_MODEL),
        out_shape=jax.ShapeDtypeStruct((n, D_MODEL), F32),
        scratch_shapes=[pltpu.VMEM(a.shape[1:], BF16) for a in (wpa, wpb, wpc, wo)],
        compiler_params=_cparams("arbitrary"),
        name="merge_outproj",
    )(h2d, proj, proj, proj, ya, yb, yc, wpa, wpb, wpc, wo)


MOE_TILE = 512
MOE_TOKENS_PER_STEP = 1024
ROUTE_ROWS = 8


ROUTER_ROWS = 32


def _router_kernel(h_ref, g_ref, wr_ref, br_ref, upper_ref, route_ref, wts_ref, cnt_ref, base_ref):
    i = pl.program_id(0)
    tm = h_ref.shape[0]
    lane_reps = tm // LANES
    row = lax.broadcasted_iota(jnp.int32, (ROUTER_ROWS, tm), 0)
    rowf = row.astype(F32)
    big = float(ROUTER_ROWS)
    nt = (((1,), (1,)), ((), ()))

    @pl.when(i == 0)
    def _():
        base_ref[...] = jnp.zeros_like(base_ref)

    x = h_ref[...]
    ms = jnp.mean(x * x, axis=-1, keepdims=True)
    xn = x * lax.rsqrt(ms + NORM_EPS) * g_ref[...]
    logits = lax.dot_general(wr_ref[...], xn, nt, preferred_element_type=F32, precision=lax.Precision.HIGHEST)
    logits = logits + jnp.concatenate([br_ref[...]] * lane_reps, axis=1)
    gl = jnp.where(row < N_GROUPS, logits, -jnp.inf)
    gmax = jnp.max(gl, axis=0, keepdims=True)
    g_sel = jnp.min(jnp.where(gl == gmax, rowf, big), axis=0, keepdims=True).astype(jnp.int32)
    g_gate = 1.0 / jnp.sum(jnp.exp(gl - gmax), axis=0, keepdims=True)
    eidx = row - N_GROUPS
    in_grp = (eidx >= g_sel * EXPERTS_PER_GROUP) & (eidx < (g_sel + 1) * EXPERTS_PER_GROUP)
    el = jnp.where(in_grp, logits, -jnp.inf)
    t1 = jnp.max(el, axis=0, keepdims=True)
    i1 = jnp.min(jnp.where(el == t1, rowf, big), axis=0, keepdims=True).astype(jnp.int32)
    el2 = jnp.where(row == i1, -jnp.inf, el)
    t2 = jnp.max(el2, axis=0, keepdims=True)
    i2 = jnp.min(jnp.where(el2 == t2, rowf, big), axis=0, keepdims=True).astype(jnp.int32)
    x2 = jnp.exp(t2 - t1)
    den = 1.0 + x2
    wts_ref[:, :LANES] = jnp.broadcast_to(g_gate / den, (LANES, tm)).T
    wts_ref[:, LANES:] = jnp.broadcast_to(g_gate * x2 / den, (LANES, tm)).T
    member = jnp.where(row == i1, 1.0, 0.0) + jnp.where(row == i2, 1.0, 0.0)
    earlier = jnp.dot(member.astype(BF16), upper_ref[...], preferred_element_type=F32)
    earlier = earlier + jnp.concatenate([base_ref[...]] * lane_reps, axis=1)
    rank1 = jnp.sum(jnp.where(row == i1, earlier, 0.0), axis=0, keepdims=True)
    rank2 = jnp.sum(jnp.where(row == i2, earlier, 0.0), axis=0, keepdims=True)
    base_ref[...] = base_ref[...] + jnp.sum(member, axis=1, keepdims=True)
    cnt_ref[...] = base_ref[...]
    out_row = lax.broadcasted_iota(jnp.int32, (ROUTE_ROWS, tm), 0)
    route_ref[...] = jnp.where(out_row == 0, (i1 - N_GROUPS).astype(F32),
                               jnp.where(out_row == 1, (i2 - N_GROUPS).astype(F32),
                                         jnp.where(out_row == 2, rank1, jnp.where(out_row == 3, rank2, 0.0))))


def _router(h2d, g, w_router, b_router, layer, tm=1024):
    n = h2d.shape[0]
    return pl.pallas_call(
        _router_kernel,
        grid=(n // tm,),
        in_specs=[
            pl.BlockSpec((tm, D_MODEL), lambda i: (i, 0)),
            pl.BlockSpec((None, 1, D_MODEL), lambda i: (layer, 0, 0)),
            pl.BlockSpec((None, ROUTER_ROWS, D_MODEL), lambda i: (layer, 0, 0)),
            pl.BlockSpec((None, ROUTER_ROWS, LANES), lambda i: (layer, 0, 0)),
            pl.BlockSpec((tm, tm), lambda i: (0, 0)),
        ],
        out_specs=[pl.BlockSpec((ROUTE_ROWS, tm), lambda i: (0, i)),
                   pl.BlockSpec((tm, 2 * LANES), lambda i: (i, 0)),
                   pl.BlockSpec((ROUTER_ROWS, LANES), lambda i: (0, 0))],
        out_shape=[jax.ShapeDtypeStruct((ROUTE_ROWS, n), F32), jax.ShapeDtypeStruct((n, 2 * LANES), F32),
                   jax.ShapeDtypeStruct((ROUTER_ROWS, LANES), F32)],
        scratch_shapes=[pltpu.VMEM((ROUTER_ROWS, LANES), F32)],
        compiler_params=_cparams("arbitrary"),
        name="moe_router",
    )(h2d, g, w_router, b_router, jnp.triu(jnp.ones((tm, tm), BF16), k=1))


def _dispatch_tables(route, counts, n):
    cnt = counts[N_GROUPS:N_GROUPS + N_EXPERTS, 0].astype(jnp.int32)
    padded = (cnt + MOE_TILE - 1) // MOE_TILE * MOE_TILE
    ends = jnp.cumsum(padded)
    starts = ends - padded
    eids = jnp.arange(N_EXPERTS, dtype=jnp.int32)
    pos = []
    for k in range(2):
        e = route[k].astype(jnp.int32)
        start_e = jnp.sum(jnp.where(e[:, None] == eids[None, :], starts[None, :], 0), axis=1)
        pos.append(start_e + route[2 + k].astype(jnp.int32))
    tb = MOE_TOKENS_PER_STEP
    idx = jnp.concatenate([pos[0].reshape(n // tb, 1, tb), pos[1].reshape(n // tb, 1, tb)], axis=-1)
    n_tiles = 2 * n // MOE_TILE + N_EXPERTS
    tile_ids = jnp.arange(n_tiles, dtype=jnp.int32)
    n_used = ends[-1] // MOE_TILE
    tile_e = jnp.sum((tile_ids[:, None] * MOE_TILE >= ends[None, :]).astype(jnp.int32), axis=1)
    last_e = jnp.max(jnp.where(tile_ids < n_used, tile_e, 0))
    tile_e = jnp.where(tile_ids < n_used, tile_e, last_e).astype(jnp.int32)
    zero_rows = jnp.where(padded > 0, ends - MOE_TILE, n_tiles * MOE_TILE).astype(jnp.int32)
    return idx, tile_e, n_used.reshape(1).astype(jnp.int32), zero_rows, n_tiles


def _dispatch_kernel(zero_rows_ref, n_used_ref, idx_ref, h_ref, xs_hbm, zeros_ref, stage_ref, sem, row_sem):
    i = pl.program_id(0)
    tb = MOE_TOKENS_PER_STEP
    n_tiles = xs_hbm.shape[0] // MOE_TILE - 1

    def zero_tile(row0):
        return pltpu.make_async_copy(zeros_ref, xs_hbm.at[pl.ds(pl.multiple_of(row0, MOE_TILE), MOE_TILE)], sem)

    @pl.when(i == 0)
    def _():
        zeros_ref[...] = jnp.zeros_like(zeros_ref)
        fills = [zero_tile(zero_rows_ref[e]) for e in range(N_EXPERTS)]
        for c in fills:
            c.start()
        for c in fills:
            c.wait()

        def tail(j, carry):
            tile = n_used_ref[0] + j

            @pl.when(tile <= n_tiles)
            def _():
                c = zero_tile(tile * MOE_TILE)
                c.start()
                c.wait()
            return carry

        lax.fori_loop(0, N_EXPERTS + 1, tail, 0)

    slot = i % 2
    stage_ref[slot] = h_ref[...]

    def row(t, carry):
        src = stage_ref.at[slot, pl.ds(t, 1)]
        for k in range(2):
            pltpu.make_async_copy(src, xs_hbm.at[pl.ds(idx_ref[0, 0, k * tb + t], 1)],
                                  row_sem.at[slot]).start(priority=k)
        return carry

    lax.fori_loop(0, tb, row, 0, unroll=True)

    def drain(s):
        for _ in range(2):
            pltpu.make_async_copy(stage_ref.at[s], xs_hbm.at[pl.ds(0, tb)], row_sem.at[s]).wait()

    @pl.when(i > 0)
    def _():
        drain(1 - slot)

    @pl.when(i == pl.num_programs(0) - 1)
    def _():
        drain(slot)


def _dispatch(h2d, idx, zero_rows, n_used, n_tiles):
    n = h2d.shape[0]
    tb = MOE_TOKENS_PER_STEP
    assert 2 * n // MOE_TILE + N_EXPERTS == n_tiles
    return pl.pallas_call(
        _dispatch_kernel,
        grid_spec=pltpu.PrefetchScalarGridSpec(
            num_scalar_prefetch=2,
            grid=(n // tb,),
            in_specs=[pl.BlockSpec((1, 1, 2 * tb), lambda i, z, u: (i, 0, 0), memory_space=pltpu.SMEM),
                      pl.BlockSpec((tb, D_MODEL), lambda i, z, u: (i, 0))],
            out_specs=pl.BlockSpec(memory_space=pl.ANY),
            scratch_shapes=[pltpu.VMEM((MOE_TILE, D_MODEL), F32), pltpu.VMEM((2, tb, D_MODEL), F32),
                            pltpu.SemaphoreType.DMA(()), pltpu.SemaphoreType.DMA((2,))],
        ),
        out_shape=jax.ShapeDtypeStruct(((n_tiles + 1) * MOE_TILE, D_MODEL), F32),
        compiler_params=pltpu.CompilerParams(dimension_semantics=("arbitrary",), vmem_limit_bytes=VMEM_LIMIT,
                                             disable_bounds_checks=True),
        name="moe_dispatch",
    )(zero_rows, n_used, idx, h2d)


def _expert_kernel(tile_e_ref, n_used_ref, x_ref, g_ref, w1_ref, w3_ref, w2_ref, y_ref, w1b_ref, w3b_ref, w2b_ref):
    i = pl.program_id(0)
    used = i < n_used_ref[0]

    @pl.when(jnp.logical_not(used))
    def _():
        y_ref[...] = jnp.zeros_like(y_ref)

    @pl.when(used & ((i == 0) | (tile_e_ref[i] != tile_e_ref[jnp.maximum(i - 1, 0)])))
    def _():
        w1b_ref[...] = w1_ref[...].astype(BF16)
        w3b_ref[...] = w3_ref[...].astype(BF16)
        w2b_ref[...] = w2_ref[...].astype(BF16)

    @pl.when(used)
    def _():
        x = x_ref[...]
        ms = jnp.mean(x * x, axis=-1, keepdims=True)
        xn = (x * lax.rsqrt(ms + NORM_EPS) * g_ref[...]).astype(BF16)
        a = jnp.dot(xn, w1b_ref[...], preferred_element_type=F32)
        b = jnp.dot(xn, w3b_ref[...], preferred_element_type=F32)
        hmid = (a * jax.nn.sigmoid(a) * b).astype(BF16)
        y_ref[...] = jnp.dot(hmid, w2b_ref[...], preferred_element_type=F32)


def _experts(xs, g, w1, w3, w2, tile_e, n_used, layer, n_tiles):
    row_map = lambda i, te, nu: (i, 0)
    in_row_map = lambda i, te, nu: (jnp.minimum(i, nu[0] - 1), 0)
    w_map = lambda i, te, nu: (layer, te[i], 0, 0)
    return pl.pallas_call(
        _expert_kernel,
        grid_spec=pltpu.PrefetchScalarGridSpec(
            num_scalar_prefetch=2,
            grid=(n_tiles,),
            in_specs=[pl.BlockSpec((MOE_TILE, D_MODEL), in_row_map),
                      pl.BlockSpec((None, 1, D_MODEL), lambda i, te, nu: (layer, 0, 0)),
                      pl.BlockSpec((None, None, D_MODEL, D_FF_EXPERT), w_map),
                      pl.BlockSpec((None, None, D_MODEL, D_FF_EXPERT), w_map),
                      pl.BlockSpec((None, None, D_FF_EXPERT, D_MODEL), w_map)],
            out_specs=pl.BlockSpec((MOE_TILE, D_MODEL), row_map),
            scratch_shapes=[pltpu.VMEM((D_MODEL, D_FF_EXPERT), BF16), pltpu.VMEM((D_MODEL, D_FF_EXPERT), BF16),
                            pltpu.VMEM((D_FF_EXPERT, D_MODEL), BF16)],
        ),
        out_shape=jax.ShapeDtypeStruct((n_tiles * MOE_TILE, D_MODEL), F32),
        compiler_params=_cparams("arbitrary"),
        name="moe_experts",
    )(tile_e, n_used, xs, g, w1, w3, w2)


def _combine_kernel(idx_ref, ys_hbm, h_ref, wts_ref, gf_ref, o_ref, buf_ref, sem, *, final_norm, n_blocks):
    i = pl.program_id(0)
    tb = h_ref.shape[0]

    @pl.when(i < n_blocks)
    def _():
        slot = i % 2

        def row(t, carry):
            for k in range(2):
                pltpu.make_async_copy(ys_hbm.at[pl.ds(idx_ref[0, 0, k * tb + t], 1)],
                                      buf_ref.at[slot, k, pl.ds(t, 1)], sem.at[slot]).start(priority=k)
            return carry

        lax.fori_loop(0, tb, row, 0, unroll=True)

    @pl.when(i > 0)
    def _():
        slot = (i - 1) % 2
        for k in range(2):
            pltpu.make_async_copy(ys_hbm.at[pl.ds(0, tb)], buf_ref.at[slot, k], sem.at[slot]).wait()
        reps = D_MODEL // LANES
        w = wts_ref[...]
        w_top1 = jnp.concatenate([w[:, :LANES]] * reps, axis=1)
        w_top2 = jnp.concatenate([w[:, LANES:]] * reps, axis=1)
        out = h_ref[...] + w_top1 * buf_ref[slot, 0] + w_top2 * buf_ref[slot, 1]
        if final_norm:
            ms = jnp.mean(out * out, axis=-1, keepdims=True)
            out = out * lax.rsqrt(ms + NORM_EPS) * gf_ref[...]
        o_ref[...] = out


def _combine(h2d, ys, idx, wts, g_final, final_norm, batch):
    n = h2d.shape[0]
    tb = MOE_TOKENS_PER_STEP
    steps_per_seq = n // batch // tb
    n_blocks = n // tb
    prev = lambda i: jnp.maximum(i - 1, 0)
    return pl.pallas_call(
        functools.partial(_combine_kernel, final_norm=final_norm, n_blocks=n_blocks),
        grid=(n_blocks + 1,),
        in_specs=[pl.BlockSpec((1, 1, 2 * tb), lambda i: (jnp.minimum(i, n_blocks - 1), 0, 0),
                               memory_space=pltpu.SMEM),
                  pl.BlockSpec(memory_space=pl.ANY),
                  pl.BlockSpec((tb, D_MODEL), lambda i: (prev(i), 0)),
                  pl.BlockSpec((tb, 2 * LANES), lambda i: (prev(i), 0)),
                  pl.BlockSpec((1, D_MODEL), lambda i: (0, 0))],
        out_specs=pl.BlockSpec((None, tb, D_MODEL),
                               lambda i: (prev(i) // steps_per_seq, prev(i) % steps_per_seq, 0)),
        out_shape=jax.ShapeDtypeStruct((batch, n // batch, D_MODEL), F32),
        scratch_shapes=[pltpu.VMEM((2, 2, tb, D_MODEL), F32), pltpu.SemaphoreType.DMA((2,))],
        compiler_params=pltpu.CompilerParams(dimension_semantics=("arbitrary",), vmem_limit_bytes=VMEM_LIMIT,
                                             disable_bounds_checks=True),
        name="moe_combine",
    )(idx, ys, h2d, wts, g_final)


def _moe(h2d, g, w_router, b_router, w1, w3, w2, layer, g_final, final_norm, batch):
    n = h2d.shape[0]
    route, wts, counts = _router(h2d, g, w_router, b_router, layer)
    idx, tile_e, n_used, zero_rows, n_tiles = _dispatch_tables(route, counts, n)
    xs = _dispatch(h2d, idx, zero_rows, n_used, n_tiles)
    ys = _experts(xs, g, w1, w3, w2, tile_e, n_used, layer, n_tiles)
    return _combine(h2d, ys, idx, wts, g_final, final_norm, batch)


def _router_params(wg, bg, we, be):
    n_l = wg.shape[0]
    pad = ROUTER_ROWS - N_GROUPS - N_EXPERTS
    w = jnp.concatenate([wg, we, jnp.zeros((n_l, D_MODEL, pad), F32)], axis=-1).astype(F32).transpose(0, 2, 1)
    b = jnp.concatenate([bg, be, jnp.zeros((n_l, pad), F32)], axis=-1).astype(F32)
    return w, jnp.broadcast_to(b[:, :, None], (n_l, ROUTER_ROWS, LANES))


def kernel(x, w_in, na_rpb, lam_q1, lam_k1, lam_q2, lam_k2, diff_subln, w_pa, w_pb, w_pc, w_o, norm_mix, norm_ffn,
           router_group_w, router_group_b, router_expert_w, router_expert_b, w1, w3, w2, norm_final):
    B, T, D = x.shape
    depth = w_in.shape[0]
    rows = T // GRID_W
    wr = min(NA_WIN_R, rows)
    rope_tab = _rope_table(T)
    row3 = lambda a: a[:, None, :]
    w_in_bf = _permute_cols(w_in)
    bias_tab = _na_bias_table(na_rpb, wr)
    g_mix, g_ffn = row3(norm_mix), row3(norm_ffn)
    lams = [row3(a) for a in (lam_q1, lam_k1, lam_q2, lam_k2)]
    subln = row3(diff_subln)
    w_router, b_router = _router_params(router_group_w, router_group_b, router_expert_w, router_expert_b)
    h = x.reshape(B * T, D)
    for l in range(depth):
        lam_init = 0.8 - 0.6 * math.exp(-0.3 * l)
        proj, *cgs = _inproj(h, g_mix, w_in_bf, rope_tab, l, B, T)
        ya = _na_attention(proj, bias_tab, l, B, T)
        yb = _diff_attention(proj, *lams, subln, l, lam_init, B, T)
        yc = _dilated_branch(cgs)
        h = _merge(h, proj, ya, yb, yc, w_pa, w_pb, w_pc, w_o, l)
        out = _moe(h, g_ffn, w_router, b_router, w1, w3, w2, l, norm_final[None, :], l == depth - 1, B)
        h = out.reshape(B * T, D)
    return out
```
